```python
import jax, jax.numpy as jnp
from jax import lax
import numpy as np

D_MODEL = 2048
BATCH = 1
SEQ = 8192
DEPTH = 1

CTX_LEN = 256
GRID_W = 64
D_MIX = D_MODEL
GLA_HEADS = 4
GLA_DK = 128
GLA_DV = 256
GLA_KW = GLA_HEADS * GLA_DK
GLA_VW = GLA_HEADS * GLA_DV
GATE_RANK = 16
GATE_NORM = 16.0
CHUNK = 64
CONV_W = D_MIX - GLA_VW
CONV_K = 3
FFN_HIDDEN = -(-8 * D_MODEL // (3 * 256)) * 256
IN_COLS = 2 * GLA_KW + 2 * GLA_VW + 2 * GATE_RANK + 3 * CONV_W
EPS = 1e-6

kernel_name = "hymba_style_gla_shortconv_flow_block"


def rmsnorm(x, g):
    x32 = x.astype(jnp.float32)
    y = x32 * lax.rsqrt(jnp.mean(x32 * x32, axis=-1, keepdims=True) + EPS)
    return y.astype(x.dtype) * g


def modulate(h, shift, scale):
    return h * (1.0 + scale) + shift


def split_proj(p):
    sizes = [GLA_KW, GLA_KW, GLA_VW, GLA_VW, GATE_RANK, GATE_RANK, CONV_W, CONV_W, CONV_W]
    idx = np.cumsum(sizes)[:-1].tolist()
    return jnp.split(p, idx, axis=-1)


def gla_chunked(q, k, v, log_a, s0):
    b_, n, h, _ = q.shape
    dv = v.shape[-1]
    nc = n // CHUNK

    def blk(t):
        return t.reshape(b_, nc, CHUNK, h, t.shape[-1]).transpose(0, 3, 1, 2, 4)

    q, k, v, log_a = blk(q), blk(k), blk(v), blk(log_a)
    cum = jnp.cumsum(log_a, axis=3)
    last = cum[:, :, :, -1:, :]
    q_e = q * jnp.exp(cum)
    k_e = k * jnp.exp(-cum)
    tri = jnp.tril(jnp.ones((CHUNK, CHUNK), dtype=bool))
    scores = jnp.where(tri, jnp.einsum('bhnid,bhnjd->bhnij', q_e, k_e), 0.0)
    o_intra = jnp.einsum('bhnij,bhnjv->bhniv', scores, v)
    ds = jnp.einsum('bhnjd,bhnjv->bhndv', k * jnp.exp(last - cum), v)
    decay = jnp.exp(last[:, :, :, 0, :])

    def step(s, inp):
        d, u = inp
        return d[..., None] * s + u, s

    s_final, s_prev = lax.scan(step, s0, (jnp.moveaxis(decay, 2, 0), jnp.moveaxis(ds, 2, 0)))
    s_prev = jnp.moveaxis(s_prev, 0, 2)
    o = o_intra + jnp.einsum('bhnid,bhndv->bhniv', q_e, s_prev)
    return o.transpose(0, 2, 3, 1, 4).reshape(b_, n, h, dv), s_final


def gla_bidir(q, k, v, la_f, la_b, s0_f, s0_b):
    o_f, s_f = gla_chunked(q, k, v, la_f, s0_f)
    flip = lambda t: jnp.flip(t, axis=1)
    o_b, s_b = gla_chunked(flip(q), flip(k), flip(v), flip(la_b), s0_b)
    return o_f + flip(o_b), s_f, s_b


def gla_inputs(q, k, v, lf, lb, w_gf, b_gf, w_gb, b_gb):
    b_, n = q.shape[:2]
    heads = lambda t, d: t.reshape(b_, n, GLA_HEADS, d).astype(jnp.float32)
    qh = heads(q, GLA_DK) * (GLA_DK ** -0.5)
    kh = heads(k, GLA_DK)
    vh = heads(v, GLA_DV)
    la_f = heads(jax.nn.log_sigmoid((lf @ w_gf + b_gf).astype(jnp.float32)) / GATE_NORM, GLA_DK)
    la_b = heads(jax.nn.log_sigmoid((lb @ w_gb + b_gb).astype(jnp.float32)) / GATE_NORM, GLA_DK)
    return qh, kh, vh, la_f, la_b


def conv_grid(u, w):
    b_, n, ch = u.shape
    rows = n // GRID_W
    gp = jnp.pad(u.reshape(b_, rows, GRID_W, ch), ((0, 0), (0, 0), (CONV_K // 2, CONV_K // 2), (0, 0)))
    y = sum(w[j] * gp[:, :, j:j + GRID_W] for j in range(CONV_K))
    return y.reshape(b_, n, ch)


def conv_seq(u, w):
    n = u.shape[1]
    up = jnp.pad(u, ((0, 0), (CONV_K // 2, CONV_K // 2), (0, 0)))
    return sum(w[j] * up[:, j:j + n] for j in range(CONV_K))


def mix_output(o_gla, g, cb, cc, cx, conv_fn, conv_w, gla_g, w_out):
    b_, n = g.shape[:2]
    o32 = o_gla * lax.rsqrt(jnp.mean(o_gla * o_gla, axis=-1, keepdims=True) + EPS)
    o = (o32.astype(g.dtype) * gla_g).reshape(b_, n, GLA_VW) * jax.nn.silu(g)
    y_conv = cb * conv_fn(cc * cx, conv_w)
    return jnp.concatenate([o, y_conv], axis=-1) @ w_out


def swiglu(h, wg, wu, wd):
    return (jax.nn.silu(h @ wg) * (h @ wu)) @ wd


def setup_inputs(seed: int = 0) -> dict:
    key = jax.random.key(seed)
    ks = jax.random.split(key, 24)
    nrm = lambda k, shape, s: jax.random.normal(k, shape, jnp.float32) * s
    D, L = D_MODEL, DEPTH
    return {
        "x": nrm(ks[0], (BATCH, SEQ, D), 1.0),
        "c": nrm(ks[1], (BATCH, D), 1.0),
        "ctx": nrm(ks[2], (BATCH, CTX_LEN, D), 1.0),
        "c_ctx": nrm(ks[3], (D,), 1.0),
        "w_mod": nrm(ks[4], (L, D, 6 * D), 0.5 * D ** -0.5),
        "b_mod": nrm(ks[5], (L, 6 * D), 0.01),
        "norm1_g": 1.0 + nrm(ks[6], (L, D), 0.05),
        "norm2_g": 1.0 + nrm(ks[7], (L, D), 0.05),
        "w_in": nrm(ks[8], (L, D, IN_COLS), D ** -0.5),
        "w_gate_f": nrm(ks[9], (L, GATE_RANK, GLA_KW), GATE_RANK ** -0.5),
        "b_gate_f": nrm(ks[10], (L, GLA_KW), 0.1),
        "w_gate_b": nrm(ks[11], (L, GATE_RANK, GLA_KW), GATE_RANK ** -0.5),
        "b_gate_b": nrm(ks[12], (L, GLA_KW), 0.1),
        "gla_norm_g": 1.0 + nrm(ks[13], (L, GLA_DV), 0.05),
        "conv_w": nrm(ks[14], (L, CONV_K, CONV_W), CONV_K ** -0.5),
        "w_out": nrm(ks[15], (L, D_MIX, D), D_MIX ** -0.5),
        "w_ffn_gate": nrm(ks[16], (L, D, FFN_HIDDEN), D ** -0.5),
        "w_ffn_up": nrm(ks[17], (L, D, FFN_HIDDEN), D ** -0.5),
        "w_ffn_down": nrm(ks[18], (L, FFN_HIDDEN, D), FFN_HIDDEN ** -0.5),
        "final_g": 1.0 + nrm(ks[19], (D,), 0.05),
    }


def reference(x, c, ctx, c_ctx, w_mod, b_mod, norm1_g, norm2_g, w_in, w_gate_f, b_gate_f,
              w_gate_b, b_gate_b, gla_norm_g, conv_w, w_out, w_ffn_gate, w_ffn_up, w_ffn_down,
              final_g):
    b_ = x.shape[0]
    for i in range(DEPTH):
        mx = jax.nn.silu(c) @ w_mod[i] + b_mod[i]
        mc = jax.nn.silu(c_ctx) @ w_mod[i] + b_mod[i]
        sh1, sc1, gt1, sh2, sc2, gt2 = [t[:, None, :] for t in jnp.split(mx, 6, axis=-1)]
        sh1c, sc1c, gt1c, sh2c, sc2c, gt2c = jnp.split(mc, 6, axis=-1)

        hc = modulate(rmsnorm(ctx, norm1_g[i]), sh1c, sc1c)
        qc, kc, vc, gc, lfc, lbc, cbc, ccc, cxc = split_proj(hc @ w_in[i])
        qh, kh, vh, la_f, la_b = gla_inputs(qc, kc, vc, lfc, lbc, w_gate_f[i], b_gate_f[i],
                                             w_gate_b[i], b_gate_b[i])
        s0 = jnp.zeros((b_, GLA_HEADS, GLA_DK, GLA_DV), jnp.float32)
        o_c, s_f, s_b = gla_bidir(qh, kh, vh, la_f, la_b, s0, s0)

        hx = modulate(rmsnorm(x, norm1_g[i]), sh1, sc1)
        q, k, v, g, lf, lb, cb, cc, cx = split_proj(hx @ w_in[i])
        qh, kh, vh, la_f, la_b = gla_inputs(q, k, v, lf, lb, w_gate_f[i], b_gate_f[i],
                                             w_gate_b[i], b_gate_b[i])
        o_x, _, _ = gla_bidir(qh, kh, vh, la_f, la_b, s_f, s_b)
        mix = mix_output(o_x, g, cb, cc, cx, conv_grid, conv_w[i], gla_norm_g[i], w_out[i])
        x_next = x + gt1 * mix
        h2 = modulate(rmsnorm(x_next, norm2_g[i]), sh2, sc2)
        x_next = x_next + gt2 * swiglu(h2, w_ffn_gate[i], w_ffn_up[i], w_ffn_down[i])

        if i < DEPTH - 1:
            mix_c = mix_output(o_c, gc, cbc, ccc, cxc, conv_seq, conv_w[i], gla_norm_g[i], w_out[i])
            ctx = ctx + gt1c * mix_c
            h2c = modulate(rmsnorm(ctx, norm2_g[i]), sh2c, sc2c)
            ctx = ctx + gt2c * swiglu(h2c, w_ffn_gate[i], w_ffn_up[i], w_ffn_down[i])
        x = x_next
    return rmsnorm(x, final_g)
```

```python
import functools

import jax
import jax.numpy as jnp
from jax import lax
from jax.experimental import pallas as pl
from jax.experimental.pallas import tpu as pltpu

F32 = jnp.float32
BF16 = jnp.bfloat16

D_MODEL = 2048
GRID_W = 64
GLA_HEADS = 4
GLA_DK = 128
GLA_DV = 256
GLA_KW = GLA_HEADS * GLA_DK
GLA_VW = GLA_HEADS * GLA_DV
GATE_RANK = 16
GATE_NORM = 16.0
CHUNK = 64
CONV_W = D_MODEL - GLA_VW
FFN_HIDDEN = 5632
EPS = 1e-6

LANES = 128
SUBLANES = 8
VMEM_BYTES = 64 * 1024 * 1024

P_MAIN = 2 * GLA_KW + 2 * GLA_VW + 3 * CONV_W
P_GATE = 256
P_COLS = P_MAIN + P_GATE
P_BN = 1280

MOD_BN = 1024
FFN_BH = 512
NORM_ROWS = 128


def _params(grid_rank, vmem_bytes):
    return pltpu.CompilerParams(
        dimension_semantics=("arbitrary",) * grid_rank,
        vmem_limit_bytes=min(int(vmem_bytes), VMEM_BYTES - 6 * 1024 * 1024),
    )


def _silu(v):
    return v * jax.nn.sigmoid(v)


def _norm_mod_rows(x_ref, g, mul, sh, dst_ref, rows):
    for r0 in range(0, rows, NORM_ROWS):
        xs = x_ref[r0:r0 + NORM_ROWS, :]
        ms = jnp.mean(xs * xs, axis=-1, keepdims=True)
        y = xs * lax.rsqrt(ms + EPS)
        dst_ref[r0:r0 + NORM_ROWS, :] = ((y * g) * mul + sh).astype(BF16)


def _mod_kernel(c_ref, w_ref, b_ref, o_ref, sb_ref):
    @pl.when(pl.program_id(0) == 0)
    def _():
        s = _silu(c_ref[...])
        for r in range(2):
            sb_ref[r] = jnp.broadcast_to(s[:, r:r + 1], (D_MODEL, LANES))

    nt = MOD_BN // LANES

    def body(kk, accs):
        r0 = pl.multiple_of(kk * SUBLANES, SUBLANES)
        s0 = sb_ref[0, pl.ds(r0, SUBLANES), :]
        s1 = sb_ref[1, pl.ds(r0, SUBLANES), :]
        new = []
        for t in range(nt):
            w8 = w_ref[pl.ds(r0, SUBLANES), t * LANES:(t + 1) * LANES]
            new.append(accs[2 * t] + w8 * s0)
            new.append(accs[2 * t + 1] + w8 * s1)
        return tuple(new)

    zero = jnp.zeros((SUBLANES, LANES), F32)
    accs = lax.fori_loop(0, D_MODEL // SUBLANES, body, (zero,) * (2 * nt), unroll=8)
    o_ref[...] = jnp.zeros(o_ref.shape, F32)
    for t in range(nt):
        b = b_ref[:, t * LANES:(t + 1) * LANES]
        for r in range(2):
            o_ref[r:r + 1, t * LANES:(t + 1) * LANES] = (
                jnp.sum(accs[2 * t + r], axis=0, keepdims=True) + b)


def _mod_call(c_t, w_mod, b_mod):
    n = w_mod.shape[1]
    return pl.pallas_call(
        _mod_kernel,
        grid=(n // MOD_BN,),
        in_specs=[
            pl.BlockSpec((D_MODEL, SUBLANES), lambda j: (0, 0)),
            pl.BlockSpec((D_MODEL, MOD_BN), lambda j: (0, j)),
            pl.BlockSpec((1, MOD_BN), lambda j: (0, j)),
        ],
        out_specs=pl.BlockSpec((SUBLANES, MOD_BN), lambda j: (0, j)),
        out_shape=jax.ShapeDtypeStruct((SUBLANES, n), F32),
        scratch_shapes=[pltpu.VMEM((2, D_MODEL, LANES), F32)],
        compiler_params=_params(1, 2 * D_MODEL * MOD_BN * 4 + 8 * 1024 * 1024),
        name="mod",
    )(c_t, w_mod, b_mod)


def _in_proj_kernel(x_ref, g_ref, sh_ref, sc_ref, w_ref, o_ref, h_ref, *, row, bm):
    @pl.when(pl.program_id(1) == 0)
    def _():
        _norm_mod_rows(x_ref, g_ref[...], 1.0 + sc_ref[row:row + 1, :],
                       sh_ref[row:row + 1, :], h_ref, bm)

    o_ref[...] = jnp.dot(h_ref[...], w_ref[...], preferred_element_type=F32)


def _in_proj_call(x, g, mod, w, *, row, bm):
    m = x.shape[0]
    kern = functools.partial(_in_proj_kernel, row=row, bm=bm)
    vmem = (2 * bm * D_MODEL * 4 + bm * D_MODEL * 2 + 2 * D_MODEL * P_BN * 2
            + 2 * bm * P_BN * 4 + 8 * NORM_ROWS * D_MODEL * 4 + 4 * 1024 * 1024)
    return pl.pallas_call(
        kern,
        grid=(m // bm, P_COLS // P_BN),
        in_specs=[
            pl.BlockSpec((bm, D_MODEL), lambda i, j: (i, 0)),
            pl.BlockSpec((1, D_MODEL), lambda i, j: (0, 0)),
            pl.BlockSpec((SUBLANES, D_MODEL), lambda i, j: (0, 0)),
            pl.BlockSpec((SUBLANES, D_MODEL), lambda i, j: (0, 1)),
            pl.BlockSpec((D_MODEL, P_BN), lambda i, j: (0, j)),
        ],
        out_specs=pl.BlockSpec((bm, P_BN), lambda i, j: (i, j)),
        out_shape=jax.ShapeDtypeStruct((m, P_COLS), F32),
        scratch_shapes=[pltpu.VMEM((bm, D_MODEL), BF16)],
        compiler_params=_params(2, vmem),
        name="in_proj",
    )(x, g, mod, mod, w)


def _gla_direction(q_ref, k_ref, v_ref, l_ref, wg_ref, bg_ref, s_ref, o_ref, *, d, reverse, tb):
    z = jnp.dot(l_ref[...].astype(BF16), wg_ref[...], preferred_element_type=F32) + bg_ref[...]
    log_a = (jnp.minimum(z, 0.0) - jnp.log1p(jnp.exp(-jnp.abs(z)))) * (1.0 / GATE_NORM)

    r = lax.broadcasted_iota(jnp.int32, (tb, tb), 0)
    cc = lax.broadcasted_iota(jnp.int32, (tb, tb), 1)
    same_chunk = (r // CHUNK) == (cc // CHUNK)
    tri = jnp.where(same_chunk & ((cc >= r) if reverse else (cc <= r)), 1.0, 0.0).astype(BF16)
    hi = log_a.astype(BF16)
    rem = log_a - hi.astype(F32)
    mid = rem.astype(BF16)
    lo = (rem - mid.astype(F32)).astype(BF16)
    cum = (jnp.dot(tri, hi, preferred_element_type=F32)
           + jnp.dot(tri, mid, preferred_element_type=F32)
           + jnp.dot(tri, lo, preferred_element_type=F32))

    ii = lax.broadcasted_iota(jnp.int32, (CHUNK, CHUNK), 0)
    jj = lax.broadcasted_iota(jnp.int32, (CHUNK, CHUNK), 1)
    mask = (jj >= ii) if reverse else (jj <= ii)
    nc = tb // CHUNK
    for c in (range(nc - 1, -1, -1) if reverse else range(nc)):
        r0 = c * CHUNK
        cum_c = cum[r0:r0 + CHUNK, :]
        last = cum_c[0:1, :] if reverse else cum_c[CHUNK - 1:CHUNK, :]
        q_c = q_ref[r0:r0 + CHUNK, :]
        k_c = k_ref[r0:r0 + CHUNK, :]
        q_e = ((q_c * (GLA_DK ** -0.5)) * jnp.exp(cum_c)).astype(BF16)
        k_e = (k_c * jnp.exp(-cum_c)).astype(BF16)
        k_d = (k_c * jnp.exp(last - cum_c)).astype(BF16)
        decay = jnp.exp(last)
        for h in range(GLA_HEADS):
            ks = slice(h * GLA_DK, (h + 1) * GLA_DK)
            vs = slice(h * GLA_DV, (h + 1) * GLA_DV)
            v_h = v_ref[r0:r0 + CHUNK, vs].astype(BF16)
            scores = lax.dot_general(q_e[:, ks], k_e[:, ks], (((1,), (1,)), ((), ())),
                                     preferred_element_type=F32)
            scores = jnp.where(mask, scores, 0.0).astype(BF16)
            s_old = s_ref[d, h]
            o_ref[r0:r0 + CHUNK, vs] = (
                jnp.dot(scores, v_h, preferred_element_type=F32)
                + jnp.dot(q_e[:, ks], s_old.astype(BF16), preferred_element_type=F32))
            ds = lax.dot_general(k_d[:, ks], v_h, (((0,), (0,)), ((), ())),
                                 preferred_element_type=F32)
            dcol = jnp.transpose(jnp.broadcast_to(decay[:, ks], (SUBLANES, GLA_DK)))[:, 0:1]
            s_ref[d, h] = dcol * s_old + ds


def _gla_kernel(qf, kf, vf, lf, qb, kb, vb, lb, wgf, bgf, wgb, bgb, s0f, s0b,
                of, ob, sf, sb, s_ref, *, tb):
    i = pl.program_id(0)

    @pl.when(i == 0)
    def _():
        s_ref[0] = s0f[...]
        s_ref[1] = s0b[...]

    _gla_direction(qf, kf, vf, lf, wgf, bgf, s_ref, of, d=0, reverse=False, tb=tb)
    _gla_direction(qb, kb, vb, lb, wgb, bgb, s_ref, ob, d=1, reverse=True, tb=tb)

    @pl.when(i == pl.num_programs(0) - 1)
    def _():
        sf[...] = s_ref[0]
        sb[...] = s_ref[1]


def _gla_call(p, wgf, bgf, wgb, bgb, s0f, s0b, *, tb):
    n = p.shape[0]
    nb = n // tb
    fwd = lambda i: i
    bwd = lambda i: nb - 1 - i

    def specs(pos):
        return [
            pl.BlockSpec((tb, GLA_KW), lambda i: (pos(i), 0)),
            pl.BlockSpec((tb, GLA_KW), lambda i: (pos(i), 1)),
            pl.BlockSpec((tb, GLA_VW), lambda i: (pos(i), 1)),
            pl.BlockSpec((tb, LANES), lambda i: (pos(i), P_MAIN // LANES)),
        ]

    const2 = lambda i: (0, 0)
    const3 = lambda i: (0, 0, 0)
    state_shape = (GLA_HEADS, GLA_DK, GLA_DV)
    state_spec = pl.BlockSpec(state_shape, const3)
    kern = functools.partial(_gla_kernel, tb=tb)
    vmem = (2 * 2 * tb * (2 * GLA_KW + GLA_VW + LANES) * 4 + 2 * 2 * tb * GLA_VW * 4
            + 10 * GLA_HEADS * GLA_DK * GLA_DV * 4 + 24 * tb * GLA_KW * 4 + 8 * 1024 * 1024)
    return pl.pallas_call(
        kern,
        grid=(nb,),
        in_specs=specs(fwd) + specs(bwd) + [
            pl.BlockSpec((LANES, GLA_KW), const2), pl.BlockSpec((1, GLA_KW), const2),
            pl.BlockSpec((LANES, GLA_KW), const2), pl.BlockSpec((1, GLA_KW), const2),
            state_spec, state_spec,
        ],
        out_specs=[
            pl.BlockSpec((tb, GLA_VW), lambda i: (i, 0)),
            pl.BlockSpec((tb, GLA_VW), lambda i: (nb - 1 - i, 0)),
            state_spec, state_spec,
        ],
        out_shape=[
            jax.ShapeDtypeStruct((n, GLA_VW), F32), jax.ShapeDtypeStruct((n, GLA_VW), F32),
            jax.ShapeDtypeStruct(state_shape, F32), jax.ShapeDtypeStruct(state_shape, F32),
        ],
        scratch_shapes=[pltpu.VMEM((2,) + state_shape, F32)],
        compiler_params=_params(1, vmem),
        name="gla",
    )(p, p, p, p, p, p, p, p, wgf, bgf, wgb, bgb, s0f, s0b)


def _mix_kernel(of_ref, ob_ref, g_ref, cb_ref, cc_ref, cx_ref, x_ref, gt_ref, gg_ref, cw_ref,
                wo_ref, o_ref, m_ref, *, bm):
    gg = gg_ref[...]
    for h in range(GLA_HEADS):
        vs = slice(h * GLA_DV, (h + 1) * GLA_DV)
        o = of_ref[:, vs] + ob_ref[:, vs]
        o32 = o * lax.rsqrt(jnp.mean(o * o, axis=-1, keepdims=True) + EPS)
        m_ref[:, vs] = ((o32 * gg) * _silu(g_ref[:, vs])).astype(BF16)

    u = cc_ref[...] * cx_ref[...]
    col = lax.broadcasted_iota(jnp.int32, u.shape, 0) % GRID_W
    u_prev = jnp.where(col == 0, 0.0, pltpu.roll(u, 1, 0))
    u_next = jnp.where(col == GRID_W - 1, 0.0, pltpu.roll(u, bm - 1, 0))
    conv = cw_ref[0:1, :] * u_prev + cw_ref[1:2, :] * u + cw_ref[2:3, :] * u_next
    m_ref[:, GLA_VW:] = (cb_ref[...] * conv).astype(BF16)

    mixed = jnp.dot(m_ref[...], wo_ref[...], preferred_element_type=F32)
    o_ref[...] = x_ref[...] + gt_ref[0:1, :] * mixed


def _mix_call(o_f, o_b, p, x, mod, gla_g, conv_w, w_out, *, bm):
    n = x.shape[0]
    kern = functools.partial(_mix_kernel, bm=bm)
    pcol = lambda k: pl.BlockSpec((bm, GLA_VW), lambda i, k=k: (i, k))
    const2 = lambda i: (0, 0)
    vmem = (2 * 6 * bm * GLA_VW * 4 + 2 * 2 * bm * D_MODEL * 4 + 2 * D_MODEL * D_MODEL * 2
            + bm * D_MODEL * 2 + 8 * bm * GLA_VW * 4 + 4 * 1024 * 1024)
    return pl.pallas_call(
        kern,
        grid=(n // bm,),
        in_specs=[
            pl.BlockSpec((bm, GLA_VW), lambda i: (i, 0)),
            pl.BlockSpec((bm, GLA_VW), lambda i: (i, 0)),
            pcol(2), pcol(3), pcol(4), pcol(5),
            pl.BlockSpec((bm, D_MODEL), lambda i: (i, 0)),
            pl.BlockSpec((SUBLANES, D_MODEL), lambda i: (0, 2)),
            pl.BlockSpec((1, GLA_DV), const2),
            pl.BlockSpec((3, CONV_W), const2),
            pl.BlockSpec((D_MODEL, D_MODEL), const2),
        ],
        out_specs=pl.BlockSpec((bm, D_MODEL), lambda i: (i, 0)),
        out_shape=jax.ShapeDtypeStruct((n, D_MODEL), F32),
        scratch_shapes=[pltpu.VMEM((bm, D_MODEL), BF16)],
        compiler_params=_params(1, vmem),
        name="mix",
    )(o_f, o_b, p, p, p, p, x, mod, gla_g, conv_w, w_out)


def _ffn_kernel(x_ref, g_ref, sh_ref, sc_ref, gt_ref, wg_ref, wu_ref, wd_ref, fg_ref,
                o_ref, h_ref, acc_ref, *, bm):
    j = pl.program_id(1)

    @pl.when(j == 0)
    def _():
        _norm_mod_rows(x_ref, g_ref[...], 1.0 + sc_ref[0:1, :], sh_ref[0:1, :], h_ref, bm)

    h = h_ref[...]
    gate = jnp.dot(h, wg_ref[...], preferred_element_type=F32)
    up = jnp.dot(h, wu_ref[...], preferred_element_type=F32)
    part = jnp.dot((_silu(gate) * up).astype(BF16), wd_ref[...], preferred_element_type=F32)

    @pl.when(j == 0)
    def _():
        acc_ref[...] = part

    @pl.when(j > 0)
    def _():
        acc_ref[...] += part

    @pl.when(j == pl.num_programs(1) - 1)
    def _():
        fg = fg_ref[...]
        gt = gt_ref[0:1, :]
        for r0 in range(0, bm, NORM_ROWS):
            rows = slice(r0, r0 + NORM_ROWS)
            xn = x_ref[rows, :] + gt * acc_ref[rows, :]
            ms = jnp.mean(xn * xn, axis=-1, keepdims=True)
            o_ref[rows, :] = (xn * lax.rsqrt(ms + EPS)) * fg


def _ffn_call(x, g, mod, wg, wu, wd, fg, *, bm):
    n = x.shape[0]
    kern = functools.partial(_ffn_kernel, bm=bm)
    modspec = lambda k: pl.BlockSpec((SUBLANES, D_MODEL), lambda i, j, k=k: (0, k))
    vmem = (2 * 2 * bm * D_MODEL * 4 + bm * D_MODEL * (2 + 4) + 2 * 3 * D_MODEL * FFN_BH * 2
            + 6 * bm * FFN_BH * 4 + 2 * bm * D_MODEL * 4 + 4 * 1024 * 1024)
    return pl.pallas_call(
        kern,
        grid=(n // bm, FFN_HIDDEN // FFN_BH),
        in_specs=[
            pl.BlockSpec((bm, D_MODEL), lambda i, j: (i, 0)),
            pl.BlockSpec((1, D_MODEL), lambda i, j: (0, 0)),
            modspec(3), modspec(4), modspec(5),
            pl.BlockSpec((D_MODEL, FFN_BH), lambda i, j: (0, j)),
            pl.BlockSpec((D_MODEL, FFN_BH), lambda i, j: (0, j)),
            pl.BlockSpec((FFN_BH, D_MODEL), lambda i, j: (j, 0)),
            pl.BlockSpec((1, D_MODEL), lambda i, j: (0, 0)),
        ],
        out_specs=pl.BlockSpec((bm, D_MODEL), lambda i, j: (i, 0)),
        out_shape=jax.ShapeDtypeStruct((n, D_MODEL), F32),
        scratch_shapes=[pltpu.VMEM((bm, D_MODEL), BF16), pltpu.VMEM((bm, D_MODEL), F32)],
        compiler_params=_params(2, vmem),
        name="ffn",
    )(x, g, mod, mod, mod, wg, wu, wd, fg)


def kernel(x, c, ctx, c_ctx, w_mod, b_mod, norm1_g, norm2_g, w_in, w_gate_f, b_gate_f,
           w_gate_b, b_gate_b, gla_norm_g, conv_w, w_out, w_ffn_gate, w_ffn_up, w_ffn_down,
           final_g):
    assert x.shape[0] == 1 and w_mod.shape[0] == 1, "batch 1, depth 1 only"
    seq, ctx_len = x.shape[1], ctx.shape[1]
    x2, ctx2 = x[0], ctx[0]

    c_t = jnp.concatenate(
        [c[0][:, None], c_ctx[:, None], jnp.zeros((D_MODEL, SUBLANES - 2), F32)], axis=1)
    mod = _mod_call(c_t, w_mod[0], b_mod)

    gate_lo = 2 * GLA_KW + 2 * GLA_VW
    gate_hi = gate_lo + 2 * GATE_RANK
    w_in_r = jnp.concatenate(
        [w_in[0][:, :gate_lo], w_in[0][:, gate_hi:], w_in[0][:, gate_lo:gate_hi],
         jnp.zeros((D_MODEL, P_GATE - 2 * GATE_RANK), F32)], axis=1).astype(BF16)
    pad_f = jnp.zeros((LANES, GLA_KW), F32).at[:GATE_RANK].set(w_gate_f[0]).astype(BF16)
    pad_b = jnp.zeros((LANES, GLA_KW), F32).at[GATE_RANK:2 * GATE_RANK].set(w_gate_b[0]).astype(BF16)
    g1 = norm1_g[0][None]

    p_c = _in_proj_call(ctx2, g1, mod, w_in_r, row=1, bm=ctx_len)
    zero_state = jnp.zeros((GLA_HEADS, GLA_DK, GLA_DV), F32)
    _, _, s_f, s_b = _gla_call(p_c, pad_f, b_gate_f, pad_b, b_gate_b, zero_state, zero_state,
                               tb=ctx_len)

    p_x = _in_proj_call(x2, g1, mod, w_in_r, row=0, bm=512)
    o_f, o_b, _, _ = _gla_call(p_x, pad_f, b_gate_f, pad_b, b_gate_b, s_f, s_b, tb=256)
    x_mid = _mix_call(o_f, o_b, p_x, x2, mod, gla_norm_g, conv_w[0], w_out[0].astype(BF16),
                      bm=256)
    out = _ffn_call(x_mid, norm2_g[0][None], mod, w_ffn_gate[0].astype(BF16),
                    w_ffn_up[0].astype(BF16), w_ffn_down[0].astype(BF16), final_g[None], bm=512)
    return out[None]
```

```python
import functools

import jax
import jax.numpy as jnp
from jax import lax
from jax.experimental import pallas as pl
from jax.experimental.pallas import tpu as pltpu

F32 = jnp.float32
BF16 = jnp.bfloat16

D_MODEL = 2048
GRID_W = 64
GLA_HEADS = 4
GLA_DK = 128
GLA_DV = 256
GLA_KW = GLA_HEADS * GLA_DK
GLA_VW = GLA_HEADS * GLA_DV
GATE_RANK = 16
GATE_NORM = 16.0
CHUNK = 64
CONV_W = D_MODEL - GLA_VW
FFN_HIDDEN = 5632
EPS = 1e-6

LANES = 128
SUBLANES = 8
VMEM_BYTES = 64 * 1024 * 1024

P_MAIN = 2 * GLA_KW + 2 * GLA_VW + 3 * CONV_W
P_GATE = 256
P_COLS = P_MAIN + P_GATE
P_BN = 1280

MOD_BN = 1024
FFN_BH = 512
NORM_ROWS = 128


def _params(grid_rank, vmem_bytes):
    return pltpu.CompilerParams(
        dimension_semantics=("arbitrary",) * grid_rank,
        vmem_limit_bytes=min(int(vmem_bytes), VMEM_BYTES - 6 * 1024 * 1024),
    )


def _silu(v):
    return v * jax.nn.sigmoid(v)


def _norm_mod_rows(x_ref, g, mul, sh, dst_ref, rows):
    for r0 in range(0, rows, NORM_ROWS):
        xs = x_ref[r0:r0 + NORM_ROWS, :]
        ms = jnp.mean(xs * xs, axis=-1, keepdims=True)
        y = xs * lax.rsqrt(ms + EPS)
        dst_ref[r0:r0 + NORM_ROWS, :] = ((y * g) * mul + sh).astype(BF16)


def _mod_kernel(c_ref, w_ref, b_ref, o_ref, sb_ref):
    @pl.when(pl.program_id(0) == 0)
    def _():
        s = _silu(c_ref[...])
        for r in range(2):
            sb_ref[r] = jnp.broadcast_to(s[:, r:r + 1], (D_MODEL, LANES))

    nt = MOD_BN // LANES

    def body(kk, accs):
        r0 = pl.multiple_of(kk * SUBLANES, SUBLANES)
        s0 = sb_ref[0, pl.ds(r0, SUBLANES), :]
        s1 = sb_ref[1, pl.ds(r0, SUBLANES), :]
        new = []
        for t in range(nt):
            w8 = w_ref[pl.ds(r0, SUBLANES), t * LANES:(t + 1) * LANES]
            new.append(accs[2 * t] + w8 * s0)
            new.append(accs[2 * t + 1] + w8 * s1)
        return tuple(new)

    zero = jnp.zeros((SUBLANES, LANES), F32)
    accs = lax.fori_loop(0, D_MODEL // SUBLANES, body, (zero,) * (2 * nt), unroll=8)
    o_ref[...] = jnp.zeros(o_ref.shape, F32)
    for t in range(nt):
        b = b_ref[:, t * LANES:(t + 1) * LANES]
        for r in range(2):
            o_ref[r:r + 1, t * LANES:(t + 1) * LANES] = (
                jnp.sum(accs[2 * t + r], axis=0, keepdims=True) + b)


def _mod_call(c_t, w_mod, b_mod):
    n = w_mod.shape[1]
    return pl.pallas_call(
        _mod_kernel,
        grid=(n // MOD_BN,),
        in_specs=[
            pl.BlockSpec((D_MODEL, SUBLANES), lambda j: (0, 0)),
            pl.BlockSpec((D_MODEL, MOD_BN), lambda j: (0, j)),
            pl.BlockSpec((1, MOD_BN), lambda j: (0, j)),
        ],
        out_specs=pl.BlockSpec((SUBLANES, MOD_BN), lambda j: (0, j)),
        out_shape=jax.ShapeDtypeStruct((SUBLANES, n), F32),
        scratch_shapes=[pltpu.VMEM((2, D_MODEL, LANES), F32)],
        compiler_params=_params(1, 2 * D_MODEL * MOD_BN * 4 + 8 * 1024 * 1024),
        name="mod",
    )(c_t, w_mod, b_mod)


def _in_proj_kernel(x_ref, g_ref, sh_ref, sc_ref, w_ref, o_ref, h_ref, *, row, bm):
    @pl.when(pl.program_id(1) == 0)
    def _():
        _norm_mod_rows(x_ref, g_ref[...], 1.0 + sc_ref[row:row + 1, :],
                       sh_ref[row:row + 1, :], h_ref, bm)

    o_ref[...] = jnp.dot(h_ref[...], w_ref[...], preferred_element_type=F32).astype(BF16)


def _in_proj_call(x, g, mod, w, *, row, bm):
    m = x.shape[0]
    kern = functools.partial(_in_proj_kernel, row=row, bm=bm)
    vmem = (2 * bm * D_MODEL * 4 + bm * D_MODEL * 2 + 2 * D_MODEL * P_BN * 2
            + 3 * bm * P_BN * 4 + 8 * NORM_ROWS * D_MODEL * 4 + 4 * 1024 * 1024)
    return pl.pallas_call(
        kern,
        grid=(m // bm, P_COLS // P_BN),
        in_specs=[
            pl.BlockSpec((bm, D_MODEL), lambda i, j: (i, 0)),
            pl.BlockSpec((1, D_MODEL), lambda i, j: (0, 0)),
            pl.BlockSpec((SUBLANES, D_MODEL), lambda i, j: (0, 0)),
            pl.BlockSpec((SUBLANES, D_MODEL), lambda i, j: (0, 1)),
            pl.BlockSpec((D_MODEL, P_BN), lambda i, j: (0, j)),
        ],
        out_specs=pl.BlockSpec((bm, P_BN), lambda i, j: (i, j)),
        out_shape=jax.ShapeDtypeStruct((m, P_COLS), BF16),
        scratch_shapes=[pltpu.VMEM((bm, D_MODEL), BF16)],
        compiler_params=_params(2, vmem),
        name="in_proj",
    )(x, g, mod, mod, w)


def _gla_prepare(l_ref, wg_ref, bg_ref, *, reverse, tb):
    z = jnp.dot(l_ref[...], wg_ref[...], preferred_element_type=F32) + bg_ref[...]
    log_a = (jnp.minimum(z, 0.0) - jnp.log1p(jnp.exp(-jnp.abs(z)))) * (1.0 / GATE_NORM)

    r = lax.broadcasted_iota(jnp.int32, (tb, tb), 0)
    cc = lax.broadcasted_iota(jnp.int32, (tb, tb), 1)
    same_chunk = (r // CHUNK) == (cc // CHUNK)
    tri = jnp.where(same_chunk & ((cc >= r) if reverse else (cc <= r)), 1.0, 0.0).astype(BF16)
    hi = log_a.astype(BF16)
    rem = log_a - hi.astype(F32)
    mid = rem.astype(BF16)
    lo = (rem - mid.astype(F32)).astype(BF16)
    return (jnp.dot(tri, hi, preferred_element_type=F32)
            + jnp.dot(tri, mid, preferred_element_type=F32)
            + jnp.dot(tri, lo, preferred_element_type=F32))


def _gla_kernel(qf, kf, vf, lf, qb, kb, vb, lb, wgf, bgf, wgb, bgb, s0f, s0b,
                of, ob, sf, sb, s_ref, *, tb):
    i = pl.program_id(0)

    @pl.when(i == 0)
    def _():
        s_ref[0] = s0f[...]
        s_ref[1] = s0b[...]

    dirs = (
        (qf, kf, vf, of, _gla_prepare(lf, wgf, bgf, reverse=False, tb=tb), False),
        (qb, kb, vb, ob, _gla_prepare(lb, wgb, bgb, reverse=True, tb=tb), True),
    )
    ii = lax.broadcasted_iota(jnp.int32, (CHUNK, CHUNK), 0)
    jj = lax.broadcasted_iota(jnp.int32, (CHUNK, CHUNK), 1)
    nc = tb // CHUNK
    for step in range(nc):
        items = []
        for d, (q_ref, k_ref, v_ref, o_ref, cum, reverse) in enumerate(dirs):
            r0 = (nc - 1 - step if reverse else step) * CHUNK
            cum_c = cum[r0:r0 + CHUNK, :]
            last = cum_c[0:1, :] if reverse else cum_c[CHUNK - 1:CHUNK, :]
            q_c = q_ref[r0:r0 + CHUNK, :].astype(F32)
            k_c = k_ref[r0:r0 + CHUNK, :].astype(F32)
            q_e = ((q_c * (GLA_DK ** -0.5)) * jnp.exp(cum_c)).astype(BF16)
            k_e = (k_c * jnp.exp(-cum_c)).astype(BF16)
            k_d = (k_c * jnp.exp(last - cum_c)).astype(BF16)
            decay = jnp.exp(last)
            mask = (jj >= ii) if reverse else (jj <= ii)
            for h in range(GLA_HEADS):
                ks = slice(h * GLA_DK, (h + 1) * GLA_DK)
                vs = slice(h * GLA_DV, (h + 1) * GLA_DV)
                items.append(dict(
                    d=d, h=h, r0=r0, vs=vs, o_ref=o_ref, mask=mask,
                    q_e=q_e[:, ks], k_e=k_e[:, ks], k_d=k_d[:, ks], decay=decay[:, ks],
                    v=v_ref[r0:r0 + CHUNK, vs]))
        for it in items:
            it["scores"] = lax.dot_general(it["q_e"], it["k_e"], (((1,), (1,)), ((), ())),
                                           preferred_element_type=F32)
        for it in items:
            it["ds"] = lax.dot_general(it["k_d"], it["v"], (((0,), (0,)), ((), ())),
                                       preferred_element_type=F32)
        for it in items:
            scores = jnp.where(it["mask"], it["scores"], 0.0).astype(BF16)
            s_old = s_ref[it["d"], it["h"]]
            it["o_ref"][it["r0"]:it["r0"] + CHUNK, it["vs"]] = (
                jnp.dot(scores, it["v"], preferred_element_type=F32)
                + jnp.dot(it["q_e"], s_old.astype(BF16), preferred_element_type=F32))
            dcol = jnp.transpose(jnp.broadcast_to(it["decay"], (SUBLANES, GLA_DK)))[:, 0:1]
            s_ref[it["d"], it["h"]] = dcol * s_old + it["ds"]

    @pl.when(i == pl.num_programs(0) - 1)
    def _():
        sf[...] = s_ref[0]
        sb[...] = s_ref[1]


def _gla_call(p, wgf, bgf, wgb, bgb, s0f, s0b, *, tb):
    n = p.shape[0]
    nb = n // tb
    fwd = lambda i: i
    bwd = lambda i: nb - 1 - i

    def specs(pos):
        return [
            pl.BlockSpec((tb, GLA_KW), lambda i: (pos(i), 0)),
            pl.BlockSpec((tb, GLA_KW), lambda i: (pos(i), 1)),
            pl.BlockSpec((tb, GLA_VW), lambda i: (pos(i), 1)),
            pl.BlockSpec((tb, LANES), lambda i: (pos(i), P_MAIN // LANES)),
        ]

    const2 = lambda i: (0, 0)
    const3 = lambda i: (0, 0, 0)
    state_shape = (GLA_HEADS, GLA_DK, GLA_DV)
    state_spec = pl.BlockSpec(state_shape, const3)
    kern = functools.partial(_gla_kernel, tb=tb)
    vmem = (2 * 2 * tb * (2 * GLA_KW + GLA_VW + LANES) * 2 + 2 * 2 * tb * GLA_VW * 4
            + 10 * GLA_HEADS * GLA_DK * GLA_DV * 4 + 24 * tb * GLA_KW * 4 + 8 * 1024 * 1024)
    return pl.pallas_call(
        kern,
        grid=(nb,),
        in_specs=specs(fwd) + specs(bwd) + [
            pl.BlockSpec((LANES, GLA_KW), const2), pl.BlockSpec((1, GLA_KW), const2),
            pl.BlockSpec((LANES, GLA_KW), const2), pl.BlockSpec((1, GLA_KW), const2),
            state_spec, state_spec,
        ],
        out_specs=[
            pl.BlockSpec((tb, GLA_VW), lambda i: (i, 0)),
            pl.BlockSpec((tb, GLA_VW), lambda i: (nb - 1 - i, 0)),
            state_spec, state_spec,
        ],
        out_shape=[
            jax.ShapeDtypeStruct((n, GLA_VW), F32), jax.ShapeDtypeStruct((n, GLA_VW), F32),
            jax.ShapeDtypeStruct(state_shape, F32), jax.ShapeDtypeStruct(state_shape, F32),
        ],
        scratch_shapes=[pltpu.VMEM((2,) + state_shape, F32)],
        compiler_params=_params(1, vmem),
        name="gla",
    )(p, p, p, p, p, p, p, p, wgf, bgf, wgb, bgb, s0f, s0b)


def _mix_kernel(of_ref, ob_ref, g_ref, cb_ref, cc_ref, cx_ref, x_ref, gt_ref, gg_ref, cw_ref,
                wo_ref, o_ref, m_ref, *, bm):
    gg = gg_ref[...]
    for h in range(GLA_HEADS):
        vs = slice(h * GLA_DV, (h + 1) * GLA_DV)
        o = of_ref[:, vs] + ob_ref[:, vs]
        o32 = o * lax.rsqrt(jnp.mean(o * o, axis=-1, keepdims=True) + EPS)
        m_ref[:, vs] = ((o32 * gg) * _silu(g_ref[:, vs].astype(F32))).astype(BF16)

    u = cc_ref[...].astype(F32) * cx_ref[...].astype(F32)
    col = lax.broadcasted_iota(jnp.int32, u.shape, 0) % GRID_W
    u_prev = jnp.where(col == 0, 0.0, pltpu.roll(u, 1, 0))
    u_next = jnp.where(col == GRID_W - 1, 0.0, pltpu.roll(u, bm - 1, 0))
    conv = cw_ref[0:1, :] * u_prev + cw_ref[1:2, :] * u + cw_ref[2:3, :] * u_next
    m_ref[:, GLA_VW:] = (cb_ref[...].astype(F32) * conv).astype(BF16)

    mixed = jnp.dot(m_ref[...], wo_ref[...], preferred_element_type=F32)
    o_ref[...] = x_ref[...] + gt_ref[0:1, :] * mixed


def _mix_call(o_f, o_b, p, x, mod, gla_g, conv_w, w_out, *, bm):
    n = x.shape[0]
    kern = functools.partial(_mix_kernel, bm=bm)
    pcol = lambda k: pl.BlockSpec((bm, GLA_VW), lambda i, k=k: (i, k))
    const2 = lambda i: (0, 0)
    vmem = (2 * 2 * bm * GLA_VW * 4 + 2 * 4 * bm * GLA_VW * 2 + 2 * 2 * bm * D_MODEL * 4
            + 2 * D_MODEL * D_MODEL * 2
            + bm * D_MODEL * 2 + 8 * bm * GLA_VW * 4 + 4 * 1024 * 1024)
    return pl.pallas_call(
        kern,
        grid=(n // bm,),
        in_specs=[
            pl.BlockSpec((bm, GLA_VW), lambda i: (i, 0)),
            pl.BlockSpec((bm, GLA_VW), lambda i: (i, 0)),
            pcol(2), pcol(3), pcol(4), pcol(5),
            pl.BlockSpec((bm, D_MODEL), lambda i: (i, 0)),
            pl.BlockSpec((SUBLANES, D_MODEL), lambda i: (0, 2)),
            pl.BlockSpec((1, GLA_DV), const2),
            pl.BlockSpec((3, CONV_W), const2),
            pl.BlockSpec((D_MODEL, D_MODEL), const2),
        ],
        out_specs=pl.BlockSpec((bm, D_MODEL), lambda i: (i, 0)),
        out_shape=jax.ShapeDtypeStruct((n, D_MODEL), F32),
        scratch_shapes=[pltpu.VMEM((bm, D_MODEL), BF16)],
        compiler_params=_params(1, vmem),
        name="mix",
    )(o_f, o_b, p, p, p, p, x, mod, gla_g, conv_w, w_out)


def _ffn_kernel(x_ref, g_ref, sh_ref, sc_ref, gt_ref, wg_ref, wu_ref, wd_ref, fg_ref,
                o_ref, h_ref, *, bm):
    j = pl.program_id(1)

    @pl.when(j == 0)
    def _():
        _norm_mod_rows(x_ref, g_ref[...], 1.0 + sc_ref[0:1, :], sh_ref[0:1, :], h_ref, bm)
        o_ref[...] = x_ref[...]

    h = h_ref[...]
    gate = jnp.dot(h, wg_ref[...], preferred_element_type=F32)
    up = jnp.dot(h, wu_ref[...], preferred_element_type=F32)
    part = jnp.dot((_silu(gate) * up).astype(BF16), wd_ref[...], preferred_element_type=F32)
    o_ref[...] += gt_ref[0:1, :] * part

    @pl.when(j == pl.num_programs(1) - 1)
    def _():
        fg = fg_ref[...]
        for r0 in range(0, bm, NORM_ROWS):
            rows = slice(r0, r0 + NORM_ROWS)
            xn = o_ref[rows, :]
            ms = jnp.mean(xn * xn, axis=-1, keepdims=True)
            o_ref[rows, :] = (xn * lax.rsqrt(ms + EPS)) * fg


def _ffn_call(x, g, mod, wg, wu, wd, fg, *, bm):
    n = x.shape[0]
    kern = functools.partial(_ffn_kernel, bm=bm)
    modspec = lambda k: pl.BlockSpec((SUBLANES, D_MODEL), lambda i, j, k=k: (0, k))
    vmem = (3 * bm * D_MODEL * 4 + bm * D_MODEL * 2 + 2 * 3 * D_MODEL * FFN_BH * 2
            + 4 * bm * FFN_BH * 4 + bm * D_MODEL * 4 + 4 * 1024 * 1024)
    return pl.pallas_call(
        kern,
        grid=(n // bm, FFN_HIDDEN // FFN_BH),
        in_specs=[
            pl.BlockSpec((bm, D_MODEL), lambda i, j: (i, 0), pipeline_mode=pl.Buffered(1)),
            pl.BlockSpec((1, D_MODEL), lambda i, j: (0, 0)),
            modspec(3), modspec(4), modspec(5),
            pl.BlockSpec((D_MODEL, FFN_BH), lambda i, j: (0, j)),
            pl.BlockSpec((D_MODEL, FFN_BH), lambda i, j: (0, j)),
            pl.BlockSpec((FFN_BH, D_MODEL), lambda i, j: (j, 0)),
            pl.BlockSpec((1, D_MODEL), lambda i, j: (0, 0)),
        ],
        out_specs=pl.BlockSpec((bm, D_MODEL), lambda i, j: (i, 0)),
        out_shape=jax.ShapeDtypeStruct((n, D_MODEL), F32),
        scratch_shapes=[pltpu.VMEM((bm, D_MODEL), BF16)],
        compiler_params=_params(2, vmem),
        name="ffn",
    )(x, g, mod, mod, mod, wg, wu, wd, fg)


def kernel(x, c, ctx, c_ctx, w_mod, b_mod, norm1_g, norm2_g, w_in, w_gate_f, b_gate_f,
           w_gate_b, b_gate_b, gla_norm_g, conv_w, w_out, w_ffn_gate, w_ffn_up, w_ffn_down,
           final_g):
    assert x.shape[0] == 1 and w_mod.shape[0] == 1, "batch 1, depth 1 only"
    seq, ctx_len = x.shape[1], ctx.shape[1]
    x2, ctx2 = x[0], ctx[0]

    c_t = jnp.concatenate(
        [c[0][:, None], c_ctx[:, None], jnp.zeros((D_MODEL, SUBLANES - 2), F32)], axis=1)
    mod = _mod_call(c_t, w_mod[0], b_mod)

    gate_lo = 2 * GLA_KW + 2 * GLA_VW
    gate_hi = gate_lo + 2 * GATE_RANK
    w_in_b = w_in[0].astype(BF16)
    w_in_r = jnp.concatenate(
        [w_in_b[:, :gate_lo], w_in_b[:, gate_hi:], w_in_b[:, gate_lo:gate_hi],
         jnp.zeros((D_MODEL, P_GATE - 2 * GATE_RANK), BF16)], axis=1)
    pad_f = jnp.zeros((LANES, GLA_KW), F32).at[:GATE_RANK].set(w_gate_f[0]).astype(BF16)
    pad_b = jnp.zeros((LANES, GLA_KW), F32).at[GATE_RANK:2 * GATE_RANK].set(w_gate_b[0]).astype(BF16)
    g1 = norm1_g[0][None]

    p_c = _in_proj_call(ctx2, g1, mod, w_in_r, row=1, bm=ctx_len)
    zero_state = jnp.zeros((GLA_HEADS, GLA_DK, GLA_DV), F32)
    _, _, s_f, s_b = _gla_call(p_c, pad_f, b_gate_f, pad_b, b_gate_b, zero_state, zero_state,
                               tb=ctx_len)

    p_x = _in_proj_call(x2, g1, mod, w_in_r, row=0, bm=1024)
    o_f, o_b, _, _ = _gla_call(p_x, pad_f, b_gate_f, pad_b, b_gate_b, s_f, s_b, tb=256)
    x_mid = _mix_call(o_f, o_b, p_x, x2, mod, gla_norm_g, conv_w[0], w_out[0].astype(BF16),
                      bm=256)
    out = _ffn_call(x_mid, norm2_g[0][None], mod, w_ffn_gate[0].astype(BF16),
                    w_ffn_up[0].astype(BF16), w_ffn_down[0].astype(BF16), final_g[None], bm=1024)
    return out[None]
```

```python
import functools

import jax
import jax.numpy as jnp
from jax import lax
from jax.experimental import pallas as pl
from jax.experimental.pallas import tpu as pltpu

F32 = jnp.float32
BF16 = jnp.bfloat16

D_MODEL = 2048
GRID_W = 64
GLA_HEADS = 4
GLA_DK = 128
GLA_DV = 256
GLA_KW = GLA_HEADS * GLA_DK
GLA_VW = GLA_HEADS * GLA_DV
GATE_RANK = 16
GATE_NORM = 16.0
CHUNK = 64
CONV_W = D_MODEL - GLA_VW
FFN_HIDDEN = 5632
EPS = 1e-6

LANES = 128
SUBLANES = 8
VMEM_BYTES = 64 * 1024 * 1024

GATE_COL = 2 * GLA_KW + 2 * GLA_VW
GATE_COLS = 2 * GATE_RANK
P_COLS = GATE_COL + 3 * CONV_W
P_BN = 1024
P_SHIFT_FROM = GATE_COL // P_BN

MOD_BN = 1024
FFN_BH = 512
NORM_ROWS = 128


def _params(grid_rank, vmem_bytes):
    return pltpu.CompilerParams(
        dimension_semantics=("arbitrary",) * grid_rank,
        vmem_limit_bytes=min(int(vmem_bytes), VMEM_BYTES - 6 * 1024 * 1024),
    )


def _silu(v):
    return v * jax.nn.sigmoid(v)


def _norm_mod_rows(x_ref, g, mul, sh, dst_ref, rows):
    for r0 in range(0, rows, NORM_ROWS):
        xs = x_ref[r0:r0 + NORM_ROWS, :]
        ms = jnp.mean(xs * xs, axis=-1, keepdims=True)
        y = xs * lax.rsqrt(ms + EPS)
        dst_ref[r0:r0 + NORM_ROWS, :] = ((y * g) * mul + sh).astype(BF16)


def _mod_kernel(c_ref, w_ref, b_ref, o_ref, sb_ref):
    @pl.when(pl.program_id(0) == 0)
    def _():
        s = _silu(c_ref[...])
        for r in range(2):
            sb_ref[r] = jnp.broadcast_to(s[:, r:r + 1], (D_MODEL, LANES))

    nt = MOD_BN // LANES

    def body(kk, accs):
        r0 = pl.multiple_of(kk * SUBLANES, SUBLANES)
        s0 = sb_ref[0, pl.ds(r0, SUBLANES), :]
        s1 = sb_ref[1, pl.ds(r0, SUBLANES), :]
        new = []
        for t in range(nt):
            w8 = w_ref[pl.ds(r0, SUBLANES), t * LANES:(t + 1) * LANES]
            new.append(accs[2 * t] + w8 * s0)
            new.append(accs[2 * t + 1] + w8 * s1)
        return tuple(new)

    zero = jnp.zeros((SUBLANES, LANES), F32)
    accs = lax.fori_loop(0, D_MODEL // SUBLANES, body, (zero,) * (2 * nt), unroll=8)
    o_ref[...] = jnp.zeros(o_ref.shape, F32)
    for t in range(nt):
        b = b_ref[:, t * LANES:(t + 1) * LANES]
        for r in range(2):
            o_ref[r:r + 1, t * LANES:(t + 1) * LANES] = (
                jnp.sum(accs[2 * t + r], axis=0, keepdims=True) + b)


def _mod_call(c_t, w_mod, b_mod):
    n = w_mod.shape[1]
    return pl.pallas_call(
        _mod_kernel,
        grid=(n // MOD_BN,),
        in_specs=[
            pl.BlockSpec((D_MODEL, SUBLANES), lambda j: (0, 0)),
            pl.BlockSpec((D_MODEL, MOD_BN), lambda j: (0, j)),
            pl.BlockSpec((1, MOD_BN), lambda j: (0, j)),
        ],
        out_specs=pl.BlockSpec((SUBLANES, MOD_BN), lambda j: (0, j)),
        out_shape=jax.ShapeDtypeStruct((SUBLANES, n), F32),
        scratch_shapes=[pltpu.VMEM((2, D_MODEL, LANES), F32)],
        compiler_params=_params(1, 2 * D_MODEL * MOD_BN * 4 + 8 * 1024 * 1024),
        name="mod",
    )(c_t, w_mod, b_mod)


def _prenorm_kernel(x_ref, g_ref, sh_ref, sc_ref, wl_ref, h_ref, l_ref, *, row, bm):
    _norm_mod_rows(x_ref, g_ref[...], 1.0 + sc_ref[row:row + 1, :], sh_ref[row:row + 1, :],
                   h_ref, bm)
    l_ref[...] = jnp.dot(h_ref[...], wl_ref[...].astype(BF16),
                         preferred_element_type=F32).astype(BF16)


def _prenorm_call(x, g, mod, w_in, *, row, bm):
    m = x.shape[0]
    kern = functools.partial(_prenorm_kernel, row=row, bm=bm)
    vmem = (2 * bm * D_MODEL * (4 + 2) + 4 * D_MODEL * LANES * 4 + 8 * NORM_ROWS * D_MODEL * 4
            + 4 * 1024 * 1024)
    return pl.pallas_call(
        kern,
        grid=(m // bm,),
        in_specs=[
            pl.BlockSpec((bm, D_MODEL), lambda i: (i, 0)),
            pl.BlockSpec((1, D_MODEL), lambda i: (0, 0)),
            pl.BlockSpec((SUBLANES, D_MODEL), lambda i: (0, 0)),
            pl.BlockSpec((SUBLANES, D_MODEL), lambda i: (0, 1)),
            pl.BlockSpec((D_MODEL, LANES), lambda i: (0, GATE_COL // LANES)),
        ],
        out_specs=[pl.BlockSpec((bm, D_MODEL), lambda i: (i, 0)),
                   pl.BlockSpec((bm, LANES), lambda i: (i, 0))],
        out_shape=[jax.ShapeDtypeStruct((m, D_MODEL), BF16),
                   jax.ShapeDtypeStruct((m, LANES), BF16)],
        compiler_params=_params(1, vmem),
        name="prenorm",
    )(x, g, mod, mod, w_in)


W_ROWS = 256


def _in_proj_kernel(h_ref, wa_ref, wb_ref, o_ref, w_ref):
    j = pl.program_id(0)

    @pl.when(pl.program_id(1) == 0)
    def _():
        @pl.when(j < P_SHIFT_FROM)
        def _():
            for r0 in range(0, D_MODEL, W_ROWS):
                w_ref[r0:r0 + W_ROWS, :] = wa_ref[r0:r0 + W_ROWS, :].astype(BF16)

        @pl.when(j >= P_SHIFT_FROM)
        def _():
            for r0 in range(0, D_MODEL, W_ROWS):
                win = jnp.concatenate(
                    [wa_ref[r0:r0 + W_ROWS, :], wb_ref[r0:r0 + W_ROWS, :]], axis=1)
                win = pltpu.roll(win, P_BN + LANES - GATE_COLS, 1)
                w_ref[r0:r0 + W_ROWS, :] = win[:, :P_BN].astype(BF16)

    o_ref[...] = jnp.dot(h_ref[...], w_ref[...], preferred_element_type=F32).astype(BF16)


def _in_proj_call(h, w_in, *, bm, n_tiles):
    m = h.shape[0]
    vmem = (2 * bm * D_MODEL * 2 + 2 * D_MODEL * (P_BN + LANES) * 4 + D_MODEL * P_BN * 2
            + 2 * bm * P_BN * 2 + bm * P_BN * 4 + 6 * W_ROWS * (P_BN + LANES) * 4
            + 4 * 1024 * 1024)
    return pl.pallas_call(
        _in_proj_kernel,
        grid=(n_tiles, m // bm),
        in_specs=[
            pl.BlockSpec((bm, D_MODEL), lambda j, i: (i, 0)),
            pl.BlockSpec((D_MODEL, P_BN), lambda j, i: (0, j)),
            pl.BlockSpec((D_MODEL, LANES), lambda j, i: (0, (j + 1) * (P_BN // LANES))),
        ],
        out_specs=pl.BlockSpec((bm, P_BN), lambda j, i: (i, j)),
        out_shape=jax.ShapeDtypeStruct((m, n_tiles * P_BN), BF16),
        scratch_shapes=[pltpu.VMEM((D_MODEL, P_BN), BF16)],
        compiler_params=_params(2, vmem),
        name="in_proj",
    )(h, w_in, w_in)


def _gla_prepare(l_ref, wg_ref, bg_ref, *, reverse, tb):
    z = jnp.dot(l_ref[...], wg_ref[...], preferred_element_type=F32) + bg_ref[...]
    log_a = (jnp.minimum(z, 0.0) - jnp.log1p(jnp.exp(-jnp.abs(z)))) * (1.0 / GATE_NORM)

    r = lax.broadcasted_iota(jnp.int32, (tb, tb), 0)
    cc = lax.broadcasted_iota(jnp.int32, (tb, tb), 1)
    same_chunk = (r // CHUNK) == (cc // CHUNK)
    tri = jnp.where(same_chunk & ((cc >= r) if reverse else (cc <= r)), 1.0, 0.0).astype(BF16)
    hi = log_a.astype(BF16)
    rem = log_a - hi.astype(F32)
    mid = rem.astype(BF16)
    lo = (rem - mid.astype(F32)).astype(BF16)
    return (jnp.dot(tri, hi, preferred_element_type=F32)
            + jnp.dot(tri, mid, preferred_element_type=F32)
            + jnp.dot(tri, lo, preferred_element_type=F32))


def _gla_kernel(qf, kf, vf, lf, qb, kb, vb, lb, wgf, bgf, wgb, bgb, s0f, s0b,
                of, ob, sf, sb, s_ref, *, tb):
    i = pl.program_id(0)

    @pl.when(i == 0)
    def _():
        s_ref[0] = s0f[...]
        s_ref[1] = s0b[...]

    dirs = (
        (qf, kf, vf, of, _gla_prepare(lf, wgf, bgf, reverse=False, tb=tb), False),
        (qb, kb, vb, ob, _gla_prepare(lb, wgb, bgb, reverse=True, tb=tb), True),
    )
    ii = lax.broadcasted_iota(jnp.int32, (CHUNK, CHUNK), 0)
    jj = lax.broadcasted_iota(jnp.int32, (CHUNK, CHUNK), 1)
    nc = tb // CHUNK
    for step in range(nc):
        items = []
        for d, (q_ref, k_ref, v_ref, o_ref, cum, reverse) in enumerate(dirs):
            r0 = (nc - 1 - step if reverse else step) * CHUNK
            cum_c = cum[r0:r0 + CHUNK, :]
            last = cum_c[0:1, :] if reverse else cum_c[CHUNK - 1:CHUNK, :]
            q_c = q_ref[r0:r0 + CHUNK, :].astype(F32)
            k_c = k_ref[r0:r0 + CHUNK, :].astype(F32)
            q_e = ((q_c * (GLA_DK ** -0.5)) * jnp.exp(cum_c)).astype(BF16)
            k_e = (k_c * jnp.exp(-cum_c)).astype(BF16)
            k_d = (k_c * jnp.exp(last - cum_c)).astype(BF16)
            decay = jnp.exp(last)
            mask = (jj >= ii) if reverse else (jj <= ii)
            for h in range(GLA_HEADS):
                ks = slice(h * GLA_DK, (h + 1) * GLA_DK)
                vs = slice(h * GLA_DV, (h + 1) * GLA_DV)
                items.append(dict(
                    d=d, h=h, r0=r0, vs=vs, o_ref=o_ref, mask=mask,
                    q_e=q_e[:, ks], k_e=k_e[:, ks], k_d=k_d[:, ks], decay=decay[:, ks],
                    v=v_ref[r0:r0 + CHUNK, vs]))
        for it in items:
            it["scores"] = lax.dot_general(it["q_e"], it["k_e"], (((1,), (1,)), ((), ())),
                                           preferred_element_type=F32)
        for it in items:
            it["ds"] = lax.dot_general(it["k_d"], it["v"], (((0,), (0,)), ((), ())),
                                       preferred_element_type=F32)
        for it in items:
            scores = jnp.where(it["mask"], it["scores"], 0.0).astype(BF16)
            s_old = s_ref[it["d"], it["h"]]
            it["o_ref"][it["r0"]:it["r0"] + CHUNK, it["vs"]] = (
                jnp.dot(scores, it["v"], preferred_element_type=F32)
                + jnp.dot(it["q_e"], s_old.astype(BF16), preferred_element_type=F32))
            dcol = jnp.transpose(jnp.broadcast_to(it["decay"], (SUBLANES, GLA_DK)))[:, 0:1]
            s_ref[it["d"], it["h"]] = dcol * s_old + it["ds"]

    @pl.when(i == pl.num_programs(0) - 1)
    def _():
        sf[...] = s_ref[0]
        sb[...] = s_ref[1]


def _gla_call(p, l, wgf, bgf, wgb, bgb, s0f, s0b, *, tb):
    n = p.shape[0]
    nb = n // tb
    fwd = lambda i: i
    bwd = lambda i: nb - 1 - i

    def specs(pos):
        return [
            pl.BlockSpec((tb, GLA_KW), lambda i: (pos(i), 0)),
            pl.BlockSpec((tb, GLA_KW), lambda i: (pos(i), 1)),
            pl.BlockSpec((tb, GLA_VW), lambda i: (pos(i), 1)),
            pl.BlockSpec((tb, LANES), lambda i: (pos(i), 0)),
        ]

    const2 = lambda i: (0, 0)
    const3 = lambda i: (0, 0, 0)
    state_shape = (GLA_HEADS, GLA_DK, GLA_DV)
    state_spec = pl.BlockSpec(state_shape, const3)
    kern = functools.partial(_gla_kernel, tb=tb)
    vmem = (2 * 2 * tb * (2 * GLA_KW + GLA_VW + LANES) * 2 + 2 * 2 * tb * GLA_VW * 4
            + 10 * GLA_HEADS * GLA_DK * GLA_DV * 4 + 24 * tb * GLA_KW * 4 + 8 * 1024 * 1024)
    return pl.pallas_call(
        kern,
        grid=(nb,),
        in_specs=specs(fwd) + specs(bwd) + [
            pl.BlockSpec((LANES, GLA_KW), const2), pl.BlockSpec((1, GLA_KW), const2),
            pl.BlockSpec((LANES, GLA_KW), const2), pl.BlockSpec((1, GLA_KW), const2),
            state_spec, state_spec,
        ],
        out_specs=[
            pl.BlockSpec((tb, GLA_VW), lambda i: (i, 0)),
            pl.BlockSpec((tb, GLA_VW), lambda i: (nb - 1 - i, 0)),
            state_spec, state_spec,
        ],
        out_shape=[
            jax.ShapeDtypeStruct((n, GLA_VW), F32), jax.ShapeDtypeStruct((n, GLA_VW), F32),
            jax.ShapeDtypeStruct(state_shape, F32), jax.ShapeDtypeStruct(state_shape, F32),
        ],
        scratch_shapes=[pltpu.VMEM((2,) + state_shape, F32)],
        compiler_params=_params(1, vmem),
        name="gla",
    )(p, p, p, l, p, p, p, l, wgf, bgf, wgb, bgb, s0f, s0b)


def _mix_kernel(of_ref, ob_ref, g_ref, cb_ref, cc_ref, cx_ref, x_ref, gt_ref, gg_ref, cw_ref,
                wo_ref, o_ref, m_ref, *, bm):
    gg = gg_ref[...]
    for h in range(GLA_HEADS):
        vs = slice(h * GLA_DV, (h + 1) * GLA_DV)
        o = of_ref[:, vs] + ob_ref[:, vs]
        o32 = o * lax.rsqrt(jnp.mean(o * o, axis=-1, keepdims=True) + EPS)
        m_ref[:, vs] = ((o32 * gg) * _silu(g_ref[:, vs].astype(F32))).astype(BF16)

    u = cc_ref[...].astype(F32) * cx_ref[...].astype(F32)
    col = lax.broadcasted_iota(jnp.int32, u.shape, 0) % GRID_W
    u_prev = jnp.where(col == 0, 0.0, pltpu.roll(u, 1, 0))
    u_next = jnp.where(col == GRID_W - 1, 0.0, pltpu.roll(u, bm - 1, 0))
    conv = cw_ref[0:1, :] * u_prev + cw_ref[1:2, :] * u + cw_ref[2:3, :] * u_next
    m_ref[:, GLA_VW:] = (cb_ref[...].astype(F32) * conv).astype(BF16)

    mixed = jnp.dot(m_ref[...], wo_ref[...], preferred_element_type=F32)
    o_ref[...] = x_ref[...] + gt_ref[0:1, :] * mixed


def _mix_call(o_f, o_b, p, x, mod, gla_g, conv_w, w_out, *, bm):
    n = x.shape[0]
    kern = functools.partial(_mix_kernel, bm=bm)
    pcol = lambda k: pl.BlockSpec((bm, GLA_VW), lambda i, k=k: (i, k))
    const2 = lambda i: (0, 0)
    vmem = (2 * 2 * bm * GLA_VW * 4 + 2 * 4 * bm * GLA_VW * 2 + 2 * 2 * bm * D_MODEL * 4
            + 2 * D_MODEL * D_MODEL * 2
            + bm * D_MODEL * 2 + 8 * bm * GLA_VW * 4 + 4 * 1024 * 1024)
    return pl.pallas_call(
        kern,
        grid=(n // bm,),
        in_specs=[
            pl.BlockSpec((bm, GLA_VW), lambda i: (i, 0)),
            pl.BlockSpec((bm, GLA_VW), lambda i: (i, 0)),
            pcol(2), pcol(3), pcol(4), pcol(5),
            pl.BlockSpec((bm, D_MODEL), lambda i: (i, 0)),
            pl.BlockSpec((SUBLANES, D_MODEL), lambda i: (0, 2)),
            pl.BlockSpec((1, GLA_DV), const2),
            pl.BlockSpec((3, CONV_W), const2),
            pl.BlockSpec((D_MODEL, D_MODEL), const2),
        ],
        out_specs=pl.BlockSpec((bm, D_MODEL), lambda i: (i, 0)),
        out_shape=jax.ShapeDtypeStruct((n, D_MODEL), F32),
        scratch_shapes=[pltpu.VMEM((bm, D_MODEL), BF16)],
        compiler_params=_params(1, vmem),
        name="mix",
    )(o_f, o_b, p, p, p, p, x, mod, gla_g, conv_w, w_out)


def _ffn_kernel(x_ref, g_ref, sh_ref, sc_ref, gt_ref, wg_ref, wu_ref, wd_ref, fg_ref,
                o_ref, h_ref, *, bm):
    j = pl.program_id(1)

    @pl.when(j == 0)
    def _():
        _norm_mod_rows(x_ref, g_ref[...], 1.0 + sc_ref[0:1, :], sh_ref[0:1, :], h_ref, bm)
        o_ref[...] = x_ref[...]

    h = h_ref[...]
    gate = jnp.dot(h, wg_ref[...], preferred_element_type=F32)
    up = jnp.dot(h, wu_ref[...], preferred_element_type=F32)
    part = jnp.dot((_silu(gate) * up).astype(BF16), wd_ref[...], preferred_element_type=F32)
    o_ref[...] += gt_ref[0:1, :] * part

    @pl.when(j == pl.num_programs(1) - 1)
    def _():
        fg = fg_ref[...]
        for r0 in range(0, bm, NORM_ROWS):
            rows = slice(r0, r0 + NORM_ROWS)
            xn = o_ref[rows, :]
            ms = jnp.mean(xn * xn, axis=-1, keepdims=True)
            o_ref[rows, :] = (xn * lax.rsqrt(ms + EPS)) * fg


def _ffn_call(x, g, mod, wg, wu, wd, fg, *, bm):
    n = x.shape[0]
    kern = functools.partial(_ffn_kernel, bm=bm)
    modspec = lambda k: pl.BlockSpec((SUBLANES, D_MODEL), lambda i, j, k=k: (0, k))
    vmem = (3 * bm * D_MODEL * 4 + bm * D_MODEL * 2 + 2 * 3 * D_MODEL * FFN_BH * 2
            + 4 * bm * FFN_BH * 4 + bm * D_MODEL * 4 + 4 * 1024 * 1024)
    return pl.pallas_call(
        kern,
        grid=(n // bm, FFN_HIDDEN // FFN_BH),
        in_specs=[
            pl.BlockSpec((bm, D_MODEL), lambda i, j: (i, 0), pipeline_mode=pl.Buffered(1)),
            pl.BlockSpec((1, D_MODEL), lambda i, j: (0, 0)),
            modspec(3), modspec(4), modspec(5),
            pl.BlockSpec((D_MODEL, FFN_BH), lambda i, j: (0, j)),
            pl.BlockSpec((D_MODEL, FFN_BH), lambda i, j: (0, j)),
            pl.BlockSpec((FFN_BH, D_MODEL), lambda i, j: (j, 0)),
            pl.BlockSpec((1, D_MODEL), lambda i, j: (0, 0)),
        ],
        out_specs=pl.BlockSpec((bm, D_MODEL), lambda i, j: (i, 0)),
        out_shape=jax.ShapeDtypeStruct((n, D_MODEL), F32),
        scratch_shapes=[pltpu.VMEM((bm, D_MODEL), BF16)],
        compiler_params=_params(2, vmem),
        name="ffn",
    )(x, g, mod, mod, mod, wg, wu, wd, fg)


def kernel(x, c, ctx, c_ctx, w_mod, b_mod, norm1_g, norm2_g, w_in, w_gate_f, b_gate_f,
           w_gate_b, b_gate_b, gla_norm_g, conv_w, w_out, w_ffn_gate, w_ffn_up, w_ffn_down,
           final_g):
    assert x.shape[0] == 1 and w_mod.shape[0] == 1, "batch 1, depth 1 only"
    seq, ctx_len = x.shape[1], ctx.shape[1]
    x2, ctx2 = x[0], ctx[0]

    c_t = jnp.concatenate(
        [c[0][:, None], c_ctx[:, None], jnp.zeros((D_MODEL, SUBLANES - 2), F32)], axis=1)
    mod = _mod_call(c_t, w_mod[0], b_mod)

    pad_f = jnp.zeros((LANES, GLA_KW), F32).at[:GATE_RANK].set(w_gate_f[0]).astype(BF16)
    pad_b = jnp.zeros((LANES, GLA_KW), F32).at[GATE_RANK:2 * GATE_RANK].set(w_gate_b[0]).astype(BF16)
    g1 = norm1_g[0][None]

    h_c, l_c = _prenorm_call(ctx2, g1, mod, w_in[0], row=1, bm=ctx_len)
    p_c = _in_proj_call(h_c, w_in[0], bm=ctx_len, n_tiles=2)
    zero_state = jnp.zeros((GLA_HEADS, GLA_DK, GLA_DV), F32)
    _, _, s_f, s_b = _gla_call(p_c, l_c, pad_f, b_gate_f, pad_b, b_gate_b, zero_state,
                               zero_state, tb=ctx_len)

    h_x, l_x = _prenorm_call(x2, g1, mod, w_in[0], row=0, bm=512)
    p_x = _in_proj_call(h_x, w_in[0], bm=1024, n_tiles=P_COLS // P_BN)
    o_f, o_b, _, _ = _gla_call(p_x, l_x, pad_f, b_gate_f, pad_b, b_gate_b, s_f, s_b, tb=256)
    x_mid = _mix_call(o_f, o_b, p_x, x2, mod, gla_norm_g, conv_w[0], w_out[0].astype(BF16),
                      bm=256)
    out = _ffn_call(x_mid, norm2_g[0][None], mod, w_ffn_gate[0].astype(BF16),
                    w_ffn_up[0].astype(BF16), w_ffn_down[0].astype(BF16), final_g[None], bm=1024)
    return out[None]
```

```python
import functools

import jax
import jax.numpy as jnp
from jax import lax
from jax.experimental import pallas as pl
from jax.experimental.pallas import tpu as pltpu

F32 = jnp.float32
BF16 = jnp.bfloat16

D_MODEL = 2048
GRID_W = 64
GLA_HEADS = 4
GLA_DK = 128
GLA_DV = 256
GLA_KW = GLA_HEADS * GLA_DK
GLA_VW = GLA_HEADS * GLA_DV
GATE_RANK = 16
GATE_NORM = 16.0
CHUNK = 64
CONV_W = D_MODEL - GLA_VW
FFN_HIDDEN = 5632
EPS = 1e-6

LANES = 128
SUBLANES = 8
VMEM_BYTES = 64 * 1024 * 1024

GATE_COL = 2 * GLA_KW + 2 * GLA_VW
GATE_COLS = 2 * GATE_RANK
P_COLS = GATE_COL + 3 * CONV_W
P_BN = 1024
P_SHIFT_FROM = GATE_COL // P_BN

_NT_DIMS = (((1,), (1,)), ((), ()))

MOD_BN = 1024
FFN_BH = 512
NORM_ROWS = 128


def _params(grid_rank, vmem_bytes):
    return pltpu.CompilerParams(
        dimension_semantics=("arbitrary",) * grid_rank,
        vmem_limit_bytes=min(int(vmem_bytes), VMEM_BYTES - 6 * 1024 * 1024),
    )


def _silu(v):
    return v * jax.nn.sigmoid(v)


def _norm_mod_rows(x_ref, g, mul, sh, dst_ref, rows):
    for r0 in range(0, rows, NORM_ROWS):
        xs = x_ref[r0:r0 + NORM_ROWS, :]
        ms = jnp.mean(xs * xs, axis=-1, keepdims=True)
        y = xs * lax.rsqrt(ms + EPS)
        dst_ref[r0:r0 + NORM_ROWS, :] = ((y * g) * mul + sh).astype(BF16)


def _mod_kernel(c_ref, w_ref, b_ref, o_ref, sb_ref):
    @pl.when(pl.program_id(0) == 0)
    def _():
        s = _silu(c_ref[...])
        for r in range(2):
            sb_ref[r] = jnp.broadcast_to(s[:, r:r + 1], (D_MODEL, LANES))

    nt = MOD_BN // LANES

    def body(kk, accs):
        r0 = pl.multiple_of(kk * SUBLANES, SUBLANES)
        s0 = sb_ref[0, pl.ds(r0, SUBLANES), :]
        s1 = sb_ref[1, pl.ds(r0, SUBLANES), :]
        new = []
        for t in range(nt):
            w8 = w_ref[pl.ds(r0, SUBLANES), t * LANES:(t + 1) * LANES]
            new.append(accs[2 * t] + w8 * s0)
            new.append(accs[2 * t + 1] + w8 * s1)
        return tuple(new)

    zero = jnp.zeros((SUBLANES, LANES), F32)
    accs = lax.fori_loop(0, D_MODEL // SUBLANES, body, (zero,) * (2 * nt), unroll=8)
    o_ref[...] = jnp.zeros(o_ref.shape, F32)
    for t in range(nt):
        b = b_ref[:, t * LANES:(t + 1) * LANES]
        for r in range(2):
            o_ref[r:r + 1, t * LANES:(t + 1) * LANES] = (
                jnp.sum(accs[2 * t + r], axis=0, keepdims=True) + b)


def _mod_call(c_t, w_mod, b_mod):
    n = w_mod.shape[1]
    return pl.pallas_call(
        _mod_kernel,
        grid=(n // MOD_BN,),
        in_specs=[
            pl.BlockSpec((D_MODEL, SUBLANES), lambda j: (0, 0)),
            pl.BlockSpec((D_MODEL, MOD_BN), lambda j: (0, j)),
            pl.BlockSpec((1, MOD_BN), lambda j: (0, j)),
        ],
        out_specs=pl.BlockSpec((SUBLANES, MOD_BN), lambda j: (0, j)),
        out_shape=jax.ShapeDtypeStruct((SUBLANES, n), F32),
        scratch_shapes=[pltpu.VMEM((2, D_MODEL, LANES), F32)],
        compiler_params=_params(1, 2 * D_MODEL * MOD_BN * 4 + 8 * 1024 * 1024),
        name="mod",
    )(c_t, w_mod, b_mod)


def _prenorm_kernel(x_ref, g_ref, sh_ref, sc_ref, wl_ref, h_ref, l_ref, *, row, bm):
    _norm_mod_rows(x_ref, g_ref[...], 1.0 + sc_ref[row:row + 1, :], sh_ref[row:row + 1, :],
                   h_ref, bm)
    l_ref[...] = lax.dot_general(h_ref[...], wl_ref[...].astype(BF16), _NT_DIMS,
                                 preferred_element_type=F32).astype(BF16)


def _prenorm_call(x, g, mod, w_in_t, *, row, bm):
    m = x.shape[0]
    kern = functools.partial(_prenorm_kernel, row=row, bm=bm)
    vmem = (2 * bm * D_MODEL * (4 + 2) + 4 * D_MODEL * LANES * 4 + 8 * NORM_ROWS * D_MODEL * 4
            + 4 * 1024 * 1024)
    return pl.pallas_call(
        kern,
        grid=(m // bm,),
        in_specs=[
            pl.BlockSpec((bm, D_MODEL), lambda i: (i, 0)),
            pl.BlockSpec((1, D_MODEL), lambda i: (0, 0)),
            pl.BlockSpec((SUBLANES, D_MODEL), lambda i: (0, 0)),
            pl.BlockSpec((SUBLANES, D_MODEL), lambda i: (0, 1)),
            pl.BlockSpec((LANES, D_MODEL), lambda i: (GATE_COL // LANES, 0)),
        ],
        out_specs=[pl.BlockSpec((bm, D_MODEL), lambda i: (i, 0)),
                   pl.BlockSpec((bm, LANES), lambda i: (i, 0))],
        out_shape=[jax.ShapeDtypeStruct((m, D_MODEL), BF16),
                   jax.ShapeDtypeStruct((m, LANES), BF16)],
        compiler_params=_params(1, vmem),
        name="prenorm",
    )(x, g, mod, mod, w_in_t)


W_ROWS = 256


def _in_proj_kernel(h_ref, wa_ref, wb_ref, o_ref, w_ref):
    j = pl.program_id(0)

    @pl.when(pl.program_id(1) == 0)
    def _():
        @pl.when(j < P_SHIFT_FROM)
        def _():
            for r0 in range(0, P_BN, W_ROWS):
                w_ref[r0:r0 + W_ROWS, :] = wa_ref[r0:r0 + W_ROWS, :].astype(BF16)

        @pl.when(j >= P_SHIFT_FROM)
        def _():
            keep = P_BN - GATE_COLS
            for r0 in range(0, keep, W_ROWS):
                r1 = min(r0 + W_ROWS, keep)
                w_ref[r0:r1, :] = wa_ref[GATE_COLS + r0:GATE_COLS + r1, :].astype(BF16)
            w_ref[keep:, :] = wb_ref[...].astype(BF16)

    o_ref[...] = lax.dot_general(h_ref[...], w_ref[...], _NT_DIMS,
                                 preferred_element_type=F32).astype(BF16)


def _in_proj_call(h, w_in_t, *, bm, n_tiles):
    m = h.shape[0]
    vmem = (2 * bm * D_MODEL * 2 + 2 * (P_BN + GATE_COLS) * D_MODEL * 4 + P_BN * D_MODEL * 2
            + 2 * bm * P_BN * 2 + bm * P_BN * 4 + 4 * W_ROWS * D_MODEL * 4 + 4 * 1024 * 1024)
    return pl.pallas_call(
        _in_proj_kernel,
        grid=(n_tiles, m // bm),
        in_specs=[
            pl.BlockSpec((bm, D_MODEL), lambda j, i: (i, 0)),
            pl.BlockSpec((P_BN, D_MODEL), lambda j, i: (j, 0)),
            pl.BlockSpec((GATE_COLS, D_MODEL), lambda j, i: ((j + 1) * (P_BN // GATE_COLS), 0)),
        ],
        out_specs=pl.BlockSpec((bm, P_BN), lambda j, i: (i, j)),
        out_shape=jax.ShapeDtypeStruct((m, n_tiles * P_BN), BF16),
        scratch_shapes=[pltpu.VMEM((P_BN, D_MODEL), BF16)],
        compiler_params=_params(2, vmem),
        name="in_proj",
    )(h, w_in_t, w_in_t)


def _gla_prepare(l_ref, wg_ref, bg_ref, *, reverse, tb):
    z = jnp.dot(l_ref[...], wg_ref[...], preferred_element_type=F32) + bg_ref[...]
    log_a = (jnp.minimum(z, 0.0) - jnp.log1p(jnp.exp(-jnp.abs(z)))) * (1.0 / GATE_NORM)

    r = lax.broadcasted_iota(jnp.int32, (tb, tb), 0)
    cc = lax.broadcasted_iota(jnp.int32, (tb, tb), 1)
    same_chunk = (r // CHUNK) == (cc // CHUNK)
    tri = jnp.where(same_chunk & ((cc >= r) if reverse else (cc <= r)), 1.0, 0.0).astype(BF16)
    hi = log_a.astype(BF16)
    rem = log_a - hi.astype(F32)
    mid = rem.astype(BF16)
    lo = (rem - mid.astype(F32)).astype(BF16)
    return (jnp.dot(tri, hi, preferred_element_type=F32)
            + jnp.dot(tri, mid, preferred_element_type=F32)
            + jnp.dot(tri, lo, preferred_element_type=F32))


def _gla_kernel(qf, kf, vf, lf, qb, kb, vb, lb, wgf, bgf, wgb, bgb, s0f, s0b,
                of, ob, sf, sb, s_ref, *, tb):
    i = pl.program_id(0)

    @pl.when(i == 0)
    def _():
        s_ref[0] = s0f[...]
        s_ref[1] = s0b[...]

    dirs = (
        (qf, kf, vf, of, _gla_prepare(lf, wgf, bgf, reverse=False, tb=tb), False),
        (qb, kb, vb, ob, _gla_prepare(lb, wgb, bgb, reverse=True, tb=tb), True),
    )
    ii = lax.broadcasted_iota(jnp.int32, (CHUNK, CHUNK), 0)
    jj = lax.broadcasted_iota(jnp.int32, (CHUNK, CHUNK), 1)
    nc = tb // CHUNK
    for step in range(nc):
        items = []
        for d, (q_ref, k_ref, v_ref, o_ref, cum, reverse) in enumerate(dirs):
            r0 = (nc - 1 - step if reverse else step) * CHUNK
            cum_c = cum[r0:r0 + CHUNK, :]
            last = cum_c[0:1, :] if reverse else cum_c[CHUNK - 1:CHUNK, :]
            q_c = q_ref[r0:r0 + CHUNK, :].astype(F32)
            k_c = k_ref[r0:r0 + CHUNK, :].astype(F32)
            q_e = ((q_c * (GLA_DK ** -0.5)) * jnp.exp(cum_c)).astype(BF16)
            k_e = (k_c * jnp.exp(-cum_c)).astype(BF16)
            k_d = (k_c * jnp.exp(last - cum_c)).astype(BF16)
            decay = jnp.exp(last)
            mask = (jj >= ii) if reverse else (jj <= ii)
            for h in range(GLA_HEADS):
                ks = slice(h * GLA_DK, (h + 1) * GLA_DK)
                vs = slice(h * GLA_DV, (h + 1) * GLA_DV)
                items.append(dict(
                    d=d, h=h, r0=r0, vs=vs, o_ref=o_ref, mask=mask,
                    q_e=q_e[:, ks], k_e=k_e[:, ks], k_d=k_d[:, ks], decay=decay[:, ks],
                    v=v_ref[r0:r0 + CHUNK, vs]))
        for it in items:
            it["scores"] = lax.dot_general(it["q_e"], it["k_e"], (((1,), (1,)), ((), ())),
                                           preferred_element_type=F32)
        for it in items:
            it["ds"] = lax.dot_general(it["k_d"], it["v"], (((0,), (0,)), ((), ())),
                                       preferred_element_type=F32)
        for it in items:
            scores = jnp.where(it["mask"], it["scores"], 0.0).astype(BF16)
            s_old = s_ref[it["d"], it["h"]]
            it["o_ref"][it["r0"]:it["r0"] + CHUNK, it["vs"]] = (
                jnp.dot(scores, it["v"], preferred_element_type=F32)
                + jnp.dot(it["q_e"], s_old.astype(BF16), preferred_element_type=F32))
            dcol = jnp.transpose(jnp.broadcast_to(it["decay"], (SUBLANES, GLA_DK)))[:, 0:1]
            s_ref[it["d"], it["h"]] = dcol * s_old + it["ds"]

    @pl.when(i == pl.num_programs(0) - 1)
    def _():
        sf[...] = s_ref[0]
        sb[...] = s_ref[1]


def _gla_call(p, l, wgf, bgf, wgb, bgb, s0f, s0b, *, tb):
    n = p.shape[0]
    nb = n // tb
    fwd = lambda i: i
    bwd = lambda i: nb - 1 - i

    def specs(pos):
        return [
            pl.BlockSpec((tb, GLA_KW), lambda i: (pos(i), 0)),
            pl.BlockSpec((tb, GLA_KW), lambda i: (pos(i), 1)),
            pl.BlockSpec((tb, GLA_VW), lambda i: (pos(i), 1)),
            pl.BlockSpec((tb, LANES), lambda i: (pos(i), 0)),
        ]

    const2 = lambda i: (0, 0)
    const3 = lambda i: (0, 0, 0)
    state_shape = (GLA_HEADS, GLA_DK, GLA_DV)
    state_spec = pl.BlockSpec(state_shape, const3)
    kern = functools.partial(_gla_kernel, tb=tb)
    vmem = (2 * 2 * tb * (2 * GLA_KW + GLA_VW + LANES) * 2 + 2 * 2 * tb * GLA_VW * 4
            + 10 * GLA_HEADS * GLA_DK * GLA_DV * 4 + 24 * tb * GLA_KW * 4 + 8 * 1024 * 1024)
    return pl.pallas_call(
        kern,
        grid=(nb,),
        in_specs=specs(fwd) + specs(bwd) + [
            pl.BlockSpec((LANES, GLA_KW), const2), pl.BlockSpec((1, GLA_KW), const2),
            pl.BlockSpec((LANES, GLA_KW), const2), pl.BlockSpec((1, GLA_KW), const2),
            state_spec, state_spec,
        ],
        out_specs=[
            pl.BlockSpec((tb, GLA_VW), lambda i: (i, 0)),
            pl.BlockSpec((tb, GLA_VW), lambda i: (nb - 1 - i, 0)),
            state_spec, state_spec,
        ],
        out_shape=[
            jax.ShapeDtypeStruct((n, GLA_VW), F32), jax.ShapeDtypeStruct((n, GLA_VW), F32),
            jax.ShapeDtypeStruct(state_shape, F32), jax.ShapeDtypeStruct(state_shape, F32),
        ],
        scratch_shapes=[pltpu.VMEM((2,) + state_shape, F32)],
        compiler_params=_params(1, vmem),
        name="gla",
    )(p, p, p, l, p, p, p, l, wgf, bgf, wgb, bgb, s0f, s0b)


def _mix_kernel(of_ref, ob_ref, g_ref, cb_ref, cc_ref, cx_ref, x_ref, gt_ref, gg_ref, cw_ref,
                wo_ref, o_ref, m_ref, *, bm):
    gg = gg_ref[...]
    for h in range(GLA_HEADS):
        vs = slice(h * GLA_DV, (h + 1) * GLA_DV)
        o = of_ref[:, vs] + ob_ref[:, vs]
        o32 = o * lax.rsqrt(jnp.mean(o * o, axis=-1, keepdims=True) + EPS)
        m_ref[:, vs] = ((o32 * gg) * _silu(g_ref[:, vs].astype(F32))).astype(BF16)

    u = cc_ref[...].astype(F32) * cx_ref[...].astype(F32)
    col = lax.broadcasted_iota(jnp.int32, u.shape, 0) % GRID_W
    u_prev = jnp.where(col == 0, 0.0, pltpu.roll(u, 1, 0))
    u_next = jnp.where(col == GRID_W - 1, 0.0, pltpu.roll(u, bm - 1, 0))
    conv = cw_ref[0:1, :] * u_prev + cw_ref[1:2, :] * u + cw_ref[2:3, :] * u_next
    m_ref[:, GLA_VW:] = (cb_ref[...].astype(F32) * conv).astype(BF16)

    mixed = jnp.dot(m_ref[...], wo_ref[...], preferred_element_type=F32)
    o_ref[...] = x_ref[...] + gt_ref[0:1, :] * mixed


def _mix_call(o_f, o_b, p, x, mod, gla_g, conv_w, w_out, *, bm):
    n = x.shape[0]
    kern = functools.partial(_mix_kernel, bm=bm)
    pcol = lambda k: pl.BlockSpec((bm, GLA_VW), lambda i, k=k: (i, k))
    const2 = lambda i: (0, 0)
    vmem = (2 * 2 * bm * GLA_VW * 4 + 2 * 4 * bm * GLA_VW * 2 + 2 * 2 * bm * D_MODEL * 4
            + 2 * D_MODEL * D_MODEL * 2
            + bm * D_MODEL * 2 + 8 * bm * GLA_VW * 4 + 4 * 1024 * 1024)
    return pl.pallas_call(
        kern,
        grid=(n // bm,),
        in_specs=[
            pl.BlockSpec((bm, GLA_VW), lambda i: (i, 0)),
            pl.BlockSpec((bm, GLA_VW), lambda i: (i, 0)),
            pcol(2), pcol(3), pcol(4), pcol(5),
            pl.BlockSpec((bm, D_MODEL), lambda i: (i, 0)),
            pl.BlockSpec((SUBLANES, D_MODEL), lambda i: (0, 2)),
            pl.BlockSpec((1, GLA_DV), const2),
            pl.BlockSpec((3, CONV_W), const2),
            pl.BlockSpec((D_MODEL, D_MODEL), const2),
        ],
        out_specs=pl.BlockSpec((bm, D_MODEL), lambda i: (i, 0)),
        out_shape=jax.ShapeDtypeStruct((n, D_MODEL), F32),
        scratch_shapes=[pltpu.VMEM((bm, D_MODEL), BF16)],
        compiler_params=_params(1, vmem),
        name="mix",
    )(o_f, o_b, p, p, p, p, x, mod, gla_g, conv_w, w_out)


def _ffn_kernel(x_ref, g_ref, sh_ref, sc_ref, gt_ref, wg_ref, wu_ref, wd_ref, fg_ref,
                o_ref, h_ref, *, bm):
    j = pl.program_id(1)

    @pl.when(j == 0)
    def _():
        _norm_mod_rows(x_ref, g_ref[...], 1.0 + sc_ref[0:1, :], sh_ref[0:1, :], h_ref, bm)
        o_ref[...] = x_ref[...]

    h = h_ref[...]
    gate = jnp.dot(h, wg_ref[...], preferred_element_type=F32)
    up = jnp.dot(h, wu_ref[...], preferred_element_type=F32)
    part = jnp.dot((_silu(gate) * up).astype(BF16), wd_ref[...], preferred_element_type=F32)
    o_ref[...] += gt_ref[0:1, :] * part

    @pl.when(j == pl.num_programs(1) - 1)
    def _():
        fg = fg_ref[...]
        for r0 in range(0, bm, NORM_ROWS):
            rows = slice(r0, r0 + NORM_ROWS)
            xn = o_ref[rows, :]
            ms = jnp.mean(xn * xn, axis=-1, keepdims=True)
            o_ref[rows, :] = (xn * lax.rsqrt(ms + EPS)) * fg


def _ffn_call(x, g, mod, wg, wu, wd, fg, *, bm):
    n = x.shape[0]
    kern = functools.partial(_ffn_kernel, bm=bm)
    modspec = lambda k: pl.BlockSpec((SUBLANES, D_MODEL), lambda i, j, k=k: (0, k))
    vmem = (3 * bm * D_MODEL * 4 + bm * D_MODEL * 2 + 2 * 3 * D_MODEL * FFN_BH * 2
            + 4 * bm * FFN_BH * 4 + bm * D_MODEL * 4 + 4 * 1024 * 1024)
    return pl.pallas_call(
        kern,
        grid=(n // bm, FFN_HIDDEN // FFN_BH),
        in_specs=[
            pl.BlockSpec((bm, D_MODEL), lambda i, j: (i, 0), pipeline_mode=pl.Buffered(1)),
            pl.BlockSpec((1, D_MODEL), lambda i, j: (0, 0)),
            modspec(3), modspec(4), modspec(5),
            pl.BlockSpec((D_MODEL, FFN_BH), lambda i, j: (0, j)),
            pl.BlockSpec((D_MODEL, FFN_BH), lambda i, j: (0, j)),
            pl.BlockSpec((FFN_BH, D_MODEL), lambda i, j: (j, 0)),
            pl.BlockSpec((1, D_MODEL), lambda i, j: (0, 0)),
        ],
        out_specs=pl.BlockSpec((bm, D_MODEL), lambda i, j: (i, 0)),
        out_shape=jax.ShapeDtypeStruct((n, D_MODEL), F32),
        scratch_shapes=[pltpu.VMEM((bm, D_MODEL), BF16)],
        compiler_params=_params(2, vmem),
        name="ffn",
    )(x, g, mod, mod, mod, wg, wu, wd, fg)


def kernel(x, c, ctx, c_ctx, w_mod, b_mod, norm1_g, norm2_g, w_in, w_gate_f, b_gate_f,
           w_gate_b, b_gate_b, gla_norm_g, conv_w, w_out, w_ffn_gate, w_ffn_up, w_ffn_down,
           final_g):
    assert x.shape[0] == 1 and w_mod.shape[0] == 1, "batch 1, depth 1 only"
    seq, ctx_len = x.shape[1], ctx.shape[1]
    x2, ctx2 = x[0], ctx[0]

    c_t = jnp.concatenate(
        [c[0][:, None], c_ctx[:, None], jnp.zeros((D_MODEL, SUBLANES - 2), F32)], axis=1)
    mod = _mod_call(c_t, w_mod[0], b_mod)

    pad_f = jnp.zeros((LANES, GLA_KW), F32).at[:GATE_RANK].set(w_gate_f[0]).astype(BF16)
    pad_b = jnp.zeros((LANES, GLA_KW), F32).at[GATE_RANK:2 * GATE_RANK].set(w_gate_b[0]).astype(BF16)
    g1 = norm1_g[0][None]
    w_in_t = jnp.transpose(w_in[0])

    h_c, l_c = _prenorm_call(ctx2, g1, mod, w_in_t, row=1, bm=ctx_len)
    p_c = _in_proj_call(h_c, w_in_t, bm=ctx_len, n_tiles=2)
    zero_state = jnp.zeros((GLA_HEADS, GLA_DK, GLA_DV), F32)
    _, _, s_f, s_b = _gla_call(p_c, l_c, pad_f, b_gate_f, pad_b, b_gate_b, zero_state,
                               zero_state, tb=ctx_len)

    h_x, l_x = _prenorm_call(x2, g1, mod, w_in_t, row=0, bm=512)
    p_x = _in_proj_call(h_x, w_in_t, bm=1024, n_tiles=P_COLS // P_BN)
    o_f, o_b, _, _ = _gla_call(p_x, l_x, pad_f, b_gate_f, pad_b, b_gate_b, s_f, s_b, tb=256)
    x_mid = _mix_call(o_f, o_b, p_x, x2, mod, gla_norm_g, conv_w[0], w_out[0].astype(BF16),
                      bm=256)
    out = _ffn_call(x_mid, norm2_g[0][None], mod, w_ffn_gate[0].astype(BF16),
                    w_ffn_up[0].astype(BF16), w_ffn_down[0].astype(BF16), final_g[None], bm=1024)
    return out[None]
```

```python
import functools

import jax
import jax.numpy as jnp
from jax import lax
from jax.experimental import pallas as pl
from jax.experimental.pallas import tpu as pltpu

F32 = jnp.float32
BF16 = jnp.bfloat16

D_MODEL = 2048
GRID_W = 64
GLA_HEADS = 4
GLA_DK = 128
GLA_DV = 256
GLA_KW = GLA_HEADS * GLA_DK
GLA_VW = GLA_HEADS * GLA_DV
GATE_RANK = 16
GATE_NORM = 16.0
CHUNK = 64
CONV_W = D_MODEL - GLA_VW
FFN_HIDDEN = 5632
EPS = 1e-6

LANES = 128
SUBLANES = 8
VMEM_BYTES = 64 * 1024 * 1024

GATE_COL = 2 * GLA_KW + 2 * GLA_VW
GATE_COLS = 2 * GATE_RANK
P_COLS = GATE_COL + 3 * CONV_W
P_BN = 1024
P_SHIFT_FROM = GATE_COL // P_BN

_NT_DIMS = (((1,), (1,)), ((), ()))

MOD_BN = 1024
FFN_BH = 512
NORM_ROWS = 128


def _params(grid_rank, vmem_bytes):
    return pltpu.CompilerParams(
        dimension_semantics=("arbitrary",) * grid_rank,
        vmem_limit_bytes=min(int(vmem_bytes), VMEM_BYTES - 6 * 1024 * 1024),
    )


def _silu(v):
    return v * jax.nn.sigmoid(v)


def _norm_mod_rows(x_ref, g, mul, sh, dst_ref, rows):
    for r0 in range(0, rows, NORM_ROWS):
        xs = x_ref[r0:r0 + NORM_ROWS, :]
        ms = jnp.mean(xs * xs, axis=-1, keepdims=True)
        y = xs * lax.rsqrt(ms + EPS)
        dst_ref[r0:r0 + NORM_ROWS, :] = ((y * g) * mul + sh).astype(BF16)


def _mod_kernel(c_ref, w_ref, b_ref, o_ref, sb_ref):
    @pl.when(pl.program_id(0) == 0)
    def _():
        s = _silu(c_ref[...])
        for r in range(2):
            sb_ref[r] = jnp.broadcast_to(s[:, r:r + 1], (D_MODEL, LANES))

    nt = MOD_BN // LANES

    def body(kk, accs):
        r0 = pl.multiple_of(kk * SUBLANES, SUBLANES)
        s0 = sb_ref[0, pl.ds(r0, SUBLANES), :]
        s1 = sb_ref[1, pl.ds(r0, SUBLANES), :]
        new = []
        for t in range(nt):
            w8 = w_ref[pl.ds(r0, SUBLANES), t * LANES:(t + 1) * LANES]
            new.append(accs[2 * t] + w8 * s0)
            new.append(accs[2 * t + 1] + w8 * s1)
        return tuple(new)

    zero = jnp.zeros((SUBLANES, LANES), F32)
    accs = lax.fori_loop(0, D_MODEL // SUBLANES, body, (zero,) * (2 * nt), unroll=8)
    o_ref[...] = jnp.zeros(o_ref.shape, F32)
    for t in range(nt):
        b = b_ref[:, t * LANES:(t + 1) * LANES]
        for r in range(2):
            o_ref[r:r + 1, t * LANES:(t + 1) * LANES] = (
                jnp.sum(accs[2 * t + r], axis=0, keepdims=True) + b)


def _mod_call(c_t, w_mod, b_mod):
    n = w_mod.shape[1]
    return pl.pallas_call(
        _mod_kernel,
        grid=(n // MOD_BN,),
        in_specs=[
            pl.BlockSpec((D_MODEL, SUBLANES), lambda j: (0, 0)),
            pl.BlockSpec((D_MODEL, MOD_BN), lambda j: (0, j)),
            pl.BlockSpec((1, MOD_BN), lambda j: (0, j)),
        ],
        out_specs=pl.BlockSpec((SUBLANES, MOD_BN), lambda j: (0, j)),
        out_shape=jax.ShapeDtypeStruct((SUBLANES, n), F32),
        scratch_shapes=[pltpu.VMEM((2, D_MODEL, LANES), F32)],
        compiler_params=_params(1, 2 * D_MODEL * MOD_BN * 4 + 8 * 1024 * 1024),
        name="mod",
    )(c_t, w_mod, b_mod)


def _prenorm_kernel(x_ref, g_ref, sh_ref, sc_ref, wl_ref, h_ref, l_ref, *, row, bm):
    _norm_mod_rows(x_ref, g_ref[...], 1.0 + sc_ref[row:row + 1, :], sh_ref[row:row + 1, :],
                   h_ref, bm)
    l_ref[...] = lax.dot_general(h_ref[...], wl_ref[...].astype(BF16), _NT_DIMS,
                                 preferred_element_type=F32).astype(BF16)


def _prenorm_call(x, g, mod, w_in_t, *, row, bm):
    m = x.shape[0]
    kern = functools.partial(_prenorm_kernel, row=row, bm=bm)
    vmem = (2 * bm * D_MODEL * (4 + 2) + 4 * D_MODEL * LANES * 4 + 8 * NORM_ROWS * D_MODEL * 4
            + 4 * 1024 * 1024)
    return pl.pallas_call(
        kern,
        grid=(m // bm,),
        in_specs=[
            pl.BlockSpec((bm, D_MODEL), lambda i: (i, 0)),
            pl.BlockSpec((1, D_MODEL), lambda i: (0, 0)),
            pl.BlockSpec((SUBLANES, D_MODEL), lambda i: (0, 0)),
            pl.BlockSpec((SUBLANES, D_MODEL), lambda i: (0, 1)),
            pl.BlockSpec((LANES, D_MODEL), lambda i: (GATE_COL // LANES, 0)),
        ],
        out_specs=[pl.BlockSpec((bm, D_MODEL), lambda i: (i, 0)),
                   pl.BlockSpec((bm, LANES), lambda i: (i, 0))],
        out_shape=[jax.ShapeDtypeStruct((m, D_MODEL), BF16),
                   jax.ShapeDtypeStruct((m, LANES), BF16)],
        compiler_params=_params(1, vmem),
        name="prenorm",
    )(x, g, mod, mod, w_in_t)


W_ROWS = 256


def _in_proj_kernel(h_ref, wa_ref, wb_ref, o_ref, w_ref):
    j = pl.program_id(0)

    @pl.when(pl.program_id(1) == 0)
    def _():
        @pl.when(j < P_SHIFT_FROM)
        def _():
            for r0 in range(0, P_BN, W_ROWS):
                w_ref[r0:r0 + W_ROWS, :] = wa_ref[r0:r0 + W_ROWS, :].astype(BF16)

        @pl.when(j >= P_SHIFT_FROM)
        def _():
            keep = P_BN - GATE_COLS
            for r0 in range(0, keep, W_ROWS):
                r1 = min(r0 + W_ROWS, keep)
                w_ref[r0:r1, :] = wa_ref[GATE_COLS + r0:GATE_COLS + r1, :].astype(BF16)
            w_ref[keep:, :] = wb_ref[...].astype(BF16)

    o_ref[...] = lax.dot_general(h_ref[...], w_ref[...], _NT_DIMS,
                                 preferred_element_type=F32).astype(BF16)


def _in_proj_call(h, w_in_t, *, bm, n_tiles):
    m = h.shape[0]
    vmem = (2 * bm * D_MODEL * 2 + 2 * (P_BN + GATE_COLS) * D_MODEL * 4 + P_BN * D_MODEL * 2
            + 2 * bm * P_BN * 2 + bm * P_BN * 4 + 4 * W_ROWS * D_MODEL * 4 + 4 * 1024 * 1024)
    return pl.pallas_call(
        _in_proj_kernel,
        grid=(n_tiles, m // bm),
        in_specs=[
            pl.BlockSpec((bm, D_MODEL), lambda j, i: (i, 0)),
            pl.BlockSpec((P_BN, D_MODEL), lambda j, i: (j, 0)),
            pl.BlockSpec((GATE_COLS, D_MODEL), lambda j, i: ((j + 1) * (P_BN // GATE_COLS), 0)),
        ],
        out_specs=pl.BlockSpec((bm, P_BN), lambda j, i: (i, j)),
        out_shape=jax.ShapeDtypeStruct((m, n_tiles * P_BN), BF16),
        scratch_shapes=[pltpu.VMEM((P_BN, D_MODEL), BF16)],
        compiler_params=_params(2, vmem),
        name="in_proj",
    )(h, w_in_t, w_in_t)


def _gla_log_decay_split(l_ref, wg_ref, bg_ref):
    z = jnp.dot(l_ref[...], wg_ref[...], preferred_element_type=F32) + bg_ref[...]
    log_a = (jnp.minimum(z, 0.0) - jnp.log(1.0 + jnp.exp(-jnp.abs(z)))) * (1.0 / GATE_NORM)
    hi = log_a.astype(BF16)
    lo = (log_a - hi.astype(F32)).astype(BF16)
    return hi, lo


def _gla_chunk_cumsum(pieces, *, reverse, tb):
    r = lax.broadcasted_iota(jnp.int32, (tb, tb), 0)
    cc = lax.broadcasted_iota(jnp.int32, (tb, tb), 1)
    same_chunk = (r // CHUNK) == (cc // CHUNK)
    tri = jnp.where(same_chunk & ((cc >= r) if reverse else (cc <= r)), 1.0, 0.0).astype(BF16)
    hi, lo = pieces
    return (jnp.dot(tri, hi, preferred_element_type=F32)
            + jnp.dot(tri, lo, preferred_element_type=F32))


def _gla_kernel(qf, kf, vf, lf, lf_next, qb, kb, vb, lb, lb_next, wgf, bgf, wgb, bgb, s0f, s0b,
                of, ob, sf, sb, s_ref, cum_ref, *, tb):
    i = pl.program_id(0)

    @pl.when(i == 0)
    def _():
        s_ref[0] = s0f[...]
        s_ref[1] = s0b[...]
        cum_ref[0] = _gla_chunk_cumsum(_gla_log_decay_split(lf, wgf, bgf), reverse=False, tb=tb)
        cum_ref[1] = _gla_chunk_cumsum(_gla_log_decay_split(lb, wgb, bgb), reverse=True, tb=tb)

    next_f = _gla_log_decay_split(lf_next, wgf, bgf)
    next_b = _gla_log_decay_split(lb_next, wgb, bgb)
    dirs = ((qf, kf, vf, of, False), (qb, kb, vb, ob, True))
    ii = lax.broadcasted_iota(jnp.int32, (CHUNK, CHUNK), 0)
    jj = lax.broadcasted_iota(jnp.int32, (CHUNK, CHUNK), 1)
    nc = tb // CHUNK

    def start(step):
        items = []
        for d, (q_ref, k_ref, v_ref, o_ref, reverse) in enumerate(dirs):
            r0 = (nc - 1 - step if reverse else step) * CHUNK
            cum_c = cum_ref[d, r0:r0 + CHUNK, :]
            last = cum_c[0:1, :] if reverse else cum_c[CHUNK - 1:CHUNK, :]
            q_c = q_ref[r0:r0 + CHUNK, :].astype(F32)
            k_c = k_ref[r0:r0 + CHUNK, :].astype(F32)
            q_e = ((q_c * (GLA_DK ** -0.5)) * jnp.exp(cum_c)).astype(BF16)
            k_e = (k_c * jnp.exp(-cum_c)).astype(BF16)
            k_d = (k_c * jnp.exp(last - cum_c)).astype(BF16)
            decay = jnp.exp(last)
            mask = (jj >= ii) if reverse else (jj <= ii)
            for h in range(GLA_HEADS):
                ks = slice(h * GLA_DK, (h + 1) * GLA_DK)
                vs = slice(h * GLA_DV, (h + 1) * GLA_DV)
                items.append(dict(
                    d=d, h=h, r0=r0, vs=vs, o_ref=o_ref, mask=mask,
                    q_e=q_e[:, ks], k_e=k_e[:, ks], k_d=k_d[:, ks], decay=decay[:, ks],
                    v=v_ref[r0:r0 + CHUNK, vs]))
        for it in items:
            it["scores"] = lax.dot_general(it["q_e"], it["k_e"], _NT_DIMS,
                                           preferred_element_type=F32)
        return items

    items = start(0)
    next_cum = None
    for step in range(nc):
        following = start(step + 1) if step + 1 < nc else None
        if step == nc // 2:
            next_cum = (_gla_chunk_cumsum(next_f, reverse=False, tb=tb),
                        _gla_chunk_cumsum(next_b, reverse=True, tb=tb))
        for it in items:
            scores = jnp.where(it["mask"], it["scores"], 0.0).astype(BF16)
            s_old = s_ref[it["d"], it["h"]]
            it["o_ref"][it["r0"]:it["r0"] + CHUNK, it["vs"]] = (
                jnp.dot(scores, it["v"], preferred_element_type=F32)
                + jnp.dot(it["q_e"], s_old.astype(BF16), preferred_element_type=F32))
            ds = lax.dot_general(it["k_d"], it["v"], (((0,), (0,)), ((), ())),
                                 preferred_element_type=F32)
            dcol = jnp.transpose(jnp.broadcast_to(it["decay"], (SUBLANES, GLA_DK)))[:, 0:1]
            s_ref[it["d"], it["h"]] = dcol * s_old + ds
        items = following
    cum_ref[0] = next_cum[0]
    cum_ref[1] = next_cum[1]

    @pl.when(i == pl.num_programs(0) - 1)
    def _():
        sf[...] = s_ref[0]
        sb[...] = s_ref[1]


def _gla_call(p, l, wgf, bgf, wgb, bgb, s0f, s0b, *, tb):
    n = p.shape[0]
    nb = n // tb
    fwd = lambda i: i
    bwd = lambda i: nb - 1 - i
    fwd_next = lambda i: jnp.minimum(i + 1, nb - 1)
    bwd_next = lambda i: jnp.maximum(nb - 2 - i, 0)

    def specs(pos, pos_next):
        return [
            pl.BlockSpec((tb, GLA_KW), lambda i: (pos(i), 0)),
            pl.BlockSpec((tb, GLA_KW), lambda i: (pos(i), 1)),
            pl.BlockSpec((tb, GLA_VW), lambda i: (pos(i), 1)),
            pl.BlockSpec((tb, LANES), lambda i: (pos(i), 0)),
            pl.BlockSpec((tb, LANES), lambda i: (pos_next(i), 0)),
        ]

    const2 = lambda i: (0, 0)
    const3 = lambda i: (0, 0, 0)
    state_shape = (GLA_HEADS, GLA_DK, GLA_DV)
    state_spec = pl.BlockSpec(state_shape, const3)
    kern = functools.partial(_gla_kernel, tb=tb)
    vmem = (2 * 2 * tb * (2 * GLA_KW + GLA_VW + LANES) * 2 + 2 * 2 * tb * GLA_VW * 4
            + 10 * GLA_HEADS * GLA_DK * GLA_DV * 4 + 24 * tb * GLA_KW * 4 + 8 * 1024 * 1024)
    return pl.pallas_call(
        kern,
        grid=(nb,),
        in_specs=specs(fwd, fwd_next) + specs(bwd, bwd_next) + [
            pl.BlockSpec((LANES, GLA_KW), const2), pl.BlockSpec((1, GLA_KW), const2),
            pl.BlockSpec((LANES, GLA_KW), const2), pl.BlockSpec((1, GLA_KW), const2),
            state_spec, state_spec,
        ],
        out_specs=[
            pl.BlockSpec((tb, GLA_VW), lambda i: (i, 0)),
            pl.BlockSpec((tb, GLA_VW), lambda i: (nb - 1 - i, 0)),
            state_spec, state_spec,
        ],
        out_shape=[
            jax.ShapeDtypeStruct((n, GLA_VW), F32), jax.ShapeDtypeStruct((n, GLA_VW), F32),
            jax.ShapeDtypeStruct(state_shape, F32), jax.ShapeDtypeStruct(state_shape, F32),
        ],
        scratch_shapes=[pltpu.VMEM((2,) + state_shape, F32), pltpu.VMEM((2, tb, GLA_KW), F32)],
        compiler_params=_params(1, vmem),
        name="gla",
    )(p, p, p, l, l, p, p, p, l, l, wgf, bgf, wgb, bgb, s0f, s0b)


def _mix_kernel(of_ref, ob_ref, g_ref, cb_ref, cc_ref, cx_ref, x_ref, gt_ref, gg_ref, cw_ref,
                wo_ref, o_ref, m_ref, w_ref, *, bm):
    @pl.when(pl.program_id(0) == 0)
    def _():
        for r0 in range(0, D_MODEL, W_ROWS):
            w_ref[r0:r0 + W_ROWS, :] = wo_ref[r0:r0 + W_ROWS, :].astype(BF16)

    gg = gg_ref[...]
    for h in range(GLA_HEADS):
        vs = slice(h * GLA_DV, (h + 1) * GLA_DV)
        o = of_ref[:, vs] + ob_ref[:, vs]
        o32 = o * lax.rsqrt(jnp.mean(o * o, axis=-1, keepdims=True) + EPS)
        m_ref[:, vs] = ((o32 * gg) * _silu(g_ref[:, vs].astype(F32))).astype(BF16)

    u = cc_ref[...].astype(F32) * cx_ref[...].astype(F32)
    col = lax.broadcasted_iota(jnp.int32, u.shape, 0) % GRID_W
    u_prev = jnp.where(col == 0, 0.0, pltpu.roll(u, 1, 0))
    u_next = jnp.where(col == GRID_W - 1, 0.0, pltpu.roll(u, bm - 1, 0))
    conv = cw_ref[0:1, :] * u_prev + cw_ref[1:2, :] * u + cw_ref[2:3, :] * u_next
    m_ref[:, GLA_VW:] = (cb_ref[...].astype(F32) * conv).astype(BF16)

    mixed = jnp.dot(m_ref[...], w_ref[...], preferred_element_type=F32)
    o_ref[...] = x_ref[...] + gt_ref[0:1, :] * mixed


def _mix_call(o_f, o_b, p, x, mod, gla_g, conv_w, w_out, *, bm):
    n = x.shape[0]
    kern = functools.partial(_mix_kernel, bm=bm)
    pcol = lambda k: pl.BlockSpec((bm, GLA_VW), lambda i, k=k: (i, k))
    const2 = lambda i: (0, 0)
    vmem = (2 * 2 * bm * GLA_VW * 4 + 2 * 4 * bm * GLA_VW * 2 + 2 * 2 * bm * D_MODEL * 4
            + D_MODEL * D_MODEL * (4 + 2) + bm * D_MODEL * 2 + 8 * bm * GLA_VW * 4
            + 4 * 1024 * 1024)
    return pl.pallas_call(
        kern,
        grid=(n // bm,),
        in_specs=[
            pl.BlockSpec((bm, GLA_VW), lambda i: (i, 0)),
            pl.BlockSpec((bm, GLA_VW), lambda i: (i, 0)),
            pcol(2), pcol(3), pcol(4), pcol(5),
            pl.BlockSpec((bm, D_MODEL), lambda i: (i, 0)),
            pl.BlockSpec((SUBLANES, D_MODEL), lambda i: (0, 2)),
            pl.BlockSpec((1, GLA_DV), const2),
            pl.BlockSpec((3, CONV_W), const2),
            pl.BlockSpec((D_MODEL, D_MODEL), const2, pipeline_mode=pl.Buffered(1)),
        ],
        out_specs=pl.BlockSpec((bm, D_MODEL), lambda i: (i, 0)),
        out_shape=jax.ShapeDtypeStruct((n, D_MODEL), F32),
        scratch_shapes=[pltpu.VMEM((bm, D_MODEL), BF16), pltpu.VMEM((D_MODEL, D_MODEL), BF16)],
        compiler_params=_params(1, vmem),
        name="mix",
    )(o_f, o_b, p, p, p, p, x, mod, gla_g, conv_w, w_out)


def _ffn_kernel(x_ref, g_ref, sh_ref, sc_ref, gt_ref, wg_ref, wu_ref, wd_ref, fg_ref,
                o_ref, h_ref, *, bm):
    j = pl.program_id(1)

    @pl.when(j == 0)
    def _():
        _norm_mod_rows(x_ref, g_ref[...], 1.0 + sc_ref[0:1, :], sh_ref[0:1, :], h_ref, bm)
        o_ref[...] = x_ref[...]

    h = h_ref[...]
    gate = jnp.dot(h, wg_ref[...], preferred_element_type=F32)
    up = jnp.dot(h, wu_ref[...], preferred_element_type=F32)
    part = jnp.dot((_silu(gate) * up).astype(BF16), wd_ref[...], preferred_element_type=F32)
    o_ref[...] += gt_ref[0:1, :] * part

    @pl.when(j == pl.num_programs(1) - 1)
    def _():
        fg = fg_ref[...]
        for r0 in range(0, bm, NORM_ROWS):
            rows = slice(r0, r0 + NORM_ROWS)
            xn = o_ref[rows, :]
            ms = jnp.mean(xn * xn, axis=-1, keepdims=True)
            o_ref[rows, :] = (xn * lax.rsqrt(ms + EPS)) * fg


def _ffn_call(x, g, mod, wg, wu, wd, fg, *, bm):
    n = x.shape[0]
    kern = functools.partial(_ffn_kernel, bm=bm)
    modspec = lambda k: pl.BlockSpec((SUBLANES, D_MODEL), lambda i, j, k=k: (0, k))
    vmem = (3 * bm * D_MODEL * 4 + bm * D_MODEL * 2 + 2 * 3 * D_MODEL * FFN_BH * 2
            + 4 * bm * FFN_BH * 4 + bm * D_MODEL * 4 + 4 * 1024 * 1024)
    return pl.pallas_call(
        kern,
        grid=(n // bm, FFN_HIDDEN // FFN_BH),
        in_specs=[
            pl.BlockSpec((bm, D_MODEL), lambda i, j: (i, 0), pipeline_mode=pl.Buffered(1)),
            pl.BlockSpec((1, D_MODEL), lambda i, j: (0, 0)),
            modspec(3), modspec(4), modspec(5),
            pl.BlockSpec((D_MODEL, FFN_BH), lambda i, j: (0, j)),
            pl.BlockSpec((D_MODEL, FFN_BH), lambda i, j: (0, j)),
            pl.BlockSpec((FFN_BH, D_MODEL), lambda i, j: (j, 0)),
            pl.BlockSpec((1, D_MODEL), lambda i, j: (0, 0)),
        ],
        out_specs=pl.BlockSpec((bm, D_MODEL), lambda i, j: (i, 0)),
        out_shape=jax.ShapeDtypeStruct((n, D_MODEL), F32),
        scratch_shapes=[pltpu.VMEM((bm, D_MODEL), BF16)],
        compiler_params=_params(2, vmem),
        name="ffn",
    )(x, g, mod, mod, mod, wg, wu, wd, fg)


def kernel(x, c, ctx, c_ctx, w_mod, b_mod, norm1_g, norm2_g, w_in, w_gate_f, b_gate_f,
           w_gate_b, b_gate_b, gla_norm_g, conv_w, w_out, w_ffn_gate, w_ffn_up, w_ffn_down,
           final_g):
    assert x.shape[0] == 1 and w_mod.shape[0] == 1, "batch 1, depth 1 only"
    seq, ctx_len = x.shape[1], ctx.shape[1]
    x2, ctx2 = x[0], ctx[0]

    c_t = jnp.concatenate(
        [c[0][:, None], c_ctx[:, None], jnp.zeros((D_MODEL, SUBLANES - 2), F32)], axis=1)
    mod = _mod_call(c_t, w_mod[0], b_mod)

    pad_f = jnp.zeros((LANES, GLA_KW), F32).at[:GATE_RANK].set(w_gate_f[0]).astype(BF16)
    pad_b = jnp.zeros((LANES, GLA_KW), F32).at[GATE_RANK:2 * GATE_RANK].set(w_gate_b[0]).astype(BF16)
    g1 = norm1_g[0][None]
    w_in_t = jnp.transpose(w_in[0])

    h_c, l_c = _prenorm_call(ctx2, g1, mod, w_in_t, row=1, bm=ctx_len)
    p_c = _in_proj_call(h_c, w_in_t, bm=ctx_len, n_tiles=2)
    zero_state = jnp.zeros((GLA_HEADS, GLA_DK, GLA_DV), F32)
    _, _, s_f, s_b = _gla_call(p_c, l_c, pad_f, b_gate_f, pad_b, b_gate_b, zero_state,
                               zero_state, tb=ctx_len)

    h_x, l_x = _prenorm_call(x2, g1, mod, w_in_t, row=0, bm=512)
    p_x = _in_proj_call(h_x, w_in_t, bm=1024, n_tiles=P_COLS // P_BN)
    o_f, o_b, _, _ = _gla_call(p_x, l_x, pad_f, b_gate_f, pad_b, b_gate_b, s_f, s_b, tb=256)
    x_mid = _mix_call(o_f, o_b, p_x, x2, mod, gla_norm_g, conv_w[0], w_out[0],
                      bm=256)
    out = _ffn_call(x_mid, norm2_g[0][None], mod, w_ffn_gate[0].astype(BF16),
                    w_ffn_up[0].astype(BF16), w_ffn_down[0].astype(BF16), final_g[None], bm=1024)
    return out[None]
```

```python
import functools

import jax
import jax.numpy as jnp
from jax import lax
from jax.experimental import pallas as pl
from jax.experimental.pallas import tpu as pltpu

F32 = jnp.float32
BF16 = jnp.bfloat16

D_MODEL = 2048
GRID_W = 64
GLA_HEADS = 4
GLA_DK = 128
GLA_DV = 256
GLA_KW = GLA_HEADS * GLA_DK
GLA_VW = GLA_HEADS * GLA_DV
GATE_RANK = 16
GATE_NORM = 16.0
CHUNK = 64
CONV_W = D_MODEL - GLA_VW
FFN_HIDDEN = 5632
EPS = 1e-6

LANES = 128
SUBLANES = 8
VMEM_BYTES = 64 * 1024 * 1024

GATE_COL = 2 * GLA_KW + 2 * GLA_VW
GATE_COLS = 2 * GATE_RANK
P_COLS = GATE_COL + 3 * CONV_W
P_BN = 1024
P_SHIFT_FROM = GATE_COL // P_BN

_NT_DIMS = (((1,), (1,)), ((), ()))

MOD_BN = 1024
FFN_BH = 512
NORM_ROWS = 128


def _params(grid_rank, vmem_bytes):
    return pltpu.CompilerParams(
        dimension_semantics=("arbitrary",) * grid_rank,
        vmem_limit_bytes=min(int(vmem_bytes), VMEM_BYTES - 6 * 1024 * 1024),
    )


def _silu(v):
    return v * jax.nn.sigmoid(v)


def _norm_mod_rows(x_ref, g, mul, sh, dst_ref, rows):
    for r0 in range(0, rows, NORM_ROWS):
        xs = x_ref[r0:r0 + NORM_ROWS, :]
        ms = jnp.mean(xs * xs, axis=-1, keepdims=True)
        y = xs * lax.rsqrt(ms + EPS)
        dst_ref[r0:r0 + NORM_ROWS, :] = ((y * g) * mul + sh).astype(BF16)


def _mod_kernel(c_ref, w_ref, b_ref, o_ref, sb_ref):
    @pl.when(pl.program_id(0) == 0)
    def _():
        s = _silu(c_ref[...])
        for r in range(2):
            sb_ref[r] = jnp.broadcast_to(s[:, r:r + 1], (D_MODEL, LANES))

    nt = MOD_BN // LANES

    def body(kk, accs):
        r0 = pl.multiple_of(kk * SUBLANES, SUBLANES)
        s0 = sb_ref[0, pl.ds(r0, SUBLANES), :]
        s1 = sb_ref[1, pl.ds(r0, SUBLANES), :]
        new = []
        for t in range(nt):
            w8 = w_ref[pl.ds(r0, SUBLANES), t * LANES:(t + 1) * LANES]
            new.append(accs[2 * t] + w8 * s0)
            new.append(accs[2 * t + 1] + w8 * s1)
        return tuple(new)

    zero = jnp.zeros((SUBLANES, LANES), F32)
    accs = lax.fori_loop(0, D_MODEL // SUBLANES, body, (zero,) * (2 * nt), unroll=8)
    o_ref[...] = jnp.zeros(o_ref.shape, F32)
    for t in range(nt):
        b = b_ref[:, t * LANES:(t + 1) * LANES]
        for r in range(2):
            o_ref[r:r + 1, t * LANES:(t + 1) * LANES] = (
                jnp.sum(accs[2 * t + r], axis=0, keepdims=True) + b)


def _mod_call(c_t, w_mod, b_mod):
    n = w_mod.shape[1]
    return pl.pallas_call(
        _mod_kernel,
        grid=(n // MOD_BN,),
        in_specs=[
            pl.BlockSpec((D_MODEL, SUBLANES), lambda j: (0, 0)),
            pl.BlockSpec((D_MODEL, MOD_BN), lambda j: (0, j)),
            pl.BlockSpec((1, MOD_BN), lambda j: (0, j)),
        ],
        out_specs=pl.BlockSpec((SUBLANES, MOD_BN), lambda j: (0, j)),
        out_shape=jax.ShapeDtypeStruct((SUBLANES, n), F32),
        scratch_shapes=[pltpu.VMEM((2, D_MODEL, LANES), F32)],
        compiler_params=_params(1, 2 * D_MODEL * MOD_BN * 4 + 8 * 1024 * 1024),
        name="mod",
    )(c_t, w_mod, b_mod)


def _prenorm_kernel(x_ref, g_ref, sh_ref, sc_ref, wl_ref, h_ref, l_ref, *, row, bm):
    _norm_mod_rows(x_ref, g_ref[...], 1.0 + sc_ref[row:row + 1, :], sh_ref[row:row + 1, :],
                   h_ref, bm)
    l_ref[...] = lax.dot_general(h_ref[...], wl_ref[...].astype(BF16), _NT_DIMS,
                                 preferred_element_type=F32).astype(BF16)


def _prenorm_call(x, g, mod, w_in_t, *, row, bm):
    m = x.shape[0]
    kern = functools.partial(_prenorm_kernel, row=row, bm=bm)
    vmem = (2 * bm * D_MODEL * (4 + 2) + 4 * D_MODEL * LANES * 4 + 8 * NORM_ROWS * D_MODEL * 4
            + 4 * 1024 * 1024)
    return pl.pallas_call(
        kern,
        grid=(m // bm,),
        in_specs=[
            pl.BlockSpec((bm, D_MODEL), lambda i: (i, 0)),
            pl.BlockSpec((1, D_MODEL), lambda i: (0, 0)),
            pl.BlockSpec((SUBLANES, D_MODEL), lambda i: (0, 0)),
            pl.BlockSpec((SUBLANES, D_MODEL), lambda i: (0, 1)),
            pl.BlockSpec((LANES, D_MODEL), lambda i: (GATE_COL // LANES, 0)),
        ],
        out_specs=[pl.BlockSpec((bm, D_MODEL), lambda i: (i, 0)),
                   pl.BlockSpec((bm, LANES), lambda i: (i, 0))],
        out_shape=[jax.ShapeDtypeStruct((m, D_MODEL), BF16),
                   jax.ShapeDtypeStruct((m, LANES), BF16)],
        compiler_params=_params(1, vmem),
        name="prenorm",
    )(x, g, mod, mod, w_in_t)


W_ROWS = 256


class _CastJob:
    def __init__(self, sources, rows, stage_ref, out_ref, sem_in, sem_out):
        self.sources, self.rows = sources, rows
        self.stage_ref, self.out_ref, self.sem_in, self.sem_out = stage_ref, out_ref, sem_in, sem_out
        self.n_blocks = sum(n for _, _, n in sources)

    def _for_block(self, c, fn):
        first = 0
        for src, dst, n in self.sources:
            @pl.when((c >= first) & (c < first + n))
            def _(src=src, dst=dst, first=first):
                fn(src, dst, pl.multiple_of((c - first) * self.rows, self.rows), lax.rem(c, 2))
            first += n

    def _fetch(self, src, dst, r, slot):
        return pltpu.make_async_copy(src.at[pl.ds(r, self.rows), :], self.stage_ref.at[slot],
                                     self.sem_in.at[slot])

    def _store(self, src, dst, r, slot):
        return pltpu.make_async_copy(self.out_ref, dst.at[pl.ds(r, self.rows), :],
                                     self.sem_out.at[0])

    def before(self, step):
        @pl.when(step == 0)
        def _():
            self._for_block(step, lambda *a: self._fetch(*a).start())
        self._for_block(step, lambda *a: self._fetch(*a).wait())
        self._for_block(step - 1, lambda *a: self._store(*a).wait())

    def cast(self, step):
        self.out_ref[...] = self.stage_ref[lax.rem(step, 2)].astype(BF16)

    def after(self, step):
        self._for_block(step, lambda *a: self._store(*a).start())
        self._for_block(step + 1, lambda *a: self._fetch(*a).start())


def _cast_scratch(rows, cols):
    return [pltpu.VMEM((2, rows, cols), F32), pltpu.VMEM((rows, cols), BF16),
            pltpu.SemaphoreType.DMA((2,)), pltpu.SemaphoreType.DMA((1,))]


IN_PROJ_CAST_ROWS = 128
GLA_CAST_ROWS = 256


def _in_proj_kernel(h_ref, wa_ref, wb_ref, *refs, cast):
    j = pl.program_id(0)
    if cast:
        wg_ref, wu_ref, o_ref, wg_out, wu_out, w_ref, stage_ref, cast_ref, sem_in, sem_out = refs
        n = D_MODEL // IN_PROJ_CAST_ROWS
        job = _CastJob(((wg_ref, wg_out, n), (wu_ref, wu_out, n)), IN_PROJ_CAST_ROWS,
                       stage_ref, cast_ref, sem_in, sem_out)
        step = j * pl.num_programs(1) + pl.program_id(1)
        job.before(step)
    else:
        o_ref, w_ref = refs

    @pl.when(pl.program_id(1) == 0)
    def _():
        @pl.when(j < P_SHIFT_FROM)
        def _():
            for r0 in range(0, P_BN, W_ROWS):
                w_ref[r0:r0 + W_ROWS, :] = wa_ref[r0:r0 + W_ROWS, :].astype(BF16)

        @pl.when(j >= P_SHIFT_FROM)
        def _():
            keep = P_BN - GATE_COLS
            for r0 in range(0, keep, W_ROWS):
                r1 = min(r0 + W_ROWS, keep)
                w_ref[r0:r1, :] = wa_ref[GATE_COLS + r0:GATE_COLS + r1, :].astype(BF16)
            w_ref[keep:, :] = wb_ref[...].astype(BF16)

    if cast:
        job.cast(step)
    o_ref[...] = lax.dot_general(h_ref[...], w_ref[...], _NT_DIMS,
                                 preferred_element_type=F32).astype(BF16)
    if cast:
        job.after(step)


def _in_proj_call(h, w_in_t, cast_weights=None, *, bm, n_tiles):
    m = h.shape[0]
    cast = cast_weights is not None
    vmem = (2 * bm * D_MODEL * 2 + 2 * (P_BN + GATE_COLS) * D_MODEL * 4 + P_BN * D_MODEL * 2
            + 2 * bm * P_BN * 2 + bm * P_BN * 4 + 4 * W_ROWS * D_MODEL * 4 + 4 * 1024 * 1024)
    in_specs = [
        pl.BlockSpec((bm, D_MODEL), lambda j, i: (i, 0)),
        pl.BlockSpec((P_BN, D_MODEL), lambda j, i: (j, 0)),
        pl.BlockSpec((GATE_COLS, D_MODEL), lambda j, i: ((j + 1) * (P_BN // GATE_COLS), 0)),
    ]
    out_specs = [pl.BlockSpec((bm, P_BN), lambda j, i: (i, j))]
    out_shape = [jax.ShapeDtypeStruct((m, n_tiles * P_BN), BF16)]
    scratch = [pltpu.VMEM((P_BN, D_MODEL), BF16)]
    operands = [h, w_in_t, w_in_t]
    if cast:
        assert n_tiles * (m // bm) > 2 * D_MODEL // IN_PROJ_CAST_ROWS
        any_spec = pl.BlockSpec(memory_space=pl.ANY)
        in_specs += [any_spec, any_spec]
        out_specs += [any_spec, any_spec]
        out_shape += [jax.ShapeDtypeStruct(w.shape, BF16) for w in cast_weights]
        scratch += _cast_scratch(IN_PROJ_CAST_ROWS, FFN_HIDDEN)
        operands += list(cast_weights)
        vmem += IN_PROJ_CAST_ROWS * FFN_HIDDEN * (2 * 4 + 2 * 2)
    return pl.pallas_call(
        functools.partial(_in_proj_kernel, cast=cast),
        grid=(n_tiles, m // bm),
        in_specs=in_specs,
        out_specs=out_specs,
        out_shape=out_shape,
        scratch_shapes=scratch,
        compiler_params=_params(2, vmem),
        name="in_proj",
    )(*operands)


def _gla_log_decay_split(l_ref, wg_ref, bg_ref):
    z = jnp.dot(l_ref[...], wg_ref[...], preferred_element_type=F32) + bg_ref[...]
    log_a = (jnp.minimum(z, 0.0) - jnp.log(1.0 + jnp.exp(-jnp.abs(z)))) * (1.0 / GATE_NORM)
    hi = log_a.astype(BF16)
    lo = (log_a - hi.astype(F32)).astype(BF16)
    return hi, lo


def _gla_chunk_cumsum(pieces, *, reverse, tb):
    r = lax.broadcasted_iota(jnp.int32, (tb, tb), 0)
    cc = lax.broadcasted_iota(jnp.int32, (tb, tb), 1)
    same_chunk = (r // CHUNK) == (cc // CHUNK)
    tri = jnp.where(same_chunk & ((cc >= r) if reverse else (cc <= r)), 1.0, 0.0).astype(BF16)
    hi, lo = pieces
    return (jnp.dot(tri, hi, preferred_element_type=F32)
            + jnp.dot(tri, lo, preferred_element_type=F32))


def _gla_kernel(qf, kf, vf, lf, lf_next, qb, kb, vb, lb, lb_next, wgf, bgf, wgb, bgb, s0f, s0b,
                *refs, tb, cast):
    i = pl.program_id(0)
    if cast:
        (wd_ref, of, ob, sf, sb, wd_out, s_ref, cum_ref,
         stage_ref, cast_ref, sem_in, sem_out) = refs
        job = _CastJob(((wd_ref, wd_out, FFN_HIDDEN // GLA_CAST_ROWS),), GLA_CAST_ROWS,
                       stage_ref, cast_ref, sem_in, sem_out)
        job.before(i)
    else:
        of, ob, sf, sb, s_ref, cum_ref = refs

    @pl.when(i == 0)
    def _():
        s_ref[0] = s0f[...]
        s_ref[1] = s0b[...]
        cum_ref[0] = _gla_chunk_cumsum(_gla_log_decay_split(lf, wgf, bgf), reverse=False, tb=tb)
        cum_ref[1] = _gla_chunk_cumsum(_gla_log_decay_split(lb, wgb, bgb), reverse=True, tb=tb)

    if cast:
        job.cast(i)
    next_f = _gla_log_decay_split(lf_next, wgf, bgf)
    next_b = _gla_log_decay_split(lb_next, wgb, bgb)
    dirs = ((qf, kf, vf, of, False), (qb, kb, vb, ob, True))
    ii = lax.broadcasted_iota(jnp.int32, (CHUNK, CHUNK), 0)
    jj = lax.broadcasted_iota(jnp.int32, (CHUNK, CHUNK), 1)
    nc = tb // CHUNK

    def start(step):
        items = []
        for d, (q_ref, k_ref, v_ref, o_ref, reverse) in enumerate(dirs):
            r0 = (nc - 1 - step if reverse else step) * CHUNK
            cum_c = cum_ref[d, r0:r0 + CHUNK, :]
            last = cum_c[0:1, :] if reverse else cum_c[CHUNK - 1:CHUNK, :]
            q_c = q_ref[r0:r0 + CHUNK, :].astype(F32)
            k_c = k_ref[r0:r0 + CHUNK, :].astype(F32)
            q_e = ((q_c * (GLA_DK ** -0.5)) * jnp.exp(cum_c)).astype(BF16)
            k_e = (k_c * jnp.exp(-cum_c)).astype(BF16)
            k_d = (k_c * jnp.exp(last - cum_c)).astype(BF16)
            decay = jnp.exp(last)
            mask = (jj >= ii) if reverse else (jj <= ii)
            for h in range(GLA_HEADS):
                ks = slice(h * GLA_DK, (h + 1) * GLA_DK)
                vs = slice(h * GLA_DV, (h + 1) * GLA_DV)
                items.append(dict(
                    d=d, h=h, r0=r0, vs=vs, o_ref=o_ref, mask=mask,
                    q_e=q_e[:, ks], k_e=k_e[:, ks], k_d=k_d[:, ks], decay=decay[:, ks],
                    v=v_ref[r0:r0 + CHUNK, vs]))
        for it in items:
            it["scores"] = lax.dot_general(it["q_e"], it["k_e"], _NT_DIMS,
                                           preferred_element_type=F32)
        return items

    items = start(0)
    next_cum = None
    for step in range(nc):
        following = start(step + 1) if step + 1 < nc else None
        if step == nc // 2:
            next_cum = (_gla_chunk_cumsum(next_f, reverse=False, tb=tb),
                        _gla_chunk_cumsum(next_b, reverse=True, tb=tb))
        for it in items:
            scores = jnp.where(it["mask"], it["scores"], 0.0).astype(BF16)
            s_old = s_ref[it["d"], it["h"]]
            it["o_ref"][it["r0"]:it["r0"] + CHUNK, it["vs"]] = (
                jnp.dot(scores, it["v"], preferred_element_type=F32)
                + jnp.dot(it["q_e"], s_old.astype(BF16), preferred_element_type=F32))
            ds = lax.dot_general(it["k_d"], it["v"], (((0,), (0,)), ((), ())),
                                 preferred_element_type=F32)
            dcol = jnp.transpose(jnp.broadcast_to(it["decay"], (SUBLANES, GLA_DK)))[:, 0:1]
            s_ref[it["d"], it["h"]] = dcol * s_old + ds
        items = following
    cum_ref[0] = next_cum[0]
    cum_ref[1] = next_cum[1]
    if cast:
        job.after(i)

    @pl.when(i == pl.num_programs(0) - 1)
    def _():
        sf[...] = s_ref[0]
        sb[...] = s_ref[1]


def _gla_call(p, l, wgf, bgf, wgb, bgb, s0f, s0b, cast_weight=None, *, tb):
    n = p.shape[0]
    nb = n // tb
    fwd = lambda i: i
    bwd = lambda i: nb - 1 - i
    fwd_next = lambda i: jnp.minimum(i + 1, nb - 1)
    bwd_next = lambda i: jnp.maximum(nb - 2 - i, 0)

    def specs(pos, pos_next):
        return [
            pl.BlockSpec((tb, GLA_KW), lambda i: (pos(i), 0)),
            pl.BlockSpec((tb, GLA_KW), lambda i: (pos(i), 1)),
            pl.BlockSpec((tb, GLA_VW), lambda i: (pos(i), 1)),
            pl.BlockSpec((tb, LANES), lambda i: (pos(i), 0)),
            pl.BlockSpec((tb, LANES), lambda i: (pos_next(i), 0)),
        ]

    const2 = lambda i: (0, 0)
    const3 = lambda i: (0, 0, 0)
    state_shape = (GLA_HEADS, GLA_DK, GLA_DV)
    state_spec = pl.BlockSpec(state_shape, const3)
    cast = cast_weight is not None
    kern = functools.partial(_gla_kernel, tb=tb, cast=cast)
    vmem = (2 * 2 * tb * (2 * GLA_KW + GLA_VW + LANES) * 2 + 2 * 2 * tb * GLA_VW * 4
            + 10 * GLA_HEADS * GLA_DK * GLA_DV * 4 + 24 * tb * GLA_KW * 4 + 8 * 1024 * 1024)
    in_specs = specs(fwd, fwd_next) + specs(bwd, bwd_next) + [
        pl.BlockSpec((LANES, GLA_KW), const2), pl.BlockSpec((1, GLA_KW), const2),
        pl.BlockSpec((LANES, GLA_KW), const2), pl.BlockSpec((1, GLA_KW), const2),
        state_spec, state_spec,
    ]
    out_specs = [
        pl.BlockSpec((tb, GLA_VW), lambda i: (i, 0)),
        pl.BlockSpec((tb, GLA_VW), lambda i: (nb - 1 - i, 0)),
        state_spec, state_spec,
    ]
    out_shape = [
        jax.ShapeDtypeStruct((n, GLA_VW), F32), jax.ShapeDtypeStruct((n, GLA_VW), F32),
        jax.ShapeDtypeStruct(state_shape, F32), jax.ShapeDtypeStruct(state_shape, F32),
    ]
    scratch = [pltpu.VMEM((2,) + state_shape, F32), pltpu.VMEM((2, tb, GLA_KW), F32)]
    operands = [p, p, p, l, l, p, p, p, l, l, wgf, bgf, wgb, bgb, s0f, s0b]
    if cast:
        assert nb > FFN_HIDDEN // GLA_CAST_ROWS
        any_spec = pl.BlockSpec(memory_space=pl.ANY)
        in_specs.append(any_spec)
        out_specs.append(any_spec)
        out_shape.append(jax.ShapeDtypeStruct(cast_weight.shape, BF16))
        scratch += _cast_scratch(GLA_CAST_ROWS, D_MODEL)
        operands.append(cast_weight)
        vmem += GLA_CAST_ROWS * D_MODEL * (2 * 4 + 2 * 2)
    return pl.pallas_call(
        kern,
        grid=(nb,),
        in_specs=in_specs,
        out_specs=out_specs,
        out_shape=out_shape,
        scratch_shapes=scratch,
        compiler_params=_params(1, vmem),
        name="gla",
    )(*operands)


def _mix_kernel(of_ref, ob_ref, g_ref, cb_ref, cc_ref, cx_ref, x_ref, gt_ref, gg_ref, cw_ref,
                wo_ref, o_ref, m_ref, w_ref, *, bm):
    @pl.when(pl.program_id(0) == 0)
    def _():
        for r0 in range(0, D_MODEL, W_ROWS):
            w_ref[r0:r0 + W_ROWS, :] = wo_ref[r0:r0 + W_ROWS, :].astype(BF16)

    gg = gg_ref[...]
    for h in range(GLA_HEADS):
        vs = slice(h * GLA_DV, (h + 1) * GLA_DV)
        o = of_ref[:, vs] + ob_ref[:, vs]
        o32 = o * lax.rsqrt(jnp.mean(o * o, axis=-1, keepdims=True) + EPS)
        m_ref[:, vs] = ((o32 * gg) * _silu(g_ref[:, vs].astype(F32))).astype(BF16)

    u = cc_ref[...].astype(F32) * cx_ref[...].astype(F32)
    col = lax.broadcasted_iota(jnp.int32, u.shape, 0) % GRID_W
    u_prev = jnp.where(col == 0, 0.0, pltpu.roll(u, 1, 0))
    u_next = jnp.where(col == GRID_W - 1, 0.0, pltpu.roll(u, bm - 1, 0))
    conv = cw_ref[0:1, :] * u_prev + cw_ref[1:2, :] * u + cw_ref[2:3, :] * u_next
    m_ref[:, GLA_VW:] = (cb_ref[...].astype(F32) * conv).astype(BF16)

    mixed = jnp.dot(m_ref[...], w_ref[...], preferred_element_type=F32)
    o_ref[...] = x_ref[...] + gt_ref[0:1, :] * mixed


def _mix_call(o_f, o_b, p, x, mod, gla_g, conv_w, w_out, *, bm):
    n = x.shape[0]
    kern = functools.partial(_mix_kernel, bm=bm)
    pcol = lambda k: pl.BlockSpec((bm, GLA_VW), lambda i, k=k: (i, k))
    const2 = lambda i: (0, 0)
    vmem = (2 * 2 * bm * GLA_VW * 4 + 2 * 4 * bm * GLA_VW * 2 + 2 * 2 * bm * D_MODEL * 4
            + D_MODEL * D_MODEL * (4 + 2) + bm * D_MODEL * 2 + 8 * bm * GLA_VW * 4
            + 4 * 1024 * 1024)
    return pl.pallas_call(
        kern,
        grid=(n // bm,),
        in_specs=[
            pl.BlockSpec((bm, GLA_VW), lambda i: (i, 0)),
            pl.BlockSpec((bm, GLA_VW), lambda i: (i, 0)),
            pcol(2), pcol(3), pcol(4), pcol(5),
            pl.BlockSpec((bm, D_MODEL), lambda i: (i, 0)),
            pl.BlockSpec((SUBLANES, D_MODEL), lambda i: (0, 2)),
            pl.BlockSpec((1, GLA_DV), const2),
            pl.BlockSpec((3, CONV_W), const2),
            pl.BlockSpec((D_MODEL, D_MODEL), const2, pipeline_mode=pl.Buffered(1)),
        ],
        out_specs=pl.BlockSpec((bm, D_MODEL), lambda i: (i, 0)),
        out_shape=jax.ShapeDtypeStruct((n, D_MODEL), F32),
        scratch_shapes=[pltpu.VMEM((bm, D_MODEL), BF16), pltpu.VMEM((D_MODEL, D_MODEL), BF16)],
        compiler_params=_params(1, vmem),
        name="mix",
    )(o_f, o_b, p, p, p, p, x, mod, gla_g, conv_w, w_out)


def _ffn_kernel(x_ref, g_ref, sh_ref, sc_ref, gt_ref, wg_ref, wu_ref, wd_ref, fg_ref,
                o_ref, h_ref, *, bm):
    j = pl.program_id(1)

    @pl.when(j == 0)
    def _():
        _norm_mod_rows(x_ref, g_ref[...], 1.0 + sc_ref[0:1, :], sh_ref[0:1, :], h_ref, bm)
        o_ref[...] = x_ref[...]

    h = h_ref[...]
    gate = jnp.dot(h, wg_ref[...], preferred_element_type=F32)
    up = jnp.dot(h, wu_ref[...], preferred_element_type=F32)
    part = jnp.dot((_silu(gate) * up).astype(BF16), wd_ref[...], preferred_element_type=F32)
    o_ref[...] += gt_ref[0:1, :] * part

    @pl.when(j == pl.num_programs(1) - 1)
    def _():
        fg = fg_ref[...]
        for r0 in range(0, bm, NORM_ROWS):
            rows = slice(r0, r0 + NORM_ROWS)
            xn = o_ref[rows, :]
            ms = jnp.mean(xn * xn, axis=-1, keepdims=True)
            o_ref[rows, :] = (xn * lax.rsqrt(ms + EPS)) * fg


def _ffn_call(x, g, mod, wg, wu, wd, fg, *, bm):
    n = x.shape[0]
    kern = functools.partial(_ffn_kernel, bm=bm)
    modspec = lambda k: pl.BlockSpec((SUBLANES, D_MODEL), lambda i, j, k=k: (0, k))
    vmem = (3 * bm * D_MODEL * 4 + bm * D_MODEL * 2 + 2 * 3 * D_MODEL * FFN_BH * 2
            + 4 * bm * FFN_BH * 4 + bm * D_MODEL * 4 + 4 * 1024 * 1024)
    return pl.pallas_call(
        kern,
        grid=(n // bm, FFN_HIDDEN // FFN_BH),
        in_specs=[
            pl.BlockSpec((bm, D_MODEL), lambda i, j: (i, 0), pipeline_mode=pl.Buffered(1)),
            pl.BlockSpec((1, D_MODEL), lambda i, j: (0, 0)),
            modspec(3), modspec(4), modspec(5),
            pl.BlockSpec((D_MODEL, FFN_BH), lambda i, j: (0, j)),
            pl.BlockSpec((D_MODEL, FFN_BH), lambda i, j: (0, j)),
            pl.BlockSpec((FFN_BH, D_MODEL), lambda i, j: (j, 0)),
            pl.BlockSpec((1, D_MODEL), lambda i, j: (0, 0)),
        ],
        out_specs=pl.BlockSpec((bm, D_MODEL), lambda i, j: (i, 0)),
        out_shape=jax.ShapeDtypeStruct((n, D_MODEL), F32),
        scratch_shapes=[pltpu.VMEM((bm, D_MODEL), BF16)],
        compiler_params=_params(2, vmem),
        name="ffn",
    )(x, g, mod, mod, mod, wg, wu, wd, fg)


def kernel(x, c, ctx, c_ctx, w_mod, b_mod, norm1_g, norm2_g, w_in, w_gate_f, b_gate_f,
           w_gate_b, b_gate_b, gla_norm_g, conv_w, w_out, w_ffn_gate, w_ffn_up, w_ffn_down,
           final_g):
    assert x.shape[0] == 1 and w_mod.shape[0] == 1, "batch 1, depth 1 only"
    seq, ctx_len = x.shape[1], ctx.shape[1]
    x2, ctx2 = x[0], ctx[0]

    c_t = jnp.concatenate(
        [c[0][:, None], c_ctx[:, None], jnp.zeros((D_MODEL, SUBLANES - 2), F32)], axis=1)
    mod = _mod_call(c_t, w_mod[0], b_mod)

    pad_f = jnp.zeros((LANES, GLA_KW), F32).at[:GATE_RANK].set(w_gate_f[0]).astype(BF16)
    pad_b = jnp.zeros((LANES, GLA_KW), F32).at[GATE_RANK:2 * GATE_RANK].set(w_gate_b[0]).astype(BF16)
    g1 = norm1_g[0][None]
    w_in_t = jnp.transpose(w_in[0])

    h_c, l_c = _prenorm_call(ctx2, g1, mod, w_in_t, row=1, bm=ctx_len)
    p_c, = _in_proj_call(h_c, w_in_t, bm=ctx_len, n_tiles=2)
    zero_state = jnp.zeros((GLA_HEADS, GLA_DK, GLA_DV), F32)
    _, _, s_f, s_b = _gla_call(p_c, l_c, pad_f, b_gate_f, pad_b, b_gate_b, zero_state,
                               zero_state, tb=ctx_len)

    h_x, l_x = _prenorm_call(x2, g1, mod, w_in_t, row=0, bm=512)
    p_x, wg_b, wu_b = _in_proj_call(h_x, w_in_t, (w_ffn_gate[0], w_ffn_up[0]), bm=1024,
                                    n_tiles=P_COLS // P_BN)
    o_f, o_b, _, _, wd_b = _gla_call(p_x, l_x, pad_f, b_gate_f, pad_b, b_gate_b, s_f, s_b,
                                     w_ffn_down[0], tb=256)
    x_mid = _mix_call(o_f, o_b, p_x, x2, mod, gla_norm_g, conv_w[0], w_out[0],
                      bm=256)
    out = _ffn_call(x_mid, norm2_g[0][None], mod, wg_b, wu_b, wd_b, final_g[None], bm=1024)
    return out[None]
```

```python
import functools

import jax
import jax.numpy as jnp
from jax import lax
from jax.experimental import pallas as pl
from jax.experimental.pallas import tpu as pltpu

F32 = jnp.float32
BF16 = jnp.bfloat16

D_MODEL = 2048
GRID_W = 64
GLA_HEADS = 4
GLA_DK = 128
GLA_DV = 256
GLA_KW = GLA_HEADS * GLA_DK
GLA_VW = GLA_HEADS * GLA_DV
GATE_RANK = 16
GATE_NORM = 16.0
CHUNK = 64
CONV_W = D_MODEL - GLA_VW
FFN_HIDDEN = 5632
EPS = 1e-6

LANES = 128
SUBLANES = 8
VMEM_BYTES = 64 * 1024 * 1024

GATE_COL = 2 * GLA_KW + 2 * GLA_VW
GATE_COLS = 2 * GATE_RANK
P_COLS = GATE_COL + 3 * CONV_W
P_BN = 1024
P_SHIFT_FROM = GATE_COL // P_BN

_NT_DIMS = (((1,), (1,)), ((), ()))

MOD_BN = 1024
FFN_BH = 512
NORM_ROWS = 128


def _params(grid_rank, vmem_bytes):
    return pltpu.CompilerParams(
        dimension_semantics=("arbitrary",) * grid_rank,
        vmem_limit_bytes=min(int(vmem_bytes), VMEM_BYTES - 6 * 1024 * 1024),
    )


def _silu(v):
    return v * jax.nn.sigmoid(v)


def _norm_mod_rows(x_ref, g, mul, sh, dst_ref, rows):
    for r0 in range(0, rows, NORM_ROWS):
        xs = x_ref[r0:r0 + NORM_ROWS, :]
        ms = jnp.mean(xs * xs, axis=-1, keepdims=True)
        y = xs * lax.rsqrt(ms + EPS)
        dst_ref[r0:r0 + NORM_ROWS, :] = ((y * g) * mul + sh).astype(BF16)


def _mod_kernel(c_ref, w_ref, b_ref, o_ref, sb_ref):
    @pl.when(pl.program_id(0) == 0)
    def _():
        s = _silu(c_ref[...])
        for r in range(2):
            sb_ref[r] = jnp.broadcast_to(s[:, r:r + 1], (D_MODEL, LANES))

    nt = MOD_BN // LANES

    def body(kk, accs):
        r0 = pl.multiple_of(kk * SUBLANES, SUBLANES)
        s0 = sb_ref[0, pl.ds(r0, SUBLANES), :]
        s1 = sb_ref[1, pl.ds(r0, SUBLANES), :]
        new = []
        for t in range(nt):
            w8 = w_ref[pl.ds(r0, SUBLANES), t * LANES:(t + 1) * LANES]
            new.append(accs[2 * t] + w8 * s0)
            new.append(accs[2 * t + 1] + w8 * s1)
        return tuple(new)

    zero = jnp.zeros((SUBLANES, LANES), F32)
    accs = lax.fori_loop(0, D_MODEL // SUBLANES, body, (zero,) * (2 * nt), unroll=8)
    o_ref[...] = jnp.zeros(o_ref.shape, F32)
    for t in range(nt):
        b = b_ref[:, t * LANES:(t + 1) * LANES]
        for r in range(2):
            o_ref[r:r + 1, t * LANES:(t + 1) * LANES] = (
                jnp.sum(accs[2 * t + r], axis=0, keepdims=True) + b)


def _mod_call(c_t, w_mod, b_mod):
    n = w_mod.shape[1]
    return pl.pallas_call(
        _mod_kernel,
        grid=(n // MOD_BN,),
        in_specs=[
            pl.BlockSpec((D_MODEL, SUBLANES), lambda j: (0, 0)),
            pl.BlockSpec((D_MODEL, MOD_BN), lambda j: (0, j)),
            pl.BlockSpec((1, MOD_BN), lambda j: (0, j)),
        ],
        out_specs=pl.BlockSpec((SUBLANES, MOD_BN), lambda j: (0, j)),
        out_shape=jax.ShapeDtypeStruct((SUBLANES, n), F32),
        scratch_shapes=[pltpu.VMEM((2, D_MODEL, LANES), F32)],
        compiler_params=_params(1, 2 * D_MODEL * MOD_BN * 4 + 8 * 1024 * 1024),
        name="mod",
    )(c_t, w_mod, b_mod)


def _prenorm_kernel(x_ref, g_ref, sh_ref, sc_ref, wl_ref, h_ref, l_ref, *, row, bm):
    _norm_mod_rows(x_ref, g_ref[...], 1.0 + sc_ref[row:row + 1, :], sh_ref[row:row + 1, :],
                   h_ref, bm)
    l_ref[...] = lax.dot_general(h_ref[...], wl_ref[...].astype(BF16), _NT_DIMS,
                                 preferred_element_type=F32).astype(BF16)


def _prenorm_call(x, g, mod, w_in_t, *, row, bm):
    m = x.shape[0]
    kern = functools.partial(_prenorm_kernel, row=row, bm=bm)
    vmem = (2 * bm * D_MODEL * (4 + 2) + 4 * D_MODEL * LANES * 4 + 8 * NORM_ROWS * D_MODEL * 4
            + 4 * 1024 * 1024)
    return pl.pallas_call(
        kern,
        grid=(m // bm,),
        in_specs=[
            pl.BlockSpec((bm, D_MODEL), lambda i: (i, 0)),
            pl.BlockSpec((1, D_MODEL), lambda i: (0, 0)),
            pl.BlockSpec((SUBLANES, D_MODEL), lambda i: (0, 0)),
            pl.BlockSpec((SUBLANES, D_MODEL), lambda i: (0, 1)),
            pl.BlockSpec((LANES, D_MODEL), lambda i: (GATE_COL // LANES, 0)),
        ],
        out_specs=[pl.BlockSpec((bm, D_MODEL), lambda i: (i, 0)),
                   pl.BlockSpec((bm, LANES), lambda i: (i, 0))],
        out_shape=[jax.ShapeDtypeStruct((m, D_MODEL), BF16),
                   jax.ShapeDtypeStruct((m, LANES), BF16)],
        compiler_params=_params(1, vmem),
        name="prenorm",
    )(x, g, mod, mod, w_in_t)


W_ROWS = 256


class _CastJob:
    def __init__(self, sources, rows, stage_ref, out_ref, sem_in, sem_out):
        self.sources, self.rows = sources, rows
        self.stage_ref, self.out_ref, self.sem_in, self.sem_out = stage_ref, out_ref, sem_in, sem_out
        self.n_blocks = sum(n for _, _, n in sources)

    def _for_block(self, c, fn):
        first = 0
        for src, dst, n in self.sources:
            @pl.when((c >= first) & (c < first + n))
            def _(src=src, dst=dst, first=first):
                fn(src, dst, pl.multiple_of((c - first) * self.rows, self.rows), lax.rem(c, 2))
            first += n

    def _fetch(self, src, dst, r, slot):
        return pltpu.make_async_copy(src.at[pl.ds(r, self.rows), :], self.stage_ref.at[slot],
                                     self.sem_in.at[slot])

    def _store(self, src, dst, r, slot):
        return pltpu.make_async_copy(self.out_ref.at[slot], dst.at[pl.ds(r, self.rows), :],
                                     self.sem_out.at[slot])

    def before(self, step):
        @pl.when(step == 0)
        def _():
            self._for_block(step, lambda *a: self._fetch(*a).start())
        self._for_block(step, lambda *a: self._fetch(*a).wait())
        self._for_block(step + 1, lambda *a: self._fetch(*a).start())
        self._for_block(step - 2, lambda *a: self._store(*a).wait())

    def cast(self, step):
        slot = lax.rem(step, 2)
        self.out_ref[slot] = self.stage_ref[slot].astype(BF16)

    def after(self, step):
        self._for_block(step, lambda *a: self._store(*a).start())


def _cast_scratch(rows, cols):
    return [pltpu.VMEM((2, rows, cols), F32), pltpu.VMEM((2, rows, cols), BF16),
            pltpu.SemaphoreType.DMA((2,)), pltpu.SemaphoreType.DMA((2,))]


IN_PROJ_CAST_ROWS = 128
GLA_CAST_ROWS = 256


def _in_proj_kernel(h_ref, wa_ref, wb_ref, *refs, cast):
    j = pl.program_id(0)
    if cast:
        wg_ref, wu_ref, o_ref, wg_out, wu_out, w_ref, stage_ref, cast_ref, sem_in, sem_out = refs
        n = D_MODEL // IN_PROJ_CAST_ROWS
        job = _CastJob(((wg_ref, wg_out, n), (wu_ref, wu_out, n)), IN_PROJ_CAST_ROWS,
                       stage_ref, cast_ref, sem_in, sem_out)
        step = j * pl.num_programs(1) + pl.program_id(1)
        job.before(step)
    else:
        o_ref, w_ref = refs

    @pl.when(pl.program_id(1) == 0)
    def _():
        @pl.when(j < P_SHIFT_FROM)
        def _():
            for r0 in range(0, P_BN, W_ROWS):
                w_ref[r0:r0 + W_ROWS, :] = wa_ref[r0:r0 + W_ROWS, :].astype(BF16)

        @pl.when(j >= P_SHIFT_FROM)
        def _():
            keep = P_BN - GATE_COLS
            for r0 in range(0, keep, W_ROWS):
                r1 = min(r0 + W_ROWS, keep)
                w_ref[r0:r1, :] = wa_ref[GATE_COLS + r0:GATE_COLS + r1, :].astype(BF16)
            w_ref[keep:, :] = wb_ref[...].astype(BF16)

    if cast:
        job.cast(step)
    o_ref[...] = lax.dot_general(h_ref[...], w_ref[...], _NT_DIMS,
                                 preferred_element_type=F32).astype(BF16)
    if cast:
        job.after(step)


def _in_proj_call(h, w_in_t, cast_weights=None, *, bm, n_tiles):
    m = h.shape[0]
    cast = cast_weights is not None
    vmem = (2 * bm * D_MODEL * 2 + 2 * (P_BN + GATE_COLS) * D_MODEL * 4 + P_BN * D_MODEL * 2
            + 2 * bm * P_BN * 2 + bm * P_BN * 4 + 4 * W_ROWS * D_MODEL * 4 + 4 * 1024 * 1024)
    in_specs = [
        pl.BlockSpec((bm, D_MODEL), lambda j, i: (i, 0)),
        pl.BlockSpec((P_BN, D_MODEL), lambda j, i: (j, 0)),
        pl.BlockSpec((GATE_COLS, D_MODEL), lambda j, i: ((j + 1) * (P_BN // GATE_COLS), 0)),
    ]
    out_specs = [pl.BlockSpec((bm, P_BN), lambda j, i: (i, j))]
    out_shape = [jax.ShapeDtypeStruct((m, n_tiles * P_BN), BF16)]
    scratch = [pltpu.VMEM((P_BN, D_MODEL), BF16)]
    operands = [h, w_in_t, w_in_t]
    if cast:
        assert n_tiles * (m // bm) >= 2 * D_MODEL // IN_PROJ_CAST_ROWS + 2
        any_spec = pl.BlockSpec(memory_space=pl.ANY)
        in_specs += [any_spec, any_spec]
        out_specs += [any_spec, any_spec]
        out_shape += [jax.ShapeDtypeStruct(w.shape, BF16) for w in cast_weights]
        scratch += _cast_scratch(IN_PROJ_CAST_ROWS, FFN_HIDDEN)
        operands += list(cast_weights)
        vmem += IN_PROJ_CAST_ROWS * FFN_HIDDEN * (2 * 4 + 2 * 2)
    return pl.pallas_call(
        functools.partial(_in_proj_kernel, cast=cast),
        grid=(n_tiles, m // bm),
        in_specs=in_specs,
        out_specs=out_specs,
        out_shape=out_shape,
        scratch_shapes=scratch,
        compiler_params=_params(2, vmem),
        name="in_proj",
    )(*operands)


def _gla_log_decay_split(l_ref, wg_ref, bg_ref):
    z = jnp.dot(l_ref[...], wg_ref[...], preferred_element_type=F32) + bg_ref[...]
    log_a = (jnp.minimum(z, 0.0) - jnp.log(1.0 + jnp.exp(-jnp.abs(z)))) * (1.0 / GATE_NORM)
    hi = log_a.astype(BF16)
    lo = (log_a - hi.astype(F32)).astype(BF16)
    return hi, lo


def _gla_chunk_cumsum(pieces, *, reverse, tb):
    r = lax.broadcasted_iota(jnp.int32, (tb, tb), 0)
    cc = lax.broadcasted_iota(jnp.int32, (tb, tb), 1)
    same_chunk = (r // CHUNK) == (cc // CHUNK)
    tri = jnp.where(same_chunk & ((cc >= r) if reverse else (cc <= r)), 1.0, 0.0).astype(BF16)
    hi, lo = pieces
    return (jnp.dot(tri, hi, preferred_element_type=F32)
            + jnp.dot(tri, lo, preferred_element_type=F32))


def _gla_kernel(qf, kf, vf, lf, lf_next, qb, kb, vb, lb, lb_next, wgf, bgf, wgb, bgb, s0f, s0b,
                *refs, tb, cast):
    i = pl.program_id(0)
    if cast:
        (wd_ref, of, ob, sf, sb, wd_out, s_ref, cum_ref,
         stage_ref, cast_ref, sem_in, sem_out) = refs
        job = _CastJob(((wd_ref, wd_out, FFN_HIDDEN // GLA_CAST_ROWS),), GLA_CAST_ROWS,
                       stage_ref, cast_ref, sem_in, sem_out)
        job.before(i)
    else:
        of, ob, sf, sb, s_ref, cum_ref = refs

    @pl.when(i == 0)
    def _():
        s_ref[0] = s0f[...]
        s_ref[1] = s0b[...]
        cum_ref[0] = _gla_chunk_cumsum(_gla_log_decay_split(lf, wgf, bgf), reverse=False, tb=tb)
        cum_ref[1] = _gla_chunk_cumsum(_gla_log_decay_split(lb, wgb, bgb), reverse=True, tb=tb)

    if cast:
        job.cast(i)
    next_f = _gla_log_decay_split(lf_next, wgf, bgf)
    next_b = _gla_log_decay_split(lb_next, wgb, bgb)
    dirs = ((qf, kf, vf, of, False), (qb, kb, vb, ob, True))
    ii = lax.broadcasted_iota(jnp.int32, (CHUNK, CHUNK), 0)
    jj = lax.broadcasted_iota(jnp.int32, (CHUNK, CHUNK), 1)
    nc = tb // CHUNK

    def start(step):
        items = []
        for d, (q_ref, k_ref, v_ref, o_ref, reverse) in enumerate(dirs):
            r0 = (nc - 1 - step if reverse else step) * CHUNK
            cum_c = cum_ref[d, r0:r0 + CHUNK, :]
            last = cum_c[0:1, :] if reverse else cum_c[CHUNK - 1:CHUNK, :]
            q_c = q_ref[r0:r0 + CHUNK, :].astype(F32)
            k_c = k_ref[r0:r0 + CHUNK, :].astype(F32)
            q_e = ((q_c * (GLA_DK ** -0.5)) * jnp.exp(cum_c)).astype(BF16)
            k_e = (k_c * jnp.exp(-cum_c)).astype(BF16)
            k_d = (k_c * jnp.exp(last - cum_c)).astype(BF16)
            decay = jnp.exp(last)
            mask = (jj >= ii) if reverse else (jj <= ii)
            for h in range(GLA_HEADS):
                ks = slice(h * GLA_DK, (h + 1) * GLA_DK)
                vs = slice(h * GLA_DV, (h + 1) * GLA_DV)
                items.append(dict(
                    d=d, h=h, r0=r0, vs=vs, o_ref=o_ref, mask=mask,
                    q_e=q_e[:, ks], k_e=k_e[:, ks], k_d=k_d[:, ks], decay=decay[:, ks],
                    v=v_ref[r0:r0 + CHUNK, vs]))
        for it in items:
            it["scores"] = lax.dot_general(it["q_e"], it["k_e"], _NT_DIMS,
                                           preferred_element_type=F32)
        return items

    items = start(0)
    next_cum = None
    for step in range(nc):
        following = start(step + 1) if step + 1 < nc else None
        if step == nc // 2:
            next_cum = (_gla_chunk_cumsum(next_f, reverse=False, tb=tb),
                        _gla_chunk_cumsum(next_b, reverse=True, tb=tb))
        for it in items:
            scores = jnp.where(it["mask"], it["scores"], 0.0).astype(BF16)
            s_old = s_ref[it["d"], it["h"]]
            it["o_ref"][it["r0"]:it["r0"] + CHUNK, it["vs"]] = (
                jnp.dot(scores, it["v"], preferred_element_type=F32)
                + jnp.dot(it["q_e"], s_old.astype(BF16), preferred_element_type=F32))
            ds = lax.dot_general(it["k_d"], it["v"], (((0,), (0,)), ((), ())),
                                 preferred_element_type=F32)
            dcol = jnp.transpose(jnp.broadcast_to(it["decay"], (SUBLANES, GLA_DK)))[:, 0:1]
            s_ref[it["d"], it["h"]] = dcol * s_old + ds
        items = following
    cum_ref[0] = next_cum[0]
    cum_ref[1] = next_cum[1]
    if cast:
        job.after(i)

    @pl.when(i == pl.num_programs(0) - 1)
    def _():
        sf[...] = s_ref[0]
        sb[...] = s_ref[1]


def _gla_call(p, l, wgf, bgf, wgb, bgb, s0f, s0b, cast_weight=None, *, tb):
    n = p.shape[0]
    nb = n // tb
    fwd = lambda i: i
    bwd = lambda i: nb - 1 - i
    fwd_next = lambda i: jnp.minimum(i + 1, nb - 1)
    bwd_next = lambda i: jnp.maximum(nb - 2 - i, 0)

    def specs(pos, pos_next):
        return [
            pl.BlockSpec((tb, GLA_KW), lambda i: (pos(i), 0)),
            pl.BlockSpec((tb, GLA_KW), lambda i: (pos(i), 1)),
            pl.BlockSpec((tb, GLA_VW), lambda i: (pos(i), 1)),
            pl.BlockSpec((tb, LANES), lambda i: (pos(i), 0)),
            pl.BlockSpec((tb, LANES), lambda i: (pos_next(i), 0)),
        ]

    const2 = lambda i: (0, 0)
    const3 = lambda i: (0, 0, 0)
    state_shape = (GLA_HEADS, GLA_DK, GLA_DV)
    state_spec = pl.BlockSpec(state_shape, const3)
    cast = cast_weight is not None
    kern = functools.partial(_gla_kernel, tb=tb, cast=cast)
    vmem = (2 * 2 * tb * (2 * GLA_KW + GLA_VW + LANES) * 2 + 2 * 2 * tb * GLA_VW * 4
            + 10 * GLA_HEADS * GLA_DK * GLA_DV * 4 + 24 * tb * GLA_KW * 4 + 8 * 1024 * 1024)
    in_specs = specs(fwd, fwd_next) + specs(bwd, bwd_next) + [
        pl.BlockSpec((LANES, GLA_KW), const2), pl.BlockSpec((1, GLA_KW), const2),
        pl.BlockSpec((LANES, GLA_KW), const2), pl.BlockSpec((1, GLA_KW), const2),
        state_spec, state_spec,
    ]
    out_specs = [
        pl.BlockSpec((tb, GLA_VW), lambda i: (i, 0)),
        pl.BlockSpec((tb, GLA_VW), lambda i: (nb - 1 - i, 0)),
        state_spec, state_spec,
    ]
    out_shape = [
        jax.ShapeDtypeStruct((n, GLA_VW), F32), jax.ShapeDtypeStruct((n, GLA_VW), F32),
        jax.ShapeDtypeStruct(state_shape, F32), jax.ShapeDtypeStruct(state_shape, F32),
    ]
    scratch = [pltpu.VMEM((2,) + state_shape, F32), pltpu.VMEM((2, tb, GLA_KW), F32)]
    operands = [p, p, p, l, l, p, p, p, l, l, wgf, bgf, wgb, bgb, s0f, s0b]
    if cast:
        assert nb >= FFN_HIDDEN // GLA_CAST_ROWS + 2
        any_spec = pl.BlockSpec(memory_space=pl.ANY)
        in_specs.append(any_spec)
        out_specs.append(any_spec)
        out_shape.append(jax.ShapeDtypeStruct(cast_weight.shape, BF16))
        scratch += _cast_scratch(GLA_CAST_ROWS, D_MODEL)
        operands.append(cast_weight)
        vmem += GLA_CAST_ROWS * D_MODEL * (2 * 4 + 2 * 2)
    return pl.pallas_call(
        kern,
        grid=(nb,),
        in_specs=in_specs,
        out_specs=out_specs,
        out_shape=out_shape,
        scratch_shapes=scratch,
        compiler_params=_params(1, vmem),
        name="gla",
    )(*operands)


def _mix_kernel(of_ref, ob_ref, g_ref, cb_ref, cc_ref, cx_ref, x_ref, gt_ref, gg_ref, cw_ref,
                wo_ref, o_ref, m_ref, w_ref, *, bm):
    @pl.when(pl.program_id(0) == 0)
    def _():
        for r0 in range(0, D_MODEL, W_ROWS):
            w_ref[r0:r0 + W_ROWS, :] = wo_ref[r0:r0 + W_ROWS, :].astype(BF16)

    gg = gg_ref[...]
    for h in range(GLA_HEADS):
        vs = slice(h * GLA_DV, (h + 1) * GLA_DV)
        o = of_ref[:, vs] + ob_ref[:, vs]
        o32 = o * lax.rsqrt(jnp.mean(o * o, axis=-1, keepdims=True) + EPS)
        m_ref[:, vs] = ((o32 * gg) * _silu(g_ref[:, vs].astype(F32))).astype(BF16)

    u = cc_ref[...].astype(F32) * cx_ref[...].astype(F32)
    col = lax.broadcasted_iota(jnp.int32, u.shape, 0) % GRID_W
    u_prev = jnp.where(col == 0, 0.0, pltpu.roll(u, 1, 0))
    u_next = jnp.where(col == GRID_W - 1, 0.0, pltpu.roll(u, bm - 1, 0))
    conv = cw_ref[0:1, :] * u_prev + cw_ref[1:2, :] * u + cw_ref[2:3, :] * u_next
    m_ref[:, GLA_VW:] = (cb_ref[...].astype(F32) * conv).astype(BF16)

    mixed = jnp.dot(m_ref[...], w_ref[...], preferred_element_type=F32)
    o_ref[...] = x_ref[...] + gt_ref[0:1, :] * mixed


def _mix_call(o_f, o_b, p, x, mod, gla_g, conv_w, w_out, *, bm):
    n = x.shape[0]
    kern = functools.partial(_mix_kernel, bm=bm)
    pcol = lambda k: pl.BlockSpec((bm, GLA_VW), lambda i, k=k: (i, k))
    const2 = lambda i: (0, 0)
    vmem = (2 * 2 * bm * GLA_VW * 4 + 2 * 4 * bm * GLA_VW * 2 + 2 * 2 * bm * D_MODEL * 4
            + D_MODEL * D_MODEL * (4 + 2) + bm * D_MODEL * 2 + 8 * bm * GLA_VW * 4
            + 4 * 1024 * 1024)
    return pl.pallas_call(
        kern,
        grid=(n // bm,),
        in_specs=[
            pl.BlockSpec((bm, GLA_VW), lambda i: (i, 0)),
            pl.BlockSpec((bm, GLA_VW), lambda i: (i, 0)),
            pcol(2), pcol(3), pcol(4), pcol(5),
            pl.BlockSpec((bm, D_MODEL), lambda i: (i, 0)),
            pl.BlockSpec((SUBLANES, D_MODEL), lambda i: (0, 2)),
            pl.BlockSpec((1, GLA_DV), const2),
            pl.BlockSpec((3, CONV_W), const2),
            pl.BlockSpec((D_MODEL, D_MODEL), const2, pipeline_mode=pl.Buffered(1)),
        ],
        out_specs=pl.BlockSpec((bm, D_MODEL), lambda i: (i, 0)),
        out_shape=jax.ShapeDtypeStruct((n, D_MODEL), F32),
        scratch_shapes=[pltpu.VMEM((bm, D_MODEL), BF16), pltpu.VMEM((D_MODEL, D_MODEL), BF16)],
        compiler_params=_params(1, vmem),
        name="mix",
    )(o_f, o_b, p, p, p, p, x, mod, gla_g, conv_w, w_out)


def _ffn_kernel(x_ref, g_ref, sh_ref, sc_ref, gt_ref, wg_ref, wu_ref, wd_ref, fg_ref,
                o_ref, h_ref, *, bm):
    j = pl.program_id(1)

    @pl.when(j == 0)
    def _():
        _norm_mod_rows(x_ref, g_ref[...], 1.0 + sc_ref[0:1, :], sh_ref[0:1, :], h_ref, bm)
        o_ref[...] = x_ref[...]

    h = h_ref[...]
    gate = jnp.dot(h, wg_ref[...], preferred_element_type=F32)
    up = jnp.dot(h, wu_ref[...], preferred_element_type=F32)
    part = jnp.dot((_silu(gate) * up).astype(BF16), wd_ref[...], preferred_element_type=F32)
    o_ref[...] += gt_ref[0:1, :] * part

    @pl.when(j == pl.num_programs(1) - 1)
    def _():
        fg = fg_ref[...]
        for r0 in range(0, bm, NORM_ROWS):
            rows = slice(r0, r0 + NORM_ROWS)
            xn = o_ref[rows, :]
            ms = jnp.mean(xn * xn, axis=-1, keepdims=True)
            o_ref[rows, :] = (xn * lax.rsqrt(ms + EPS)) * fg


def _ffn_call(x, g, mod, wg, wu, wd, fg, *, bm):
    n = x.shape[0]
    kern = functools.partial(_ffn_kernel, bm=bm)
    modspec = lambda k: pl.BlockSpec((SUBLANES, D_MODEL), lambda i, j, k=k: (0, k))
    vmem = (3 * bm * D_MODEL * 4 + bm * D_MODEL * 2 + 2 * 3 * D_MODEL * FFN_BH * 2
            + 4 * bm * FFN_BH * 4 + bm * D_MODEL * 4 + 4 * 1024 * 1024)
    return pl.pallas_call(
        kern,
        grid=(n // bm, FFN_HIDDEN // FFN_BH),
        in_specs=[
            pl.BlockSpec((bm, D_MODEL), lambda i, j: (i, 0), pipeline_mode=pl.Buffered(1)),
            pl.BlockSpec((1, D_MODEL), lambda i, j: (0, 0)),
            modspec(3), modspec(4), modspec(5),
            pl.BlockSpec((D_MODEL, FFN_BH), lambda i, j: (0, j)),
            pl.BlockSpec((D_MODEL, FFN_BH), lambda i, j: (0, j)),
            pl.BlockSpec((FFN_BH, D_MODEL), lambda i, j: (j, 0)),
            pl.BlockSpec((1, D_MODEL), lambda i, j: (0, 0)),
        ],
        out_specs=pl.BlockSpec((bm, D_MODEL), lambda i, j: (i, 0)),
        out_shape=jax.ShapeDtypeStruct((n, D_MODEL), F32),
        scratch_shapes=[pltpu.VMEM((bm, D_MODEL), BF16)],
        compiler_params=_params(2, vmem),
        name="ffn",
    )(x, g, mod, mod, mod, wg, wu, wd, fg)


def kernel(x, c, ctx, c_ctx, w_mod, b_mod, norm1_g, norm2_g, w_in, w_gate_f, b_gate_f,
           w_gate_b, b_gate_b, gla_norm_g, conv_w, w_out, w_ffn_gate, w_ffn_up, w_ffn_down,
           final_g):
    assert x.shape[0] == 1 and w_mod.shape[0] == 1, "batch 1, depth 1 only"
    seq, ctx_len = x.shape[1], ctx.shape[1]
    x2, ctx2 = x[0], ctx[0]

    c_t = jnp.concatenate(
        [c[0][:, None], c_ctx[:, None], jnp.zeros((D_MODEL, SUBLANES - 2), F32)], axis=1)
    mod = _mod_call(c_t, w_mod[0], b_mod)

    pad_f = jnp.zeros((LANES, GLA_KW), F32).at[:GATE_RANK].set(w_gate_f[0]).astype(BF16)
    pad_b = jnp.zeros((LANES, GLA_KW), F32).at[GATE_RANK:2 * GATE_RANK].set(w_gate_b[0]).astype(BF16)
    g1 = norm1_g[0][None]
    w_in_t = jnp.transpose(w_in[0])

    h_c, l_c = _prenorm_call(ctx2, g1, mod, w_in_t, row=1, bm=ctx_len)
    p_c, = _in_proj_call(h_c, w_in_t, bm=ctx_len, n_tiles=2)
    zero_state = jnp.zeros((GLA_HEADS, GLA_DK, GLA_DV), F32)
    _, _, s_f, s_b = _gla_call(p_c, l_c, pad_f, b_gate_f, pad_b, b_gate_b, zero_state,
                               zero_state, tb=ctx_len)

    h_x, l_x = _prenorm_call(x2, g1, mod, w_in_t, row=0, bm=512)
    p_x, wg_b, wu_b = _in_proj_call(h_x, w_in_t, (w_ffn_gate[0], w_ffn_up[0]), bm=1024,
                                    n_tiles=P_COLS // P_BN)
    o_f, o_b, _, _, wd_b = _gla_call(p_x, l_x, pad_f, b_gate_f, pad_b, b_gate_b, s_f, s_b,
                                     w_ffn_down[0], tb=256)
    x_mid = _mix_call(o_f, o_b, p_x, x2, mod, gla_norm_g, conv_w[0], w_out[0],
                      bm=256)
    out = _ffn_call(x_mid, norm2_g[0][None], mod, wg_b, wu_b, wd_b, final_g[None], bm=1024)
    return out[None]
```

```python
import functools

import jax
import jax.numpy as jnp
from jax import lax
from jax.experimental import pallas as pl
from jax.experimental.pallas import tpu as pltpu

F32 = jnp.float32
BF16 = jnp.bfloat16

D_MODEL = 2048
GRID_W = 64
GLA_HEADS = 4
GLA_DK = 128
GLA_DV = 256
GLA_KW = GLA_HEADS * GLA_DK
GLA_VW = GLA_HEADS * GLA_DV
GATE_RANK = 16
GATE_NORM = 16.0
CHUNK = 64
CONV_W = D_MODEL - GLA_VW
FFN_HIDDEN = 5632
EPS = 1e-6

LANES = 128
SUBLANES = 8
VMEM_BYTES = 64 * 1024 * 1024

GATE_COL = 2 * GLA_KW + 2 * GLA_VW
GATE_COLS = 2 * GATE_RANK
P_COLS = GATE_COL + 3 * CONV_W
P_BN = 1024
P_SHIFT_FROM = GATE_COL // P_BN

_NT_DIMS = (((1,), (1,)), ((), ()))

MOD_BN = 1024
EARLY_MOD_COLS = 2 * D_MODEL
LATE_MOD_BN = 256
FFN_BH = 512
NORM_ROWS = 16


def _params(grid_rank, vmem_bytes):
    return pltpu.CompilerParams(
        dimension_semantics=("arbitrary",) * grid_rank,
        vmem_limit_bytes=min(int(vmem_bytes), VMEM_BYTES - 6 * 1024 * 1024),
    )


def _silu(v):
    return v * jax.nn.sigmoid(v)


def _for_row_chunks(rows, body):
    for r0 in range(0, rows, NORM_ROWS):
        body(slice(r0, r0 + NORM_ROWS))


def _norm_mod_rows(x_ref, g, mul, sh, dst_ref, rows, copy_ref=None):
    def body(rs):
        xs = x_ref[rs, :]
        ms = jnp.mean(xs * xs, axis=-1, keepdims=True)
        y = xs * lax.rsqrt(ms + EPS)
        dst_ref[rs, :] = ((y * g) * mul + sh).astype(BF16)
        if copy_ref is not None:
            copy_ref[rs, :] = xs
    _for_row_chunks(rows, body)


def _mod_prepare(c_ref, sb_ref):
    s = _silu(c_ref[...])
    for r in range(2):
        sb_ref[r] = jnp.broadcast_to(s[:, r:r + 1], (D_MODEL, LANES))


def _mod_matvec(w_ref, b_ref, sb_ref, o_ref, *, unrolled):
    nt = w_ref.shape[1] // LANES

    def body(kk, accs):
        r0 = kk * SUBLANES if unrolled else pl.multiple_of(kk * SUBLANES, SUBLANES)
        s0 = sb_ref[0, pl.ds(r0, SUBLANES), :]
        s1 = sb_ref[1, pl.ds(r0, SUBLANES), :]
        new = []
        for t in range(nt):
            w8 = w_ref[pl.ds(r0, SUBLANES), t * LANES:(t + 1) * LANES]
            new.append(accs[2 * t] + w8 * s0)
            new.append(accs[2 * t + 1] + w8 * s1)
        return tuple(new)

    accs = (jnp.zeros((SUBLANES, LANES), F32),) * (2 * nt)
    if unrolled:
        for kk in range(D_MODEL // SUBLANES):
            accs = body(kk, accs)
    else:
        accs = lax.fori_loop(0, D_MODEL // SUBLANES, body, accs, unroll=8)
    o_ref[...] = jnp.zeros(o_ref.shape, F32)
    for t in range(nt):
        b = b_ref[:, t * LANES:(t + 1) * LANES]
        for r in range(2):
            o_ref[r:r + 1, t * LANES:(t + 1) * LANES] = (
                jnp.sum(accs[2 * t + r], axis=0, keepdims=True) + b)


def _mod_kernel(c_ref, w_ref, b_ref, o_ref, sb_ref):
    @pl.when(pl.program_id(0) == 0)
    def _():
        _mod_prepare(c_ref, sb_ref)

    _mod_matvec(w_ref, b_ref, sb_ref, o_ref, unrolled=False)


def _mod_call(c_t, w_mod, b_mod, n):
    return pl.pallas_call(
        _mod_kernel,
        grid=(n // MOD_BN,),
        in_specs=[
            pl.BlockSpec((D_MODEL, SUBLANES), lambda j: (0, 0)),
            pl.BlockSpec((D_MODEL, MOD_BN), lambda j: (0, j)),
            pl.BlockSpec((1, MOD_BN), lambda j: (0, j)),
        ],
        out_specs=pl.BlockSpec((SUBLANES, MOD_BN), lambda j: (0, j)),
        out_shape=jax.ShapeDtypeStruct((SUBLANES, n), F32),
        scratch_shapes=[pltpu.VMEM((2, D_MODEL, LANES), F32)],
        compiler_params=_params(1, 2 * D_MODEL * MOD_BN * 4 + 8 * 1024 * 1024),
        name="mod",
    )(c_t, w_mod, b_mod)


def _prenorm_kernel(x_ref, g_ref, sh_ref, sc_ref, wl_ref, h_ref, l_ref, *, row, bm):
    _norm_mod_rows(x_ref, g_ref[...], 1.0 + sc_ref[row:row + 1, :], sh_ref[row:row + 1, :],
                   h_ref, bm)
    l_ref[...] = lax.dot_general(h_ref[...], wl_ref[...].astype(BF16), _NT_DIMS,
                                 preferred_element_type=F32).astype(BF16)


def _prenorm_call(x, g, mod, w_in_t, *, row, bm):
    m = x.shape[0]
    kern = functools.partial(_prenorm_kernel, row=row, bm=bm)
    vmem = (2 * bm * D_MODEL * (4 + 2) + 4 * D_MODEL * LANES * 4 + 8 * NORM_ROWS * D_MODEL * 4
            + 4 * 1024 * 1024)
    return pl.pallas_call(
        kern,
        grid=(m // bm,),
        in_specs=[
            pl.BlockSpec((bm, D_MODEL), lambda i: (i, 0)),
            pl.BlockSpec((1, D_MODEL), lambda i: (0, 0)),
            pl.BlockSpec((SUBLANES, D_MODEL), lambda i: (0, 0)),
            pl.BlockSpec((SUBLANES, D_MODEL), lambda i: (0, 1)),
            pl.BlockSpec((LANES, D_MODEL), lambda i: (GATE_COL // LANES, 0)),
        ],
        out_specs=[pl.BlockSpec((bm, D_MODEL), lambda i: (i, 0)),
                   pl.BlockSpec((bm, LANES), lambda i: (i, 0))],
        out_shape=[jax.ShapeDtypeStruct((m, D_MODEL), BF16),
                   jax.ShapeDtypeStruct((m, LANES), BF16)],
        compiler_params=_params(1, vmem),
        name="prenorm",
    )(x, g, mod, mod, w_in_t)


W_ROWS = 256


class _CastJob:
    def __init__(self, sources, rows, stage_ref, out_ref, sem_in, sem_out):
        self.sources, self.rows = sources, rows
        self.stage_ref, self.out_ref, self.sem_in, self.sem_out = stage_ref, out_ref, sem_in, sem_out
        self.n_blocks = sum(n for _, _, n in sources)

    def _for_block(self, c, fn):
        first = 0
        for src, dst, n in self.sources:
            @pl.when((c >= first) & (c < first + n))
            def _(src=src, dst=dst, first=first):
                fn(src, dst, pl.multiple_of((c - first) * self.rows, self.rows), lax.rem(c, 2))
            first += n

    def _fetch(self, src, dst, r, slot):
        return pltpu.make_async_copy(src.at[pl.ds(r, self.rows), :], self.stage_ref.at[slot],
                                     self.sem_in.at[slot])

    def _store(self, src, dst, r, slot):
        return pltpu.make_async_copy(self.out_ref.at[slot], dst.at[pl.ds(r, self.rows), :],
                                     self.sem_out.at[slot])

    def before(self, step):
        @pl.when(step == 0)
        def _():
            self._for_block(step, lambda *a: self._fetch(*a).start())
        self._for_block(step, lambda *a: self._fetch(*a).wait())
        self._for_block(step + 1, lambda *a: self._fetch(*a).start())
        self._for_block(step - 2, lambda *a: self._store(*a).wait())

    def cast(self, step):
        slot = lax.rem(step, 2)
        self.out_ref[slot] = self.stage_ref[slot].astype(BF16)

    def after(self, step):
        self._for_block(step, lambda *a: self._store(*a).start())


def _cast_scratch(rows, cols):
    return [pltpu.VMEM((2, rows, cols), F32), pltpu.VMEM((2, rows, cols), BF16),
            pltpu.SemaphoreType.DMA((2,)), pltpu.SemaphoreType.DMA((2,))]


IN_PROJ_CAST_ROWS = 128
GLA_CAST_ROWS = 256


def _in_proj_kernel(h_ref, wa_ref, wb_ref, *refs, side_jobs):
    j = pl.program_id(0)
    if side_jobs:
        (wg_ref, wu_ref, c_ref, wm_ref, bm_ref, o_ref, wg_out, wu_out, mod_ref,
         w_ref, stage_ref, cast_ref, sem_in, sem_out, sb_ref) = refs
        n = D_MODEL // IN_PROJ_CAST_ROWS
        job = _CastJob(((wg_ref, wg_out, n), (wu_ref, wu_out, n)), IN_PROJ_CAST_ROWS,
                       stage_ref, cast_ref, sem_in, sem_out)
        step = j * pl.num_programs(1) + pl.program_id(1)
        job.before(step)

        @pl.when(step == 0)
        def _():
            _mod_prepare(c_ref, sb_ref)
    else:
        o_ref, w_ref = refs

    @pl.when(pl.program_id(1) == 0)
    def _():
        @pl.when(j < P_SHIFT_FROM)
        def _():
            for r0 in range(0, P_BN, W_ROWS):
                w_ref[r0:r0 + W_ROWS, :] = wa_ref[r0:r0 + W_ROWS, :].astype(BF16)

        @pl.when(j >= P_SHIFT_FROM)
        def _():
            keep = P_BN - GATE_COLS
            for r0 in range(0, keep, W_ROWS):
                r1 = min(r0 + W_ROWS, keep)
                w_ref[r0:r1, :] = wa_ref[GATE_COLS + r0:GATE_COLS + r1, :].astype(BF16)
            w_ref[keep:, :] = wb_ref[...].astype(BF16)

    o_ref[...] = lax.dot_general(h_ref[...], w_ref[...], _NT_DIMS,
                                 preferred_element_type=F32).astype(BF16)
    if side_jobs:
        job.cast(step)
        _mod_matvec(wm_ref, bm_ref, sb_ref, mod_ref, unrolled=True)
        job.after(step)


def _in_proj_call(h, w_in_t, side=None, *, bm, n_tiles):
    m = h.shape[0]
    ni = m // bm
    vmem = (2 * bm * D_MODEL * 2 + 2 * (P_BN + GATE_COLS) * D_MODEL * 4 + P_BN * D_MODEL * 2
            + 2 * bm * P_BN * 2 + bm * P_BN * 4 + 4 * W_ROWS * D_MODEL * 4 + 4 * 1024 * 1024)
    in_specs = [
        pl.BlockSpec((bm, D_MODEL), lambda j, i: (i, 0)),
        pl.BlockSpec((P_BN, D_MODEL), lambda j, i: (j, 0)),
        pl.BlockSpec((GATE_COLS, D_MODEL), lambda j, i: ((j + 1) * (P_BN // GATE_COLS), 0)),
    ]
    out_specs = [pl.BlockSpec((bm, P_BN), lambda j, i: (i, j))]
    out_shape = [jax.ShapeDtypeStruct((m, n_tiles * P_BN), BF16)]
    scratch = [pltpu.VMEM((P_BN, D_MODEL), BF16)]
    operands = [h, w_in_t, w_in_t]
    if side is not None:
        w_gate, w_up, c_t, w_mod, b_mod, first_col = side
        n_steps = n_tiles * ni
        assert n_steps >= 2 * D_MODEL // IN_PROJ_CAST_ROWS + 2
        any_spec = pl.BlockSpec(memory_space=pl.ANY)
        n_mod = w_mod.shape[1] - first_col
        n_blocks = n_mod // LATE_MOD_BN
        assert n_mod % LATE_MOD_BN == 0 and first_col % LATE_MOD_BN == 0 and n_blocks <= n_steps
        mod_block = lambda j, i: jnp.minimum(j * ni + i, n_blocks - 1)
        in_specs += [
            any_spec, any_spec,
            pl.BlockSpec((D_MODEL, SUBLANES), lambda j, i: (0, 0)),
            pl.BlockSpec((D_MODEL, LATE_MOD_BN),
                         lambda j, i: (0, first_col // LATE_MOD_BN + mod_block(j, i))),
            pl.BlockSpec((1, LATE_MOD_BN),
                         lambda j, i: (0, first_col // LATE_MOD_BN + mod_block(j, i))),
        ]
        out_specs += [any_spec, any_spec,
                      pl.BlockSpec((SUBLANES, LATE_MOD_BN), lambda j, i: (0, mod_block(j, i)))]
        out_shape += [jax.ShapeDtypeStruct(w_gate.shape, BF16),
                      jax.ShapeDtypeStruct(w_up.shape, BF16),
                      jax.ShapeDtypeStruct((SUBLANES, n_mod), F32)]
        scratch += _cast_scratch(IN_PROJ_CAST_ROWS, FFN_HIDDEN)
        scratch += [pltpu.VMEM((2, D_MODEL, LANES), F32)]
        operands += [w_gate, w_up, c_t, w_mod, b_mod]
        vmem += (IN_PROJ_CAST_ROWS * FFN_HIDDEN * (2 * 4 + 2 * 2)
                 + 2 * D_MODEL * LATE_MOD_BN * 4 + 2 * D_MODEL * LANES * 4)
    return pl.pallas_call(
        functools.partial(_in_proj_kernel, side_jobs=side is not None),
        grid=(n_tiles, ni),
        in_specs=in_specs,
        out_specs=out_specs,
        out_shape=out_shape,
        scratch_shapes=scratch,
        compiler_params=_params(2, vmem),
        name="in_proj",
    )(*operands)


def _gla_log_decay_split(l_ref, wg_ref, bg_ref):
    z = jnp.dot(l_ref[...], wg_ref[...], preferred_element_type=F32) + bg_ref[...]
    log_a = (jnp.minimum(z, 0.0) - jnp.log(1.0 + jnp.exp(-jnp.abs(z)))) * (1.0 / GATE_NORM)
    hi = log_a.astype(BF16)
    lo = (log_a - hi.astype(F32)).astype(BF16)
    return hi, lo


def _gla_chunk_cumsum(pieces, *, reverse, tb):
    r = lax.broadcasted_iota(jnp.int32, (tb, tb), 0)
    cc = lax.broadcasted_iota(jnp.int32, (tb, tb), 1)
    same_chunk = (r // CHUNK) == (cc // CHUNK)
    tri = jnp.where(same_chunk & ((cc >= r) if reverse else (cc <= r)), 1.0, 0.0).astype(BF16)
    hi, lo = pieces
    return (jnp.dot(tri, hi, preferred_element_type=F32)
            + jnp.dot(tri, lo, preferred_element_type=F32))


def _gla_kernel(qf, kf, vf, lf, lf_next, qb, kb, vb, lb, lb_next, wgf, bgf, wgb, bgb, s0f, s0b,
                *refs, tb, cast):
    i = pl.program_id(0)
    if cast:
        (wd_ref, of, ob, sf, sb, wd_out, s_ref, cum_ref,
         stage_ref, cast_ref, sem_in, sem_out) = refs
        job = _CastJob(((wd_ref, wd_out, FFN_HIDDEN // GLA_CAST_ROWS),), GLA_CAST_ROWS,
                       stage_ref, cast_ref, sem_in, sem_out)
        job.before(i)
    else:
        of, ob, sf, sb, s_ref, cum_ref = refs

    @pl.when(i == 0)
    def _():
        s_ref[0] = s0f[...]
        s_ref[1] = s0b[...]
        cum_ref[0] = _gla_chunk_cumsum(_gla_log_decay_split(lf, wgf, bgf), reverse=False, tb=tb)
        cum_ref[1] = _gla_chunk_cumsum(_gla_log_decay_split(lb, wgb, bgb), reverse=True, tb=tb)

    next_f = _gla_log_decay_split(lf_next, wgf, bgf)
    next_b = _gla_log_decay_split(lb_next, wgb, bgb)
    dirs = ((qf, kf, vf, of, False), (qb, kb, vb, ob, True))
    ii = lax.broadcasted_iota(jnp.int32, (CHUNK, GLA_DK), 0)
    jj = lax.broadcasted_iota(jnp.int32, (CHUNK, GLA_DK), 1)
    nc = tb // CHUNK

    def start(step):
        items = []
        for d, (q_ref, k_ref, v_ref, o_ref, reverse) in enumerate(dirs):
            r0 = (nc - 1 - step if reverse else step) * CHUNK
            cum_c = cum_ref[d, r0:r0 + CHUNK, :]
            last = cum_c[0:1, :] if reverse else cum_c[CHUNK - 1:CHUNK, :]
            q_c = q_ref[r0:r0 + CHUNK, :].astype(F32)
            k_c = k_ref[r0:r0 + CHUNK, :].astype(F32)
            q_e = ((q_c * (GLA_DK ** -0.5)) * jnp.exp(cum_c)).astype(BF16)
            k_e = (k_c * jnp.exp(-cum_c)).astype(BF16)
            k_d = (k_c * jnp.exp(last - cum_c)).astype(BF16)
            decay = jnp.exp(last)
            mask = ((jj >= ii) & (jj < CHUNK)) if reverse else (jj <= ii)
            for h in range(GLA_HEADS):
                ks = slice(h * GLA_DK, (h + 1) * GLA_DK)
                vs = slice(h * GLA_DV, (h + 1) * GLA_DV)
                items.append(dict(
                    d=d, h=h, r0=r0, vs=vs, o_ref=o_ref, mask=mask,
                    q_e=q_e[:, ks], k_e=k_e[:, ks], k_d=k_d[:, ks], decay=decay[:, ks],
                    v=v_ref[r0:r0 + CHUNK, vs]))
        for it in items:
            k_pad = jnp.concatenate([it["k_e"], jnp.zeros((GLA_DK - CHUNK, GLA_DK), BF16)], axis=0)
            it["scores"] = lax.dot_general(it["q_e"], k_pad, _NT_DIMS,
                                           preferred_element_type=F32)
        return items

    items = start(0)
    next_cum = None
    for step in range(nc):
        following = start(step + 1) if step + 1 < nc else None
        if step == nc // 2:
            next_cum = (_gla_chunk_cumsum(next_f, reverse=False, tb=tb),
                        _gla_chunk_cumsum(next_b, reverse=True, tb=tb))
        for it in items:
            scores = jnp.where(it["mask"], it["scores"], 0.0).astype(BF16)
            s_old = s_ref[it["d"], it["h"]]
            lhs = jnp.concatenate([scores, it["q_e"]], axis=1)
            rhs = jnp.concatenate(
                [it["v"], jnp.zeros((GLA_DK - CHUNK, GLA_DV), BF16), s_old.astype(BF16)], axis=0)
            it["o_ref"][it["r0"]:it["r0"] + CHUNK, it["vs"]] = jnp.dot(
                lhs, rhs, preferred_element_type=F32)
            ds = lax.dot_general(it["k_d"], it["v"], (((0,), (0,)), ((), ())),
                                 preferred_element_type=F32)
            dcol = jnp.transpose(jnp.broadcast_to(it["decay"], (SUBLANES, GLA_DK)))[:, 0:1]
            s_ref[it["d"], it["h"]] = dcol * s_old + ds
        items = following
    cum_ref[0] = next_cum[0]
    cum_ref[1] = next_cum[1]
    if cast:
        job.cast(i)
        job.after(i)

    @pl.when(i == pl.num_programs(0) - 1)
    def _():
        sf[...] = s_ref[0]
        sb[...] = s_ref[1]


def _gla_call(p, l, wgf, bgf, wgb, bgb, s0f, s0b, cast_weight=None, *, tb):
    n = p.shape[0]
    nb = n // tb
    fwd = lambda i: i
    bwd = lambda i: nb - 1 - i
    fwd_next = lambda i: jnp.minimum(i + 1, nb - 1)
    bwd_next = lambda i: jnp.maximum(nb - 2 - i, 0)

    def specs(pos, pos_next):
        return [
            pl.BlockSpec((tb, GLA_KW), lambda i: (pos(i), 0)),
            pl.BlockSpec((tb, GLA_KW), lambda i: (pos(i), 1)),
            pl.BlockSpec((tb, GLA_VW), lambda i: (pos(i), 1)),
            pl.BlockSpec((tb, LANES), lambda i: (pos(i), 0)),
            pl.BlockSpec((tb, LANES), lambda i: (pos_next(i), 0)),
        ]

    const2 = lambda i: (0, 0)
    const3 = lambda i: (0, 0, 0)
    state_shape = (GLA_HEADS, GLA_DK, GLA_DV)
    state_spec = pl.BlockSpec(state_shape, const3)
    cast = cast_weight is not None
    kern = functools.partial(_gla_kernel, tb=tb, cast=cast)
    vmem = (2 * 2 * tb * (2 * GLA_KW + GLA_VW + LANES) * 2 + 2 * 2 * tb * GLA_VW * 4
            + 10 * GLA_HEADS * GLA_DK * GLA_DV * 4 + 24 * tb * GLA_KW * 4 + 8 * 1024 * 1024)
    in_specs = specs(fwd, fwd_next) + specs(bwd, bwd_next) + [
        pl.BlockSpec((LANES, GLA_KW), const2), pl.BlockSpec((1, GLA_KW), const2),
        pl.BlockSpec((LANES, GLA_KW), const2), pl.BlockSpec((1, GLA_KW), const2),
        state_spec, state_spec,
    ]
    out_specs = [
        pl.BlockSpec((tb, GLA_VW), lambda i: (i, 0)),
        pl.BlockSpec((tb, GLA_VW), lambda i: (nb - 1 - i, 0)),
        state_spec, state_spec,
    ]
    out_shape = [
        jax.ShapeDtypeStruct((n, GLA_VW), F32), jax.ShapeDtypeStruct((n, GLA_VW), F32),
        jax.ShapeDtypeStruct(state_shape, F32), jax.ShapeDtypeStruct(state_shape, F32),
    ]
    scratch = [pltpu.VMEM((2,) + state_shape, F32), pltpu.VMEM((2, tb, GLA_KW), F32)]
    operands = [p, p, p, l, l, p, p, p, l, l, wgf, bgf, wgb, bgb, s0f, s0b]
    if cast:
        assert nb >= FFN_HIDDEN // GLA_CAST_ROWS + 2
        any_spec = pl.BlockSpec(memory_space=pl.ANY)
        in_specs.append(any_spec)
        out_specs.append(any_spec)
        out_shape.append(jax.ShapeDtypeStruct(cast_weight.shape, BF16))
        scratch += _cast_scratch(GLA_CAST_ROWS, D_MODEL)
        operands.append(cast_weight)
        vmem += GLA_CAST_ROWS * D_MODEL * (2 * 4 + 2 * 2)
    return pl.pallas_call(
        kern,
        grid=(nb,),
        in_specs=in_specs,
        out_specs=out_specs,
        out_shape=out_shape,
        scratch_shapes=scratch,
        compiler_params=_params(1, vmem),
        name="gla",
    )(*operands)


def _mix_kernel(of_ref, ob_ref, g_ref, cb_ref, cc_ref, cx_ref, x_ref, gt_ref, gg_ref, cw_ref,
                wo_ref, o_ref, m_ref, w_ref, *, bm):
    @pl.when(pl.program_id(0) == 0)
    def _():
        for r0 in range(0, D_MODEL, W_ROWS):
            w_ref[r0:r0 + W_ROWS, :] = wo_ref[r0:r0 + W_ROWS, :].astype(BF16)

    gg = gg_ref[...]
    for h in range(GLA_HEADS):
        vs = slice(h * GLA_DV, (h + 1) * GLA_DV)
        o = of_ref[:, vs] + ob_ref[:, vs]
        o32 = o * lax.rsqrt(jnp.mean(o * o, axis=-1, keepdims=True) + EPS)
        m_ref[:, vs] = ((o32 * gg) * _silu(g_ref[:, vs].astype(F32))).astype(BF16)

    u = cc_ref[...].astype(F32) * cx_ref[...].astype(F32)
    col = lax.broadcasted_iota(jnp.int32, u.shape, 0) % GRID_W
    u_prev = jnp.where(col == 0, 0.0, pltpu.roll(u, 1, 0))
    u_next = jnp.where(col == GRID_W - 1, 0.0, pltpu.roll(u, bm - 1, 0))
    conv = cw_ref[0:1, :] * u_prev + cw_ref[1:2, :] * u + cw_ref[2:3, :] * u_next
    m_ref[:, GLA_VW:] = (cb_ref[...].astype(F32) * conv).astype(BF16)

    mixed = jnp.dot(m_ref[...], w_ref[...], preferred_element_type=F32)
    o_ref[...] = x_ref[...] + gt_ref[0:1, :] * mixed


def _mix_call(o_f, o_b, p, x, mod, gla_g, conv_w, w_out, *, bm):
    n = x.shape[0]
    kern = functools.partial(_mix_kernel, bm=bm)
    pcol = lambda k: pl.BlockSpec((bm, GLA_VW), lambda i, k=k: (i, k))
    const2 = lambda i: (0, 0)
    vmem = (2 * 2 * bm * GLA_VW * 4 + 2 * 4 * bm * GLA_VW * 2 + 2 * 2 * bm * D_MODEL * 4
            + D_MODEL * D_MODEL * (4 + 2) + bm * D_MODEL * 2 + 8 * bm * GLA_VW * 4
            + 4 * 1024 * 1024)
    return pl.pallas_call(
        kern,
        grid=(n // bm,),
        in_specs=[
            pl.BlockSpec((bm, GLA_VW), lambda i: (i, 0)),
            pl.BlockSpec((bm, GLA_VW), lambda i: (i, 0)),
            pcol(2), pcol(3), pcol(4), pcol(5),
            pl.BlockSpec((bm, D_MODEL), lambda i: (i, 0)),
            pl.BlockSpec((SUBLANES, D_MODEL), lambda i: (0, 0)),
            pl.BlockSpec((1, GLA_DV), const2),
            pl.BlockSpec((3, CONV_W), const2),
            pl.BlockSpec((D_MODEL, D_MODEL), const2, pipeline_mode=pl.Buffered(1)),
        ],
        out_specs=pl.BlockSpec((bm, D_MODEL), lambda i: (i, 0)),
        out_shape=jax.ShapeDtypeStruct((n, D_MODEL), F32),
        scratch_shapes=[pltpu.VMEM((bm, D_MODEL), BF16), pltpu.VMEM((D_MODEL, D_MODEL), BF16)],
        compiler_params=_params(1, vmem),
        name="mix",
    )(o_f, o_b, p, p, p, p, x, mod, gla_g, conv_w, w_out)


def _ffn_kernel(x_ref, g_ref, sh_ref, sc_ref, gt_ref, wg_ref, wu_ref, wd_ref, fg_ref,
                o_ref, h_ref, *, bm):
    j = pl.program_id(1)

    @pl.when(j == 0)
    def _():
        _norm_mod_rows(x_ref, g_ref[...], 1.0 + sc_ref[0:1, :], sh_ref[0:1, :], h_ref, bm,
                       copy_ref=o_ref)

    h = h_ref[...]
    gate = jnp.dot(h, wg_ref[...], preferred_element_type=F32)
    up = jnp.dot(h, wu_ref[...], preferred_element_type=F32)
    part = jnp.dot((_silu(gate) * up).astype(BF16), wd_ref[...], preferred_element_type=F32)
    o_ref[...] += gt_ref[0:1, :] * part

    @pl.when(j == pl.num_programs(1) - 1)
    def _():
        fg = fg_ref[...]

        def final_norm(rs):
            xn = o_ref[rs, :]
            ms = jnp.mean(xn * xn, axis=-1, keepdims=True)
            o_ref[rs, :] = (xn * lax.rsqrt(ms + EPS)) * fg
        _for_row_chunks(bm, final_norm)


def _ffn_call(x, g, mod, wg, wu, wd, fg, *, bm):
    n = x.shape[0]
    kern = functools.partial(_ffn_kernel, bm=bm)
    modspec = lambda k: pl.BlockSpec((SUBLANES, D_MODEL), lambda i, j, k=k: (0, k))
    vmem = (3 * bm * D_MODEL * 4 + bm * D_MODEL * 2 + 2 * 3 * D_MODEL * FFN_BH * 2
            + 4 * bm * FFN_BH * 4 + bm * D_MODEL * 4 + 4 * 1024 * 1024)
    return pl.pallas_call(
        kern,
        grid=(n // bm, FFN_HIDDEN // FFN_BH),
        in_specs=[
            pl.BlockSpec((bm, D_MODEL), lambda i, j: (i, 0), pipeline_mode=pl.Buffered(1)),
            pl.BlockSpec((1, D_MODEL), lambda i, j: (0, 0)),
            modspec(1), modspec(2), modspec(3),
            pl.BlockSpec((D_MODEL, FFN_BH), lambda i, j: (0, j)),
            pl.BlockSpec((D_MODEL, FFN_BH), lambda i, j: (0, j)),
            pl.BlockSpec((FFN_BH, D_MODEL), lambda i, j: (j, 0)),
            pl.BlockSpec((1, D_MODEL), lambda i, j: (0, 0)),
        ],
        out_specs=pl.BlockSpec((bm, D_MODEL), lambda i, j: (i, 0)),
        out_shape=jax.ShapeDtypeStruct((n, D_MODEL), F32),
        scratch_shapes=[pltpu.VMEM((bm, D_MODEL), BF16)],
        compiler_params=_params(2, vmem),
        name="ffn",
    )(x, g, mod, mod, mod, wg, wu, wd, fg)


def kernel(x, c, ctx, c_ctx, w_mod, b_mod, norm1_g, norm2_g, w_in, w_gate_f, b_gate_f,
           w_gate_b, b_gate_b, gla_norm_g, conv_w, w_out, w_ffn_gate, w_ffn_up, w_ffn_down,
           final_g):
    assert x.shape[0] == 1 and w_mod.shape[0] == 1, "batch 1, depth 1 only"
    seq, ctx_len = x.shape[1], ctx.shape[1]
    x2, ctx2 = x[0], ctx[0]

    c_t = jnp.concatenate(
        [c[0][:, None], c_ctx[:, None], jnp.zeros((D_MODEL, SUBLANES - 2), F32)], axis=1)
    mod = _mod_call(c_t, w_mod[0], b_mod, EARLY_MOD_COLS)

    pad_f = jnp.zeros((LANES, GLA_KW), F32).at[:GATE_RANK].set(w_gate_f[0]).astype(BF16)
    pad_b = jnp.zeros((LANES, GLA_KW), F32).at[GATE_RANK:2 * GATE_RANK].set(w_gate_b[0]).astype(BF16)
    g1 = norm1_g[0][None]
    w_in_t = jnp.transpose(w_in[0])

    h_c, l_c = _prenorm_call(ctx2, g1, mod, w_in_t, row=1, bm=ctx_len)
    p_c, = _in_proj_call(h_c, w_in_t, bm=ctx_len, n_tiles=2)
    zero_state = jnp.zeros((GLA_HEADS, GLA_DK, GLA_DV), F32)
    _, _, s_f, s_b = _gla_call(p_c, l_c, pad_f, b_gate_f, pad_b, b_gate_b, zero_state,
                               zero_state, tb=ctx_len)

    h_x, l_x = _prenorm_call(x2, g1, mod, w_in_t, row=0, bm=512)
    p_x, wg_b, wu_b, mod_late = _in_proj_call(
        h_x, w_in_t, (w_ffn_gate[0], w_ffn_up[0], c_t, w_mod[0], b_mod, EARLY_MOD_COLS),
        bm=1024, n_tiles=P_COLS // P_BN)
    o_f, o_b, _, _, wd_b = _gla_call(p_x, l_x, pad_f, b_gate_f, pad_b, b_gate_b, s_f, s_b,
                                     w_ffn_down[0], tb=256)
    x_mid = _mix_call(o_f, o_b, p_x, x2, mod_late, gla_norm_g, conv_w[0], w_out[0],
                      bm=256)
    out = _ffn_call(x_mid, norm2_g[0][None], mod_late, wg_b, wu_b, wd_b, final_g[None],
                    bm=1024)
    return out[None]
```

```python
import functools

import jax
import jax.numpy as jnp
from jax import lax
from jax.experimental import pallas as pl
from jax.experimental.pallas import tpu as pltpu

F32 = jnp.float32
BF16 = jnp.bfloat16

D_MODEL = 2048
GRID_W = 64
GLA_HEADS = 4
GLA_DK = 128
GLA_DV = 256
GLA_KW = GLA_HEADS * GLA_DK
GLA_VW = GLA_HEADS * GLA_DV
GATE_RANK = 16
GATE_NORM = 16.0
CHUNK = 64
CONV_W = D_MODEL - GLA_VW
FFN_HIDDEN = 5632
EPS = 1e-6

LANES = 128
SUBLANES = 8
VMEM_BYTES = 64 * 1024 * 1024

GATE_COL = 2 * GLA_KW + 2 * GLA_VW
GATE_COLS = 2 * GATE_RANK
P_COLS = GATE_COL + 3 * CONV_W
P_BN = 1024
P_SHIFT_FROM = GATE_COL // P_BN

_NT_DIMS = (((1,), (1,)), ((), ()))

MOD_BN = 1024
EARLY_MOD_COLS = 2 * D_MODEL
LATE_MOD_BN = 256
FFN_BH = 512
NORM_ROWS = 16


def _params(grid_rank, vmem_bytes):
    return pltpu.CompilerParams(
        dimension_semantics=("arbitrary",) * grid_rank,
        vmem_limit_bytes=min(int(vmem_bytes), VMEM_BYTES - 6 * 1024 * 1024),
    )


def _silu(v):
    return v * jax.nn.sigmoid(v)


def _for_row_chunks(rows, body):
    for r0 in range(0, rows, NORM_ROWS):
        body(slice(r0, r0 + NORM_ROWS))


def _norm_mod_rows(x_ref, g, mul, sh, dst_ref, rows, copy_ref=None):
    def body(rs):
        xs = x_ref[rs, :]
        ms = jnp.mean(xs * xs, axis=-1, keepdims=True)
        y = xs * lax.rsqrt(ms + EPS)
        dst_ref[rs, :] = ((y * g) * mul + sh).astype(BF16)
        if copy_ref is not None:
            copy_ref[rs, :] = xs
    _for_row_chunks(rows, body)


def _mod_prepare(c_ref, sb_ref):
    s = _silu(c_ref[...])
    for r in range(2):
        sb_ref[r] = jnp.broadcast_to(s[:, r:r + 1], (D_MODEL, LANES))


def _mod_matvec(w_ref, b_ref, sb_ref, o_ref, *, unrolled):
    nt = w_ref.shape[1] // LANES

    def body(kk, accs):
        r0 = kk * SUBLANES if unrolled else pl.multiple_of(kk * SUBLANES, SUBLANES)
        s0 = sb_ref[0, pl.ds(r0, SUBLANES), :]
        s1 = sb_ref[1, pl.ds(r0, SUBLANES), :]
        new = []
        for t in range(nt):
            w8 = w_ref[pl.ds(r0, SUBLANES), t * LANES:(t + 1) * LANES]
            new.append(accs[2 * t] + w8 * s0)
            new.append(accs[2 * t + 1] + w8 * s1)
        return tuple(new)

    accs = (jnp.zeros((SUBLANES, LANES), F32),) * (2 * nt)
    if unrolled:
        for kk in range(D_MODEL // SUBLANES):
            accs = body(kk, accs)
    else:
        accs = lax.fori_loop(0, D_MODEL // SUBLANES, body, accs, unroll=8)
    o_ref[...] = jnp.zeros(o_ref.shape, F32)
    for t in range(nt):
        b = b_ref[:, t * LANES:(t + 1) * LANES]
        for r in range(2):
            o_ref[r:r + 1, t * LANES:(t + 1) * LANES] = (
                jnp.sum(accs[2 * t + r], axis=0, keepdims=True) + b)


def _mod_kernel(c_ref, w_ref, b_ref, o_ref, sb_ref):
    @pl.when(pl.program_id(0) == 0)
    def _():
        _mod_prepare(c_ref, sb_ref)

    _mod_matvec(w_ref, b_ref, sb_ref, o_ref, unrolled=False)


def _mod_call(c_t, w_mod, b_mod, n):
    return pl.pallas_call(
        _mod_kernel,
        grid=(n // MOD_BN,),
        in_specs=[
            pl.BlockSpec((D_MODEL, SUBLANES), lambda j: (0, 0)),
            pl.BlockSpec((D_MODEL, MOD_BN), lambda j: (0, j)),
            pl.BlockSpec((1, MOD_BN), lambda j: (0, j)),
        ],
        out_specs=pl.BlockSpec((SUBLANES, MOD_BN), lambda j: (0, j)),
        out_shape=jax.ShapeDtypeStruct((SUBLANES, n), F32),
        scratch_shapes=[pltpu.VMEM((2, D_MODEL, LANES), F32)],
        compiler_params=_params(1, 2 * D_MODEL * MOD_BN * 4 + 8 * 1024 * 1024),
        name="mod",
    )(c_t, w_mod, b_mod)


def _prenorm_kernel(x_ref, g_ref, sh_ref, sc_ref, wl_ref, h_ref, l_ref, *, row, bm):
    _norm_mod_rows(x_ref, g_ref[...], 1.0 + sc_ref[row:row + 1, :], sh_ref[row:row + 1, :],
                   h_ref, bm)
    l_ref[...] = lax.dot_general(h_ref[...], wl_ref[...].astype(BF16), _NT_DIMS,
                                 preferred_element_type=F32).astype(BF16)


def _prenorm_call(x, g, mod, w_in_t, *, row, bm):
    m = x.shape[0]
    kern = functools.partial(_prenorm_kernel, row=row, bm=bm)
    vmem = (2 * bm * D_MODEL * (4 + 2) + 4 * D_MODEL * LANES * 4 + 8 * NORM_ROWS * D_MODEL * 4
            + 4 * 1024 * 1024)
    return pl.pallas_call(
        kern,
        grid=(m // bm,),
        in_specs=[
            pl.BlockSpec((bm, D_MODEL), lambda i: (i, 0)),
            pl.BlockSpec((1, D_MODEL), lambda i: (0, 0)),
            pl.BlockSpec((SUBLANES, D_MODEL), lambda i: (0, 0)),
            pl.BlockSpec((SUBLANES, D_MODEL), lambda i: (0, 1)),
            pl.BlockSpec((LANES, D_MODEL), lambda i: (GATE_COL // LANES, 0)),
        ],
        out_specs=[pl.BlockSpec((bm, D_MODEL), lambda i: (i, 0)),
                   pl.BlockSpec((bm, LANES), lambda i: (i, 0))],
        out_shape=[jax.ShapeDtypeStruct((m, D_MODEL), BF16),
                   jax.ShapeDtypeStruct((m, LANES), BF16)],
        compiler_params=_params(1, vmem),
        name="prenorm",
    )(x, g, mod, mod, w_in_t)


W_ROWS = 256


class _CastJob:
    def __init__(self, sources, rows, stage_ref, out_ref, sem_in, sem_out):
        self.sources, self.rows = sources, rows
        self.stage_ref, self.out_ref, self.sem_in, self.sem_out = stage_ref, out_ref, sem_in, sem_out
        self.n_blocks = sum(n for _, _, n in sources)

    def _for_block(self, c, fn):
        first = 0
        for src, dst, n in self.sources:
            @pl.when((c >= first) & (c < first + n))
            def _(src=src, dst=dst, first=first):
                fn(src, dst, pl.multiple_of((c - first) * self.rows, self.rows), lax.rem(c, 2))
            first += n

    def _fetch(self, src, dst, r, slot):
        return pltpu.make_async_copy(src.at[pl.ds(r, self.rows), :], self.stage_ref.at[slot],
                                     self.sem_in.at[slot])

    def _store(self, src, dst, r, slot):
        return pltpu.make_async_copy(self.out_ref.at[slot], dst.at[pl.ds(r, self.rows), :],
                                     self.sem_out.at[slot])

    def before(self, step):
        @pl.when(step == 0)
        def _():
            self._for_block(step, lambda *a: self._fetch(*a).start())
        self._for_block(step, lambda *a: self._fetch(*a).wait())
        self._for_block(step + 1, lambda *a: self._fetch(*a).start())
        self._for_block(step - 2, lambda *a: self._store(*a).wait())

    def cast(self, step):
        slot = lax.rem(step, 2)
        self.out_ref[slot] = self.stage_ref[slot].astype(BF16)

    def after(self, step):
        self._for_block(step, lambda *a: self._store(*a).start())


def _cast_scratch(rows, cols):
    return [pltpu.VMEM((2, rows, cols), F32), pltpu.VMEM((2, rows, cols), BF16),
            pltpu.SemaphoreType.DMA((2,)), pltpu.SemaphoreType.DMA((2,))]


IN_PROJ_CAST_ROWS = 128
GLA_CAST_ROWS = 256


def _in_proj_kernel(h_ref, wa_ref, wb_ref, *refs, side_jobs):
    j = pl.program_id(0)
    if side_jobs:
        (wg_ref, wu_ref, c_ref, wm_ref, bm_ref, o_ref, wg_out, wu_out, mod_ref,
         w_ref, stage_ref, cast_ref, sem_in, sem_out, sb_ref) = refs
        n = D_MODEL // IN_PROJ_CAST_ROWS
        job = _CastJob(((wg_ref, wg_out, n), (wu_ref, wu_out, n)), IN_PROJ_CAST_ROWS,
                       stage_ref, cast_ref, sem_in, sem_out)
        step = j * pl.num_programs(1) + pl.program_id(1)
        job.before(step)

        @pl.when(step == 0)
        def _():
            _mod_prepare(c_ref, sb_ref)
    else:
        o_ref, w_ref = refs

    @pl.when(pl.program_id(1) == 0)
    def _():
        @pl.when(j < P_SHIFT_FROM)
        def _():
            for r0 in range(0, P_BN, W_ROWS):
                w_ref[r0:r0 + W_ROWS, :] = wa_ref[r0:r0 + W_ROWS, :].astype(BF16)

        @pl.when(j >= P_SHIFT_FROM)
        def _():
            keep = P_BN - GATE_COLS
            for r0 in range(0, keep, W_ROWS):
                r1 = min(r0 + W_ROWS, keep)
                w_ref[r0:r1, :] = wa_ref[GATE_COLS + r0:GATE_COLS + r1, :].astype(BF16)
            w_ref[keep:, :] = wb_ref[...].astype(BF16)

    o_ref[...] = lax.dot_general(h_ref[...], w_ref[...], _NT_DIMS,
                                 preferred_element_type=F32).astype(BF16)
    if side_jobs:
        job.cast(step)
        _mod_matvec(wm_ref, bm_ref, sb_ref, mod_ref, unrolled=True)
        job.after(step)


def _in_proj_call(h, w_in_t, side=None, *, bm, n_tiles):
    m = h.shape[0]
    ni = m // bm
    vmem = (2 * bm * D_MODEL * 2 + 2 * (P_BN + GATE_COLS) * D_MODEL * 4 + P_BN * D_MODEL * 2
            + 2 * bm * P_BN * 2 + bm * P_BN * 4 + 4 * W_ROWS * D_MODEL * 4 + 4 * 1024 * 1024)
    in_specs = [
        pl.BlockSpec((bm, D_MODEL), lambda j, i: (i, 0)),
        pl.BlockSpec((P_BN, D_MODEL), lambda j, i: (j, 0)),
        pl.BlockSpec((GATE_COLS, D_MODEL), lambda j, i: ((j + 1) * (P_BN // GATE_COLS), 0)),
    ]
    out_specs = [pl.BlockSpec((bm, P_BN), lambda j, i: (i, j))]
    out_shape = [jax.ShapeDtypeStruct((m, n_tiles * P_BN), BF16)]
    scratch = [pltpu.VMEM((P_BN, D_MODEL), BF16)]
    operands = [h, w_in_t, w_in_t]
    if side is not None:
        w_gate, w_up, c_t, w_mod, b_mod, first_col = side
        n_steps = n_tiles * ni
        assert n_steps >= 2 * D_MODEL // IN_PROJ_CAST_ROWS + 2
        any_spec = pl.BlockSpec(memory_space=pl.ANY)
        n_mod = w_mod.shape[1] - first_col
        n_blocks = n_mod // LATE_MOD_BN
        assert n_mod % LATE_MOD_BN == 0 and first_col % LATE_MOD_BN == 0 and n_blocks <= n_steps
        mod_block = lambda j, i: jnp.minimum(j * ni + i, n_blocks - 1)
        in_specs += [
            any_spec, any_spec,
            pl.BlockSpec((D_MODEL, SUBLANES), lambda j, i: (0, 0)),
            pl.BlockSpec((D_MODEL, LATE_MOD_BN),
                         lambda j, i: (0, first_col // LATE_MOD_BN + mod_block(j, i))),
            pl.BlockSpec((1, LATE_MOD_BN),
                         lambda j, i: (0, first_col // LATE_MOD_BN + mod_block(j, i))),
        ]
        out_specs += [any_spec, any_spec,
                      pl.BlockSpec((SUBLANES, LATE_MOD_BN), lambda j, i: (0, mod_block(j, i)))]
        out_shape += [jax.ShapeDtypeStruct(w_gate.shape, BF16),
                      jax.ShapeDtypeStruct(w_up.shape, BF16),
                      jax.ShapeDtypeStruct((SUBLANES, n_mod), F32)]
        scratch += _cast_scratch(IN_PROJ_CAST_ROWS, FFN_HIDDEN)
        scratch += [pltpu.VMEM((2, D_MODEL, LANES), F32)]
        operands += [w_gate, w_up, c_t, w_mod, b_mod]
        vmem += (IN_PROJ_CAST_ROWS * FFN_HIDDEN * (2 * 4 + 2 * 2)
                 + 2 * D_MODEL * LATE_MOD_BN * 4 + 2 * D_MODEL * LANES * 4)
    return pl.pallas_call(
        functools.partial(_in_proj_kernel, side_jobs=side is not None),
        grid=(n_tiles, ni),
        in_specs=in_specs,
        out_specs=out_specs,
        out_shape=out_shape,
        scratch_shapes=scratch,
        compiler_params=_params(2, vmem),
        name="in_proj",
    )(*operands)


def _gla_log_decay_split(l_ref, wg_ref, bg_ref):
    z = jnp.dot(l_ref[...], wg_ref[...], preferred_element_type=F32) + bg_ref[...]
    log_a = (jnp.minimum(z, 0.0) - jnp.log(1.0 + jnp.exp(-jnp.abs(z)))) * (1.0 / GATE_NORM)
    hi = log_a.astype(BF16)
    lo = (log_a - hi.astype(F32)).astype(BF16)
    return hi, lo


def _gla_chunk_cumsum(pieces, *, reverse, tb):
    r = lax.broadcasted_iota(jnp.int32, (tb, tb), 0)
    cc = lax.broadcasted_iota(jnp.int32, (tb, tb), 1)
    same_chunk = (r // CHUNK) == (cc // CHUNK)
    tri = jnp.where(same_chunk & ((cc >= r) if reverse else (cc <= r)), 1.0, 0.0).astype(BF16)
    hi, lo = pieces
    return (jnp.dot(tri, hi, preferred_element_type=F32)
            + jnp.dot(tri, lo, preferred_element_type=F32))


def _gla_kernel(qf, kf, vf, lf, lf_next, qb, kb, vb, lb, lb_next, wgf, bgf, wgb, bgb, s0f, s0b,
                *refs, tb, cast):
    i = pl.program_id(0)
    if cast:
        (wd_ref, of, ob, sf, sb, wd_out, s_ref, cum_ref,
         stage_ref, cast_ref, sem_in, sem_out) = refs
        job = _CastJob(((wd_ref, wd_out, FFN_HIDDEN // GLA_CAST_ROWS),), GLA_CAST_ROWS,
                       stage_ref, cast_ref, sem_in, sem_out)
        job.before(i)
    else:
        of, ob, sf, sb, s_ref, cum_ref = refs

    @pl.when(i == 0)
    def _():
        s_ref[0] = s0f[...]
        s_ref[1] = s0b[...]
        cum_ref[0] = _gla_chunk_cumsum(_gla_log_decay_split(lf, wgf, bgf), reverse=False, tb=tb)
        cum_ref[1] = _gla_chunk_cumsum(_gla_log_decay_split(lb, wgb, bgb), reverse=True, tb=tb)

    next_f = _gla_log_decay_split(lf_next, wgf, bgf)
    next_b = _gla_log_decay_split(lb_next, wgb, bgb)
    dirs = ((qf, kf, vf, of, False), (qb, kb, vb, ob, True))
    ii = lax.broadcasted_iota(jnp.int32, (CHUNK, GLA_DK), 0)
    jj = lax.broadcasted_iota(jnp.int32, (CHUNK, GLA_DK), 1)
    nc = tb // CHUNK

    def start(step):
        items = []
        for d, (q_ref, k_ref, v_ref, o_ref, reverse) in enumerate(dirs):
            r0 = (nc - 1 - step if reverse else step) * CHUNK
            cum_c = cum_ref[d, r0:r0 + CHUNK, :]
            last = cum_c[0:1, :] if reverse else cum_c[CHUNK - 1:CHUNK, :]
            q_c = q_ref[r0:r0 + CHUNK, :].astype(F32)
            k_c = k_ref[r0:r0 + CHUNK, :].astype(F32)
            q_e = ((q_c * (GLA_DK ** -0.5)) * jnp.exp(cum_c)).astype(BF16)
            k_e = (k_c * jnp.exp(-cum_c)).astype(BF16)
            k_d = (k_c * jnp.exp(last - cum_c)).astype(BF16)
            decay = jnp.exp(last)
            mask = ((jj >= ii) & (jj < CHUNK)) if reverse else (jj <= ii)
            for h in range(GLA_HEADS):
                ks = slice(h * GLA_DK, (h + 1) * GLA_DK)
                vs = slice(h * GLA_DV, (h + 1) * GLA_DV)
                items.append(dict(
                    d=d, h=h, r0=r0, vs=vs, o_ref=o_ref, mask=mask,
                    q_e=q_e[:, ks], k_e=k_e[:, ks], k_d=k_d[:, ks], decay=decay[:, ks],
                    v=v_ref[r0:r0 + CHUNK, vs]))
        for it in items:
            k_pad = jnp.concatenate([it["k_e"], jnp.zeros((GLA_DK - CHUNK, GLA_DK), BF16)], axis=0)
            it["scores"] = lax.dot_general(it["q_e"], k_pad, _NT_DIMS,
                                           preferred_element_type=F32)
        return items

    items = start(0)
    next_cum = None
    for step in range(nc):
        following = start(step + 1) if step + 1 < nc else None
        if step == nc // 2:
            next_cum = (_gla_chunk_cumsum(next_f, reverse=False, tb=tb),
                        _gla_chunk_cumsum(next_b, reverse=True, tb=tb))
        for it in items:
            scores = jnp.where(it["mask"], it["scores"], 0.0).astype(BF16)
            s_old = s_ref[it["d"], it["h"]]
            lhs = jnp.concatenate([scores, it["q_e"]], axis=1)
            rhs = jnp.concatenate(
                [it["v"], jnp.zeros((GLA_DK - CHUNK, GLA_DV), BF16), s_old.astype(BF16)], axis=0)
            it["o_ref"][it["r0"]:it["r0"] + CHUNK, it["vs"]] = jnp.dot(
                lhs, rhs, preferred_element_type=F32)
            ds = lax.dot_general(it["k_d"], it["v"], (((0,), (0,)), ((), ())),
                                 preferred_element_type=F32)
            dcol = jnp.transpose(jnp.broadcast_to(it["decay"], (SUBLANES, GLA_DK)))[:, 0:1]
            s_ref[it["d"], it["h"]] = dcol * s_old + ds
        items = following
    cum_ref[0] = next_cum[0]
    cum_ref[1] = next_cum[1]
    if cast:
        job.cast(i)
        job.after(i)

    @pl.when(i == pl.num_programs(0) - 1)
    def _():
        sf[...] = s_ref[0]
        sb[...] = s_ref[1]


def _gla_call(p, l, wgf, bgf, wgb, bgb, s0f, s0b, cast_weight=None, *, tb):
    n = p.shape[0]
    nb = n // tb
    fwd = lambda i: i
    bwd = lambda i: nb - 1 - i
    fwd_next = lambda i: jnp.minimum(i + 1, nb - 1)
    bwd_next = lambda i: jnp.maximum(nb - 2 - i, 0)

    def specs(pos, pos_next):
        return [
            pl.BlockSpec((tb, GLA_KW), lambda i: (pos(i), 0)),
            pl.BlockSpec((tb, GLA_KW), lambda i: (pos(i), 1)),
            pl.BlockSpec((tb, GLA_VW), lambda i: (pos(i), 1)),
            pl.BlockSpec((tb, LANES), lambda i: (pos(i), 0)),
            pl.BlockSpec((tb, LANES), lambda i: (pos_next(i), 0)),
        ]

    const2 = lambda i: (0, 0)
    const3 = lambda i: (0, 0, 0)
    state_shape = (GLA_HEADS, GLA_DK, GLA_DV)
    state_spec = pl.BlockSpec(state_shape, const3)
    cast = cast_weight is not None
    kern = functools.partial(_gla_kernel, tb=tb, cast=cast)
    vmem = (2 * 2 * tb * (2 * GLA_KW + GLA_VW + LANES) * 2 + 2 * 2 * tb * GLA_VW * 4
            + 10 * GLA_HEADS * GLA_DK * GLA_DV * 4 + 24 * tb * GLA_KW * 4 + 8 * 1024 * 1024)
    in_specs = specs(fwd, fwd_next) + specs(bwd, bwd_next) + [
        pl.BlockSpec((LANES, GLA_KW), const2), pl.BlockSpec((1, GLA_KW), const2),
        pl.BlockSpec((LANES, GLA_KW), const2), pl.BlockSpec((1, GLA_KW), const2),
        state_spec, state_spec,
    ]
    out_specs = [
        pl.BlockSpec((tb, GLA_VW), lambda i: (i, 0)),
        pl.BlockSpec((tb, GLA_VW), lambda i: (nb - 1 - i, 0)),
        state_spec, state_spec,
    ]
    out_shape = [
        jax.ShapeDtypeStruct((n, GLA_VW), F32), jax.ShapeDtypeStruct((n, GLA_VW), F32),
        jax.ShapeDtypeStruct(state_shape, F32), jax.ShapeDtypeStruct(state_shape, F32),
    ]
    scratch = [pltpu.VMEM((2,) + state_shape, F32), pltpu.VMEM((2, tb, GLA_KW), F32)]
    operands = [p, p, p, l, l, p, p, p, l, l, wgf, bgf, wgb, bgb, s0f, s0b]
    if cast:
        assert nb >= FFN_HIDDEN // GLA_CAST_ROWS + 2
        any_spec = pl.BlockSpec(memory_space=pl.ANY)
        in_specs.append(any_spec)
        out_specs.append(any_spec)
        out_shape.append(jax.ShapeDtypeStruct(cast_weight.shape, BF16))
        scratch += _cast_scratch(GLA_CAST_ROWS, D_MODEL)
        operands.append(cast_weight)
        vmem += GLA_CAST_ROWS * D_MODEL * (2 * 4 + 2 * 2)
    return pl.pallas_call(
        kern,
        grid=(nb,),
        in_specs=in_specs,
        out_specs=out_specs,
        out_shape=out_shape,
        scratch_shapes=scratch,
        compiler_params=_params(1, vmem),
        name="gla",
    )(*operands)


def _mix_kernel(of_ref, ob_ref, g_ref, cb_ref, cc_ref, cx_ref, x_ref, gt_ref, gg_ref, cw_ref,
                wo_ref, o_ref, m_ref, w_ref, *, bm):
    @pl.when(pl.program_id(0) == 0)
    def _():
        for r0 in range(0, D_MODEL, W_ROWS):
            w_ref[r0:r0 + W_ROWS, :] = wo_ref[r0:r0 + W_ROWS, :].astype(BF16)

    gg = gg_ref[...]
    for h in range(GLA_HEADS):
        vs = slice(h * GLA_DV, (h + 1) * GLA_DV)
        o = of_ref[:, vs] + ob_ref[:, vs]
        o32 = o * lax.rsqrt(jnp.mean(o * o, axis=-1, keepdims=True) + EPS)
        m_ref[:, vs] = ((o32 * gg) * _silu(g_ref[:, vs].astype(F32))).astype(BF16)

    u = cc_ref[...].astype(F32) * cx_ref[...].astype(F32)
    col = lax.broadcasted_iota(jnp.int32, u.shape, 0) % GRID_W
    u_prev = jnp.where(col == 0, 0.0, pltpu.roll(u, 1, 0))
    u_next = jnp.where(col == GRID_W - 1, 0.0, pltpu.roll(u, bm - 1, 0))
    conv = cw_ref[0:1, :] * u_prev + cw_ref[1:2, :] * u + cw_ref[2:3, :] * u_next
    m_ref[:, GLA_VW:] = (cb_ref[...].astype(F32) * conv).astype(BF16)

    mixed = jnp.dot(m_ref[...], w_ref[...], preferred_element_type=F32)
    o_ref[...] = x_ref[...] + gt_ref[0:1, :] * mixed


def _mix_call(o_f, o_b, p, x, mod, gla_g, conv_w, w_out, *, bm):
    n = x.shape[0]
    kern = functools.partial(_mix_kernel, bm=bm)
    pcol = lambda k: pl.BlockSpec((bm, GLA_VW), lambda i, k=k: (i, k))
    const2 = lambda i: (0, 0)
    vmem = (2 * 2 * bm * GLA_VW * 4 + 2 * 4 * bm * GLA_VW * 2 + 2 * 2 * bm * D_MODEL * 4
            + D_MODEL * D_MODEL * (4 + 2) + bm * D_MODEL * 2 + 8 * bm * GLA_VW * 4
            + 4 * 1024 * 1024)
    return pl.pallas_call(
        kern,
        grid=(n // bm,),
        in_specs=[
            pl.BlockSpec((bm, GLA_VW), lambda i: (i, 0)),
            pl.BlockSpec((bm, GLA_VW), lambda i: (i, 0)),
            pcol(2), pcol(3), pcol(4), pcol(5),
            pl.BlockSpec((bm, D_MODEL), lambda i: (i, 0)),
            pl.BlockSpec((SUBLANES, D_MODEL), lambda i: (0, 0)),
            pl.BlockSpec((1, GLA_DV), const2),
            pl.BlockSpec((3, CONV_W), const2),
            pl.BlockSpec((D_MODEL, D_MODEL), const2, pipeline_mode=pl.Buffered(1)),
        ],
        out_specs=pl.BlockSpec((bm, D_MODEL), lambda i: (i, 0)),
        out_shape=jax.ShapeDtypeStruct((n, D_MODEL), F32),
        scratch_shapes=[pltpu.VMEM((bm, D_MODEL), BF16), pltpu.VMEM((D_MODEL, D_MODEL), BF16)],
        compiler_params=_params(1, vmem),
        name="mix",
    )(o_f, o_b, p, p, p, p, x, mod, gla_g, conv_w, w_out)


def _ffn_kernel(x_ref, g_ref, sh_ref, sc_ref, gt_ref, wg_ref, wu_ref, wd_ref, fg_ref,
                o_ref, h_ref, *, bm):
    j = pl.program_id(1)

    @pl.when(j == 0)
    def _():
        _norm_mod_rows(x_ref, g_ref[...], 1.0 + sc_ref[0:1, :], sh_ref[0:1, :], h_ref, bm,
                       copy_ref=o_ref)

    h = h_ref[...]
    gate = jnp.dot(h, wg_ref[...], preferred_element_type=F32)
    up = jnp.dot(h, wu_ref[...], preferred_element_type=F32)
    part = jnp.dot((_silu(gate) * up).astype(BF16), wd_ref[...], preferred_element_type=F32)
    o_ref[...] += gt_ref[0:1, :] * part

    @pl.when(j == pl.num_programs(1) - 1)
    def _():
        fg = fg_ref[...]

        def final_norm(rs):
            xn = o_ref[rs, :]
            ms = jnp.mean(xn * xn, axis=-1, keepdims=True)
            o_ref[rs, :] = (xn * lax.rsqrt(ms + EPS)) * fg
        _for_row_chunks(bm, final_norm)


def _ffn_call(x, g, mod, wg, wu, wd, fg, *, bm):
    n = x.shape[0]
    kern = functools.partial(_ffn_kernel, bm=bm)
    modspec = lambda k: pl.BlockSpec((SUBLANES, D_MODEL), lambda i, j, k=k: (0, k))
    vmem = (3 * bm * D_MODEL * 4 + bm * D_MODEL * 2 + 2 * 3 * D_MODEL * FFN_BH * 2
            + 4 * bm * FFN_BH * 4 + bm * D_MODEL * 4 + 4 * 1024 * 1024)
    return pl.pallas_call(
        kern,
        grid=(n // bm, FFN_HIDDEN // FFN_BH),
        in_specs=[
            pl.BlockSpec((bm, D_MODEL), lambda i, j: (i, 0)),
            pl.BlockSpec((1, D_MODEL), lambda i, j: (0, 0)),
            modspec(1), modspec(2), modspec(3),
            pl.BlockSpec((D_MODEL, FFN_BH), lambda i, j: (0, j)),
            pl.BlockSpec((D_MODEL, FFN_BH), lambda i, j: (0, j)),
            pl.BlockSpec((FFN_BH, D_MODEL), lambda i, j: (j, 0)),
            pl.BlockSpec((1, D_MODEL), lambda i, j: (0, 0)),
        ],
        out_specs=pl.BlockSpec((bm, D_MODEL), lambda i, j: (i, 0)),
        out_shape=jax.ShapeDtypeStruct((n, D_MODEL), F32),
        scratch_shapes=[pltpu.VMEM((bm, D_MODEL), BF16)],
        compiler_params=_params(2, vmem),
        name="ffn",
    )(x, g, mod, mod, mod, wg, wu, wd, fg)


def kernel(x, c, ctx, c_ctx, w_mod, b_mod, norm1_g, norm2_g, w_in, w_gate_f, b_gate_f,
           w_gate_b, b_gate_b, gla_norm_g, conv_w, w_out, w_ffn_gate, w_ffn_up, w_ffn_down,
           final_g):
    assert x.shape[0] == 1 and w_mod.shape[0] == 1, "batch 1, depth 1 only"
    seq, ctx_len = x.shape[1], ctx.shape[1]
    x2, ctx2 = x[0], ctx[0]

    c_t = jnp.concatenate(
        [c[0][:, None], c_ctx[:, None], jnp.zeros((D_MODEL, SUBLANES - 2), F32)], axis=1)
    mod = _mod_call(c_t, w_mod[0], b_mod, EARLY_MOD_COLS)

    pad_f = jnp.zeros((LANES, GLA_KW), F32).at[:GATE_RANK].set(w_gate_f[0]).astype(BF16)
    pad_b = jnp.zeros((LANES, GLA_KW), F32).at[GATE_RANK:2 * GATE_RANK].set(w_gate_b[0]).astype(BF16)
    g1 = norm1_g[0][None]
    w_in_t = jnp.transpose(w_in[0])

    h_c, l_c = _prenorm_call(ctx2, g1, mod, w_in_t, row=1, bm=ctx_len)
    p_c, = _in_proj_call(h_c, w_in_t, bm=ctx_len, n_tiles=2)
    zero_state = jnp.zeros((GLA_HEADS, GLA_DK, GLA_DV), F32)
    _, _, s_f, s_b = _gla_call(p_c, l_c, pad_f, b_gate_f, pad_b, b_gate_b, zero_state,
                               zero_state, tb=ctx_len)

    h_x, l_x = _prenorm_call(x2, g1, mod, w_in_t, row=0, bm=1024)
    p_x, wg_b, wu_b, mod_late = _in_proj_call(
        h_x, w_in_t, (w_ffn_gate[0], w_ffn_up[0], c_t, w_mod[0], b_mod, EARLY_MOD_COLS),
        bm=1024, n_tiles=P_COLS // P_BN)
    o_f, o_b, _, _, wd_b = _gla_call(p_x, l_x, pad_f, b_gate_f, pad_b, b_gate_b, s_f, s_b,
                                     w_ffn_down[0], tb=256)
    x_mid = _mix_call(o_f, o_b, p_x, x2, mod_late, gla_norm_g, conv_w[0], w_out[0],
                      bm=256)
    out = _ffn_call(x_mid, norm2_g[0][None], mod_late, wg_b, wu_b, wd_b, final_g[None],
                    bm=1024)
    return out[None]
```

```python
import functools

import jax
import jax.numpy as jnp
from jax import lax
from jax.experimental import pallas as pl
from jax.experimental.pallas import tpu as pltpu

F32 = jnp.float32
BF16 = jnp.bfloat16

D_MODEL = 2048
GRID_W = 64
GLA_HEADS = 4
GLA_DK = 128
GLA_DV = 256
GLA_KW = GLA_HEADS * GLA_DK
GLA_VW = GLA_HEADS * GLA_DV
GATE_RANK = 16
GATE_NORM = 16.0
CHUNK = 64
CONV_W = D_MODEL - GLA_VW
FFN_HIDDEN = 5632
EPS = 1e-6

LANES = 128
SUBLANES = 8
VMEM_BYTES = 64 * 1024 * 1024

GATE_COL = 2 * GLA_KW + 2 * GLA_VW
GATE_COLS = 2 * GATE_RANK
P_COLS = GATE_COL + 3 * CONV_W
P_BN = 1024
P_SHIFT_FROM = GATE_COL // P_BN

_NT_DIMS = (((1,), (1,)), ((), ()))

MOD_BN = 1024
EARLY_MOD_COLS = 2 * D_MODEL
LATE_MOD_BN = 256
FFN_BH = 512
NORM_ROWS = 16


def _params(grid_rank, vmem_bytes):
    return pltpu.CompilerParams(
        dimension_semantics=("arbitrary",) * grid_rank,
        vmem_limit_bytes=min(int(vmem_bytes), VMEM_BYTES - 6 * 1024 * 1024),
    )


def _silu(v):
    return v * jax.nn.sigmoid(v)


def _for_row_chunks(rows, body):
    for r0 in range(0, rows, NORM_ROWS):
        body(slice(r0, r0 + NORM_ROWS))


def _norm_mod_rows(x_ref, g, mul, sh, dst_ref, rows, copy_ref=None):
    def body(rs):
        xs = x_ref[rs, :]
        ms = jnp.mean(xs * xs, axis=-1, keepdims=True)
        y = xs * lax.rsqrt(ms + EPS)
        dst_ref[rs, :] = ((y * g) * mul + sh).astype(BF16)
        if copy_ref is not None:
            copy_ref[rs, :] = xs
    _for_row_chunks(rows, body)


def _mod_prepare(c_ref, sb_ref):
    s = _silu(c_ref[...])
    for r in range(2):
        sb_ref[r] = jnp.broadcast_to(s[:, r:r + 1], (D_MODEL, LANES))


def _mod_matvec(w_ref, b_ref, sb_ref, o_ref, *, unrolled):
    nt = w_ref.shape[1] // LANES

    def body(kk, accs):
        r0 = kk * SUBLANES if unrolled else pl.multiple_of(kk * SUBLANES, SUBLANES)
        s0 = sb_ref[0, pl.ds(r0, SUBLANES), :]
        s1 = sb_ref[1, pl.ds(r0, SUBLANES), :]
        new = []
        for t in range(nt):
            w8 = w_ref[pl.ds(r0, SUBLANES), t * LANES:(t + 1) * LANES]
            new.append(accs[2 * t] + w8 * s0)
            new.append(accs[2 * t + 1] + w8 * s1)
        return tuple(new)

    accs = (jnp.zeros((SUBLANES, LANES), F32),) * (2 * nt)
    if unrolled:
        for kk in range(D_MODEL // SUBLANES):
            accs = body(kk, accs)
    else:
        accs = lax.fori_loop(0, D_MODEL // SUBLANES, body, accs, unroll=8)
    o_ref[...] = jnp.zeros(o_ref.shape, F32)
    for t in range(nt):
        b = b_ref[:, t * LANES:(t + 1) * LANES]
        for r in range(2):
            o_ref[r:r + 1, t * LANES:(t + 1) * LANES] = (
                jnp.sum(accs[2 * t + r], axis=0, keepdims=True) + b)


def _mod_kernel(c_ref, w_ref, b_ref, o_ref, sb_ref):
    @pl.when(pl.program_id(0) == 0)
    def _():
        _mod_prepare(c_ref, sb_ref)

    _mod_matvec(w_ref, b_ref, sb_ref, o_ref, unrolled=False)


def _mod_call(c_t, w_mod, b_mod, n):
    return pl.pallas_call(
        _mod_kernel,
        grid=(n // MOD_BN,),
        in_specs=[
            pl.BlockSpec((D_MODEL, SUBLANES), lambda j: (0, 0)),
            pl.BlockSpec((D_MODEL, MOD_BN), lambda j: (0, j)),
            pl.BlockSpec((1, MOD_BN), lambda j: (0, j)),
        ],
        out_specs=pl.BlockSpec((SUBLANES, MOD_BN), lambda j: (0, j)),
        out_shape=jax.ShapeDtypeStruct((SUBLANES, n), F32),
        scratch_shapes=[pltpu.VMEM((2, D_MODEL, LANES), F32)],
        compiler_params=_params(1, 2 * D_MODEL * MOD_BN * 4 + 8 * 1024 * 1024),
        name="mod",
    )(c_t, w_mod, b_mod)


def _prenorm_kernel(x_ref, g_ref, sh_ref, sc_ref, wl_ref, h_ref, l_ref, *, row, bm):
    _norm_mod_rows(x_ref, g_ref[...], 1.0 + sc_ref[row:row + 1, :], sh_ref[row:row + 1, :],
                   h_ref, bm)
    l_ref[...] = lax.dot_general(h_ref[...], wl_ref[...].astype(BF16), _NT_DIMS,
                                 preferred_element_type=F32).astype(BF16)


def _prenorm_call(x, g, mod, w_in_t, *, row, bm):
    m = x.shape[0]
    kern = functools.partial(_prenorm_kernel, row=row, bm=bm)
    vmem = (2 * bm * D_MODEL * (4 + 2) + 4 * D_MODEL * LANES * 4 + 8 * NORM_ROWS * D_MODEL * 4
            + 4 * 1024 * 1024)
    return pl.pallas_call(
        kern,
        grid=(m // bm,),
        in_specs=[
            pl.BlockSpec((bm, D_MODEL), lambda i: (i, 0)),
            pl.BlockSpec((1, D_MODEL), lambda i: (0, 0)),
            pl.BlockSpec((SUBLANES, D_MODEL), lambda i: (0, 0)),
            pl.BlockSpec((SUBLANES, D_MODEL), lambda i: (0, 1)),
            pl.BlockSpec((LANES, D_MODEL), lambda i: (GATE_COL // LANES, 0)),
        ],
        out_specs=[pl.BlockSpec((bm, D_MODEL), lambda i: (i, 0)),
                   pl.BlockSpec((bm, LANES), lambda i: (i, 0))],
        out_shape=[jax.ShapeDtypeStruct((m, D_MODEL), BF16),
                   jax.ShapeDtypeStruct((m, LANES), BF16)],
        compiler_params=_params(1, vmem),
        name="prenorm",
    )(x, g, mod, mod, w_in_t)


W_ROWS = 256


class _CastJob:
    def __init__(self, sources, rows, stage_ref, out_ref, sem_in, sem_out):
        self.sources, self.rows = sources, rows
        self.stage_ref, self.out_ref, self.sem_in, self.sem_out = stage_ref, out_ref, sem_in, sem_out
        self.n_blocks = sum(n for _, _, n in sources)

    def _for_block(self, c, fn):
        first = 0
        for src, dst, n in self.sources:
            @pl.when((c >= first) & (c < first + n))
            def _(src=src, dst=dst, first=first):
                fn(src, dst, pl.multiple_of((c - first) * self.rows, self.rows), lax.rem(c, 2))
            first += n

    def _fetch(self, src, dst, r, slot):
        return pltpu.make_async_copy(src.at[pl.ds(r, self.rows), :], self.stage_ref.at[slot],
                                     self.sem_in.at[slot])

    def _store(self, src, dst, r, slot):
        return pltpu.make_async_copy(self.out_ref.at[slot], dst.at[pl.ds(r, self.rows), :],
                                     self.sem_out.at[slot])

    def before(self, step):
        @pl.when(step == 0)
        def _():
            self._for_block(step, lambda *a: self._fetch(*a).start())
        self._for_block(step, lambda *a: self._fetch(*a).wait())
        self._for_block(step + 1, lambda *a: self._fetch(*a).start())
        self._for_block(step - 2, lambda *a: self._store(*a).wait())

    def cast(self, step):
        slot = lax.rem(step, 2)
        self.out_ref[slot] = self.stage_ref[slot].astype(BF16)

    def after(self, step):
        self._for_block(step, lambda *a: self._store(*a).start())


def _cast_scratch(rows, cols):
    return [pltpu.VMEM((2, rows, cols), F32), pltpu.VMEM((2, rows, cols), BF16),
            pltpu.SemaphoreType.DMA((2,)), pltpu.SemaphoreType.DMA((2,))]


IN_PROJ_CAST_ROWS = 128
GLA_CAST_ROWS = 256


def _in_proj_kernel(h_ref, wa_ref, wb_ref, *refs, side_jobs):
    j = pl.program_id(0)
    if side_jobs:
        (wg_ref, wu_ref, c_ref, wm_ref, bm_ref, o_ref, wg_out, wu_out, mod_ref,
         w_ref, stage_ref, cast_ref, sem_in, sem_out, sb_ref) = refs
        n = D_MODEL // IN_PROJ_CAST_ROWS
        job = _CastJob(((wg_ref, wg_out, n), (wu_ref, wu_out, n)), IN_PROJ_CAST_ROWS,
                       stage_ref, cast_ref, sem_in, sem_out)
        step = j * pl.num_programs(1) + pl.program_id(1)
        job.before(step)

        @pl.when(step == 0)
        def _():
            _mod_prepare(c_ref, sb_ref)
    else:
        o_ref, w_ref = refs

    @pl.when(pl.program_id(1) == 0)
    def _():
        @pl.when(j < P_SHIFT_FROM)
        def _():
            for r0 in range(0, P_BN, W_ROWS):
                w_ref[r0:r0 + W_ROWS, :] = wa_ref[r0:r0 + W_ROWS, :].astype(BF16)

        @pl.when(j >= P_SHIFT_FROM)
        def _():
            keep = P_BN - GATE_COLS
            for r0 in range(0, keep, W_ROWS):
                r1 = min(r0 + W_ROWS, keep)
                w_ref[r0:r1, :] = wa_ref[GATE_COLS + r0:GATE_COLS + r1, :].astype(BF16)
            w_ref[keep:, :] = wb_ref[...].astype(BF16)

    o_ref[...] = lax.dot_general(h_ref[...], w_ref[...], _NT_DIMS,
                                 preferred_element_type=F32).astype(BF16)
    if side_jobs:
        job.cast(step)
        _mod_matvec(wm_ref, bm_ref, sb_ref, mod_ref, unrolled=True)
        job.after(step)


def _in_proj_call(h, w_in_t, side=None, *, bm, n_tiles):
    m = h.shape[0]
    ni = m // bm
    vmem = (2 * bm * D_MODEL * 2 + 2 * (P_BN + GATE_COLS) * D_MODEL * 4 + P_BN * D_MODEL * 2
            + 2 * bm * P_BN * 2 + bm * P_BN * 4 + 4 * W_ROWS * D_MODEL * 4 + 4 * 1024 * 1024)
    in_specs = [
        pl.BlockSpec((bm, D_MODEL), lambda j, i: (i, 0)),
        pl.BlockSpec((P_BN, D_MODEL), lambda j, i: (j, 0)),
        pl.BlockSpec((GATE_COLS, D_MODEL), lambda j, i: ((j + 1) * (P_BN // GATE_COLS), 0)),
    ]
    out_specs = [pl.BlockSpec((bm, P_BN), lambda j, i: (i, j))]
    out_shape = [jax.ShapeDtypeStruct((m, n_tiles * P_BN), BF16)]
    scratch = [pltpu.VMEM((P_BN, D_MODEL), BF16)]
    operands = [h, w_in_t, w_in_t]
    if side is not None:
        w_gate, w_up, c_t, w_mod, b_mod, first_col = side
        n_steps = n_tiles * ni
        assert n_steps >= 2 * D_MODEL // IN_PROJ_CAST_ROWS + 2
        any_spec = pl.BlockSpec(memory_space=pl.ANY)
        n_mod = w_mod.shape[1] - first_col
        n_blocks = n_mod // LATE_MOD_BN
        assert n_mod % LATE_MOD_BN == 0 and first_col % LATE_MOD_BN == 0 and n_blocks <= n_steps
        mod_block = lambda j, i: jnp.minimum(j * ni + i, n_blocks - 1)
        in_specs += [
            any_spec, any_spec,
            pl.BlockSpec((D_MODEL, SUBLANES), lambda j, i: (0, 0)),
            pl.BlockSpec((D_MODEL, LATE_MOD_BN),
                         lambda j, i: (0, first_col // LATE_MOD_BN + mod_block(j, i))),
            pl.BlockSpec((1, LATE_MOD_BN),
                         lambda j, i: (0, first_col // LATE_MOD_BN + mod_block(j, i))),
        ]
        out_specs += [any_spec, any_spec,
                      pl.BlockSpec((SUBLANES, LATE_MOD_BN), lambda j, i: (0, mod_block(j, i)))]
        out_shape += [jax.ShapeDtypeStruct(w_gate.shape, BF16),
                      jax.ShapeDtypeStruct(w_up.shape, BF16),
                      jax.ShapeDtypeStruct((SUBLANES, n_mod), F32)]
        scratch += _cast_scratch(IN_PROJ_CAST_ROWS, FFN_HIDDEN)
        scratch += [pltpu.VMEM((2, D_MODEL, LANES), F32)]
        operands += [w_gate, w_up, c_t, w_mod, b_mod]
        vmem += (IN_PROJ_CAST_ROWS * FFN_HIDDEN * (2 * 4 + 2 * 2)
                 + 2 * D_MODEL * LATE_MOD_BN * 4 + 2 * D_MODEL * LANES * 4)
    return pl.pallas_call(
        functools.partial(_in_proj_kernel, side_jobs=side is not None),
        grid=(n_tiles, ni),
        in_specs=in_specs,
        out_specs=out_specs,
        out_shape=out_shape,
        scratch_shapes=scratch,
        compiler_params=_params(2, vmem),
        name="in_proj",
    )(*operands)


def _gla_log_decay_split(l_ref, wg_ref, bg_ref):
    z = jnp.dot(l_ref[...], wg_ref[...], preferred_element_type=F32) + bg_ref[...]
    log_a = (jnp.minimum(z, 0.0) - jnp.log(1.0 + jnp.exp(-jnp.abs(z)))) * (1.0 / GATE_NORM)
    hi = log_a.astype(BF16)
    lo = (log_a - hi.astype(F32)).astype(BF16)
    return hi, lo


def _gla_chunk_cumsum(pieces, *, reverse, tb):
    r = lax.broadcasted_iota(jnp.int32, (tb, tb), 0)
    cc = lax.broadcasted_iota(jnp.int32, (tb, tb), 1)
    same_chunk = (r // CHUNK) == (cc // CHUNK)
    tri = jnp.where(same_chunk & ((cc >= r) if reverse else (cc <= r)), 1.0, 0.0).astype(BF16)
    hi, lo = pieces
    return (jnp.dot(tri, hi, preferred_element_type=F32)
            + jnp.dot(tri, lo, preferred_element_type=F32))


def _gla_kernel(qf, kf, vf, lf, lf_next, qb, kb, vb, lb, lb_next, wgf, bgf, wgb, bgb, s0f, s0b,
                *refs, tb, cast):
    i = pl.program_id(0)
    if cast:
        (wd_ref, of, ob, sf, sb, wd_out, s_ref, cum_ref,
         stage_ref, cast_ref, sem_in, sem_out) = refs
        job = _CastJob(((wd_ref, wd_out, FFN_HIDDEN // GLA_CAST_ROWS),), GLA_CAST_ROWS,
                       stage_ref, cast_ref, sem_in, sem_out)
        job.before(i)
    else:
        of, ob, sf, sb, s_ref, cum_ref = refs

    @pl.when(i == 0)
    def _():
        s_ref[0] = s0f[...]
        s_ref[1] = s0b[...]
        cum_ref[0] = _gla_chunk_cumsum(_gla_log_decay_split(lf, wgf, bgf), reverse=False, tb=tb)
        cum_ref[1] = _gla_chunk_cumsum(_gla_log_decay_split(lb, wgb, bgb), reverse=True, tb=tb)

    next_f = _gla_log_decay_split(lf_next, wgf, bgf)
    next_b = _gla_log_decay_split(lb_next, wgb, bgb)
    dirs = ((qf, kf, vf, of, False), (qb, kb, vb, ob, True))
    ii = lax.broadcasted_iota(jnp.int32, (CHUNK, GLA_DK), 0)
    jj = lax.broadcasted_iota(jnp.int32, (CHUNK, GLA_DK), 1)
    nc = tb // CHUNK

    def start(step):
        items = []
        for d, (q_ref, k_ref, v_ref, o_ref, reverse) in enumerate(dirs):
            r0 = (nc - 1 - step if reverse else step) * CHUNK
            cum_c = cum_ref[d, r0:r0 + CHUNK, :]
            last = cum_c[0:1, :] if reverse else cum_c[CHUNK - 1:CHUNK, :]
            q_c = q_ref[r0:r0 + CHUNK, :].astype(F32)
            k_c = k_ref[r0:r0 + CHUNK, :].astype(F32)
            q_e = ((q_c * (GLA_DK ** -0.5)) * jnp.exp(cum_c)).astype(BF16)
            k_e = (k_c * jnp.exp(-cum_c)).astype(BF16)
            k_d = (k_c * jnp.exp(last - cum_c)).astype(BF16)
            decay = jnp.exp(last)
            mask = ((jj >= ii) & (jj < CHUNK)) if reverse else (jj <= ii)
            for h in range(GLA_HEADS):
                ks = slice(h * GLA_DK, (h + 1) * GLA_DK)
                vs = slice(h * GLA_DV, (h + 1) * GLA_DV)
                items.append(dict(
                    d=d, h=h, r0=r0, vs=vs, o_ref=o_ref, mask=mask,
                    q_e=q_e[:, ks], k_e=k_e[:, ks], k_d=k_d[:, ks], decay=decay[:, ks],
                    v=v_ref[r0:r0 + CHUNK, vs]))
        for it in items:
            k_pad = jnp.concatenate([it["k_e"], jnp.zeros((GLA_DK - CHUNK, GLA_DK), BF16)], axis=0)
            it["scores"] = lax.dot_general(it["q_e"], k_pad, _NT_DIMS,
                                           preferred_element_type=F32)
        return items

    items = start(0)
    next_cum = None
    for step in range(nc):
        following = start(step + 1) if step + 1 < nc else None
        if step == nc // 2:
            next_cum = (_gla_chunk_cumsum(next_f, reverse=False, tb=tb),
                        _gla_chunk_cumsum(next_b, reverse=True, tb=tb))
        for it in items:
            scores = jnp.where(it["mask"], it["scores"], 0.0).astype(BF16)
            s_old = s_ref[it["d"], it["h"]]
            lhs = jnp.concatenate([scores, it["q_e"]], axis=1)
            rhs = jnp.concatenate(
                [it["v"], jnp.zeros((GLA_DK - CHUNK, GLA_DV), BF16), s_old.astype(BF16)], axis=0)
            it["o_ref"][it["r0"]:it["r0"] + CHUNK, it["vs"]] = jnp.dot(
                lhs, rhs, preferred_element_type=F32).astype(BF16)
            ds = lax.dot_general(it["k_d"], it["v"], (((0,), (0,)), ((), ())),
                                 preferred_element_type=F32)
            dcol = jnp.transpose(jnp.broadcast_to(it["decay"], (SUBLANES, GLA_DK)))[:, 0:1]
            s_ref[it["d"], it["h"]] = dcol * s_old + ds
        items = following
    cum_ref[0] = next_cum[0]
    cum_ref[1] = next_cum[1]
    if cast:
        job.cast(i)
        job.after(i)

    @pl.when(i == pl.num_programs(0) - 1)
    def _():
        sf[...] = s_ref[0]
        sb[...] = s_ref[1]


def _gla_call(p, l, wgf, bgf, wgb, bgb, s0f, s0b, cast_weight=None, *, tb):
    n = p.shape[0]
    nb = n // tb
    fwd = lambda i: i
    bwd = lambda i: nb - 1 - i
    fwd_next = lambda i: jnp.minimum(i + 1, nb - 1)
    bwd_next = lambda i: jnp.maximum(nb - 2 - i, 0)

    def specs(pos, pos_next):
        return [
            pl.BlockSpec((tb, GLA_KW), lambda i: (pos(i), 0)),
            pl.BlockSpec((tb, GLA_KW), lambda i: (pos(i), 1)),
            pl.BlockSpec((tb, GLA_VW), lambda i: (pos(i), 1)),
            pl.BlockSpec((tb, LANES), lambda i: (pos(i), 0)),
            pl.BlockSpec((tb, LANES), lambda i: (pos_next(i), 0)),
        ]

    const2 = lambda i: (0, 0)
    const3 = lambda i: (0, 0, 0)
    state_shape = (GLA_HEADS, GLA_DK, GLA_DV)
    state_spec = pl.BlockSpec(state_shape, const3)
    cast = cast_weight is not None
    kern = functools.partial(_gla_kernel, tb=tb, cast=cast)
    vmem = (2 * 2 * tb * (2 * GLA_KW + GLA_VW + LANES) * 2 + 2 * 2 * tb * GLA_VW * 2
            + 10 * GLA_HEADS * GLA_DK * GLA_DV * 4 + 24 * tb * GLA_KW * 4 + 8 * 1024 * 1024)
    in_specs = specs(fwd, fwd_next) + specs(bwd, bwd_next) + [
        pl.BlockSpec((LANES, GLA_KW), const2), pl.BlockSpec((1, GLA_KW), const2),
        pl.BlockSpec((LANES, GLA_KW), const2), pl.BlockSpec((1, GLA_KW), const2),
        state_spec, state_spec,
    ]
    out_specs = [
        pl.BlockSpec((tb, GLA_VW), lambda i: (i, 0)),
        pl.BlockSpec((tb, GLA_VW), lambda i: (nb - 1 - i, 0)),
        state_spec, state_spec,
    ]
    out_shape = [
        jax.ShapeDtypeStruct((n, GLA_VW), BF16), jax.ShapeDtypeStruct((n, GLA_VW), BF16),
        jax.ShapeDtypeStruct(state_shape, F32), jax.ShapeDtypeStruct(state_shape, F32),
    ]
    scratch = [pltpu.VMEM((2,) + state_shape, F32), pltpu.VMEM((2, tb, GLA_KW), F32)]
    operands = [p, p, p, l, l, p, p, p, l, l, wgf, bgf, wgb, bgb, s0f, s0b]
    if cast:
        assert nb >= FFN_HIDDEN // GLA_CAST_ROWS + 2
        any_spec = pl.BlockSpec(memory_space=pl.ANY)
        in_specs.append(any_spec)
        out_specs.append(any_spec)
        out_shape.append(jax.ShapeDtypeStruct(cast_weight.shape, BF16))
        scratch += _cast_scratch(GLA_CAST_ROWS, D_MODEL)
        operands.append(cast_weight)
        vmem += GLA_CAST_ROWS * D_MODEL * (2 * 4 + 2 * 2)
    return pl.pallas_call(
        kern,
        grid=(nb,),
        in_specs=in_specs,
        out_specs=out_specs,
        out_shape=out_shape,
        scratch_shapes=scratch,
        compiler_params=_params(1, vmem),
        name="gla",
    )(*operands)


def _mix_kernel(of_ref, ob_ref, g_ref, cb_ref, cc_ref, cx_ref, x_ref, gt_ref, gg_ref, cw_ref,
                wo_ref, o_ref, m_ref, w_ref, *, bm):
    @pl.when(pl.program_id(0) == 0)
    def _():
        for r0 in range(0, D_MODEL, W_ROWS):
            w_ref[r0:r0 + W_ROWS, :] = wo_ref[r0:r0 + W_ROWS, :].astype(BF16)

    u = cc_ref[...].astype(F32) * cx_ref[...].astype(F32)
    col = lax.broadcasted_iota(jnp.int32, u.shape, 0) % GRID_W
    u_prev = jnp.where(col == 0, 0.0, pltpu.roll(u, 1, 0))
    u_next = jnp.where(col == GRID_W - 1, 0.0, pltpu.roll(u, bm - 1, 0))
    conv = cw_ref[0:1, :] * u_prev + cw_ref[1:2, :] * u + cw_ref[2:3, :] * u_next
    m_ref[:, GLA_VW:] = (cb_ref[...].astype(F32) * conv).astype(BF16)
    mixed = jnp.dot(m_ref[:, GLA_VW:], w_ref[GLA_VW:, :], preferred_element_type=F32)

    gg = gg_ref[...]
    for h in range(GLA_HEADS):
        vs = slice(h * GLA_DV, (h + 1) * GLA_DV)
        o = of_ref[:, vs].astype(F32) + ob_ref[:, vs].astype(F32)
        o32 = o * lax.rsqrt(jnp.mean(o * o, axis=-1, keepdims=True) + EPS)
        m_ref[:, vs] = ((o32 * gg) * _silu(g_ref[:, vs].astype(F32))).astype(BF16)
        mixed = mixed + jnp.dot(m_ref[:, vs], w_ref[vs, :], preferred_element_type=F32)
    o_ref[...] = x_ref[...] + gt_ref[0:1, :] * mixed


def _mix_call(o_f, o_b, p, x, mod, gla_g, conv_w, w_out, *, bm):
    n = x.shape[0]
    kern = functools.partial(_mix_kernel, bm=bm)
    pcol = lambda k: pl.BlockSpec((bm, GLA_VW), lambda i, k=k: (i, k))
    const2 = lambda i: (0, 0)
    vmem = (2 * 6 * bm * GLA_VW * 2 + 2 * 2 * bm * D_MODEL * 4
            + D_MODEL * D_MODEL * (4 + 2) + bm * D_MODEL * 2 + 8 * bm * GLA_VW * 4
            + 4 * 1024 * 1024)
    return pl.pallas_call(
        kern,
        grid=(n // bm,),
        in_specs=[
            pl.BlockSpec((bm, GLA_VW), lambda i: (i, 0)),
            pl.BlockSpec((bm, GLA_VW), lambda i: (i, 0)),
            pcol(2), pcol(3), pcol(4), pcol(5),
            pl.BlockSpec((bm, D_MODEL), lambda i: (i, 0)),
            pl.BlockSpec((SUBLANES, D_MODEL), lambda i: (0, 0)),
            pl.BlockSpec((1, GLA_DV), const2),
            pl.BlockSpec((3, CONV_W), const2),
            pl.BlockSpec((D_MODEL, D_MODEL), const2, pipeline_mode=pl.Buffered(1)),
        ],
        out_specs=pl.BlockSpec((bm, D_MODEL), lambda i: (i, 0)),
        out_shape=jax.ShapeDtypeStruct((n, D_MODEL), F32),
        scratch_shapes=[pltpu.VMEM((bm, D_MODEL), BF16), pltpu.VMEM((D_MODEL, D_MODEL), BF16)],
        compiler_params=_params(1, vmem),
        name="mix",
    )(o_f, o_b, p, p, p, p, x, mod, gla_g, conv_w, w_out)


def _ffn_kernel(x_ref, g_ref, sh_ref, sc_ref, gt_ref, wg_ref, wu_ref, wd_ref, fg_ref,
                o_ref, h_ref, *, bm):
    j = pl.program_id(1)

    @pl.when(j == 0)
    def _():
        _norm_mod_rows(x_ref, g_ref[...], 1.0 + sc_ref[0:1, :], sh_ref[0:1, :], h_ref, bm,
                       copy_ref=o_ref)

    h = h_ref[...]
    gate = jnp.dot(h, wg_ref[...], preferred_element_type=F32)
    up = jnp.dot(h, wu_ref[...], preferred_element_type=F32)
    part = jnp.dot((_silu(gate) * up).astype(BF16), wd_ref[...], preferred_element_type=F32)
    o_ref[...] += gt_ref[0:1, :] * part

    @pl.when(j == pl.num_programs(1) - 1)
    def _():
        fg = fg_ref[...]

        def final_norm(rs):
            xn = o_ref[rs, :]
            ms = jnp.mean(xn * xn, axis=-1, keepdims=True)
            o_ref[rs, :] = (xn * lax.rsqrt(ms + EPS)) * fg
        _for_row_chunks(bm, final_norm)


def _ffn_call(x, g, mod, wg, wu, wd, fg, *, bm):
    n = x.shape[0]
    kern = functools.partial(_ffn_kernel, bm=bm)
    modspec = lambda k: pl.BlockSpec((SUBLANES, D_MODEL), lambda i, j, k=k: (0, k))
    vmem = (3 * bm * D_MODEL * 4 + bm * D_MODEL * 2 + 2 * 3 * D_MODEL * FFN_BH * 2
            + 4 * bm * FFN_BH * 4 + bm * D_MODEL * 4 + 4 * 1024 * 1024)
    return pl.pallas_call(
        kern,
        grid=(n // bm, FFN_HIDDEN // FFN_BH),
        in_specs=[
            pl.BlockSpec((bm, D_MODEL), lambda i, j: (i, 0)),
            pl.BlockSpec((1, D_MODEL), lambda i, j: (0, 0)),
            modspec(1), modspec(2), modspec(3),
            pl.BlockSpec((D_MODEL, FFN_BH), lambda i, j: (0, j)),
            pl.BlockSpec((D_MODEL, FFN_BH), lambda i, j: (0, j)),
            pl.BlockSpec((FFN_BH, D_MODEL), lambda i, j: (j, 0)),
            pl.BlockSpec((1, D_MODEL), lambda i, j: (0, 0)),
        ],
        out_specs=pl.BlockSpec((bm, D_MODEL), lambda i, j: (i, 0)),
        out_shape=jax.ShapeDtypeStruct((n, D_MODEL), F32),
        scratch_shapes=[pltpu.VMEM((bm, D_MODEL), BF16)],
        compiler_params=_params(2, vmem),
        name="ffn",
    )(x, g, mod, mod, mod, wg, wu, wd, fg)


def kernel(x, c, ctx, c_ctx, w_mod, b_mod, norm1_g, norm2_g, w_in, w_gate_f, b_gate_f,
           w_gate_b, b_gate_b, gla_norm_g, conv_w, w_out, w_ffn_gate, w_ffn_up, w_ffn_down,
           final_g):
    assert x.shape[0] == 1 and w_mod.shape[0] == 1, "batch 1, depth 1 only"
    seq, ctx_len = x.shape[1], ctx.shape[1]
    x2, ctx2 = x[0], ctx[0]

    c_t = jnp.concatenate(
        [c[0][:, None], c_ctx[:, None], jnp.zeros((D_MODEL, SUBLANES - 2), F32)], axis=1)
    mod = _mod_call(c_t, w_mod[0], b_mod, EARLY_MOD_COLS)

    pad_f = jnp.zeros((LANES, GLA_KW), F32).at[:GATE_RANK].set(w_gate_f[0]).astype(BF16)
    pad_b = jnp.zeros((LANES, GLA_KW), F32).at[GATE_RANK:2 * GATE_RANK].set(w_gate_b[0]).astype(BF16)
    g1 = norm1_g[0][None]
    w_in_t = jnp.transpose(w_in[0])

    h_c, l_c = _prenorm_call(ctx2, g1, mod, w_in_t, row=1, bm=ctx_len)
    p_c, = _in_proj_call(h_c, w_in_t, bm=ctx_len, n_tiles=2)
    zero_state = jnp.zeros((GLA_HEADS, GLA_DK, GLA_DV), F32)
    _, _, s_f, s_b = _gla_call(p_c, l_c, pad_f, b_gate_f, pad_b, b_gate_b, zero_state,
                               zero_state, tb=ctx_len)

    h_x, l_x = _prenorm_call(x2, g1, mod, w_in_t, row=0, bm=1024)
    p_x, wg_b, wu_b, mod_late = _in_proj_call(
        h_x, w_in_t, (w_ffn_gate[0], w_ffn_up[0], c_t, w_mod[0], b_mod, EARLY_MOD_COLS),
        bm=1024, n_tiles=P_COLS // P_BN)
    o_f, o_b, _, _, wd_b = _gla_call(p_x, l_x, pad_f, b_gate_f, pad_b, b_gate_b, s_f, s_b,
                                     w_ffn_down[0], tb=256)
    x_mid = _mix_call(o_f, o_b, p_x, x2, mod_late, gla_norm_g, conv_w[0], w_out[0],
                      bm=256)
    out = _ffn_call(x_mid, norm2_g[0][None], mod_late, wg_b, wu_b, wd_b, final_g[None],
                    bm=1024)
    return out[None]
```

```python
import functools

import jax
import jax.numpy as jnp
from jax import lax
from jax.experimental import pallas as pl
from jax.experimental.pallas import tpu as pltpu

F32 = jnp.float32
BF16 = jnp.bfloat16

D_MODEL = 2048
GRID_W = 64
GLA_HEADS = 4
GLA_DK = 128
GLA_DV = 256
GLA_KW = GLA_HEADS * GLA_DK
GLA_VW = GLA_HEADS * GLA_DV
GATE_RANK = 16
GATE_NORM = 16.0
CHUNK = 64
CONV_W = D_MODEL - GLA_VW
FFN_HIDDEN = 5632
EPS = 1e-6

LANES = 128
SUBLANES = 8
VMEM_BYTES = 64 * 1024 * 1024

GATE_COL = 2 * GLA_KW + 2 * GLA_VW
GATE_COLS = 2 * GATE_RANK
P_COLS = GATE_COL + 3 * CONV_W
P_BN = 1024
P_SHIFT_FROM = GATE_COL // P_BN

_NT_DIMS = (((1,), (1,)), ((), ()))

MOD_BN = 1024
EARLY_MOD_COLS = 2 * D_MODEL
LATE_MOD_BN = 256
FFN_BH = 512
NORM_ROWS = 16


def _params(grid_rank, vmem_bytes):
    return pltpu.CompilerParams(
        dimension_semantics=("arbitrary",) * grid_rank,
        vmem_limit_bytes=min(int(vmem_bytes), VMEM_BYTES - 6 * 1024 * 1024),
    )


def _silu(v):
    return v * jax.nn.sigmoid(v)


def _for_row_chunks(rows, body):
    for r0 in range(0, rows, NORM_ROWS):
        body(slice(r0, r0 + NORM_ROWS))


def _norm_mod_rows(x_ref, g, mul, sh, dst_ref, rows, copy_ref=None):
    def body(rs):
        xs = x_ref[rs, :]
        ms = jnp.mean(xs * xs, axis=-1, keepdims=True)
        y = xs * lax.rsqrt(ms + EPS)
        dst_ref[rs, :] = ((y * g) * mul + sh).astype(BF16)
        if copy_ref is not None:
            copy_ref[rs, :] = xs
    _for_row_chunks(rows, body)


def _mod_prepare(c_ref, sb_ref):
    s = _silu(c_ref[...])
    for r in range(2):
        sb_ref[r] = jnp.broadcast_to(s[:, r:r + 1], (D_MODEL, LANES))


def _mod_matvec(w_ref, b_ref, sb_ref, o_ref, *, unrolled):
    nt = w_ref.shape[1] // LANES

    def body(kk, accs):
        r0 = kk * SUBLANES if unrolled else pl.multiple_of(kk * SUBLANES, SUBLANES)
        s0 = sb_ref[0, pl.ds(r0, SUBLANES), :]
        s1 = sb_ref[1, pl.ds(r0, SUBLANES), :]
        new = []
        for t in range(nt):
            w8 = w_ref[pl.ds(r0, SUBLANES), t * LANES:(t + 1) * LANES]
            new.append(accs[2 * t] + w8 * s0)
            new.append(accs[2 * t + 1] + w8 * s1)
        return tuple(new)

    accs = (jnp.zeros((SUBLANES, LANES), F32),) * (2 * nt)
    if unrolled:
        for kk in range(D_MODEL // SUBLANES):
            accs = body(kk, accs)
    else:
        accs = lax.fori_loop(0, D_MODEL // SUBLANES, body, accs, unroll=8)
    o_ref[...] = jnp.zeros(o_ref.shape, F32)
    for t in range(nt):
        b = b_ref[:, t * LANES:(t + 1) * LANES]
        for r in range(2):
            o_ref[r:r + 1, t * LANES:(t + 1) * LANES] = (
                jnp.sum(accs[2 * t + r], axis=0, keepdims=True) + b)


def _mod_kernel(c_ref, w_ref, b_ref, o_ref, sb_ref):
    @pl.when(pl.program_id(0) == 0)
    def _():
        _mod_prepare(c_ref, sb_ref)

    _mod_matvec(w_ref, b_ref, sb_ref, o_ref, unrolled=False)


def _mod_call(c_t, w_mod, b_mod, n):
    return pl.pallas_call(
        _mod_kernel,
        grid=(n // MOD_BN,),
        in_specs=[
            pl.BlockSpec((D_MODEL, SUBLANES), lambda j: (0, 0)),
            pl.BlockSpec((D_MODEL, MOD_BN), lambda j: (0, j)),
            pl.BlockSpec((1, MOD_BN), lambda j: (0, j)),
        ],
        out_specs=pl.BlockSpec((SUBLANES, MOD_BN), lambda j: (0, j)),
        out_shape=jax.ShapeDtypeStruct((SUBLANES, n), F32),
        scratch_shapes=[pltpu.VMEM((2, D_MODEL, LANES), F32)],
        compiler_params=_params(1, 2 * D_MODEL * MOD_BN * 4 + 8 * 1024 * 1024),
        name="mod",
    )(c_t, w_mod, b_mod)


def _prenorm_kernel(x_ref, g_ref, sh_ref, sc_ref, wl_ref, h_ref, l_ref, *, row, bm):
    _norm_mod_rows(x_ref, g_ref[...], 1.0 + sc_ref[row:row + 1, :], sh_ref[row:row + 1, :],
                   h_ref, bm)
    l_ref[...] = lax.dot_general(h_ref[...], wl_ref[...].astype(BF16), _NT_DIMS,
                                 preferred_element_type=F32).astype(BF16)


def _prenorm_call(x, g, mod, w_in_t, *, row, bm):
    m = x.shape[0]
    kern = functools.partial(_prenorm_kernel, row=row, bm=bm)
    vmem = (2 * bm * D_MODEL * (4 + 2) + 4 * D_MODEL * LANES * 4 + 8 * NORM_ROWS * D_MODEL * 4
            + 4 * 1024 * 1024)
    return pl.pallas_call(
        kern,
        grid=(m // bm,),
        in_specs=[
            pl.BlockSpec((bm, D_MODEL), lambda i: (i, 0)),
            pl.BlockSpec((1, D_MODEL), lambda i: (0, 0)),
            pl.BlockSpec((SUBLANES, D_MODEL), lambda i: (0, 0)),
            pl.BlockSpec((SUBLANES, D_MODEL), lambda i: (0, 1)),
            pl.BlockSpec((LANES, D_MODEL), lambda i: (GATE_COL // LANES, 0)),
        ],
        out_specs=[pl.BlockSpec((bm, D_MODEL), lambda i: (i, 0)),
                   pl.BlockSpec((bm, LANES), lambda i: (i, 0))],
        out_shape=[jax.ShapeDtypeStruct((m, D_MODEL), BF16),
                   jax.ShapeDtypeStruct((m, LANES), BF16)],
        compiler_params=_params(1, vmem),
        name="prenorm",
    )(x, g, mod, mod, w_in_t)


W_ROWS = 256


class _CastJob:
    def __init__(self, sources, rows, stage_ref, out_ref, sem_in, sem_out):
        self.sources, self.rows = sources, rows
        self.stage_ref, self.out_ref, self.sem_in, self.sem_out = stage_ref, out_ref, sem_in, sem_out
        self.n_blocks = sum(n for _, _, n in sources)

    def _for_block(self, c, fn):
        first = 0
        for src, dst, n in self.sources:
            @pl.when((c >= first) & (c < first + n))
            def _(src=src, dst=dst, first=first):
                fn(src, dst, pl.multiple_of((c - first) * self.rows, self.rows), lax.rem(c, 2))
            first += n

    def _fetch(self, src, dst, r, slot):
        return pltpu.make_async_copy(src.at[pl.ds(r, self.rows), :], self.stage_ref.at[slot],
                                     self.sem_in.at[slot])

    def _store(self, src, dst, r, slot):
        return pltpu.make_async_copy(self.out_ref.at[slot], dst.at[pl.ds(r, self.rows), :],
                                     self.sem_out.at[slot])

    def before(self, step):
        @pl.when(step == 0)
        def _():
            self._for_block(step, lambda *a: self._fetch(*a).start())
        self._for_block(step, lambda *a: self._fetch(*a).wait())
        self._for_block(step + 1, lambda *a: self._fetch(*a).start())
        self._for_block(step - 2, lambda *a: self._store(*a).wait())

    def cast(self, step):
        slot = lax.rem(step, 2)
        self.out_ref[slot] = self.stage_ref[slot].astype(BF16)

    def after(self, step):
        self._for_block(step, lambda *a: self._store(*a).start())


def _cast_scratch(rows, cols):
    return [pltpu.VMEM((2, rows, cols), F32), pltpu.VMEM((2, rows, cols), BF16),
            pltpu.SemaphoreType.DMA((2,)), pltpu.SemaphoreType.DMA((2,))]


IN_PROJ_CAST_ROWS = 128
GLA_CAST_ROWS = 256


def _in_proj_kernel(h_ref, wa_ref, wb_ref, *refs, side_jobs):
    j = pl.program_id(0)
    if side_jobs:
        wg_ref, wu_ref, o_ref, wg_out, wu_out, w_ref, stage_ref, cast_ref, sem_in, sem_out = refs
        n = D_MODEL // IN_PROJ_CAST_ROWS
        job = _CastJob(((wg_ref, wg_out, n), (wu_ref, wu_out, n)), IN_PROJ_CAST_ROWS,
                       stage_ref, cast_ref, sem_in, sem_out)
        step = j * pl.num_programs(1) + pl.program_id(1)
        job.before(step)
    else:
        o_ref, w_ref = refs

    @pl.when(pl.program_id(1) == 0)
    def _():
        @pl.when(j < P_SHIFT_FROM)
        def _():
            for r0 in range(0, P_BN, W_ROWS):
                w_ref[r0:r0 + W_ROWS, :] = wa_ref[r0:r0 + W_ROWS, :].astype(BF16)

        @pl.when(j >= P_SHIFT_FROM)
        def _():
            keep = P_BN - GATE_COLS
            for r0 in range(0, keep, W_ROWS):
                r1 = min(r0 + W_ROWS, keep)
                w_ref[r0:r1, :] = wa_ref[GATE_COLS + r0:GATE_COLS + r1, :].astype(BF16)
            w_ref[keep:, :] = wb_ref[...].astype(BF16)

    o_ref[...] = lax.dot_general(h_ref[...], w_ref[...], _NT_DIMS,
                                 preferred_element_type=F32).astype(BF16)
    if side_jobs:
        job.cast(step)
        job.after(step)


def _in_proj_call(h, w_in_t, side=None, *, bm, n_tiles):
    m = h.shape[0]
    ni = m // bm
    vmem = (2 * bm * D_MODEL * 2 + 2 * (P_BN + GATE_COLS) * D_MODEL * 4 + P_BN * D_MODEL * 2
            + 2 * bm * P_BN * 2 + bm * P_BN * 4 + 4 * W_ROWS * D_MODEL * 4 + 4 * 1024 * 1024)
    in_specs = [
        pl.BlockSpec((bm, D_MODEL), lambda j, i: (i, 0)),
        pl.BlockSpec((P_BN, D_MODEL), lambda j, i: (j, 0)),
        pl.BlockSpec((GATE_COLS, D_MODEL), lambda j, i: ((j + 1) * (P_BN // GATE_COLS), 0)),
    ]
    out_specs = [pl.BlockSpec((bm, P_BN), lambda j, i: (i, j))]
    out_shape = [jax.ShapeDtypeStruct((m, n_tiles * P_BN), BF16)]
    scratch = [pltpu.VMEM((P_BN, D_MODEL), BF16)]
    operands = [h, w_in_t, w_in_t]
    if side is not None:
        w_gate, w_up = side
        assert n_tiles * ni >= 2 * D_MODEL // IN_PROJ_CAST_ROWS + 2
        any_spec = pl.BlockSpec(memory_space=pl.ANY)
        in_specs += [any_spec, any_spec]
        out_specs += [any_spec, any_spec]
        out_shape += [jax.ShapeDtypeStruct(w_gate.shape, BF16),
                      jax.ShapeDtypeStruct(w_up.shape, BF16)]
        scratch += _cast_scratch(IN_PROJ_CAST_ROWS, FFN_HIDDEN)
        operands += [w_gate, w_up]
        vmem += IN_PROJ_CAST_ROWS * FFN_HIDDEN * (2 * 4 + 2 * 2)
    return pl.pallas_call(
        functools.partial(_in_proj_kernel, side_jobs=side is not None),
        grid=(n_tiles, ni),
        in_specs=in_specs,
        out_specs=out_specs,
        out_shape=out_shape,
        scratch_shapes=scratch,
        compiler_params=_params(2, vmem),
        name="in_proj",
    )(*operands)


def _gla_log_decay_split(l_ref, wg_ref, bg_ref):
    z = jnp.dot(l_ref[...], wg_ref[...], preferred_element_type=F32) + bg_ref[...]
    log_a = (jnp.minimum(z, 0.0) - jnp.log(1.0 + jnp.exp(-jnp.abs(z)))) * (1.0 / GATE_NORM)
    hi = log_a.astype(BF16)
    lo = (log_a - hi.astype(F32)).astype(BF16)
    return hi, lo


def _gla_chunk_cumsum(pieces, *, reverse, tb):
    r = lax.broadcasted_iota(jnp.int32, (tb, tb), 0)
    cc = lax.broadcasted_iota(jnp.int32, (tb, tb), 1)
    same_chunk = (r // CHUNK) == (cc // CHUNK)
    tri = jnp.where(same_chunk & ((cc >= r) if reverse else (cc <= r)), 1.0, 0.0).astype(BF16)
    hi, lo = pieces
    return (jnp.dot(tri, hi, preferred_element_type=F32)
            + jnp.dot(tri, lo, preferred_element_type=F32))


def _gla_kernel(qf, kf, vf, lf, lf_next, qb, kb, vb, lb, lb_next, wgf, bgf, wgb, bgb, s0f, s0b,
                *refs, tb, side_jobs):
    i = pl.program_id(0)
    cast = side_jobs
    if side_jobs:
        (wd_ref, c_ref, wm_ref, bm_ref, of, ob, sf, sb, wd_out, mod_ref,
         s_ref, cum_ref, stage_ref, cast_ref, sem_in, sem_out, sb_ref) = refs
        job = _CastJob(((wd_ref, wd_out, FFN_HIDDEN // GLA_CAST_ROWS),), GLA_CAST_ROWS,
                       stage_ref, cast_ref, sem_in, sem_out)
        job.before(i)

        @pl.when(i == 0)
        def _():
            _mod_prepare(c_ref, sb_ref)
    else:
        of, ob, sf, sb, s_ref, cum_ref = refs

    @pl.when(i == 0)
    def _():
        s_ref[0] = s0f[...]
        s_ref[1] = s0b[...]
        cum_ref[0] = _gla_chunk_cumsum(_gla_log_decay_split(lf, wgf, bgf), reverse=False, tb=tb)
        cum_ref[1] = _gla_chunk_cumsum(_gla_log_decay_split(lb, wgb, bgb), reverse=True, tb=tb)

    dirs = ((qf, kf, vf, of, False), (qb, kb, vb, ob, True))
    ii = lax.broadcasted_iota(jnp.int32, (CHUNK, GLA_DK), 0)
    jj = lax.broadcasted_iota(jnp.int32, (CHUNK, GLA_DK), 1)
    nc = tb // CHUNK

    def start(step):
        items = []
        for d, (q_ref, k_ref, v_ref, o_ref, reverse) in enumerate(dirs):
            r0 = (nc - 1 - step if reverse else step) * CHUNK
            cum_c = cum_ref[d, r0:r0 + CHUNK, :]
            last = cum_c[0:1, :] if reverse else cum_c[CHUNK - 1:CHUNK, :]
            q_c = q_ref[r0:r0 + CHUNK, :].astype(F32)
            k_c = k_ref[r0:r0 + CHUNK, :].astype(F32)
            q_e = ((q_c * (GLA_DK ** -0.5)) * jnp.exp(cum_c)).astype(BF16)
            k_e = (k_c * jnp.exp(-cum_c)).astype(BF16)
            k_d = (k_c * jnp.exp(last - cum_c)).astype(BF16)
            decay = jnp.exp(last)
            mask = ((jj >= ii) & (jj < CHUNK)) if reverse else (jj <= ii)
            for h in range(GLA_HEADS):
                ks = slice(h * GLA_DK, (h + 1) * GLA_DK)
                vs = slice(h * GLA_DV, (h + 1) * GLA_DV)
                items.append(dict(
                    d=d, h=h, r0=r0, vs=vs, o_ref=o_ref, mask=mask,
                    q_e=q_e[:, ks], k_e=k_e[:, ks], k_d=k_d[:, ks], decay=decay[:, ks],
                    v=v_ref[r0:r0 + CHUNK, vs]))
        for it in items:
            k_pad = jnp.concatenate([it["k_e"], jnp.zeros((GLA_DK - CHUNK, GLA_DK), BF16)], axis=0)
            it["scores"] = lax.dot_general(it["q_e"], k_pad, _NT_DIMS,
                                           preferred_element_type=F32)
        return items

    items = start(0)
    next_cum = None
    for step in range(nc):
        following = start(step + 1) if step + 1 < nc else None
        if step == 1:
            next_f = _gla_log_decay_split(lf_next, wgf, bgf)
            next_b = _gla_log_decay_split(lb_next, wgb, bgb)
        if step == nc - 1:
            next_cum = (_gla_chunk_cumsum(next_f, reverse=False, tb=tb),
                        _gla_chunk_cumsum(next_b, reverse=True, tb=tb))
        for it in items:
            scores = jnp.where(it["mask"], it["scores"], 0.0).astype(BF16)
            s_old = s_ref[it["d"], it["h"]]
            lhs = jnp.concatenate([scores, it["q_e"]], axis=1)
            rhs = jnp.concatenate(
                [it["v"], jnp.zeros((GLA_DK - CHUNK, GLA_DV), BF16), s_old.astype(BF16)], axis=0)
            it["o_ref"][it["r0"]:it["r0"] + CHUNK, it["vs"]] = jnp.dot(
                lhs, rhs, preferred_element_type=F32).astype(BF16)
            ds = lax.dot_general(it["k_d"], it["v"], (((0,), (0,)), ((), ())),
                                 preferred_element_type=F32)
            dcol = jnp.transpose(jnp.broadcast_to(it["decay"], (SUBLANES, GLA_DK)))[:, 0:1]
            s_ref[it["d"], it["h"]] = dcol * s_old + ds
        items = following
    cum_ref[0] = next_cum[0]
    cum_ref[1] = next_cum[1]
    if side_jobs:
        job.cast(i)
        _mod_matvec(wm_ref, bm_ref, sb_ref, mod_ref, unrolled=True)
        job.after(i)

    @pl.when(i == pl.num_programs(0) - 1)
    def _():
        sf[...] = s_ref[0]
        sb[...] = s_ref[1]


def _gla_call(p, l, wgf, bgf, wgb, bgb, s0f, s0b, side=None, *, tb):
    n = p.shape[0]
    nb = n // tb
    fwd = lambda i: i
    bwd = lambda i: nb - 1 - i
    fwd_next = lambda i: jnp.minimum(i + 1, nb - 1)
    bwd_next = lambda i: jnp.maximum(nb - 2 - i, 0)

    def specs(pos, pos_next):
        return [
            pl.BlockSpec((tb, GLA_KW), lambda i: (pos(i), 0)),
            pl.BlockSpec((tb, GLA_KW), lambda i: (pos(i), 1)),
            pl.BlockSpec((tb, GLA_VW), lambda i: (pos(i), 1)),
            pl.BlockSpec((tb, LANES), lambda i: (pos(i), 0)),
            pl.BlockSpec((tb, LANES), lambda i: (pos_next(i), 0)),
        ]

    const2 = lambda i: (0, 0)
    const3 = lambda i: (0, 0, 0)
    state_shape = (GLA_HEADS, GLA_DK, GLA_DV)
    state_spec = pl.BlockSpec(state_shape, const3)
    kern = functools.partial(_gla_kernel, tb=tb, side_jobs=side is not None)
    vmem = (2 * 2 * tb * (2 * GLA_KW + GLA_VW + LANES) * 2 + 2 * 2 * tb * GLA_VW * 2
            + 10 * GLA_HEADS * GLA_DK * GLA_DV * 4 + 24 * tb * GLA_KW * 4 + 8 * 1024 * 1024)
    in_specs = specs(fwd, fwd_next) + specs(bwd, bwd_next) + [
        pl.BlockSpec((LANES, GLA_KW), const2), pl.BlockSpec((1, GLA_KW), const2),
        pl.BlockSpec((LANES, GLA_KW), const2), pl.BlockSpec((1, GLA_KW), const2),
        state_spec, state_spec,
    ]
    out_specs = [
        pl.BlockSpec((tb, GLA_VW), lambda i: (i, 0)),
        pl.BlockSpec((tb, GLA_VW), lambda i: (nb - 1 - i, 0)),
        state_spec, state_spec,
    ]
    out_shape = [
        jax.ShapeDtypeStruct((n, GLA_VW), BF16), jax.ShapeDtypeStruct((n, GLA_VW), BF16),
        jax.ShapeDtypeStruct(state_shape, F32), jax.ShapeDtypeStruct(state_shape, F32),
    ]
    scratch = [pltpu.VMEM((2,) + state_shape, F32), pltpu.VMEM((2, tb, GLA_KW), F32)]
    operands = [p, p, p, l, l, p, p, p, l, l, wgf, bgf, wgb, bgb, s0f, s0b]
    if side is not None:
        w_down, c_t, w_mod, b_mod, first_col = side
        assert w_down.shape == (FFN_HIDDEN, D_MODEL) and nb >= FFN_HIDDEN // GLA_CAST_ROWS + 2
        n_mod = w_mod.shape[1] - first_col
        assert n_mod == nb * LATE_MOD_BN and first_col % LATE_MOD_BN == 0
        any_spec = pl.BlockSpec(memory_space=pl.ANY)
        in_specs += [
            any_spec,
            pl.BlockSpec((D_MODEL, SUBLANES), const2),
            pl.BlockSpec((D_MODEL, LATE_MOD_BN), lambda i: (0, first_col // LATE_MOD_BN + i)),
            pl.BlockSpec((1, LATE_MOD_BN), lambda i: (0, first_col // LATE_MOD_BN + i)),
        ]
        out_specs += [any_spec, pl.BlockSpec((SUBLANES, LATE_MOD_BN), lambda i: (0, i))]
        out_shape += [jax.ShapeDtypeStruct(w_down.shape, BF16),
                      jax.ShapeDtypeStruct((SUBLANES, n_mod), F32)]
        scratch += _cast_scratch(GLA_CAST_ROWS, D_MODEL)
        scratch += [pltpu.VMEM((2, D_MODEL, LANES), F32)]
        operands += [w_down, c_t, w_mod, b_mod]
        vmem += (GLA_CAST_ROWS * D_MODEL * (2 * 4 + 2 * 2)
                 + 2 * D_MODEL * LATE_MOD_BN * 4 + 2 * D_MODEL * LANES * 4)
    return pl.pallas_call(
        kern,
        grid=(nb,),
        in_specs=in_specs,
        out_specs=out_specs,
        out_shape=out_shape,
        scratch_shapes=scratch,
        compiler_params=_params(1, vmem),
        name="gla",
    )(*operands)


def _mix_kernel(of_ref, ob_ref, g_ref, cb_ref, cc_ref, cx_ref, x_ref, gt_ref, gg_ref, cw_ref,
                wo_ref, o_ref, m_ref, w_ref, *, bm):
    @pl.when(pl.program_id(0) == 0)
    def _():
        for r0 in range(0, D_MODEL, W_ROWS):
            w_ref[r0:r0 + W_ROWS, :] = wo_ref[r0:r0 + W_ROWS, :].astype(BF16)

    u = cc_ref[...].astype(F32) * cx_ref[...].astype(F32)
    col = lax.broadcasted_iota(jnp.int32, u.shape, 0) % GRID_W
    u_prev = jnp.where(col == 0, 0.0, pltpu.roll(u, 1, 0))
    u_next = jnp.where(col == GRID_W - 1, 0.0, pltpu.roll(u, bm - 1, 0))
    conv = cw_ref[0:1, :] * u_prev + cw_ref[1:2, :] * u + cw_ref[2:3, :] * u_next
    m_ref[:, GLA_VW:] = (cb_ref[...].astype(F32) * conv).astype(BF16)
    mixed = jnp.dot(m_ref[:, GLA_VW:], w_ref[GLA_VW:, :], preferred_element_type=F32)

    gg = gg_ref[...]
    for h in range(GLA_HEADS):
        vs = slice(h * GLA_DV, (h + 1) * GLA_DV)
        o = of_ref[:, vs].astype(F32) + ob_ref[:, vs].astype(F32)
        o32 = o * lax.rsqrt(jnp.mean(o * o, axis=-1, keepdims=True) + EPS)
        m_ref[:, vs] = ((o32 * gg) * _silu(g_ref[:, vs].astype(F32))).astype(BF16)
        mixed = mixed + jnp.dot(m_ref[:, vs], w_ref[vs, :], preferred_element_type=F32)
    o_ref[...] = x_ref[...] + gt_ref[0:1, :] * mixed


def _mix_call(o_f, o_b, p, x, mod, gla_g, conv_w, w_out, *, bm):
    n = x.shape[0]
    kern = functools.partial(_mix_kernel, bm=bm)
    pcol = lambda k: pl.BlockSpec((bm, GLA_VW), lambda i, k=k: (i, k))
    const2 = lambda i: (0, 0)
    vmem = (2 * 6 * bm * GLA_VW * 2 + 2 * 2 * bm * D_MODEL * 4
            + D_MODEL * D_MODEL * (4 + 2) + bm * D_MODEL * 2 + 8 * bm * GLA_VW * 4
            + 4 * 1024 * 1024)
    return pl.pallas_call(
        kern,
        grid=(n // bm,),
        in_specs=[
            pl.BlockSpec((bm, GLA_VW), lambda i: (i, 0)),
            pl.BlockSpec((bm, GLA_VW), lambda i: (i, 0)),
            pcol(2), pcol(3), pcol(4), pcol(5),
            pl.BlockSpec((bm, D_MODEL), lambda i: (i, 0)),
            pl.BlockSpec((SUBLANES, D_MODEL), lambda i: (0, 0)),
            pl.BlockSpec((1, GLA_DV), const2),
            pl.BlockSpec((3, CONV_W), const2),
            pl.BlockSpec((D_MODEL, D_MODEL), const2, pipeline_mode=pl.Buffered(1)),
        ],
        out_specs=pl.BlockSpec((bm, D_MODEL), lambda i: (i, 0)),
        out_shape=jax.ShapeDtypeStruct((n, D_MODEL), F32),
        scratch_shapes=[pltpu.VMEM((bm, D_MODEL), BF16), pltpu.VMEM((D_MODEL, D_MODEL), BF16)],
        compiler_params=_params(1, vmem),
        name="mix",
    )(o_f, o_b, p, p, p, p, x, mod, gla_g, conv_w, w_out)


def _ffn_kernel(x_ref, g_ref, sh_ref, sc_ref, gt_ref, wg_ref, wu_ref, wd_ref, fg_ref,
                o_ref, h_ref, *, bm):
    j = pl.program_id(1)

    @pl.when(j == 0)
    def _():
        _norm_mod_rows(x_ref, g_ref[...], 1.0 + sc_ref[0:1, :], sh_ref[0:1, :], h_ref, bm,
                       copy_ref=o_ref)

    h = h_ref[...]
    gate = jnp.dot(h, wg_ref[...], preferred_element_type=F32)
    up = jnp.dot(h, wu_ref[...], preferred_element_type=F32)
    part = jnp.dot((_silu(gate) * up).astype(BF16), wd_ref[...], preferred_element_type=F32)
    o_ref[...] += gt_ref[0:1, :] * part

    @pl.when(j == pl.num_programs(1) - 1)
    def _():
        fg = fg_ref[...]

        def final_norm(rs):
            xn = o_ref[rs, :]
            ms = jnp.mean(xn * xn, axis=-1, keepdims=True)
            o_ref[rs, :] = (xn * lax.rsqrt(ms + EPS)) * fg
        _for_row_chunks(bm, final_norm)


def _ffn_call(x, g, mod, wg, wu, wd, fg, *, bm):
    n = x.shape[0]
    kern = functools.partial(_ffn_kernel, bm=bm)
    modspec = lambda k: pl.BlockSpec((SUBLANES, D_MODEL), lambda i, j, k=k: (0, k))
    vmem = (3 * bm * D_MODEL * 4 + bm * D_MODEL * 2 + 2 * 3 * D_MODEL * FFN_BH * 2
            + 4 * bm * FFN_BH * 4 + bm * D_MODEL * 4 + 4 * 1024 * 1024)
    return pl.pallas_call(
        kern,
        grid=(n // bm, FFN_HIDDEN // FFN_BH),
        in_specs=[
            pl.BlockSpec((bm, D_MODEL), lambda i, j: (i, 0)),
            pl.BlockSpec((1, D_MODEL), lambda i, j: (0, 0)),
            modspec(1), modspec(2), modspec(3),
            pl.BlockSpec((D_MODEL, FFN_BH), lambda i, j: (0, j)),
            pl.BlockSpec((D_MODEL, FFN_BH), lambda i, j: (0, j)),
            pl.BlockSpec((FFN_BH, D_MODEL), lambda i, j: (j, 0)),
            pl.BlockSpec((1, D_MODEL), lambda i, j: (0, 0)),
        ],
        out_specs=pl.BlockSpec((bm, D_MODEL), lambda i, j: (i, 0)),
        out_shape=jax.ShapeDtypeStruct((n, D_MODEL), F32),
        scratch_shapes=[pltpu.VMEM((bm, D_MODEL), BF16)],
        compiler_params=_params(2, vmem),
        name="ffn",
    )(x, g, mod, mod, mod, wg, wu, wd, fg)


def kernel(x, c, ctx, c_ctx, w_mod, b_mod, norm1_g, norm2_g, w_in, w_gate_f, b_gate_f,
           w_gate_b, b_gate_b, gla_norm_g, conv_w, w_out, w_ffn_gate, w_ffn_up, w_ffn_down,
           final_g):
    assert x.shape[0] == 1 and w_mod.shape[0] == 1, "batch 1, depth 1 only"
    seq, ctx_len = x.shape[1], ctx.shape[1]
    x2, ctx2 = x[0], ctx[0]

    c_t = jnp.concatenate(
        [c[0][:, None], c_ctx[:, None], jnp.zeros((D_MODEL, SUBLANES - 2), F32)], axis=1)
    mod = _mod_call(c_t, w_mod[0], b_mod, EARLY_MOD_COLS)

    pad_f = jnp.zeros((LANES, GLA_KW), F32).at[:GATE_RANK].set(w_gate_f[0]).astype(BF16)
    pad_b = jnp.zeros((LANES, GLA_KW), F32).at[GATE_RANK:2 * GATE_RANK].set(w_gate_b[0]).astype(BF16)
    g1 = norm1_g[0][None]
    w_in_t = jnp.transpose(w_in[0])

    h_c, l_c = _prenorm_call(ctx2, g1, mod, w_in_t, row=1, bm=ctx_len)
    p_c, = _in_proj_call(h_c, w_in_t, bm=ctx_len, n_tiles=2)
    zero_state = jnp.zeros((GLA_HEADS, GLA_DK, GLA_DV), F32)
    _, _, s_f, s_b = _gla_call(p_c, l_c, pad_f, b_gate_f, pad_b, b_gate_b, zero_state,
                               zero_state, tb=ctx_len)

    h_x, l_x = _prenorm_call(x2, g1, mod, w_in_t, row=0, bm=1024)
    p_x, wg_b, wu_b = _in_proj_call(h_x, w_in_t, (w_ffn_gate[0], w_ffn_up[0]), bm=1024,
                                    n_tiles=P_COLS // P_BN)
    o_f, o_b, _, _, wd_b, mod_late = _gla_call(
        p_x, l_x, pad_f, b_gate_f, pad_b, b_gate_b, s_f, s_b,
        (w_ffn_down[0], c_t, w_mod[0], b_mod, EARLY_MOD_COLS), tb=256)
    x_mid = _mix_call(o_f, o_b, p_x, x2, mod_late, gla_norm_g, conv_w[0], w_out[0], bm=256)
    out = _ffn_call(x_mid, norm2_g[0][None], mod_late, wg_b, wu_b, wd_b, final_g[None],
                    bm=1024)
    return out[None]
```

```python
import functools

import jax
import jax.numpy as jnp
from jax import lax
from jax.experimental import pallas as pl
from jax.experimental.pallas import tpu as pltpu

F32 = jnp.float32
BF16 = jnp.bfloat16

D_MODEL = 2048
GRID_W = 64
GLA_HEADS = 4
GLA_DK = 128
GLA_DV = 256
GLA_KW = GLA_HEADS * GLA_DK
GLA_VW = GLA_HEADS * GLA_DV
GATE_RANK = 16
GATE_NORM = 16.0
CHUNK = 64
CONV_W = D_MODEL - GLA_VW
FFN_HIDDEN = 5632
EPS = 1e-6

LANES = 128
SUBLANES = 8
VMEM_BYTES = 64 * 1024 * 1024

GATE_COL = 2 * GLA_KW + 2 * GLA_VW
GATE_COLS = 2 * GATE_RANK
P_COLS = GATE_COL + 3 * CONV_W
P_BN = 1024
P_SHIFT_FROM = GATE_COL // P_BN

_NT_DIMS = (((1,), (1,)), ((), ()))

MOD_BN = 2048
EARLY_MOD_COLS = 2 * D_MODEL
LATE_MOD_BN = 256
FFN_BH = 512
NORM_ROWS = 16


def _params(grid_rank, vmem_bytes):
    return pltpu.CompilerParams(
        dimension_semantics=("arbitrary",) * grid_rank,
        vmem_limit_bytes=min(int(vmem_bytes), VMEM_BYTES - 6 * 1024 * 1024),
    )


def _silu(v):
    return v * jax.nn.sigmoid(v)


def _for_row_chunks(rows, body):
    for r0 in range(0, rows, NORM_ROWS):
        body(slice(r0, r0 + NORM_ROWS))


def _norm_mod_rows(x_ref, g, mul, sh, dst_ref, rows, copy_ref=None):
    def body(rs):
        xs = x_ref[rs, :]
        ms = jnp.mean(xs * xs, axis=-1, keepdims=True)
        y = xs * lax.rsqrt(ms + EPS)
        dst_ref[rs, :] = ((y * g) * mul + sh).astype(BF16)
        if copy_ref is not None:
            copy_ref[rs, :] = xs
    _for_row_chunks(rows, body)


def _mod_prepare(c_ref, sb_ref):
    s = _silu(c_ref[...])
    for r in range(2):
        sb_ref[r] = jnp.broadcast_to(s[:, r:r + 1], (D_MODEL, LANES))


def _mod_matvec(w_ref, b_ref, sb_ref, o_ref, *, unrolled):
    nt = w_ref.shape[1] // LANES

    def body(kk, accs):
        r0 = kk * SUBLANES if unrolled else pl.multiple_of(kk * SUBLANES, SUBLANES)
        s0 = sb_ref[0, pl.ds(r0, SUBLANES), :]
        s1 = sb_ref[1, pl.ds(r0, SUBLANES), :]
        new = []
        for t in range(nt):
            w8 = w_ref[pl.ds(r0, SUBLANES), t * LANES:(t + 1) * LANES]
            new.append(accs[2 * t] + w8 * s0)
            new.append(accs[2 * t + 1] + w8 * s1)
        return tuple(new)

    accs = (jnp.zeros((SUBLANES, LANES), F32),) * (2 * nt)
    if unrolled:
        for kk in range(D_MODEL // SUBLANES):
            accs = body(kk, accs)
    else:
        accs = lax.fori_loop(0, D_MODEL // SUBLANES, body, accs, unroll=8)
    o_ref[...] = jnp.zeros(o_ref.shape, F32)
    for t in range(nt):
        b = b_ref[:, t * LANES:(t + 1) * LANES]
        for r in range(2):
            o_ref[r:r + 1, t * LANES:(t + 1) * LANES] = (
                jnp.sum(accs[2 * t + r], axis=0, keepdims=True) + b)


def _mod_kernel(c_ref, w_ref, b_ref, o_ref, sb_ref):
    @pl.when(pl.program_id(0) == 0)
    def _():
        _mod_prepare(c_ref, sb_ref)

    _mod_matvec(w_ref, b_ref, sb_ref, o_ref, unrolled=False)


def _mod_call(c_t, w_mod, b_mod, n):
    return pl.pallas_call(
        _mod_kernel,
        grid=(n // MOD_BN,),
        in_specs=[
            pl.BlockSpec((D_MODEL, SUBLANES), lambda j: (0, 0)),
            pl.BlockSpec((D_MODEL, MOD_BN), lambda j: (0, j)),
            pl.BlockSpec((1, MOD_BN), lambda j: (0, j)),
        ],
        out_specs=pl.BlockSpec((SUBLANES, MOD_BN), lambda j: (0, j)),
        out_shape=jax.ShapeDtypeStruct((SUBLANES, n), F32),
        scratch_shapes=[pltpu.VMEM((2, D_MODEL, LANES), F32)],
        compiler_params=_params(1, 2 * D_MODEL * MOD_BN * 4 + 8 * 1024 * 1024),
        name="mod",
    )(c_t, w_mod, b_mod)


def _prenorm_kernel(x_ref, g_ref, sh_ref, sc_ref, wl_ref, h_ref, l_ref, *, row, bm):
    _norm_mod_rows(x_ref, g_ref[...], 1.0 + sc_ref[row:row + 1, :], sh_ref[row:row + 1, :],
                   h_ref, bm)
    l_ref[...] = lax.dot_general(h_ref[...], wl_ref[...].astype(BF16), _NT_DIMS,
                                 preferred_element_type=F32).astype(BF16)


def _prenorm_call(x, g, mod, w_in_t, *, row, bm):
    m = x.shape[0]
    kern = functools.partial(_prenorm_kernel, row=row, bm=bm)
    vmem = (2 * bm * D_MODEL * (4 + 2) + 4 * D_MODEL * LANES * 4 + 8 * NORM_ROWS * D_MODEL * 4
            + 4 * 1024 * 1024)
    return pl.pallas_call(
        kern,
        grid=(m // bm,),
        in_specs=[
            pl.BlockSpec((bm, D_MODEL), lambda i: (i, 0)),
            pl.BlockSpec((1, D_MODEL), lambda i: (0, 0)),
            pl.BlockSpec((SUBLANES, D_MODEL), lambda i: (0, 0)),
            pl.BlockSpec((SUBLANES, D_MODEL), lambda i: (0, 1)),
            pl.BlockSpec((LANES, D_MODEL), lambda i: (GATE_COL // LANES, 0)),
        ],
        out_specs=[pl.BlockSpec((bm, D_MODEL), lambda i: (i, 0)),
                   pl.BlockSpec((bm, LANES), lambda i: (i, 0))],
        out_shape=[jax.ShapeDtypeStruct((m, D_MODEL), BF16),
                   jax.ShapeDtypeStruct((m, LANES), BF16)],
        compiler_params=_params(1, vmem),
        name="prenorm",
    )(x, g, mod, mod, w_in_t)


W_ROWS = 256


class _CastJob:
    def __init__(self, sources, rows, stage_ref, out_ref, sem_in, sem_out):
        self.sources, self.rows = sources, rows
        self.stage_ref, self.out_ref, self.sem_in, self.sem_out = stage_ref, out_ref, sem_in, sem_out
        self.n_blocks = sum(n for _, _, n in sources)

    def _for_block(self, c, fn):
        first = 0
        for src, dst, n in self.sources:
            @pl.when((c >= first) & (c < first + n))
            def _(src=src, dst=dst, first=first):
                fn(src, dst, pl.multiple_of((c - first) * self.rows, self.rows), lax.rem(c, 2))
            first += n

    def _fetch(self, src, dst, r, slot):
        return pltpu.make_async_copy(src.at[pl.ds(r, self.rows), :], self.stage_ref.at[slot],
                                     self.sem_in.at[slot])

    def _store(self, src, dst, r, slot):
        return pltpu.make_async_copy(self.out_ref.at[slot], dst.at[pl.ds(r, self.rows), :],
                                     self.sem_out.at[slot])

    def before(self, step):
        @pl.when(step == 0)
        def _():
            self._for_block(step, lambda *a: self._fetch(*a).start())
        self._for_block(step, lambda *a: self._fetch(*a).wait())
        self._for_block(step + 1, lambda *a: self._fetch(*a).start())
        self._for_block(step - 2, lambda *a: self._store(*a).wait())

    def cast(self, step):
        slot = lax.rem(step, 2)
        self.out_ref[slot] = self.stage_ref[slot].astype(BF16)

    def after(self, step):
        self._for_block(step, lambda *a: self._store(*a).start())


def _cast_scratch(rows, cols):
    return [pltpu.VMEM((2, rows, cols), F32), pltpu.VMEM((2, rows, cols), BF16),
            pltpu.SemaphoreType.DMA((2,)), pltpu.SemaphoreType.DMA((2,))]


IN_PROJ_CAST_ROWS = 128
GLA_CAST_ROWS = 256


def _in_proj_kernel(h_ref, wa_ref, wb_ref, *refs, side_jobs):
    j = pl.program_id(0)
    if side_jobs:
        (wg_ref, wu_ref, c_ref, wm_ref, bm_ref, o_ref, wg_out, wu_out, mod_ref,
         w_ref, stage_ref, cast_ref, sem_in, sem_out, sb_ref) = refs
        n = D_MODEL // IN_PROJ_CAST_ROWS
        job = _CastJob(((wg_ref, wg_out, n), (wu_ref, wu_out, n)), IN_PROJ_CAST_ROWS,
                       stage_ref, cast_ref, sem_in, sem_out)
        step = j * pl.num_programs(1) + pl.program_id(1)
        job.before(step)

        @pl.when(step == 0)
        def _():
            _mod_prepare(c_ref, sb_ref)
    else:
        o_ref, w_ref = refs

    @pl.when(pl.program_id(1) == 0)
    def _():
        @pl.when(j < P_SHIFT_FROM)
        def _():
            for r0 in range(0, P_BN, W_ROWS):
                w_ref[r0:r0 + W_ROWS, :] = wa_ref[r0:r0 + W_ROWS, :].astype(BF16)

        @pl.when(j >= P_SHIFT_FROM)
        def _():
            keep = P_BN - GATE_COLS
            for r0 in range(0, keep, W_ROWS):
                r1 = min(r0 + W_ROWS, keep)
                w_ref[r0:r1, :] = wa_ref[GATE_COLS + r0:GATE_COLS + r1, :].astype(BF16)
            w_ref[keep:, :] = wb_ref[...].astype(BF16)

    o_ref[...] = lax.dot_general(h_ref[...], w_ref[...], _NT_DIMS,
                                 preferred_element_type=F32).astype(BF16)
    if side_jobs:
        job.cast(step)
        _mod_matvec(wm_ref, bm_ref, sb_ref, mod_ref, unrolled=True)
        job.after(step)


def _in_proj_call(h, w_in_t, side=None, *, bm, n_tiles):
    m = h.shape[0]
    ni = m // bm
    vmem = (2 * bm * D_MODEL * 2 + 2 * (P_BN + GATE_COLS) * D_MODEL * 4 + P_BN * D_MODEL * 2
            + 2 * bm * P_BN * 2 + bm * P_BN * 4 + 4 * W_ROWS * D_MODEL * 4 + 4 * 1024 * 1024)
    in_specs = [
        pl.BlockSpec((bm, D_MODEL), lambda j, i: (i, 0)),
        pl.BlockSpec((P_BN, D_MODEL), lambda j, i: (j, 0)),
        pl.BlockSpec((GATE_COLS, D_MODEL), lambda j, i: ((j + 1) * (P_BN // GATE_COLS), 0)),
    ]
    out_specs = [pl.BlockSpec((bm, P_BN), lambda j, i: (i, j))]
    out_shape = [jax.ShapeDtypeStruct((m, n_tiles * P_BN), BF16)]
    scratch = [pltpu.VMEM((P_BN, D_MODEL), BF16)]
    operands = [h, w_in_t, w_in_t]
    if side is not None:
        w_gate, w_up, c_t, w_mod, b_mod, first_col = side
        n_steps = n_tiles * ni
        assert n_steps >= 2 * D_MODEL // IN_PROJ_CAST_ROWS + 2
        any_spec = pl.BlockSpec(memory_space=pl.ANY)
        n_mod = w_mod.shape[1] - first_col
        n_blocks = n_mod // LATE_MOD_BN
        assert n_mod % LATE_MOD_BN == 0 and first_col % LATE_MOD_BN == 0 and n_blocks <= n_steps
        mod_block = lambda j, i: jnp.minimum(j * ni + i, n_blocks - 1)
        in_specs += [
            any_spec, any_spec,
            pl.BlockSpec((D_MODEL, SUBLANES), lambda j, i: (0, 0)),
            pl.BlockSpec((D_MODEL, LATE_MOD_BN),
                         lambda j, i: (0, first_col // LATE_MOD_BN + mod_block(j, i))),
            pl.BlockSpec((1, LATE_MOD_BN),
                         lambda j, i: (0, first_col // LATE_MOD_BN + mod_block(j, i))),
        ]
        out_specs += [any_spec, any_spec,
                      pl.BlockSpec((SUBLANES, LATE_MOD_BN), lambda j, i: (0, mod_block(j, i)))]
        out_shape += [jax.ShapeDtypeStruct(w_gate.shape, BF16),
                      jax.ShapeDtypeStruct(w_up.shape, BF16),
                      jax.ShapeDtypeStruct((SUBLANES, n_mod), F32)]
        scratch += _cast_scratch(IN_PROJ_CAST_ROWS, FFN_HIDDEN)
        scratch += [pltpu.VMEM((2, D_MODEL, LANES), F32)]
        operands += [w_gate, w_up, c_t, w_mod, b_mod]
        vmem += (IN_PROJ_CAST_ROWS * FFN_HIDDEN * (2 * 4 + 2 * 2)
                 + 2 * D_MODEL * LATE_MOD_BN * 4 + 2 * D_MODEL * LANES * 4)
    return pl.pallas_call(
        functools.partial(_in_proj_kernel, side_jobs=side is not None),
        grid=(n_tiles, ni),
        in_specs=in_specs,
        out_specs=out_specs,
        out_shape=out_shape,
        scratch_shapes=scratch,
        compiler_params=_params(2, vmem),
        name="in_proj",
    )(*operands)


def _gla_log_decay_split(l_ref, wg_ref, bg_ref):
    z = jnp.dot(l_ref[...], wg_ref[...], preferred_element_type=F32) + bg_ref[...]
    log_a = (jnp.minimum(z, 0.0) - jnp.log(1.0 + jnp.exp(-jnp.abs(z)))) * (1.0 / GATE_NORM)
    hi = log_a.astype(BF16)
    lo = (log_a - hi.astype(F32)).astype(BF16)
    return hi, lo


def _gla_chunk_cumsum(pieces, *, reverse, tb):
    r = lax.broadcasted_iota(jnp.int32, (tb, tb), 0)
    cc = lax.broadcasted_iota(jnp.int32, (tb, tb), 1)
    same_chunk = (r // CHUNK) == (cc // CHUNK)
    tri = jnp.where(same_chunk & ((cc >= r) if reverse else (cc <= r)), 1.0, 0.0).astype(BF16)
    hi, lo = pieces
    return (jnp.dot(tri, hi, preferred_element_type=F32)
            + jnp.dot(tri, lo, preferred_element_type=F32))


def _gla_kernel(qf, kf, vf, lf, lf_next, qb, kb, vb, lb, lb_next, wgf, bgf, wgb, bgb, s0f, s0b,
                *refs, tb, cast):
    i = pl.program_id(0)
    if cast:
        (wd_ref, of, ob, sf, sb, wd_out, s_ref, cum_ref,
         stage_ref, cast_ref, sem_in, sem_out) = refs
        job = _CastJob(((wd_ref, wd_out, FFN_HIDDEN // GLA_CAST_ROWS),), GLA_CAST_ROWS,
                       stage_ref, cast_ref, sem_in, sem_out)
        job.before(i)
    else:
        of, ob, sf, sb, s_ref, cum_ref = refs

    @pl.when(i == 0)
    def _():
        s_ref[0] = s0f[...]
        s_ref[1] = s0b[...]
        cum_ref[0] = _gla_chunk_cumsum(_gla_log_decay_split(lf, wgf, bgf), reverse=False, tb=tb)
        cum_ref[1] = _gla_chunk_cumsum(_gla_log_decay_split(lb, wgb, bgb), reverse=True, tb=tb)

    dirs = ((qf, kf, vf, of, False), (qb, kb, vb, ob, True))
    ii = lax.broadcasted_iota(jnp.int32, (CHUNK, GLA_DK), 0)
    jj = lax.broadcasted_iota(jnp.int32, (CHUNK, GLA_DK), 1)
    nc = tb // CHUNK

    def start(step):
        items = []
        for d, (q_ref, k_ref, v_ref, o_ref, reverse) in enumerate(dirs):
            r0 = (nc - 1 - step if reverse else step) * CHUNK
            cum_c = cum_ref[d, r0:r0 + CHUNK, :]
            last = cum_c[0:1, :] if reverse else cum_c[CHUNK - 1:CHUNK, :]
            q_c = q_ref[r0:r0 + CHUNK, :].astype(F32)
            k_c = k_ref[r0:r0 + CHUNK, :].astype(F32)
            q_e = ((q_c * (GLA_DK ** -0.5)) * jnp.exp(cum_c)).astype(BF16)
            k_e = (k_c * jnp.exp(-cum_c)).astype(BF16)
            k_d = (k_c * jnp.exp(last - cum_c)).astype(BF16)
            decay = jnp.exp(last)
            mask = ((jj >= ii) & (jj < CHUNK)) if reverse else (jj <= ii)
            for h in range(GLA_HEADS):
                ks = slice(h * GLA_DK, (h + 1) * GLA_DK)
                vs = slice(h * GLA_DV, (h + 1) * GLA_DV)
                items.append(dict(
                    d=d, h=h, r0=r0, vs=vs, o_ref=o_ref, mask=mask,
                    q_e=q_e[:, ks], k_e=k_e[:, ks], k_d=k_d[:, ks], decay=decay[:, ks],
                    v=v_ref[r0:r0 + CHUNK, vs]))
        for it in items:
            k_pad = jnp.concatenate([it["k_e"], jnp.zeros((GLA_DK - CHUNK, GLA_DK), BF16)], axis=0)
            it["scores"] = lax.dot_general(it["q_e"], k_pad, _NT_DIMS,
                                           preferred_element_type=F32)
        return items

    items = start(0)
    next_cum = None
    for step in range(nc):
        following = start(step + 1) if step + 1 < nc else None
        if step == 1:
            next_f = _gla_log_decay_split(lf_next, wgf, bgf)
            next_b = _gla_log_decay_split(lb_next, wgb, bgb)
        if step == nc - 1:
            next_cum = (_gla_chunk_cumsum(next_f, reverse=False, tb=tb),
                        _gla_chunk_cumsum(next_b, reverse=True, tb=tb))
        for it in items:
            scores = jnp.where(it["mask"], it["scores"], 0.0).astype(BF16)
            s_old = s_ref[it["d"], it["h"]]
            lhs = jnp.concatenate([scores, it["q_e"]], axis=1)
            rhs = jnp.concatenate(
                [it["v"], jnp.zeros((GLA_DK - CHUNK, GLA_DV), BF16), s_old.astype(BF16)], axis=0)
            it["o_ref"][it["r0"]:it["r0"] + CHUNK, it["vs"]] = jnp.dot(
                lhs, rhs, preferred_element_type=F32).astype(BF16)
            ds = lax.dot_general(it["k_d"], it["v"], (((0,), (0,)), ((), ())),
                                 preferred_element_type=F32)
            dcol = jnp.transpose(jnp.broadcast_to(it["decay"], (SUBLANES, GLA_DK)))[:, 0:1]
            s_ref[it["d"], it["h"]] = dcol * s_old + ds
        items = following
    cum_ref[0] = next_cum[0]
    cum_ref[1] = next_cum[1]
    if cast:
        job.cast(i)
        job.after(i)

    @pl.when(i == pl.num_programs(0) - 1)
    def _():
        sf[...] = s_ref[0]
        sb[...] = s_ref[1]


def _gla_call(p, l, wgf, bgf, wgb, bgb, s0f, s0b, cast_weight=None, *, tb):
    n = p.shape[0]
    nb = n // tb
    fwd = lambda i: i
    bwd = lambda i: nb - 1 - i
    fwd_next = lambda i: jnp.minimum(i + 1, nb - 1)
    bwd_next = lambda i: jnp.maximum(nb - 2 - i, 0)

    def specs(pos, pos_next):
        return [
            pl.BlockSpec((tb, GLA_KW), lambda i: (pos(i), 0)),
            pl.BlockSpec((tb, GLA_KW), lambda i: (pos(i), 1)),
            pl.BlockSpec((tb, GLA_VW), lambda i: (pos(i), 1)),
            pl.BlockSpec((tb, LANES), lambda i: (pos(i), 0)),
            pl.BlockSpec((tb, LANES), lambda i: (pos_next(i), 0)),
        ]

    const2 = lambda i: (0, 0)
    const3 = lambda i: (0, 0, 0)
    state_shape = (GLA_HEADS, GLA_DK, GLA_DV)
    state_spec = pl.BlockSpec(state_shape, const3)
    cast = cast_weight is not None
    kern = functools.partial(_gla_kernel, tb=tb, cast=cast)
    vmem = (2 * 2 * tb * (2 * GLA_KW + GLA_VW + LANES) * 2 + 2 * 2 * tb * GLA_VW * 2
            + 10 * GLA_HEADS * GLA_DK * GLA_DV * 4 + 24 * tb * GLA_KW * 4 + 8 * 1024 * 1024)
    in_specs = specs(fwd, fwd_next) + specs(bwd, bwd_next) + [
        pl.BlockSpec((LANES, GLA_KW), const2), pl.BlockSpec((1, GLA_KW), const2),
        pl.BlockSpec((LANES, GLA_KW), const2), pl.BlockSpec((1, GLA_KW), const2),
        state_spec, state_spec,
    ]
    out_specs = [
        pl.BlockSpec((tb, GLA_VW), lambda i: (i, 0)),
        pl.BlockSpec((tb, GLA_VW), lambda i: (nb - 1 - i, 0)),
        state_spec, state_spec,
    ]
    out_shape = [
        jax.ShapeDtypeStruct((n, GLA_VW), BF16), jax.ShapeDtypeStruct((n, GLA_VW), BF16),
        jax.ShapeDtypeStruct(state_shape, F32), jax.ShapeDtypeStruct(state_shape, F32),
    ]
    scratch = [pltpu.VMEM((2,) + state_shape, F32), pltpu.VMEM((2, tb, GLA_KW), F32)]
    operands = [p, p, p, l, l, p, p, p, l, l, wgf, bgf, wgb, bgb, s0f, s0b]
    if cast:
        assert cast_weight.shape == (FFN_HIDDEN, D_MODEL)
        assert nb >= FFN_HIDDEN // GLA_CAST_ROWS + 2
        any_spec = pl.BlockSpec(memory_space=pl.ANY)
        in_specs.append(any_spec)
        out_specs.append(any_spec)
        out_shape.append(jax.ShapeDtypeStruct(cast_weight.shape, BF16))
        scratch += _cast_scratch(GLA_CAST_ROWS, D_MODEL)
        operands.append(cast_weight)
        vmem += GLA_CAST_ROWS * D_MODEL * (2 * 4 + 2 * 2)
    return pl.pallas_call(
        kern,
        grid=(nb,),
        in_specs=in_specs,
        out_specs=out_specs,
        out_shape=out_shape,
        scratch_shapes=scratch,
        compiler_params=_params(1, vmem),
        name="gla",
    )(*operands)


def _mix_kernel(of_ref, ob_ref, g_ref, cb_ref, cc_ref, cx_ref, x_ref, gt_ref, gg_ref, cw_ref,
                wo_ref, o_ref, m_ref, w_ref, *, bm):
    @pl.when(pl.program_id(0) == 0)
    def _():
        for r0 in range(0, D_MODEL, W_ROWS):
            w_ref[r0:r0 + W_ROWS, :] = wo_ref[r0:r0 + W_ROWS, :].astype(BF16)

    u = cc_ref[...].astype(F32) * cx_ref[...].astype(F32)
    col = lax.broadcasted_iota(jnp.int32, u.shape, 0) % GRID_W
    u_prev = jnp.where(col == 0, 0.0, pltpu.roll(u, 1, 0))
    u_next = jnp.where(col == GRID_W - 1, 0.0, pltpu.roll(u, bm - 1, 0))
    conv = cw_ref[0:1, :] * u_prev + cw_ref[1:2, :] * u + cw_ref[2:3, :] * u_next
    m_ref[:, GLA_VW:] = (cb_ref[...].astype(F32) * conv).astype(BF16)
    mixed = jnp.dot(m_ref[:, GLA_VW:], w_ref[GLA_VW:, :], preferred_element_type=F32)

    gg = gg_ref[...]
    for h in range(GLA_HEADS):
        vs = slice(h * GLA_DV, (h + 1) * GLA_DV)
        o = of_ref[:, vs].astype(F32) + ob_ref[:, vs].astype(F32)
        o32 = o * lax.rsqrt(jnp.mean(o * o, axis=-1, keepdims=True) + EPS)
        m_ref[:, vs] = ((o32 * gg) * _silu(g_ref[:, vs].astype(F32))).astype(BF16)
        mixed = mixed + jnp.dot(m_ref[:, vs], w_ref[vs, :], preferred_element_type=F32)
    o_ref[...] = x_ref[...] + gt_ref[0:1, :] * mixed


def _mix_call(o_f, o_b, p, x, mod, gla_g, conv_w, w_out, *, bm):
    n = x.shape[0]
    kern = functools.partial(_mix_kernel, bm=bm)
    pcol = lambda k: pl.BlockSpec((bm, GLA_VW), lambda i, k=k: (i, k))
    const2 = lambda i: (0, 0)
    vmem = (2 * 6 * bm * GLA_VW * 2 + 2 * 2 * bm * D_MODEL * 4
            + D_MODEL * D_MODEL * (4 + 2) + bm * D_MODEL * 2 + 8 * bm * GLA_VW * 4
            + 4 * 1024 * 1024)
    return pl.pallas_call(
        kern,
        grid=(n // bm,),
        in_specs=[
            pl.BlockSpec((bm, GLA_VW), lambda i: (i, 0)),
            pl.BlockSpec((bm, GLA_VW), lambda i: (i, 0)),
            pcol(2), pcol(3), pcol(4), pcol(5),
            pl.BlockSpec((bm, D_MODEL), lambda i: (i, 0)),
            pl.BlockSpec((SUBLANES, D_MODEL), lambda i: (0, 0)),
            pl.BlockSpec((1, GLA_DV), const2),
            pl.BlockSpec((3, CONV_W), const2),
            pl.BlockSpec((D_MODEL, D_MODEL), const2, pipeline_mode=pl.Buffered(1)),
        ],
        out_specs=pl.BlockSpec((bm, D_MODEL), lambda i: (i, 0)),
        out_shape=jax.ShapeDtypeStruct((n, D_MODEL), F32),
        scratch_shapes=[pltpu.VMEM((bm, D_MODEL), BF16), pltpu.VMEM((D_MODEL, D_MODEL), BF16)],
        compiler_params=_params(1, vmem),
        name="mix",
    )(o_f, o_b, p, p, p, p, x, mod, gla_g, conv_w, w_out)


def _ffn_kernel(x_ref, g_ref, sh_ref, sc_ref, gt_ref, wg_ref, wu_ref, wd_ref, fg_ref,
                o_ref, h_ref, *, bm):
    j = pl.program_id(1)

    @pl.when(j == 0)
    def _():
        _norm_mod_rows(x_ref, g_ref[...], 1.0 + sc_ref[0:1, :], sh_ref[0:1, :], h_ref, bm,
                       copy_ref=o_ref)

    h = h_ref[...]
    gate = jnp.dot(h, wg_ref[...], preferred_element_type=F32)
    up = jnp.dot(h, wu_ref[...], preferred_element_type=F32)
    part = jnp.dot((_silu(gate) * up).astype(BF16), wd_ref[...], preferred_element_type=F32)
    o_ref[...] += gt_ref[0:1, :] * part

    @pl.when(j == pl.num_programs(1) - 1)
    def _():
        fg = fg_ref[...]

        def final_norm(rs):
            xn = o_ref[rs, :]
            ms = jnp.mean(xn * xn, axis=-1, keepdims=True)
            o_ref[rs, :] = (xn * lax.rsqrt(ms + EPS)) * fg
        _for_row_chunks(bm, final_norm)


def _ffn_call(x, g, mod, wg, wu, wd, fg, *, bm):
    n = x.shape[0]
    kern = functools.partial(_ffn_kernel, bm=bm)
    modspec = lambda k: pl.BlockSpec((SUBLANES, D_MODEL), lambda i, j, k=k: (0, k))
    vmem = (3 * bm * D_MODEL * 4 + bm * D_MODEL * 2 + 2 * 3 * D_MODEL * FFN_BH * 2
            + 4 * bm * FFN_BH * 4 + bm * D_MODEL * 4 + 4 * 1024 * 1024)
    return pl.pallas_call(
        kern,
        grid=(n // bm, FFN_HIDDEN // FFN_BH),
        in_specs=[
            pl.BlockSpec((bm, D_MODEL), lambda i, j: (i, 0)),
            pl.BlockSpec((1, D_MODEL), lambda i, j: (0, 0)),
            modspec(1), modspec(2), modspec(3),
            pl.BlockSpec((D_MODEL, FFN_BH), lambda i, j: (0, j)),
            pl.BlockSpec((D_MODEL, FFN_BH), lambda i, j: (0, j)),
            pl.BlockSpec((FFN_BH, D_MODEL), lambda i, j: (j, 0)),
            pl.BlockSpec((1, D_MODEL), lambda i, j: (0, 0)),
        ],
        out_specs=pl.BlockSpec((bm, D_MODEL), lambda i, j: (i, 0)),
        out_shape=jax.ShapeDtypeStruct((n, D_MODEL), F32),
        scratch_shapes=[pltpu.VMEM((bm, D_MODEL), BF16)],
        compiler_params=_params(2, vmem),
        name="ffn",
    )(x, g, mod, mod, mod, wg, wu, wd, fg)


def kernel(x, c, ctx, c_ctx, w_mod, b_mod, norm1_g, norm2_g, w_in, w_gate_f, b_gate_f,
           w_gate_b, b_gate_b, gla_norm_g, conv_w, w_out, w_ffn_gate, w_ffn_up, w_ffn_down,
           final_g):
    assert x.shape[0] == 1 and w_mod.shape[0] == 1, "batch 1, depth 1 only"
    seq, ctx_len = x.shape[1], ctx.shape[1]
    x2, ctx2 = x[0], ctx[0]

    c_t = jnp.concatenate(
        [c[0][:, None], c_ctx[:, None], jnp.zeros((D_MODEL, SUBLANES - 2), F32)], axis=1)
    mod = _mod_call(c_t, w_mod[0], b_mod, EARLY_MOD_COLS)

    pad_f = jnp.zeros((LANES, GLA_KW), F32).at[:GATE_RANK].set(w_gate_f[0]).astype(BF16)
    pad_b = jnp.zeros((LANES, GLA_KW), F32).at[GATE_RANK:2 * GATE_RANK].set(w_gate_b[0]).astype(BF16)
    g1 = norm1_g[0][None]
    w_in_t = jnp.transpose(w_in[0])

    h_c, l_c = _prenorm_call(ctx2, g1, mod, w_in_t, row=1, bm=ctx_len)
    p_c, = _in_proj_call(h_c, w_in_t, bm=ctx_len, n_tiles=2)
    zero_state = jnp.zeros((GLA_HEADS, GLA_DK, GLA_DV), F32)
    _, _, s_f, s_b = _gla_call(p_c, l_c, pad_f, b_gate_f, pad_b, b_gate_b, zero_state,
                               zero_state, tb=ctx_len)

    h_x, l_x = _prenorm_call(x2, g1, mod, w_in_t, row=0, bm=2048)
    p_x, wg_b, wu_b, mod_late = _in_proj_call(
        h_x, w_in_t, (w_ffn_gate[0], w_ffn_up[0], c_t, w_mod[0], b_mod, EARLY_MOD_COLS),
        bm=1024, n_tiles=P_COLS // P_BN)
    o_f, o_b, _, _, wd_b = _gla_call(p_x, l_x, pad_f, b_gate_f, pad_b, b_gate_b, s_f, s_b,
                                     w_ffn_down[0], tb=256)
    x_mid = _mix_call(o_f, o_b, p_x, x2, mod_late, gla_norm_g, conv_w[0], w_out[0], bm=256)
    out = _ffn_call(x_mid, norm2_g[0][None], mod_late, wg_b, wu_b, wd_b, final_g[None],
                    bm=1024)
    return out[None]
```

```python
import functools

import jax
import jax.numpy as jnp
from jax import lax
from jax.experimental import pallas as pl
from jax.experimental.pallas import tpu as pltpu

F32 = jnp.float32
BF16 = jnp.bfloat16

D_MODEL = 2048
GRID_W = 64
GLA_HEADS = 4
GLA_DK = 128
GLA_DV = 256
GLA_KW = GLA_HEADS * GLA_DK
GLA_VW = GLA_HEADS * GLA_DV
GATE_RANK = 16
GATE_NORM = 16.0
CHUNK = 64
CONV_W = D_MODEL - GLA_VW
FFN_HIDDEN = 5632
EPS = 1e-6

LANES = 128
SUBLANES = 8
MIB = 1024 * 1024
VMEM_BYTES = 64 * MIB
VMEM_RESERVED = 6 * MIB
SPILL_BYTES = 4 * MIB

GATE_COL = 2 * GLA_KW + 2 * GLA_VW
GATE_COLS = 2 * GATE_RANK
P_COLS = GATE_COL + 3 * CONV_W
P_BN = 1024
P_SHIFT_FROM = GATE_COL // P_BN

_NT_DIMS = (((1,), (1,)), ((), ()))

MOD_BN = 2048
EARLY_MOD_COLS = 2 * D_MODEL
LATE_MOD_BN = 256
FFN_BH = 512
NORM_ROWS = 16


def _params(grid_rank, vmem_bytes):
    return pltpu.CompilerParams(
        dimension_semantics=("arbitrary",) * grid_rank,
        vmem_limit_bytes=min(int(vmem_bytes), VMEM_BYTES - VMEM_RESERVED),
    )


def _silu(v):
    return v * jax.nn.sigmoid(v)


def _for_row_chunks(rows, body):
    for r0 in range(0, rows, NORM_ROWS):
        body(slice(r0, r0 + NORM_ROWS))


def _norm_mod_rows(x_ref, g, mul, sh, dst_ref, rows, copy_ref=None):
    def body(rs):
        xs = x_ref[rs, :]
        ms = jnp.mean(xs * xs, axis=-1, keepdims=True)
        y = xs * lax.rsqrt(ms + EPS)
        dst_ref[rs, :] = ((y * g) * mul + sh).astype(BF16)
        if copy_ref is not None:
            copy_ref[rs, :] = xs
    _for_row_chunks(rows, body)


def _mod_prepare(c_ref, sb_ref):
    s = _silu(c_ref[...])
    for r in range(2):
        sb_ref[r] = jnp.broadcast_to(s[:, r:r + 1], (D_MODEL, LANES))


def _mod_matvec(w_ref, b_ref, sb_ref, o_ref, *, unrolled):
    nt = w_ref.shape[1] // LANES

    def body(kk, accs):
        r0 = kk * SUBLANES if unrolled else pl.multiple_of(kk * SUBLANES, SUBLANES)
        s0 = sb_ref[0, pl.ds(r0, SUBLANES), :]
        s1 = sb_ref[1, pl.ds(r0, SUBLANES), :]
        new = []
        for t in range(nt):
            w8 = w_ref[pl.ds(r0, SUBLANES), t * LANES:(t + 1) * LANES]
            new.append(accs[2 * t] + w8 * s0)
            new.append(accs[2 * t + 1] + w8 * s1)
        return tuple(new)

    accs = (jnp.zeros((SUBLANES, LANES), F32),) * (2 * nt)
    if unrolled:
        for kk in range(D_MODEL // SUBLANES):
            accs = body(kk, accs)
    else:
        accs = lax.fori_loop(0, D_MODEL // SUBLANES, body, accs, unroll=8)
    o_ref[...] = jnp.zeros(o_ref.shape, F32)
    for t in range(nt):
        b = b_ref[:, t * LANES:(t + 1) * LANES]
        for r in range(2):
            o_ref[r:r + 1, t * LANES:(t + 1) * LANES] = (
                jnp.sum(accs[2 * t + r], axis=0, keepdims=True) + b)


def _mod_kernel(c_ref, w_ref, b_ref, o_ref, sb_ref):
    @pl.when(pl.program_id(0) == 0)
    def _():
        _mod_prepare(c_ref, sb_ref)

    _mod_matvec(w_ref, b_ref, sb_ref, o_ref, unrolled=False)


def _mod_call(c_t, w_mod, b_mod, n):
    return pl.pallas_call(
        _mod_kernel,
        grid=(n // MOD_BN,),
        in_specs=[
            pl.BlockSpec((D_MODEL, SUBLANES), lambda j: (0, 0)),
            pl.BlockSpec((D_MODEL, MOD_BN), lambda j: (0, j)),
            pl.BlockSpec((1, MOD_BN), lambda j: (0, j)),
        ],
        out_specs=pl.BlockSpec((SUBLANES, MOD_BN), lambda j: (0, j)),
        out_shape=jax.ShapeDtypeStruct((SUBLANES, n), F32),
        scratch_shapes=[pltpu.VMEM((2, D_MODEL, LANES), F32)],
        compiler_params=_params(1, 2 * D_MODEL * MOD_BN * 4 + 2 * SPILL_BYTES),
        name="mod",
    )(c_t, w_mod, b_mod)


def _prenorm_kernel(x_ref, g_ref, sh_ref, sc_ref, wl_ref, h_ref, l_ref, *, row, bm):
    _norm_mod_rows(x_ref, g_ref[...], 1.0 + sc_ref[row:row + 1, :], sh_ref[row:row + 1, :],
                   h_ref, bm)
    l_ref[...] = lax.dot_general(h_ref[...], wl_ref[...].astype(BF16), _NT_DIMS,
                                 preferred_element_type=F32).astype(BF16)


def _prenorm_call(x, g, mod, w_in_t, *, row, bm):
    m = x.shape[0]
    kern = functools.partial(_prenorm_kernel, row=row, bm=bm)
    vmem = (2 * bm * D_MODEL * (4 + 2) + 4 * D_MODEL * LANES * 4 + 8 * NORM_ROWS * D_MODEL * 4
            + SPILL_BYTES)
    return pl.pallas_call(
        kern,
        grid=(m // bm,),
        in_specs=[
            pl.BlockSpec((bm, D_MODEL), lambda i: (i, 0)),
            pl.BlockSpec((1, D_MODEL), lambda i: (0, 0)),
            pl.BlockSpec((SUBLANES, D_MODEL), lambda i: (0, 0)),
            pl.BlockSpec((SUBLANES, D_MODEL), lambda i: (0, 1)),
            pl.BlockSpec((LANES, D_MODEL), lambda i: (GATE_COL // LANES, 0)),
        ],
        out_specs=[pl.BlockSpec((bm, D_MODEL), lambda i: (i, 0)),
                   pl.BlockSpec((bm, LANES), lambda i: (i, 0))],
        out_shape=[jax.ShapeDtypeStruct((m, D_MODEL), BF16),
                   jax.ShapeDtypeStruct((m, LANES), BF16)],
        compiler_params=_params(1, vmem),
        name="prenorm",
    )(x, g, mod, mod, w_in_t)


W_ROWS = 256


class _CastJob:
    def __init__(self, sources, rows, stage_ref, out_ref, sem_in, sem_out):
        self.sources, self.rows = sources, rows
        self.stage_ref, self.out_ref, self.sem_in, self.sem_out = stage_ref, out_ref, sem_in, sem_out
        self.n_blocks = sum(n for _, _, n in sources)

    def _for_block(self, c, fn):
        first = 0
        for src, dst, n in self.sources:
            @pl.when((c >= first) & (c < first + n))
            def _(src=src, dst=dst, first=first):
                fn(src, dst, pl.multiple_of((c - first) * self.rows, self.rows), lax.rem(c, 2))
            first += n

    def _fetch(self, src, dst, r, slot):
        return pltpu.make_async_copy(src.at[pl.ds(r, self.rows), :], self.stage_ref.at[slot],
                                     self.sem_in.at[slot])

    def _store(self, src, dst, r, slot):
        return pltpu.make_async_copy(self.out_ref.at[slot], dst.at[pl.ds(r, self.rows), :],
                                     self.sem_out.at[slot])

    def before(self, step):
        @pl.when(step == 0)
        def _():
            self._for_block(step, lambda *a: self._fetch(*a).start())
        self._for_block(step, lambda *a: self._fetch(*a).wait())
        self._for_block(step + 1, lambda *a: self._fetch(*a).start())
        self._for_block(step - 2, lambda *a: self._store(*a).wait())

    def cast(self, step):
        slot = lax.rem(step, 2)
        self.out_ref[slot] = self.stage_ref[slot].astype(BF16)

    def after(self, step):
        self._for_block(step, lambda *a: self._store(*a).start())


def _cast_scratch(rows, cols):
    return [pltpu.VMEM((2, rows, cols), F32), pltpu.VMEM((2, rows, cols), BF16),
            pltpu.SemaphoreType.DMA((2,)), pltpu.SemaphoreType.DMA((2,))]


IN_PROJ_CAST_ROWS = 128
GLA_CAST_ROWS = 256


def _in_proj_kernel(h_ref, wa_ref, wb_ref, *refs, side_jobs):
    j = pl.program_id(0)
    if side_jobs:
        (wg_ref, wu_ref, c_ref, wm_ref, bm_ref, o_ref, wg_out, wu_out, mod_ref,
         w_ref, stage_ref, cast_ref, sem_in, sem_out, sb_ref) = refs
        n = D_MODEL // IN_PROJ_CAST_ROWS
        job = _CastJob(((wg_ref, wg_out, n), (wu_ref, wu_out, n)), IN_PROJ_CAST_ROWS,
                       stage_ref, cast_ref, sem_in, sem_out)
        step = j * pl.num_programs(1) + pl.program_id(1)
        job.before(step)

        @pl.when(step == 0)
        def _():
            _mod_prepare(c_ref, sb_ref)
    else:
        o_ref, w_ref = refs

    @pl.when(pl.program_id(1) == 0)
    def _():
        @pl.when(j < P_SHIFT_FROM)
        def _():
            for r0 in range(0, P_BN, W_ROWS):
                w_ref[r0:r0 + W_ROWS, :] = wa_ref[r0:r0 + W_ROWS, :].astype(BF16)

        @pl.when(j >= P_SHIFT_FROM)
        def _():
            keep = P_BN - GATE_COLS
            for r0 in range(0, keep, W_ROWS):
                r1 = min(r0 + W_ROWS, keep)
                w_ref[r0:r1, :] = wa_ref[GATE_COLS + r0:GATE_COLS + r1, :].astype(BF16)
            w_ref[keep:, :] = wb_ref[...].astype(BF16)

    o_ref[...] = lax.dot_general(h_ref[...], w_ref[...], _NT_DIMS,
                                 preferred_element_type=F32).astype(BF16)
    if side_jobs:
        job.cast(step)
        _mod_matvec(wm_ref, bm_ref, sb_ref, mod_ref, unrolled=True)
        job.after(step)


def _in_proj_call(h, w_in_t, side=None, *, bm, n_tiles):
    m = h.shape[0]
    ni = m // bm
    vmem = (2 * bm * D_MODEL * 2 + 2 * (P_BN + GATE_COLS) * D_MODEL * 4 + P_BN * D_MODEL * 2
            + 2 * bm * P_BN * 2 + bm * P_BN * 4 + 4 * W_ROWS * D_MODEL * 4 + SPILL_BYTES)
    in_specs = [
        pl.BlockSpec((bm, D_MODEL), lambda j, i: (i, 0)),
        pl.BlockSpec((P_BN, D_MODEL), lambda j, i: (j, 0)),
        pl.BlockSpec((GATE_COLS, D_MODEL), lambda j, i: ((j + 1) * (P_BN // GATE_COLS), 0)),
    ]
    out_specs = [pl.BlockSpec((bm, P_BN), lambda j, i: (i, j))]
    out_shape = [jax.ShapeDtypeStruct((m, n_tiles * P_BN), BF16)]
    scratch = [pltpu.VMEM((P_BN, D_MODEL), BF16)]
    operands = [h, w_in_t, w_in_t]
    if side is not None:
        w_gate, w_up, c_t, w_mod, b_mod, first_col = side
        n_steps = n_tiles * ni
        assert n_steps >= 2 * D_MODEL // IN_PROJ_CAST_ROWS + 2
        any_spec = pl.BlockSpec(memory_space=pl.ANY)
        n_mod = w_mod.shape[1] - first_col
        n_blocks = n_mod // LATE_MOD_BN
        assert n_mod % LATE_MOD_BN == 0 and first_col % LATE_MOD_BN == 0 and n_blocks <= n_steps
        mod_block = lambda j, i: jnp.minimum(j * ni + i, n_blocks - 1)
        in_specs += [
            any_spec, any_spec,
            pl.BlockSpec((D_MODEL, SUBLANES), lambda j, i: (0, 0)),
            pl.BlockSpec((D_MODEL, LATE_MOD_BN),
                         lambda j, i: (0, first_col // LATE_MOD_BN + mod_block(j, i))),
            pl.BlockSpec((1, LATE_MOD_BN),
                         lambda j, i: (0, first_col // LATE_MOD_BN + mod_block(j, i))),
        ]
        out_specs += [any_spec, any_spec,
                      pl.BlockSpec((SUBLANES, LATE_MOD_BN), lambda j, i: (0, mod_block(j, i)))]
        out_shape += [jax.ShapeDtypeStruct(w_gate.shape, BF16),
                      jax.ShapeDtypeStruct(w_up.shape, BF16),
                      jax.ShapeDtypeStruct((SUBLANES, n_mod), F32)]
        scratch += _cast_scratch(IN_PROJ_CAST_ROWS, FFN_HIDDEN)
        scratch += [pltpu.VMEM((2, D_MODEL, LANES), F32)]
        operands += [w_gate, w_up, c_t, w_mod, b_mod]
        vmem += (IN_PROJ_CAST_ROWS * FFN_HIDDEN * (2 * 4 + 2 * 2)
                 + 2 * D_MODEL * LATE_MOD_BN * 4 + 2 * D_MODEL * LANES * 4)
    return pl.pallas_call(
        functools.partial(_in_proj_kernel, side_jobs=side is not None),
        grid=(n_tiles, ni),
        in_specs=in_specs,
        out_specs=out_specs,
        out_shape=out_shape,
        scratch_shapes=scratch,
        compiler_params=_params(2, vmem),
        name="in_proj",
    )(*operands)


def _gla_log_decay_split(l_ref, wg_ref, bg_ref):
    z = jnp.dot(l_ref[...], wg_ref[...], preferred_element_type=F32) + bg_ref[...]
    log_a = (jnp.minimum(z, 0.0) - jnp.log(1.0 + jnp.exp(-jnp.abs(z)))) * (1.0 / GATE_NORM)
    hi = log_a.astype(BF16)
    lo = (log_a - hi.astype(F32)).astype(BF16)
    return hi, lo


def _gla_chunk_cumsum(pieces, *, reverse, tb):
    r = lax.broadcasted_iota(jnp.int32, (tb, tb), 0)
    cc = lax.broadcasted_iota(jnp.int32, (tb, tb), 1)
    same_chunk = (r // CHUNK) == (cc // CHUNK)
    tri = jnp.where(same_chunk & ((cc >= r) if reverse else (cc <= r)), 1.0, 0.0).astype(BF16)
    hi, lo = pieces
    return (jnp.dot(tri, hi, preferred_element_type=F32)
            + jnp.dot(tri, lo, preferred_element_type=F32))


def _gla_kernel(qf, kf, vf, lf, lf_next, qb, kb, vb, lb, lb_next, wgf, bgf, wgb, bgb, s0f, s0b,
                *refs, tb, cast):
    i = pl.program_id(0)
    if cast:
        (wd_ref, of, ob, sf, sb, wd_out, s_ref, cum_ref,
         stage_ref, cast_ref, sem_in, sem_out) = refs
        job = _CastJob(((wd_ref, wd_out, FFN_HIDDEN // GLA_CAST_ROWS),), GLA_CAST_ROWS,
                       stage_ref, cast_ref, sem_in, sem_out)
        job.before(i)
    else:
        of, ob, sf, sb, s_ref, cum_ref = refs

    @pl.when(i == 0)
    def _():
        s_ref[0] = s0f[...]
        s_ref[1] = s0b[...]
        cum_ref[0] = _gla_chunk_cumsum(_gla_log_decay_split(lf, wgf, bgf), reverse=False, tb=tb)
        cum_ref[1] = _gla_chunk_cumsum(_gla_log_decay_split(lb, wgb, bgb), reverse=True, tb=tb)

    dirs = ((qf, kf, vf, of, False), (qb, kb, vb, ob, True))
    ii = lax.broadcasted_iota(jnp.int32, (CHUNK, GLA_DK), 0)
    jj = lax.broadcasted_iota(jnp.int32, (CHUNK, GLA_DK), 1)
    nc = tb // CHUNK

    def start(step):
        items = []
        for d, (q_ref, k_ref, v_ref, o_ref, reverse) in enumerate(dirs):
            r0 = (nc - 1 - step if reverse else step) * CHUNK
            cum_c = cum_ref[d, r0:r0 + CHUNK, :]
            last = cum_c[0:1, :] if reverse else cum_c[CHUNK - 1:CHUNK, :]
            q_c = q_ref[r0:r0 + CHUNK, :].astype(F32)
            k_c = k_ref[r0:r0 + CHUNK, :].astype(F32)
            q_e = ((q_c * (GLA_DK ** -0.5)) * jnp.exp(cum_c)).astype(BF16)
            k_e = (k_c * jnp.exp(-cum_c)).astype(BF16)
            k_d = (k_c * jnp.exp(last - cum_c)).astype(BF16)
            decay = jnp.exp(last)
            mask = ((jj >= ii) & (jj < CHUNK)) if reverse else (jj <= ii)
            for h in range(GLA_HEADS):
                ks = slice(h * GLA_DK, (h + 1) * GLA_DK)
                vs = slice(h * GLA_DV, (h + 1) * GLA_DV)
                items.append(dict(
                    d=d, h=h, r0=r0, vs=vs, o_ref=o_ref, mask=mask,
                    q_e=q_e[:, ks], k_e=k_e[:, ks], k_d=k_d[:, ks], decay=decay[:, ks],
                    v=v_ref[r0:r0 + CHUNK, vs]))
        for it in items:
            k_pad = jnp.concatenate([it["k_e"], jnp.zeros((GLA_DK - CHUNK, GLA_DK), BF16)], axis=0)
            it["scores"] = lax.dot_general(it["q_e"], k_pad, _NT_DIMS,
                                           preferred_element_type=F32)
        return items

    items = start(0)
    next_cum = None
    for step in range(nc):
        following = start(step + 1) if step + 1 < nc else None
        if step == 1:
            next_f = _gla_log_decay_split(lf_next, wgf, bgf)
            next_b = _gla_log_decay_split(lb_next, wgb, bgb)
        if step == nc - 1:
            next_cum = (_gla_chunk_cumsum(next_f, reverse=False, tb=tb),
                        _gla_chunk_cumsum(next_b, reverse=True, tb=tb))
        for it in items:
            scores = jnp.where(it["mask"], it["scores"], 0.0).astype(BF16)
            s_old = s_ref[it["d"], it["h"]]
            lhs = jnp.concatenate([scores, it["q_e"]], axis=1)
            rhs = jnp.concatenate(
                [it["v"], jnp.zeros((GLA_DK - CHUNK, GLA_DV), BF16), s_old.astype(BF16)], axis=0)
            it["o_ref"][it["r0"]:it["r0"] + CHUNK, it["vs"]] = jnp.dot(
                lhs, rhs, preferred_element_type=F32).astype(BF16)
            ds = lax.dot_general(it["k_d"], it["v"], (((0,), (0,)), ((), ())),
                                 preferred_element_type=F32)
            dcol = jnp.transpose(jnp.broadcast_to(it["decay"], (SUBLANES, GLA_DK)))[:, 0:1]
            s_ref[it["d"], it["h"]] = dcol * s_old + ds
        items = following
    cum_ref[0] = next_cum[0]
    cum_ref[1] = next_cum[1]
    if cast:
        job.cast(i)
        job.after(i)

    @pl.when(i == pl.num_programs(0) - 1)
    def _():
        sf[...] = s_ref[0]
        sb[...] = s_ref[1]


def _gla_call(p, l, wgf, bgf, wgb, bgb, s0f, s0b, cast_weight=None, *, tb):
    n = p.shape[0]
    nb = n // tb
    fwd = lambda i: i
    bwd = lambda i: nb - 1 - i
    fwd_next = lambda i: jnp.minimum(i + 1, nb - 1)
    bwd_next = lambda i: jnp.maximum(nb - 2 - i, 0)

    def specs(pos, pos_next):
        return [
            pl.BlockSpec((tb, GLA_KW), lambda i: (pos(i), 0)),
            pl.BlockSpec((tb, GLA_KW), lambda i: (pos(i), 1)),
            pl.BlockSpec((tb, GLA_VW), lambda i: (pos(i), 1)),
            pl.BlockSpec((tb, LANES), lambda i: (pos(i), 0)),
            pl.BlockSpec((tb, LANES), lambda i: (pos_next(i), 0)),
        ]

    const2 = lambda i: (0, 0)
    const3 = lambda i: (0, 0, 0)
    state_shape = (GLA_HEADS, GLA_DK, GLA_DV)
    state_spec = pl.BlockSpec(state_shape, const3)
    cast = cast_weight is not None
    kern = functools.partial(_gla_kernel, tb=tb, cast=cast)
    vmem = (2 * 2 * tb * (2 * GLA_KW + GLA_VW + LANES) * 2 + 2 * 2 * tb * GLA_VW * 2
            + 10 * GLA_HEADS * GLA_DK * GLA_DV * 4 + 24 * tb * GLA_KW * 4 + 2 * SPILL_BYTES)
    in_specs = specs(fwd, fwd_next) + specs(bwd, bwd_next) + [
        pl.BlockSpec((LANES, GLA_KW), const2), pl.BlockSpec((1, GLA_KW), const2),
        pl.BlockSpec((LANES, GLA_KW), const2), pl.BlockSpec((1, GLA_KW), const2),
        state_spec, state_spec,
    ]
    out_specs = [
        pl.BlockSpec((tb, GLA_VW), lambda i: (i, 0)),
        pl.BlockSpec((tb, GLA_VW), lambda i: (nb - 1 - i, 0)),
        state_spec, state_spec,
    ]
    out_shape = [
        jax.ShapeDtypeStruct((n, GLA_VW), BF16), jax.ShapeDtypeStruct((n, GLA_VW), BF16),
        jax.ShapeDtypeStruct(state_shape, F32), jax.ShapeDtypeStruct(state_shape, F32),
    ]
    scratch = [pltpu.VMEM((2,) + state_shape, F32), pltpu.VMEM((2, tb, GLA_KW), F32)]
    operands = [p, p, p, l, l, p, p, p, l, l, wgf, bgf, wgb, bgb, s0f, s0b]
    if cast:
        assert cast_weight.shape == (FFN_HIDDEN, D_MODEL)
        assert nb >= FFN_HIDDEN // GLA_CAST_ROWS + 2
        any_spec = pl.BlockSpec(memory_space=pl.ANY)
        in_specs.append(any_spec)
        out_specs.append(any_spec)
        out_shape.append(jax.ShapeDtypeStruct(cast_weight.shape, BF16))
        scratch += _cast_scratch(GLA_CAST_ROWS, D_MODEL)
        operands.append(cast_weight)
        vmem += GLA_CAST_ROWS * D_MODEL * (2 * 4 + 2 * 2)
    return pl.pallas_call(
        kern,
        grid=(nb,),
        in_specs=in_specs,
        out_specs=out_specs,
        out_shape=out_shape,
        scratch_shapes=scratch,
        compiler_params=_params(1, vmem),
        name="gla",
    )(*operands)


def _mix_kernel(of_ref, ob_ref, g_ref, cb_ref, cc_ref, cx_ref, x_ref, gt_ref, gg_ref, cw_ref,
                wo_ref, o_ref, m_ref, w_ref, *, bm):
    @pl.when(pl.program_id(0) == 0)
    def _():
        for r0 in range(0, D_MODEL, W_ROWS):
            w_ref[r0:r0 + W_ROWS, :] = wo_ref[r0:r0 + W_ROWS, :].astype(BF16)

    u = cc_ref[...].astype(F32) * cx_ref[...].astype(F32)
    col = lax.broadcasted_iota(jnp.int32, u.shape, 0) % GRID_W
    u_prev = jnp.where(col == 0, 0.0, pltpu.roll(u, 1, 0))
    u_next = jnp.where(col == GRID_W - 1, 0.0, pltpu.roll(u, bm - 1, 0))
    conv = cw_ref[0:1, :] * u_prev + cw_ref[1:2, :] * u + cw_ref[2:3, :] * u_next
    m_ref[:, GLA_VW:] = (cb_ref[...].astype(F32) * conv).astype(BF16)
    mixed = jnp.dot(m_ref[:, GLA_VW:], w_ref[GLA_VW:, :], preferred_element_type=F32)

    gg = gg_ref[...]
    for h in range(GLA_HEADS):
        vs = slice(h * GLA_DV, (h + 1) * GLA_DV)
        o = of_ref[:, vs].astype(F32) + ob_ref[:, vs].astype(F32)
        o32 = o * lax.rsqrt(jnp.mean(o * o, axis=-1, keepdims=True) + EPS)
        m_ref[:, vs] = ((o32 * gg) * _silu(g_ref[:, vs].astype(F32))).astype(BF16)
        mixed = mixed + jnp.dot(m_ref[:, vs], w_ref[vs, :], preferred_element_type=F32)
    o_ref[...] = x_ref[...] + gt_ref[0:1, :] * mixed


def _mix_call(o_f, o_b, p, x, mod, gla_g, conv_w, w_out, *, bm):
    n = x.shape[0]
    kern = functools.partial(_mix_kernel, bm=bm)
    pcol = lambda k: pl.BlockSpec((bm, GLA_VW), lambda i, k=k: (i, k))
    const2 = lambda i: (0, 0)
    vmem = (2 * 6 * bm * GLA_VW * 2 + 2 * 2 * bm * D_MODEL * 4
            + D_MODEL * D_MODEL * (4 + 2) + bm * D_MODEL * 2 + 8 * bm * GLA_VW * 4
            + SPILL_BYTES)
    return pl.pallas_call(
        kern,
        grid=(n // bm,),
        in_specs=[
            pl.BlockSpec((bm, GLA_VW), lambda i: (i, 0)),
            pl.BlockSpec((bm, GLA_VW), lambda i: (i, 0)),
            pcol(2), pcol(3), pcol(4), pcol(5),
            pl.BlockSpec((bm, D_MODEL), lambda i: (i, 0)),
            pl.BlockSpec((SUBLANES, D_MODEL), lambda i: (0, 0)),
            pl.BlockSpec((1, GLA_DV), const2),
            pl.BlockSpec((3, CONV_W), const2),
            pl.BlockSpec((D_MODEL, D_MODEL), const2, pipeline_mode=pl.Buffered(1)),
        ],
        out_specs=pl.BlockSpec((bm, D_MODEL), lambda i: (i, 0)),
        out_shape=jax.ShapeDtypeStruct((n, D_MODEL), F32),
        scratch_shapes=[pltpu.VMEM((bm, D_MODEL), BF16), pltpu.VMEM((D_MODEL, D_MODEL), BF16)],
        compiler_params=_params(1, vmem),
        name="mix",
    )(o_f, o_b, p, p, p, p, x, mod, gla_g, conv_w, w_out)


def _ffn_kernel(x_ref, g_ref, sh_ref, sc_ref, gt_ref, wg_ref, wu_ref, wd_ref, fg_ref,
                o_ref, h_ref, *, bm):
    j = pl.program_id(1)

    @pl.when(j == 0)
    def _():
        _norm_mod_rows(x_ref, g_ref[...], 1.0 + sc_ref[0:1, :], sh_ref[0:1, :], h_ref, bm,
                       copy_ref=o_ref)

    h = h_ref[...]
    gate = jnp.dot(h, wg_ref[...], preferred_element_type=F32)
    up = jnp.dot(h, wu_ref[...], preferred_element_type=F32)
    part = jnp.dot((_silu(gate) * up).astype(BF16), wd_ref[...], preferred_element_type=F32)
    o_ref[...] += gt_ref[0:1, :] * part

    @pl.when(j == pl.num_programs(1) - 1)
    def _():
        fg = fg_ref[...]

        def final_norm(rs):
            xn = o_ref[rs, :]
            ms = jnp.mean(xn * xn, axis=-1, keepdims=True)
            o_ref[rs, :] = (xn * lax.rsqrt(ms + EPS)) * fg
        _for_row_chunks(bm, final_norm)


def _ffn_call(x, g, mod, wg, wu, wd, fg, *, bm):
    n = x.shape[0]
    kern = functools.partial(_ffn_kernel, bm=bm)
    modspec = lambda k: pl.BlockSpec((SUBLANES, D_MODEL), lambda i, j, k=k: (0, k))
    vmem = (3 * bm * D_MODEL * 4 + bm * D_MODEL * 2 + 2 * 3 * D_MODEL * FFN_BH * 2
            + 4 * bm * FFN_BH * 4 + bm * D_MODEL * 4 + SPILL_BYTES)
    return pl.pallas_call(
        kern,
        grid=(n // bm, FFN_HIDDEN // FFN_BH),
        in_specs=[
            pl.BlockSpec((bm, D_MODEL), lambda i, j: (i, 0)),
            pl.BlockSpec((1, D_MODEL), lambda i, j: (0, 0)),
            modspec(1), modspec(2), modspec(3),
            pl.BlockSpec((D_MODEL, FFN_BH), lambda i, j: (0, j)),
            pl.BlockSpec((D_MODEL, FFN_BH), lambda i, j: (0, j)),
            pl.BlockSpec((FFN_BH, D_MODEL), lambda i, j: (j, 0)),
            pl.BlockSpec((1, D_MODEL), lambda i, j: (0, 0)),
        ],
        out_specs=pl.BlockSpec((bm, D_MODEL), lambda i, j: (i, 0)),
        out_shape=jax.ShapeDtypeStruct((n, D_MODEL), F32),
        scratch_shapes=[pltpu.VMEM((bm, D_MODEL), BF16)],
        compiler_params=_params(2, vmem),
        name="ffn",
    )(x, g, mod, mod, mod, wg, wu, wd, fg)


def kernel(x, c, ctx, c_ctx, w_mod, b_mod, norm1_g, norm2_g, w_in, w_gate_f, b_gate_f,
           w_gate_b, b_gate_b, gla_norm_g, conv_w, w_out, w_ffn_gate, w_ffn_up, w_ffn_down,
           final_g):
    assert x.shape[0] == 1 and w_mod.shape[0] == 1, "batch 1, depth 1 only"
    seq, ctx_len = x.shape[1], ctx.shape[1]
    x2, ctx2 = x[0], ctx[0]

    c_t = jnp.concatenate(
        [c[0][:, None], c_ctx[:, None], jnp.zeros((D_MODEL, SUBLANES - 2), F32)], axis=1)
    mod = _mod_call(c_t, w_mod[0], b_mod, EARLY_MOD_COLS)

    pad_f = jnp.zeros((LANES, GLA_KW), F32).at[:GATE_RANK].set(w_gate_f[0]).astype(BF16)
    pad_b = jnp.zeros((LANES, GLA_KW), F32).at[GATE_RANK:2 * GATE_RANK].set(w_gate_b[0]).astype(BF16)
    g1 = norm1_g[0][None]
    w_in_t = jnp.transpose(w_in[0])

    h_c, l_c = _prenorm_call(ctx2, g1, mod, w_in_t, row=1, bm=ctx_len)
    p_c, = _in_proj_call(h_c, w_in_t, bm=ctx_len, n_tiles=2)
    zero_state = jnp.zeros((GLA_HEADS, GLA_DK, GLA_DV), F32)
    _, _, s_f, s_b = _gla_call(p_c, l_c, pad_f, b_gate_f, pad_b, b_gate_b, zero_state,
                               zero_state, tb=ctx_len)

    h_x, l_x = _prenorm_call(x2, g1, mod, w_in_t, row=0, bm=2048)
    p_x, wg_b, wu_b, mod_late = _in_proj_call(
        h_x, w_in_t, (w_ffn_gate[0], w_ffn_up[0], c_t, w_mod[0], b_mod, EARLY_MOD_COLS),
        bm=1024, n_tiles=P_COLS // P_BN)
    o_f, o_b, _, _, wd_b = _gla_call(p_x, l_x, pad_f, b_gate_f, pad_b, b_gate_b, s_f, s_b,
                                     w_ffn_down[0], tb=256)
    x_mid = _mix_call(o_f, o_b, p_x, x2, mod_late, gla_norm_g, conv_w[0], w_out[0], bm=256)
    out = _ffn_call(x_mid, norm2_g[0][None], mod_late, wg_b, wu_b, wd_b, final_g[None],
                    bm=1024)
    return out[None]
```

```python
import functools

import jax
import jax.numpy as jnp
from jax import lax
from jax.experimental import pallas as pl
from jax.experimental.pallas import tpu as pltpu

F32 = jnp.float32
BF16 = jnp.bfloat16

D_MODEL = 2048
GRID_W = 64
GLA_HEADS = 4
GLA_DK = 128
GLA_DV = 256
GLA_KW = GLA_HEADS * GLA_DK
GLA_VW = GLA_HEADS * GLA_DV
GATE_RANK = 16
GATE_NORM = 16.0
CHUNK = 64
CONV_W = D_MODEL - GLA_VW
FFN_HIDDEN = 5632
EPS = 1e-6

LANES = 128
SUBLANES = 8
MIB = 1024 * 1024
VMEM_BYTES = 64 * MIB
VMEM_RESERVED = 6 * MIB
SPILL_BYTES = 4 * MIB

GATE_COL = 2 * GLA_KW + 2 * GLA_VW
GATE_COLS = 2 * GATE_RANK
P_COLS = GATE_COL + 3 * CONV_W
P_BN = 1024
P_SHIFT_FROM = GATE_COL // P_BN

_NT_DIMS = (((1,), (1,)), ((), ()))

MOD_BN = 2048
EARLY_MOD_COLS = 2 * D_MODEL
LATE_MOD_BN = 256
FFN_BH = 256
NORM_ROWS = 16


def _params(grid_rank, vmem_bytes):
    return pltpu.CompilerParams(
        dimension_semantics=("arbitrary",) * grid_rank,
        vmem_limit_bytes=min(int(vmem_bytes), VMEM_BYTES - VMEM_RESERVED),
    )


def _silu(v):
    return v * jax.nn.sigmoid(v)


def _for_row_chunks(rows, body):
    for r0 in range(0, rows, NORM_ROWS):
        body(slice(r0, r0 + NORM_ROWS))


def _norm_mod_rows(x_ref, g, mul, sh, dst_ref, rows, copy_ref=None):
    def body(rs):
        xs = x_ref[rs, :]
        ms = jnp.mean(xs * xs, axis=-1, keepdims=True)
        y = xs * lax.rsqrt(ms + EPS)
        dst_ref[rs, :] = ((y * g) * mul + sh).astype(BF16)
        if copy_ref is not None:
            copy_ref[rs, :] = xs
    _for_row_chunks(rows, body)


def _mod_prepare(c_ref, sb_ref):
    s = _silu(c_ref[...])
    for r in range(2):
        sb_ref[r] = jnp.broadcast_to(s[:, r:r + 1], (D_MODEL, LANES))


def _mod_matvec(w_ref, b_ref, sb_ref, o_ref, *, unrolled):
    nt = w_ref.shape[1] // LANES

    def body(kk, accs):
        r0 = kk * SUBLANES if unrolled else pl.multiple_of(kk * SUBLANES, SUBLANES)
        s0 = sb_ref[0, pl.ds(r0, SUBLANES), :]
        s1 = sb_ref[1, pl.ds(r0, SUBLANES), :]
        new = []
        for t in range(nt):
            w8 = w_ref[pl.ds(r0, SUBLANES), t * LANES:(t + 1) * LANES]
            new.append(accs[2 * t] + w8 * s0)
            new.append(accs[2 * t + 1] + w8 * s1)
        return tuple(new)

    accs = (jnp.zeros((SUBLANES, LANES), F32),) * (2 * nt)
    if unrolled:
        for kk in range(D_MODEL // SUBLANES):
            accs = body(kk, accs)
    else:
        accs = lax.fori_loop(0, D_MODEL // SUBLANES, body, accs, unroll=8)
    o_ref[...] = jnp.zeros(o_ref.shape, F32)
    for t in range(nt):
        b = b_ref[:, t * LANES:(t + 1) * LANES]
        for r in range(2):
            o_ref[r:r + 1, t * LANES:(t + 1) * LANES] = (
                jnp.sum(accs[2 * t + r], axis=0, keepdims=True) + b)


def _mod_kernel(c_ref, w_ref, b_ref, o_ref, sb_ref):
    @pl.when(pl.program_id(0) == 0)
    def _():
        _mod_prepare(c_ref, sb_ref)

    _mod_matvec(w_ref, b_ref, sb_ref, o_ref, unrolled=False)


def _mod_call(c_t, w_mod, b_mod, n):
    return pl.pallas_call(
        _mod_kernel,
        grid=(n // MOD_BN,),
        in_specs=[
            pl.BlockSpec((D_MODEL, SUBLANES), lambda j: (0, 0)),
            pl.BlockSpec((D_MODEL, MOD_BN), lambda j: (0, j)),
            pl.BlockSpec((1, MOD_BN), lambda j: (0, j)),
        ],
        out_specs=pl.BlockSpec((SUBLANES, MOD_BN), lambda j: (0, j)),
        out_shape=jax.ShapeDtypeStruct((SUBLANES, n), F32),
        scratch_shapes=[pltpu.VMEM((2, D_MODEL, LANES), F32)],
        compiler_params=_params(1, 2 * D_MODEL * MOD_BN * 4 + 2 * SPILL_BYTES),
        name="mod",
    )(c_t, w_mod, b_mod)


def _prenorm_kernel(x_ref, g_ref, sh_ref, sc_ref, wl_ref, h_ref, l_ref, *, row, bm):
    _norm_mod_rows(x_ref, g_ref[...], 1.0 + sc_ref[row:row + 1, :], sh_ref[row:row + 1, :],
                   h_ref, bm)
    l_ref[...] = lax.dot_general(h_ref[...], wl_ref[...].astype(BF16), _NT_DIMS,
                                 preferred_element_type=F32).astype(BF16)


def _prenorm_call(x, g, mod, w_in_t, *, row, bm):
    m = x.shape[0]
    kern = functools.partial(_prenorm_kernel, row=row, bm=bm)
    vmem = (2 * bm * D_MODEL * (4 + 2) + 4 * D_MODEL * LANES * 4 + 8 * NORM_ROWS * D_MODEL * 4
            + SPILL_BYTES)
    return pl.pallas_call(
        kern,
        grid=(m // bm,),
        in_specs=[
            pl.BlockSpec((bm, D_MODEL), lambda i: (i, 0)),
            pl.BlockSpec((1, D_MODEL), lambda i: (0, 0)),
            pl.BlockSpec((SUBLANES, D_MODEL), lambda i: (0, 0)),
            pl.BlockSpec((SUBLANES, D_MODEL), lambda i: (0, 1)),
            pl.BlockSpec((LANES, D_MODEL), lambda i: (GATE_COL // LANES, 0)),
        ],
        out_specs=[pl.BlockSpec((bm, D_MODEL), lambda i: (i, 0)),
                   pl.BlockSpec((bm, LANES), lambda i: (i, 0))],
        out_shape=[jax.ShapeDtypeStruct((m, D_MODEL), BF16),
                   jax.ShapeDtypeStruct((m, LANES), BF16)],
        compiler_params=_params(1, vmem),
        name="prenorm",
    )(x, g, mod, mod, w_in_t)


W_ROWS = 256


class _CastJob:
    def __init__(self, sources, rows, stage_ref, out_ref, sem_in, sem_out):
        self.sources, self.rows = sources, rows
        self.stage_ref, self.out_ref, self.sem_in, self.sem_out = stage_ref, out_ref, sem_in, sem_out
        self.n_blocks = sum(n for _, _, n in sources)

    def _for_block(self, c, fn):
        first = 0
        for src, dst, n in self.sources:
            @pl.when((c >= first) & (c < first + n))
            def _(src=src, dst=dst, first=first):
                fn(src, dst, pl.multiple_of((c - first) * self.rows, self.rows), lax.rem(c, 2))
            first += n

    def _fetch(self, src, dst, r, slot):
        return pltpu.make_async_copy(src.at[pl.ds(r, self.rows), :], self.stage_ref.at[slot],
                                     self.sem_in.at[slot])

    def _store(self, src, dst, r, slot):
        return pltpu.make_async_copy(self.out_ref.at[slot], dst.at[pl.ds(r, self.rows), :],
                                     self.sem_out.at[slot])

    def before(self, step):
        @pl.when(step == 0)
        def _():
            self._for_block(step, lambda *a: self._fetch(*a).start())
        self._for_block(step, lambda *a: self._fetch(*a).wait())
        self._for_block(step + 1, lambda *a: self._fetch(*a).start())
        self._for_block(step - 2, lambda *a: self._store(*a).wait())

    def cast(self, step):
        slot = lax.rem(step, 2)
        self.out_ref[slot] = self.stage_ref[slot].astype(BF16)

    def after(self, step):
        self._for_block(step, lambda *a: self._store(*a).start())


def _cast_scratch(rows, cols):
    return [pltpu.VMEM((2, rows, cols), F32), pltpu.VMEM((2, rows, cols), BF16),
            pltpu.SemaphoreType.DMA((2,)), pltpu.SemaphoreType.DMA((2,))]


IN_PROJ_CAST_ROWS = 128
GLA_CAST_ROWS = 256


def _in_proj_kernel(h_ref, wa_ref, wb_ref, *refs, side_jobs):
    j = pl.program_id(0)
    if side_jobs:
        (wg_ref, wu_ref, c_ref, wm_ref, bm_ref, o_ref, wg_out, wu_out, mod_ref,
         w_ref, stage_ref, cast_ref, sem_in, sem_out, sb_ref) = refs
        n = D_MODEL // IN_PROJ_CAST_ROWS
        job = _CastJob(((wg_ref, wg_out, n), (wu_ref, wu_out, n)), IN_PROJ_CAST_ROWS,
                       stage_ref, cast_ref, sem_in, sem_out)
        step = j * pl.num_programs(1) + pl.program_id(1)
        job.before(step)

        @pl.when(step == 0)
        def _():
            _mod_prepare(c_ref, sb_ref)
    else:
        o_ref, w_ref = refs

    @pl.when(pl.program_id(1) == 0)
    def _():
        @pl.when(j < P_SHIFT_FROM)
        def _():
            for r0 in range(0, P_BN, W_ROWS):
                w_ref[r0:r0 + W_ROWS, :] = wa_ref[r0:r0 + W_ROWS, :].astype(BF16)

        @pl.when(j >= P_SHIFT_FROM)
        def _():
            keep = P_BN - GATE_COLS
            for r0 in range(0, keep, W_ROWS):
                r1 = min(r0 + W_ROWS, keep)
                w_ref[r0:r1, :] = wa_ref[GATE_COLS + r0:GATE_COLS + r1, :].astype(BF16)
            w_ref[keep:, :] = wb_ref[...].astype(BF16)

    o_ref[...] = lax.dot_general(h_ref[...], w_ref[...], _NT_DIMS,
                                 preferred_element_type=F32).astype(BF16)
    if side_jobs:
        job.cast(step)
        _mod_matvec(wm_ref, bm_ref, sb_ref, mod_ref, unrolled=True)
        job.after(step)


def _in_proj_call(h, w_in_t, side=None, *, bm, n_tiles):
    m = h.shape[0]
    ni = m // bm
    vmem = (2 * bm * D_MODEL * 2 + 2 * (P_BN + GATE_COLS) * D_MODEL * 4 + P_BN * D_MODEL * 2
            + 2 * bm * P_BN * 2 + bm * P_BN * 4 + 4 * W_ROWS * D_MODEL * 4 + SPILL_BYTES)
    in_specs = [
        pl.BlockSpec((bm, D_MODEL), lambda j, i: (i, 0)),
        pl.BlockSpec((P_BN, D_MODEL), lambda j, i: (j, 0)),
        pl.BlockSpec((GATE_COLS, D_MODEL), lambda j, i: ((j + 1) * (P_BN // GATE_COLS), 0)),
    ]
    out_specs = [pl.BlockSpec((bm, P_BN), lambda j, i: (i, j))]
    out_shape = [jax.ShapeDtypeStruct((m, n_tiles * P_BN), BF16)]
    scratch = [pltpu.VMEM((P_BN, D_MODEL), BF16)]
    operands = [h, w_in_t, w_in_t]
    if side is not None:
        w_gate, w_up, c_t, w_mod, b_mod, first_col = side
        n_steps = n_tiles * ni
        assert n_steps >= 2 * D_MODEL // IN_PROJ_CAST_ROWS + 2
        any_spec = pl.BlockSpec(memory_space=pl.ANY)
        n_mod = w_mod.shape[1] - first_col
        n_blocks = n_mod // LATE_MOD_BN
        assert n_mod % LATE_MOD_BN == 0 and first_col % LATE_MOD_BN == 0 and n_blocks <= n_steps
        mod_block = lambda j, i: jnp.minimum(j * ni + i, n_blocks - 1)
        in_specs += [
            any_spec, any_spec,
            pl.BlockSpec((D_MODEL, SUBLANES), lambda j, i: (0, 0)),
            pl.BlockSpec((D_MODEL, LATE_MOD_BN),
                         lambda j, i: (0, first_col // LATE_MOD_BN + mod_block(j, i))),
            pl.BlockSpec((1, LATE_MOD_BN),
                         lambda j, i: (0, first_col // LATE_MOD_BN + mod_block(j, i))),
        ]
        out_specs += [any_spec, any_spec,
                      pl.BlockSpec((SUBLANES, LATE_MOD_BN), lambda j, i: (0, mod_block(j, i)))]
        out_shape += [jax.ShapeDtypeStruct(w_gate.shape, BF16),
                      jax.ShapeDtypeStruct(w_up.shape, BF16),
                      jax.ShapeDtypeStruct((SUBLANES, n_mod), F32)]
        scratch += _cast_scratch(IN_PROJ_CAST_ROWS, FFN_HIDDEN)
        scratch += [pltpu.VMEM((2, D_MODEL, LANES), F32)]
        operands += [w_gate, w_up, c_t, w_mod, b_mod]
        vmem += (IN_PROJ_CAST_ROWS * FFN_HIDDEN * (2 * 4 + 2 * 2)
                 + 2 * D_MODEL * LATE_MOD_BN * 4 + 2 * D_MODEL * LANES * 4)
    return pl.pallas_call(
        functools.partial(_in_proj_kernel, side_jobs=side is not None),
        grid=(n_tiles, ni),
        in_specs=in_specs,
        out_specs=out_specs,
        out_shape=out_shape,
        scratch_shapes=scratch,
        compiler_params=_params(2, vmem),
        name="in_proj",
    )(*operands)


def _gla_log_decay_split(l_ref, wg_ref, bg_ref):
    z = jnp.dot(l_ref[...], wg_ref[...], preferred_element_type=F32) + bg_ref[...]
    log_a = (jnp.minimum(z, 0.0) - jnp.log(1.0 + jnp.exp(-jnp.abs(z)))) * (1.0 / GATE_NORM)
    hi = log_a.astype(BF16)
    lo = (log_a - hi.astype(F32)).astype(BF16)
    return hi, lo


def _gla_chunk_cumsum(pieces, *, reverse, tb):
    r = lax.broadcasted_iota(jnp.int32, (tb, tb), 0)
    cc = lax.broadcasted_iota(jnp.int32, (tb, tb), 1)
    same_chunk = (r // CHUNK) == (cc // CHUNK)
    tri = jnp.where(same_chunk & ((cc >= r) if reverse else (cc <= r)), 1.0, 0.0).astype(BF16)
    hi, lo = pieces
    return (jnp.dot(tri, hi, preferred_element_type=F32)
            + jnp.dot(tri, lo, preferred_element_type=F32))


def _gla_kernel(qf, kf, vf, lf, lf_next, qb, kb, vb, lb, lb_next, wgf, bgf, wgb, bgb, s0f, s0b,
                *refs, tb, cast):
    i = pl.program_id(0)
    if cast:
        (wd_ref, of, ob, sf, sb, wd_out, s_ref, cum_ref,
         stage_ref, cast_ref, sem_in, sem_out) = refs
        job = _CastJob(((wd_ref, wd_out, FFN_HIDDEN // GLA_CAST_ROWS),), GLA_CAST_ROWS,
                       stage_ref, cast_ref, sem_in, sem_out)
        job.before(i)
    else:
        of, ob, sf, sb, s_ref, cum_ref = refs

    @pl.when(i == 0)
    def _():
        s_ref[0] = s0f[...]
        s_ref[1] = s0b[...]
        cum_ref[0] = _gla_chunk_cumsum(_gla_log_decay_split(lf, wgf, bgf), reverse=False, tb=tb)
        cum_ref[1] = _gla_chunk_cumsum(_gla_log_decay_split(lb, wgb, bgb), reverse=True, tb=tb)

    dirs = ((qf, kf, vf, of, False), (qb, kb, vb, ob, True))
    ii = lax.broadcasted_iota(jnp.int32, (CHUNK, GLA_DK), 0)
    jj = lax.broadcasted_iota(jnp.int32, (CHUNK, GLA_DK), 1)
    nc = tb // CHUNK

    def start(step):
        items = []
        for d, (q_ref, k_ref, v_ref, o_ref, reverse) in enumerate(dirs):
            r0 = (nc - 1 - step if reverse else step) * CHUNK
            cum_c = cum_ref[d, r0:r0 + CHUNK, :]
            last = cum_c[0:1, :] if reverse else cum_c[CHUNK - 1:CHUNK, :]
            q_c = q_ref[r0:r0 + CHUNK, :].astype(F32)
            k_c = k_ref[r0:r0 + CHUNK, :].astype(F32)
            q_e = ((q_c * (GLA_DK ** -0.5)) * jnp.exp(cum_c)).astype(BF16)
            k_e = (k_c * jnp.exp(-cum_c)).astype(BF16)
            k_d = (k_c * jnp.exp(last - cum_c)).astype(BF16)
            decay = jnp.exp(last)
            mask = ((jj >= ii) & (jj < CHUNK)) if reverse else (jj <= ii)
            for h in range(GLA_HEADS):
                ks = slice(h * GLA_DK, (h + 1) * GLA_DK)
                vs = slice(h * GLA_DV, (h + 1) * GLA_DV)
                items.append(dict(
                    d=d, h=h, r0=r0, vs=vs, o_ref=o_ref, mask=mask,
                    q_e=q_e[:, ks], k_e=k_e[:, ks], k_d=k_d[:, ks], decay=decay[:, ks],
                    v=v_ref[r0:r0 + CHUNK, vs]))
        for it in items:
            k_pad = jnp.concatenate([it["k_e"], jnp.zeros((GLA_DK - CHUNK, GLA_DK), BF16)], axis=0)
            it["scores"] = lax.dot_general(it["q_e"], k_pad, _NT_DIMS,
                                           preferred_element_type=F32)
        return items

    items = start(0)
    next_cum = None
    for step in range(nc):
        following = start(step + 1) if step + 1 < nc else None
        if step == 1:
            next_f = _gla_log_decay_split(lf_next, wgf, bgf)
            next_b = _gla_log_decay_split(lb_next, wgb, bgb)
        if step == nc - 1:
            next_cum = (_gla_chunk_cumsum(next_f, reverse=False, tb=tb),
                        _gla_chunk_cumsum(next_b, reverse=True, tb=tb))
        for it in items:
            scores = jnp.where(it["mask"], it["scores"], 0.0).astype(BF16)
            s_old = s_ref[it["d"], it["h"]]
            lhs = jnp.concatenate([scores, it["q_e"]], axis=1)
            rhs = jnp.concatenate(
                [it["v"], jnp.zeros((GLA_DK - CHUNK, GLA_DV), BF16), s_old.astype(BF16)], axis=0)
            it["o_ref"][it["r0"]:it["r0"] + CHUNK, it["vs"]] = jnp.dot(
                lhs, rhs, preferred_element_type=F32).astype(BF16)
            ds = lax.dot_general(it["k_d"], it["v"], (((0,), (0,)), ((), ())),
                                 preferred_element_type=F32)
            dcol = jnp.transpose(jnp.broadcast_to(it["decay"], (SUBLANES, GLA_DK)))[:, 0:1]
            s_ref[it["d"], it["h"]] = dcol * s_old + ds
        items = following
    cum_ref[0] = next_cum[0]
    cum_ref[1] = next_cum[1]
    if cast:
        job.cast(i)
        job.after(i)

    @pl.when(i == pl.num_programs(0) - 1)
    def _():
        sf[...] = s_ref[0]
        sb[...] = s_ref[1]


def _gla_call(p, l, wgf, bgf, wgb, bgb, s0f, s0b, cast_weight=None, *, tb):
    n = p.shape[0]
    nb = n // tb
    fwd = lambda i: i
    bwd = lambda i: nb - 1 - i
    fwd_next = lambda i: jnp.minimum(i + 1, nb - 1)
    bwd_next = lambda i: jnp.maximum(nb - 2 - i, 0)

    def specs(pos, pos_next):
        return [
            pl.BlockSpec((tb, GLA_KW), lambda i: (pos(i), 0)),
            pl.BlockSpec((tb, GLA_KW), lambda i: (pos(i), 1)),
            pl.BlockSpec((tb, GLA_VW), lambda i: (pos(i), 1)),
            pl.BlockSpec((tb, LANES), lambda i: (pos(i), 0)),
            pl.BlockSpec((tb, LANES), lambda i: (pos_next(i), 0)),
        ]

    const2 = lambda i: (0, 0)
    const3 = lambda i: (0, 0, 0)
    state_shape = (GLA_HEADS, GLA_DK, GLA_DV)
    state_spec = pl.BlockSpec(state_shape, const3)
    cast = cast_weight is not None
    kern = functools.partial(_gla_kernel, tb=tb, cast=cast)
    vmem = (2 * 2 * tb * (2 * GLA_KW + GLA_VW + LANES) * 2 + 2 * 2 * tb * GLA_VW * 2
            + 10 * GLA_HEADS * GLA_DK * GLA_DV * 4 + 24 * tb * GLA_KW * 4 + 2 * SPILL_BYTES)
    in_specs = specs(fwd, fwd_next) + specs(bwd, bwd_next) + [
        pl.BlockSpec((LANES, GLA_KW), const2), pl.BlockSpec((1, GLA_KW), const2),
        pl.BlockSpec((LANES, GLA_KW), const2), pl.BlockSpec((1, GLA_KW), const2),
        state_spec, state_spec,
    ]
    out_specs = [
        pl.BlockSpec((tb, GLA_VW), lambda i: (i, 0)),
        pl.BlockSpec((tb, GLA_VW), lambda i: (nb - 1 - i, 0)),
        state_spec, state_spec,
    ]
    out_shape = [
        jax.ShapeDtypeStruct((n, GLA_VW), BF16), jax.ShapeDtypeStruct((n, GLA_VW), BF16),
        jax.ShapeDtypeStruct(state_shape, F32), jax.ShapeDtypeStruct(state_shape, F32),
    ]
    scratch = [pltpu.VMEM((2,) + state_shape, F32), pltpu.VMEM((2, tb, GLA_KW), F32)]
    operands = [p, p, p, l, l, p, p, p, l, l, wgf, bgf, wgb, bgb, s0f, s0b]
    if cast:
        assert cast_weight.shape == (FFN_HIDDEN, D_MODEL)
        assert nb >= FFN_HIDDEN // GLA_CAST_ROWS + 2
        any_spec = pl.BlockSpec(memory_space=pl.ANY)
        in_specs.append(any_spec)
        out_specs.append(any_spec)
        out_shape.append(jax.ShapeDtypeStruct(cast_weight.shape, BF16))
        scratch += _cast_scratch(GLA_CAST_ROWS, D_MODEL)
        operands.append(cast_weight)
        vmem += GLA_CAST_ROWS * D_MODEL * (2 * 4 + 2 * 2)
    return pl.pallas_call(
        kern,
        grid=(nb,),
        in_specs=in_specs,
        out_specs=out_specs,
        out_shape=out_shape,
        scratch_shapes=scratch,
        compiler_params=_params(1, vmem),
        name="gla",
    )(*operands)


def _mix_kernel(of_ref, ob_ref, g_ref, cb_ref, cc_ref, cx_ref, x_ref, gt_ref, gg_ref, cw_ref,
                wo_ref, o_ref, m_ref, w_ref, *, bm):
    @pl.when(pl.program_id(0) == 0)
    def _():
        for r0 in range(0, D_MODEL, W_ROWS):
            w_ref[r0:r0 + W_ROWS, :] = wo_ref[r0:r0 + W_ROWS, :].astype(BF16)

    u = cc_ref[...].astype(F32) * cx_ref[...].astype(F32)
    col = lax.broadcasted_iota(jnp.int32, u.shape, 0) % GRID_W
    u_prev = jnp.where(col == 0, 0.0, pltpu.roll(u, 1, 0))
    u_next = jnp.where(col == GRID_W - 1, 0.0, pltpu.roll(u, bm - 1, 0))
    conv = cw_ref[0:1, :] * u_prev + cw_ref[1:2, :] * u + cw_ref[2:3, :] * u_next
    m_ref[:, GLA_VW:] = (cb_ref[...].astype(F32) * conv).astype(BF16)
    mixed = jnp.dot(m_ref[:, GLA_VW:], w_ref[GLA_VW:, :], preferred_element_type=F32)

    gg = gg_ref[...]
    for h in range(GLA_HEADS):
        vs = slice(h * GLA_DV, (h + 1) * GLA_DV)
        o = of_ref[:, vs].astype(F32) + ob_ref[:, vs].astype(F32)
        o32 = o * lax.rsqrt(jnp.mean(o * o, axis=-1, keepdims=True) + EPS)
        m_ref[:, vs] = ((o32 * gg) * _silu(g_ref[:, vs].astype(F32))).astype(BF16)
        mixed = mixed + jnp.dot(m_ref[:, vs], w_ref[vs, :], preferred_element_type=F32)
    o_ref[...] = x_ref[...] + gt_ref[0:1, :] * mixed


def _mix_call(o_f, o_b, p, x, mod, gla_g, conv_w, w_out, *, bm):
    n = x.shape[0]
    kern = functools.partial(_mix_kernel, bm=bm)
    pcol = lambda k: pl.BlockSpec((bm, GLA_VW), lambda i, k=k: (i, k))
    const2 = lambda i: (0, 0)
    vmem = (2 * 6 * bm * GLA_VW * 2 + 2 * 2 * bm * D_MODEL * 4
            + D_MODEL * D_MODEL * (4 + 2) + bm * D_MODEL * 2 + 8 * bm * GLA_VW * 4
            + SPILL_BYTES)
    return pl.pallas_call(
        kern,
        grid=(n // bm,),
        in_specs=[
            pl.BlockSpec((bm, GLA_VW), lambda i: (i, 0)),
            pl.BlockSpec((bm, GLA_VW), lambda i: (i, 0)),
            pcol(2), pcol(3), pcol(4), pcol(5),
            pl.BlockSpec((bm, D_MODEL), lambda i: (i, 0)),
            pl.BlockSpec((SUBLANES, D_MODEL), lambda i: (0, 0)),
            pl.BlockSpec((1, GLA_DV), const2),
            pl.BlockSpec((3, CONV_W), const2),
            pl.BlockSpec((D_MODEL, D_MODEL), const2, pipeline_mode=pl.Buffered(1)),
        ],
        out_specs=pl.BlockSpec((bm, D_MODEL), lambda i: (i, 0)),
        out_shape=jax.ShapeDtypeStruct((n, D_MODEL), F32),
        scratch_shapes=[pltpu.VMEM((bm, D_MODEL), BF16), pltpu.VMEM((D_MODEL, D_MODEL), BF16)],
        compiler_params=_params(1, vmem),
        name="mix",
    )(o_f, o_b, p, p, p, p, x, mod, gla_g, conv_w, w_out)


def _ffn_kernel(x_ref, g_ref, sh_ref, sc_ref, gt_ref, wg_hbm, wu_hbm, wd_hbm, fg_ref,
                o_ref, h_ref, wgu_buf, wd_buf, sem, *, bm):
    n_pairs = FFN_HIDDEN // (2 * FFN_BH)

    def tile_copies(t, slot):
        cols = pl.ds(pl.multiple_of(t * FFN_BH, FFN_BH), FFN_BH)
        return (pltpu.make_async_copy(wg_hbm.at[:, cols], wgu_buf.at[slot, :, 0:FFN_BH],
                                      sem.at[slot, 0]),
                pltpu.make_async_copy(wu_hbm.at[:, cols], wgu_buf.at[slot, :, FFN_BH:2 * FFN_BH],
                                      sem.at[slot, 1]),
                pltpu.make_async_copy(wd_hbm.at[cols, :], wd_buf.at[slot], sem.at[slot, 2]))

    def start_tile(t, slot):
        for cp in tile_copies(t, slot):
            cp.start()

    def wait_tile(t, slot):
        for cp in tile_copies(t, slot):
            cp.wait()

    def accumulate(slot):
        gate_up = jnp.dot(h_ref[...], wgu_buf[slot], preferred_element_type=F32)
        act = (_silu(gate_up[:, :FFN_BH]) * gate_up[:, FFN_BH:]).astype(BF16)
        part = jnp.dot(act, wd_buf[slot], preferred_element_type=F32)
        o_ref[...] += gt_ref[0:1, :] * part

    def tile_pair(p, prefetch_next):
        t = 2 * p
        wait_tile(t, 0)
        start_tile(t + 1, 1)
        accumulate(0)
        wait_tile(t + 1, 1)
        if prefetch_next:
            start_tile(t + 2, 0)
        accumulate(1)

    start_tile(0, 0)
    _norm_mod_rows(x_ref, g_ref[...], 1.0 + sc_ref[0:1, :], sh_ref[0:1, :], h_ref, bm,
                   copy_ref=o_ref)

    def loop_body(p, carry):
        tile_pair(p, True)
        return carry

    lax.fori_loop(0, n_pairs - 1, loop_body, 0)
    tile_pair(n_pairs - 1, False)

    fg = fg_ref[...]

    def final_norm(rs):
        xn = o_ref[rs, :]
        ms = jnp.mean(xn * xn, axis=-1, keepdims=True)
        o_ref[rs, :] = (xn * lax.rsqrt(ms + EPS)) * fg
    _for_row_chunks(bm, final_norm)


def _ffn_call(x, g, mod, wg, wu, wd, fg, *, bm):
    n = x.shape[0]
    kern = functools.partial(_ffn_kernel, bm=bm)
    modspec = lambda k: pl.BlockSpec((SUBLANES, D_MODEL), lambda i, k=k: (0, k))
    any_spec = pl.BlockSpec(memory_space=pl.ANY)
    vmem = (4 * bm * D_MODEL * 4 + bm * D_MODEL * 2 + 2 * 3 * D_MODEL * FFN_BH * 2
            + 4 * bm * FFN_BH * 4 + bm * D_MODEL * 4 + SPILL_BYTES)
    return pl.pallas_call(
        kern,
        grid=(n // bm,),
        in_specs=[
            pl.BlockSpec((bm, D_MODEL), lambda i: (i, 0)),
            pl.BlockSpec((1, D_MODEL), lambda i: (0, 0)),
            modspec(1), modspec(2), modspec(3),
            any_spec, any_spec, any_spec,
            pl.BlockSpec((1, D_MODEL), lambda i: (0, 0)),
        ],
        out_specs=pl.BlockSpec((bm, D_MODEL), lambda i: (i, 0)),
        out_shape=jax.ShapeDtypeStruct((n, D_MODEL), F32),
        scratch_shapes=[
            pltpu.VMEM((bm, D_MODEL), BF16),
            pltpu.VMEM((2, D_MODEL, 2 * FFN_BH), BF16),
            pltpu.VMEM((2, FFN_BH, D_MODEL), BF16),
            pltpu.SemaphoreType.DMA((2, 3)),
        ],
        compiler_params=_params(1, vmem),
        name="ffn",
    )(x, g, mod, mod, mod, wg, wu, wd, fg)


def kernel(x, c, ctx, c_ctx, w_mod, b_mod, norm1_g, norm2_g, w_in, w_gate_f, b_gate_f,
           w_gate_b, b_gate_b, gla_norm_g, conv_w, w_out, w_ffn_gate, w_ffn_up, w_ffn_down,
           final_g):
    assert x.shape[0] == 1 and w_mod.shape[0] == 1, "batch 1, depth 1 only"
    seq, ctx_len = x.shape[1], ctx.shape[1]
    x2, ctx2 = x[0], ctx[0]

    c_t = jnp.concatenate(
        [c[0][:, None], c_ctx[:, None], jnp.zeros((D_MODEL, SUBLANES - 2), F32)], axis=1)
    mod = _mod_call(c_t, w_mod[0], b_mod, EARLY_MOD_COLS)

    pad_f = jnp.zeros((LANES, GLA_KW), F32).at[:GATE_RANK].set(w_gate_f[0]).astype(BF16)
    pad_b = jnp.zeros((LANES, GLA_KW), F32).at[GATE_RANK:2 * GATE_RANK].set(w_gate_b[0]).astype(BF16)
    g1 = norm1_g[0][None]
    w_in_t = jnp.transpose(w_in[0])

    h_c, l_c = _prenorm_call(ctx2, g1, mod, w_in_t, row=1, bm=ctx_len)
    p_c, = _in_proj_call(h_c, w_in_t, bm=ctx_len, n_tiles=2)
    zero_state = jnp.zeros((GLA_HEADS, GLA_DK, GLA_DV), F32)
    _, _, s_f, s_b = _gla_call(p_c, l_c, pad_f, b_gate_f, pad_b, b_gate_b, zero_state,
                               zero_state, tb=ctx_len)

    h_x, l_x = _prenorm_call(x2, g1, mod, w_in_t, row=0, bm=2048)
    p_x, wg_b, wu_b, mod_late = _in_proj_call(
        h_x, w_in_t, (w_ffn_gate[0], w_ffn_up[0], c_t, w_mod[0], b_mod, EARLY_MOD_COLS),
        bm=1024, n_tiles=P_COLS // P_BN)
    o_f, o_b, _, _, wd_b = _gla_call(p_x, l_x, pad_f, b_gate_f, pad_b, b_gate_b, s_f, s_b,
                                     w_ffn_down[0], tb=256)
    x_mid = _mix_call(o_f, o_b, p_x, x2, mod_late, gla_norm_g, conv_w[0], w_out[0], bm=256)
    out = _ffn_call(x_mid, norm2_g[0][None], mod_late, wg_b, wu_b, wd_b, final_g[None],
                    bm=1024)
    return out[None]
```

```python
import functools

import jax
import jax.numpy as jnp
from jax import lax
from jax.experimental import pallas as pl
from jax.experimental.pallas import tpu as pltpu

F32 = jnp.float32
BF16 = jnp.bfloat16

D_MODEL = 2048
GRID_W = 64
GLA_HEADS = 4
GLA_DK = 128
GLA_DV = 256
GLA_KW = GLA_HEADS * GLA_DK
GLA_VW = GLA_HEADS * GLA_DV
GATE_RANK = 16
GATE_NORM = 16.0
CHUNK = 64
CONV_W = D_MODEL - GLA_VW
FFN_HIDDEN = 5632
EPS = 1e-6

LANES = 128
SUBLANES = 8
MIB = 1024 * 1024
VMEM_BYTES = 64 * MIB
VMEM_RESERVED = 6 * MIB
SPILL_BYTES = 4 * MIB

GATE_COL = 2 * GLA_KW + 2 * GLA_VW
GATE_COLS = 2 * GATE_RANK
P_COLS = GATE_COL + 3 * CONV_W
P_BN = 1024
P_SHIFT_FROM = GATE_COL // P_BN

_NT_DIMS = (((1,), (1,)), ((), ()))

MOD_BN = 2048
EARLY_MOD_COLS = 2 * D_MODEL
LATE_MOD_BN = 256
FFN_BH = 256
NORM_ROWS = 16


def _params(grid_rank, vmem_bytes):
    return pltpu.CompilerParams(
        dimension_semantics=("arbitrary",) * grid_rank,
        vmem_limit_bytes=min(int(vmem_bytes), VMEM_BYTES - VMEM_RESERVED),
    )


def _silu(v):
    return v * jax.nn.sigmoid(v)


def _for_row_chunks(rows, body):
    for r0 in range(0, rows, NORM_ROWS):
        body(slice(r0, r0 + NORM_ROWS))


def _norm_mod_rows(x_ref, g, mul, sh, dst_ref, rows, copy_ref=None):
    def body(rs):
        xs = x_ref[rs, :]
        ms = jnp.mean(xs * xs, axis=-1, keepdims=True)
        y = xs * lax.rsqrt(ms + EPS)
        dst_ref[rs, :] = ((y * g) * mul + sh).astype(BF16)
        if copy_ref is not None:
            copy_ref[rs, :] = xs
    _for_row_chunks(rows, body)


def _mod_prepare(c_ref, sb_ref):
    s = _silu(c_ref[...])
    for r in range(2):
        sb_ref[r] = jnp.broadcast_to(s[:, r:r + 1], (D_MODEL, LANES))


def _mod_matvec(w_ref, b_ref, sb_ref, o_ref, *, unrolled):
    nt = w_ref.shape[1] // LANES

    def body(kk, accs):
        r0 = kk * SUBLANES if unrolled else pl.multiple_of(kk * SUBLANES, SUBLANES)
        s0 = sb_ref[0, pl.ds(r0, SUBLANES), :]
        s1 = sb_ref[1, pl.ds(r0, SUBLANES), :]
        new = []
        for t in range(nt):
            w8 = w_ref[pl.ds(r0, SUBLANES), t * LANES:(t + 1) * LANES]
            new.append(accs[2 * t] + w8 * s0)
            new.append(accs[2 * t + 1] + w8 * s1)
        return tuple(new)

    accs = (jnp.zeros((SUBLANES, LANES), F32),) * (2 * nt)
    if unrolled:
        for kk in range(D_MODEL // SUBLANES):
            accs = body(kk, accs)
    else:
        accs = lax.fori_loop(0, D_MODEL // SUBLANES, body, accs, unroll=8)
    o_ref[...] = jnp.zeros(o_ref.shape, F32)
    for t in range(nt):
        b = b_ref[:, t * LANES:(t + 1) * LANES]
        for r in range(2):
            o_ref[r:r + 1, t * LANES:(t + 1) * LANES] = (
                jnp.sum(accs[2 * t + r], axis=0, keepdims=True) + b)


def _mod_kernel(c_ref, w_ref, b_ref, o_ref, sb_ref):
    @pl.when(pl.program_id(0) == 0)
    def _():
        _mod_prepare(c_ref, sb_ref)

    _mod_matvec(w_ref, b_ref, sb_ref, o_ref, unrolled=False)


def _mod_call(c_t, w_mod, b_mod, n):
    return pl.pallas_call(
        _mod_kernel,
        grid=(n // MOD_BN,),
        in_specs=[
            pl.BlockSpec((D_MODEL, SUBLANES), lambda j: (0, 0)),
            pl.BlockSpec((D_MODEL, MOD_BN), lambda j: (0, j)),
            pl.BlockSpec((1, MOD_BN), lambda j: (0, j)),
        ],
        out_specs=pl.BlockSpec((SUBLANES, MOD_BN), lambda j: (0, j)),
        out_shape=jax.ShapeDtypeStruct((SUBLANES, n), F32),
        scratch_shapes=[pltpu.VMEM((2, D_MODEL, LANES), F32)],
        compiler_params=_params(1, 2 * D_MODEL * MOD_BN * 4 + 2 * SPILL_BYTES),
        name="mod",
    )(c_t, w_mod, b_mod)


def _prenorm_kernel(x_ref, g_ref, sh_ref, sc_ref, wl_ref, h_ref, l_ref, *, row, bm):
    _norm_mod_rows(x_ref, g_ref[...], 1.0 + sc_ref[row:row + 1, :], sh_ref[row:row + 1, :],
                   h_ref, bm)
    l_ref[...] = lax.dot_general(h_ref[...], wl_ref[...].astype(BF16), _NT_DIMS,
                                 preferred_element_type=F32).astype(BF16)


def _prenorm_call(x, g, mod, w_in_t, *, row, bm):
    m = x.shape[0]
    kern = functools.partial(_prenorm_kernel, row=row, bm=bm)
    vmem = (2 * bm * D_MODEL * (4 + 2) + 4 * D_MODEL * LANES * 4 + 8 * NORM_ROWS * D_MODEL * 4
            + SPILL_BYTES)
    return pl.pallas_call(
        kern,
        grid=(m // bm,),
        in_specs=[
            pl.BlockSpec((bm, D_MODEL), lambda i: (i, 0)),
            pl.BlockSpec((1, D_MODEL), lambda i: (0, 0)),
            pl.BlockSpec((SUBLANES, D_MODEL), lambda i: (0, 0)),
            pl.BlockSpec((SUBLANES, D_MODEL), lambda i: (0, 1)),
            pl.BlockSpec((LANES, D_MODEL), lambda i: (GATE_COL // LANES, 0)),
        ],
        out_specs=[pl.BlockSpec((bm, D_MODEL), lambda i: (i, 0)),
                   pl.BlockSpec((bm, LANES), lambda i: (i, 0))],
        out_shape=[jax.ShapeDtypeStruct((m, D_MODEL), BF16),
                   jax.ShapeDtypeStruct((m, LANES), BF16)],
        compiler_params=_params(1, vmem),
        name="prenorm",
    )(x, g, mod, mod, w_in_t)


W_ROWS = 256


class _CastJob:
    def __init__(self, sources, rows, stage_ref, out_ref, sem_in, sem_out, tile_cols=None):
        self.sources, self.rows, self.tile_cols = sources, rows, tile_cols
        self.stage_ref, self.out_ref, self.sem_in, self.sem_out = stage_ref, out_ref, sem_in, sem_out

    def _for_block(self, c, fn):
        first = 0
        for src, dst, n, col0 in self.sources:
            @pl.when((c >= first) & (c < first + n))
            def _(src=src, dst=dst, first=first, col0=col0):
                fn(src, dst, col0, pl.multiple_of((c - first) * self.rows, self.rows),
                   lax.rem(c, 2))
            first += n

    def _fetch(self, src, dst, col0, r, slot):
        return pltpu.make_async_copy(src.at[pl.ds(r, self.rows), :], self.stage_ref.at[slot],
                                     self.sem_in.at[slot])

    def _store(self, src, dst, col0, r, slot):
        if self.tile_cols is None:
            window = dst.at[pl.ds(r, self.rows), :]
        else:
            window = dst.at[:, pl.ds(r, self.rows), col0:col0 + self.tile_cols]
        return pltpu.make_async_copy(self.out_ref.at[slot], window, self.sem_out.at[slot])

    def before(self, step):
        @pl.when(step == 0)
        def _():
            self._for_block(step, lambda *a: self._fetch(*a).start())
        self._for_block(step, lambda *a: self._fetch(*a).wait())
        self._for_block(step + 1, lambda *a: self._fetch(*a).start())
        self._for_block(step - 2, lambda *a: self._store(*a).wait())

    def cast(self, step):
        slot = lax.rem(step, 2)
        if self.tile_cols is None:
            self.out_ref[slot] = self.stage_ref[slot].astype(BF16)
        else:
            for t in range(self.out_ref.shape[1]):
                cols = slice(t * self.tile_cols, (t + 1) * self.tile_cols)
                self.out_ref[slot, t] = self.stage_ref[slot, :, cols].astype(BF16)

    def after(self, step):
        self._for_block(step, lambda *a: self._store(*a).start())


def _cast_scratch(rows, cols, tile_cols=None):
    out_shape = (2, rows, cols) if tile_cols is None else (2, cols // tile_cols, rows, tile_cols)
    return [pltpu.VMEM((2, rows, cols), F32), pltpu.VMEM(out_shape, BF16),
            pltpu.SemaphoreType.DMA((2,)), pltpu.SemaphoreType.DMA((2,))]


IN_PROJ_CAST_ROWS = 128
GLA_CAST_ROWS = 256


def _in_proj_kernel(h_ref, wa_ref, wb_ref, *refs, side_jobs):
    j = pl.program_id(0)
    if side_jobs:
        (wg_ref, wu_ref, c_ref, wm_ref, bm_ref, o_ref, wgu_out, mod_ref,
         w_ref, stage_ref, cast_ref, sem_in, sem_out, sb_ref) = refs
        n = D_MODEL // IN_PROJ_CAST_ROWS
        job = _CastJob(((wg_ref, wgu_out, n, 0), (wu_ref, wgu_out, n, FFN_BH)),
                       IN_PROJ_CAST_ROWS, stage_ref, cast_ref, sem_in, sem_out, tile_cols=FFN_BH)
        step = j * pl.num_programs(1) + pl.program_id(1)
        job.before(step)

        @pl.when(step == 0)
        def _():
            _mod_prepare(c_ref, sb_ref)
    else:
        o_ref, w_ref = refs

    @pl.when(pl.program_id(1) == 0)
    def _():
        @pl.when(j < P_SHIFT_FROM)
        def _():
            for r0 in range(0, P_BN, W_ROWS):
                w_ref[r0:r0 + W_ROWS, :] = wa_ref[r0:r0 + W_ROWS, :].astype(BF16)

        @pl.when(j >= P_SHIFT_FROM)
        def _():
            keep = P_BN - GATE_COLS
            for r0 in range(0, keep, W_ROWS):
                r1 = min(r0 + W_ROWS, keep)
                w_ref[r0:r1, :] = wa_ref[GATE_COLS + r0:GATE_COLS + r1, :].astype(BF16)
            w_ref[keep:, :] = wb_ref[...].astype(BF16)

    o_ref[...] = lax.dot_general(h_ref[...], w_ref[...], _NT_DIMS,
                                 preferred_element_type=F32).astype(BF16)
    if side_jobs:
        job.cast(step)
        _mod_matvec(wm_ref, bm_ref, sb_ref, mod_ref, unrolled=True)
        job.after(step)


def _in_proj_call(h, w_in_t, side=None, *, bm, n_tiles):
    m = h.shape[0]
    ni = m // bm
    vmem = (2 * bm * D_MODEL * 2 + 2 * (P_BN + GATE_COLS) * D_MODEL * 4 + P_BN * D_MODEL * 2
            + 2 * bm * P_BN * 2 + bm * P_BN * 4 + 4 * W_ROWS * D_MODEL * 4 + SPILL_BYTES)
    in_specs = [
        pl.BlockSpec((bm, D_MODEL), lambda j, i: (i, 0)),
        pl.BlockSpec((P_BN, D_MODEL), lambda j, i: (j, 0)),
        pl.BlockSpec((GATE_COLS, D_MODEL), lambda j, i: ((j + 1) * (P_BN // GATE_COLS), 0)),
    ]
    out_specs = [pl.BlockSpec((bm, P_BN), lambda j, i: (i, j))]
    out_shape = [jax.ShapeDtypeStruct((m, n_tiles * P_BN), BF16)]
    scratch = [pltpu.VMEM((P_BN, D_MODEL), BF16)]
    operands = [h, w_in_t, w_in_t]
    if side is not None:
        w_gate, w_up, c_t, w_mod, b_mod, first_col = side
        n_steps = n_tiles * ni
        assert n_steps >= 2 * D_MODEL // IN_PROJ_CAST_ROWS + 2
        any_spec = pl.BlockSpec(memory_space=pl.ANY)
        n_mod = w_mod.shape[1] - first_col
        n_blocks = n_mod // LATE_MOD_BN
        assert n_mod % LATE_MOD_BN == 0 and first_col % LATE_MOD_BN == 0 and n_blocks <= n_steps
        mod_block = lambda j, i: jnp.minimum(j * ni + i, n_blocks - 1)
        in_specs += [
            any_spec, any_spec,
            pl.BlockSpec((D_MODEL, SUBLANES), lambda j, i: (0, 0)),
            pl.BlockSpec((D_MODEL, LATE_MOD_BN),
                         lambda j, i: (0, first_col // LATE_MOD_BN + mod_block(j, i))),
            pl.BlockSpec((1, LATE_MOD_BN),
                         lambda j, i: (0, first_col // LATE_MOD_BN + mod_block(j, i))),
        ]
        out_specs += [any_spec,
                      pl.BlockSpec((SUBLANES, LATE_MOD_BN), lambda j, i: (0, mod_block(j, i)))]
        assert w_gate.shape == w_up.shape == (D_MODEL, FFN_HIDDEN)
        out_shape += [jax.ShapeDtypeStruct((FFN_HIDDEN // FFN_BH, D_MODEL, 2 * FFN_BH), BF16),
                      jax.ShapeDtypeStruct((SUBLANES, n_mod), F32)]
        scratch += _cast_scratch(IN_PROJ_CAST_ROWS, FFN_HIDDEN, tile_cols=FFN_BH)
        scratch += [pltpu.VMEM((2, D_MODEL, LANES), F32)]
        operands += [w_gate, w_up, c_t, w_mod, b_mod]
        vmem += (IN_PROJ_CAST_ROWS * FFN_HIDDEN * (2 * 4 + 2 * 2)
                 + 2 * D_MODEL * LATE_MOD_BN * 4 + 2 * D_MODEL * LANES * 4)
    return pl.pallas_call(
        functools.partial(_in_proj_kernel, side_jobs=side is not None),
        grid=(n_tiles, ni),
        in_specs=in_specs,
        out_specs=out_specs,
        out_shape=out_shape,
        scratch_shapes=scratch,
        compiler_params=_params(2, vmem),
        name="in_proj",
    )(*operands)


def _gla_log_decay_split(l_ref, wg_ref, bg_ref):
    z = jnp.dot(l_ref[...], wg_ref[...], preferred_element_type=F32) + bg_ref[...]
    log_a = (jnp.minimum(z, 0.0) - jnp.log(1.0 + jnp.exp(-jnp.abs(z)))) * (1.0 / GATE_NORM)
    hi = log_a.astype(BF16)
    lo = (log_a - hi.astype(F32)).astype(BF16)
    return hi, lo


def _gla_chunk_cumsum(pieces, *, reverse, tb):
    r = lax.broadcasted_iota(jnp.int32, (tb, tb), 0)
    cc = lax.broadcasted_iota(jnp.int32, (tb, tb), 1)
    same_chunk = (r // CHUNK) == (cc // CHUNK)
    tri = jnp.where(same_chunk & ((cc >= r) if reverse else (cc <= r)), 1.0, 0.0).astype(BF16)
    hi, lo = pieces
    return (jnp.dot(tri, hi, preferred_element_type=F32)
            + jnp.dot(tri, lo, preferred_element_type=F32))


def _gla_kernel(qf, kf, vf, lf, lf_next, qb, kb, vb, lb, lb_next, wgf, bgf, wgb, bgb, s0f, s0b,
                *refs, tb, cast):
    i = pl.program_id(0)
    if cast:
        (wd_ref, of, ob, sf, sb, wd_out, s_ref, cum_ref,
         stage_ref, cast_ref, sem_in, sem_out) = refs
        job = _CastJob(((wd_ref, wd_out, FFN_HIDDEN // GLA_CAST_ROWS, 0),), GLA_CAST_ROWS,
                       stage_ref, cast_ref, sem_in, sem_out)
        job.before(i)
    else:
        of, ob, sf, sb, s_ref, cum_ref = refs

    @pl.when(i == 0)
    def _():
        s_ref[0] = s0f[...]
        s_ref[1] = s0b[...]
        cum_ref[0] = _gla_chunk_cumsum(_gla_log_decay_split(lf, wgf, bgf), reverse=False, tb=tb)
        cum_ref[1] = _gla_chunk_cumsum(_gla_log_decay_split(lb, wgb, bgb), reverse=True, tb=tb)

    dirs = ((qf, kf, vf, of, False), (qb, kb, vb, ob, True))
    ii = lax.broadcasted_iota(jnp.int32, (CHUNK, GLA_DK), 0)
    jj = lax.broadcasted_iota(jnp.int32, (CHUNK, GLA_DK), 1)
    nc = tb // CHUNK

    def start(step):
        items = []
        for d, (q_ref, k_ref, v_ref, o_ref, reverse) in enumerate(dirs):
            r0 = (nc - 1 - step if reverse else step) * CHUNK
            cum_c = cum_ref[d, r0:r0 + CHUNK, :]
            last = cum_c[0:1, :] if reverse else cum_c[CHUNK - 1:CHUNK, :]
            q_c = q_ref[r0:r0 + CHUNK, :].astype(F32)
            k_c = k_ref[r0:r0 + CHUNK, :].astype(F32)
            q_e = ((q_c * (GLA_DK ** -0.5)) * jnp.exp(cum_c)).astype(BF16)
            k_e = (k_c * jnp.exp(-cum_c)).astype(BF16)
            k_d = (k_c * jnp.exp(last - cum_c)).astype(BF16)
            decay = jnp.exp(last)
            mask = ((jj >= ii) & (jj < CHUNK)) if reverse else (jj <= ii)
            for h in range(GLA_HEADS):
                ks = slice(h * GLA_DK, (h + 1) * GLA_DK)
                vs = slice(h * GLA_DV, (h + 1) * GLA_DV)
                items.append(dict(
                    d=d, h=h, r0=r0, vs=vs, o_ref=o_ref, mask=mask,
                    q_e=q_e[:, ks], k_e=k_e[:, ks], k_d=k_d[:, ks], decay=decay[:, ks],
                    v=v_ref[r0:r0 + CHUNK, vs]))
        for it in items:
            k_pad = jnp.concatenate([it["k_e"], jnp.zeros((GLA_DK - CHUNK, GLA_DK), BF16)], axis=0)
            it["scores"] = lax.dot_general(it["q_e"], k_pad, _NT_DIMS,
                                           preferred_element_type=F32)
        return items

    items = start(0)
    next_cum = None
    for step in range(nc):
        following = start(step + 1) if step + 1 < nc else None
        if step == 1:
            next_f = _gla_log_decay_split(lf_next, wgf, bgf)
            next_b = _gla_log_decay_split(lb_next, wgb, bgb)
        if step == nc - 1:
            next_cum = (_gla_chunk_cumsum(next_f, reverse=False, tb=tb),
                        _gla_chunk_cumsum(next_b, reverse=True, tb=tb))
        for it in items:
            scores = jnp.where(it["mask"], it["scores"], 0.0).astype(BF16)
            s_old = s_ref[it["d"], it["h"]]
            lhs = jnp.concatenate([scores, it["q_e"]], axis=1)
            rhs = jnp.concatenate(
                [it["v"], jnp.zeros((GLA_DK - CHUNK, GLA_DV), BF16), s_old.astype(BF16)], axis=0)
            it["o_ref"][it["r0"]:it["r0"] + CHUNK, it["vs"]] = jnp.dot(
                lhs, rhs, preferred_element_type=F32).astype(BF16)
            ds = lax.dot_general(it["k_d"], it["v"], (((0,), (0,)), ((), ())),
                                 preferred_element_type=F32)
            dcol = jnp.transpose(jnp.broadcast_to(it["decay"], (SUBLANES, GLA_DK)))[:, 0:1]
            s_ref[it["d"], it["h"]] = dcol * s_old + ds
        items = following
    cum_ref[0] = next_cum[0]
    cum_ref[1] = next_cum[1]
    if cast:
        job.cast(i)
        job.after(i)

    @pl.when(i == pl.num_programs(0) - 1)
    def _():
        sf[...] = s_ref[0]
        sb[...] = s_ref[1]


def _gla_call(p, l, wgf, bgf, wgb, bgb, s0f, s0b, cast_weight=None, *, tb):
    n = p.shape[0]
    nb = n // tb
    fwd = lambda i: i
    bwd = lambda i: nb - 1 - i
    fwd_next = lambda i: jnp.minimum(i + 1, nb - 1)
    bwd_next = lambda i: jnp.maximum(nb - 2 - i, 0)

    def specs(pos, pos_next):
        return [
            pl.BlockSpec((tb, GLA_KW), lambda i: (pos(i), 0)),
            pl.BlockSpec((tb, GLA_KW), lambda i: (pos(i), 1)),
            pl.BlockSpec((tb, GLA_VW), lambda i: (pos(i), 1)),
            pl.BlockSpec((tb, LANES), lambda i: (pos(i), 0)),
            pl.BlockSpec((tb, LANES), lambda i: (pos_next(i), 0)),
        ]

    const2 = lambda i: (0, 0)
    const3 = lambda i: (0, 0, 0)
    state_shape = (GLA_HEADS, GLA_DK, GLA_DV)
    state_spec = pl.BlockSpec(state_shape, const3)
    cast = cast_weight is not None
    kern = functools.partial(_gla_kernel, tb=tb, cast=cast)
    vmem = (2 * 2 * tb * (2 * GLA_KW + GLA_VW + LANES) * 2 + 2 * 2 * tb * GLA_VW * 2
            + 10 * GLA_HEADS * GLA_DK * GLA_DV * 4 + 24 * tb * GLA_KW * 4 + 2 * SPILL_BYTES)
    in_specs = specs(fwd, fwd_next) + specs(bwd, bwd_next) + [
        pl.BlockSpec((LANES, GLA_KW), const2), pl.BlockSpec((1, GLA_KW), const2),
        pl.BlockSpec((LANES, GLA_KW), const2), pl.BlockSpec((1, GLA_KW), const2),
        state_spec, state_spec,
    ]
    out_specs = [
        pl.BlockSpec((tb, GLA_VW), lambda i: (i, 0)),
        pl.BlockSpec((tb, GLA_VW), lambda i: (nb - 1 - i, 0)),
        state_spec, state_spec,
    ]
    out_shape = [
        jax.ShapeDtypeStruct((n, GLA_VW), BF16), jax.ShapeDtypeStruct((n, GLA_VW), BF16),
        jax.ShapeDtypeStruct(state_shape, F32), jax.ShapeDtypeStruct(state_shape, F32),
    ]
    scratch = [pltpu.VMEM((2,) + state_shape, F32), pltpu.VMEM((2, tb, GLA_KW), F32)]
    operands = [p, p, p, l, l, p, p, p, l, l, wgf, bgf, wgb, bgb, s0f, s0b]
    if cast:
        assert cast_weight.shape == (FFN_HIDDEN, D_MODEL)
        assert nb >= FFN_HIDDEN // GLA_CAST_ROWS + 2
        any_spec = pl.BlockSpec(memory_space=pl.ANY)
        in_specs.append(any_spec)
        out_specs.append(any_spec)
        out_shape.append(jax.ShapeDtypeStruct(cast_weight.shape, BF16))
        scratch += _cast_scratch(GLA_CAST_ROWS, D_MODEL)
        operands.append(cast_weight)
        vmem += GLA_CAST_ROWS * D_MODEL * (2 * 4 + 2 * 2)
    return pl.pallas_call(
        kern,
        grid=(nb,),
        in_specs=in_specs,
        out_specs=out_specs,
        out_shape=out_shape,
        scratch_shapes=scratch,
        compiler_params=_params(1, vmem),
        name="gla",
    )(*operands)


def _mix_kernel(of_ref, ob_ref, g_ref, cb_ref, cc_ref, cx_ref, x_ref, gt_ref, gg_ref, cw_ref,
                wo_ref, o_ref, m_ref, w_ref, *, bm):
    @pl.when(pl.program_id(0) == 0)
    def _():
        for r0 in range(0, D_MODEL, W_ROWS):
            w_ref[r0:r0 + W_ROWS, :] = wo_ref[r0:r0 + W_ROWS, :].astype(BF16)

    u = cc_ref[...].astype(F32) * cx_ref[...].astype(F32)
    col = lax.broadcasted_iota(jnp.int32, u.shape, 0) % GRID_W
    u_prev = jnp.where(col == 0, 0.0, pltpu.roll(u, 1, 0))
    u_next = jnp.where(col == GRID_W - 1, 0.0, pltpu.roll(u, bm - 1, 0))
    conv = cw_ref[0:1, :] * u_prev + cw_ref[1:2, :] * u + cw_ref[2:3, :] * u_next
    m_ref[:, GLA_VW:] = (cb_ref[...].astype(F32) * conv).astype(BF16)
    mixed = jnp.dot(m_ref[:, GLA_VW:], w_ref[GLA_VW:, :], preferred_element_type=F32)

    gg = gg_ref[...]
    for h in range(GLA_HEADS):
        vs = slice(h * GLA_DV, (h + 1) * GLA_DV)
        o = of_ref[:, vs].astype(F32) + ob_ref[:, vs].astype(F32)
        o32 = o * lax.rsqrt(jnp.mean(o * o, axis=-1, keepdims=True) + EPS)
        m_ref[:, vs] = ((o32 * gg) * _silu(g_ref[:, vs].astype(F32))).astype(BF16)
        mixed = mixed + jnp.dot(m_ref[:, vs], w_ref[vs, :], preferred_element_type=F32)
    o_ref[...] = x_ref[...] + gt_ref[0:1, :] * mixed


def _mix_call(o_f, o_b, p, x, mod, gla_g, conv_w, w_out, *, bm):
    n = x.shape[0]
    kern = functools.partial(_mix_kernel, bm=bm)
    pcol = lambda k: pl.BlockSpec((bm, GLA_VW), lambda i, k=k: (i, k))
    const2 = lambda i: (0, 0)
    vmem = (2 * 6 * bm * GLA_VW * 2 + 2 * 2 * bm * D_MODEL * 4
            + D_MODEL * D_MODEL * (4 + 2) + bm * D_MODEL * 2 + 8 * bm * GLA_VW * 4
            + SPILL_BYTES)
    return pl.pallas_call(
        kern,
        grid=(n // bm,),
        in_specs=[
            pl.BlockSpec((bm, GLA_VW), lambda i: (i, 0)),
            pl.BlockSpec((bm, GLA_VW), lambda i: (i, 0)),
            pcol(2), pcol(3), pcol(4), pcol(5),
            pl.BlockSpec((bm, D_MODEL), lambda i: (i, 0)),
            pl.BlockSpec((SUBLANES, D_MODEL), lambda i: (0, 0)),
            pl.BlockSpec((1, GLA_DV), const2),
            pl.BlockSpec((3, CONV_W), const2),
            pl.BlockSpec((D_MODEL, D_MODEL), const2, pipeline_mode=pl.Buffered(1)),
        ],
        out_specs=pl.BlockSpec((bm, D_MODEL), lambda i: (i, 0)),
        out_shape=jax.ShapeDtypeStruct((n, D_MODEL), F32),
        scratch_shapes=[pltpu.VMEM((bm, D_MODEL), BF16), pltpu.VMEM((D_MODEL, D_MODEL), BF16)],
        compiler_params=_params(1, vmem),
        name="mix",
    )(o_f, o_b, p, p, p, p, x, mod, gla_g, conv_w, w_out)


def _ffn_kernel(x_ref, g_ref, sh_ref, sc_ref, gt_ref, wgu_hbm, wd_hbm, fg_ref,
                o_ref, h_ref, wgu_buf, wd_buf, sem, *, bm):
    n_pairs = FFN_HIDDEN // (2 * FFN_BH)

    def tile_copies(t, slot):
        rows = pl.ds(pl.multiple_of(t * FFN_BH, FFN_BH), FFN_BH)
        return (pltpu.make_async_copy(wgu_hbm.at[t], wgu_buf.at[slot], sem.at[slot, 0]),
                pltpu.make_async_copy(wd_hbm.at[rows, :], wd_buf.at[slot], sem.at[slot, 1]))

    def start_tile(t, slot):
        for cp in tile_copies(t, slot):
            cp.start()

    def wait_tile(t, slot):
        for cp in tile_copies(t, slot):
            cp.wait()

    def accumulate(slot):
        gate_up = jnp.dot(h_ref[...], wgu_buf[slot], preferred_element_type=F32)
        act = (_silu(gate_up[:, :FFN_BH]) * gate_up[:, FFN_BH:]).astype(BF16)
        part = jnp.dot(act, wd_buf[slot], preferred_element_type=F32)
        o_ref[...] += gt_ref[0:1, :] * part

    def tile_pair(p, prefetch_next):
        t = 2 * p
        wait_tile(t, 0)
        start_tile(t + 1, 1)
        accumulate(0)
        wait_tile(t + 1, 1)
        if prefetch_next:
            start_tile(t + 2, 0)
        accumulate(1)

    start_tile(0, 0)
    _norm_mod_rows(x_ref, g_ref[...], 1.0 + sc_ref[0:1, :], sh_ref[0:1, :], h_ref, bm,
                   copy_ref=o_ref)

    def loop_body(p, carry):
        tile_pair(p, True)
        return carry

    lax.fori_loop(0, n_pairs - 1, loop_body, 0)
    tile_pair(n_pairs - 1, False)

    fg = fg_ref[...]

    def final_norm(rs):
        xn = o_ref[rs, :]
        ms = jnp.mean(xn * xn, axis=-1, keepdims=True)
        o_ref[rs, :] = (xn * lax.rsqrt(ms + EPS)) * fg
    _for_row_chunks(bm, final_norm)


def _ffn_call(x, g, mod, wgu, wd, fg, *, bm):
    n = x.shape[0]
    kern = functools.partial(_ffn_kernel, bm=bm)
    modspec = lambda k: pl.BlockSpec((SUBLANES, D_MODEL), lambda i, k=k: (0, k))
    any_spec = pl.BlockSpec(memory_space=pl.ANY)
    vmem = (4 * bm * D_MODEL * 4 + bm * D_MODEL * 2 + 2 * 3 * D_MODEL * FFN_BH * 2
            + 4 * bm * FFN_BH * 4 + bm * D_MODEL * 4 + SPILL_BYTES)
    return pl.pallas_call(
        kern,
        grid=(n // bm,),
        in_specs=[
            pl.BlockSpec((bm, D_MODEL), lambda i: (i, 0)),
            pl.BlockSpec((1, D_MODEL), lambda i: (0, 0)),
            modspec(1), modspec(2), modspec(3),
            any_spec, any_spec,
            pl.BlockSpec((1, D_MODEL), lambda i: (0, 0)),
        ],
        out_specs=pl.BlockSpec((bm, D_MODEL), lambda i: (i, 0)),
        out_shape=jax.ShapeDtypeStruct((n, D_MODEL), F32),
        scratch_shapes=[
            pltpu.VMEM((bm, D_MODEL), BF16),
            pltpu.VMEM((2, D_MODEL, 2 * FFN_BH), BF16),
            pltpu.VMEM((2, FFN_BH, D_MODEL), BF16),
            pltpu.SemaphoreType.DMA((2, 2)),
        ],
        compiler_params=_params(1, vmem),
        name="ffn",
    )(x, g, mod, mod, mod, wgu, wd, fg)


def kernel(x, c, ctx, c_ctx, w_mod, b_mod, norm1_g, norm2_g, w_in, w_gate_f, b_gate_f,
           w_gate_b, b_gate_b, gla_norm_g, conv_w, w_out, w_ffn_gate, w_ffn_up, w_ffn_down,
           final_g):
    assert x.shape[0] == 1 and w_mod.shape[0] == 1, "batch 1, depth 1 only"
    seq, ctx_len = x.shape[1], ctx.shape[1]
    x2, ctx2 = x[0], ctx[0]

    c_t = jnp.concatenate(
        [c[0][:, None], c_ctx[:, None], jnp.zeros((D_MODEL, SUBLANES - 2), F32)], axis=1)
    mod = _mod_call(c_t, w_mod[0], b_mod, EARLY_MOD_COLS)

    pad_f = jnp.zeros((LANES, GLA_KW), F32).at[:GATE_RANK].set(w_gate_f[0]).astype(BF16)
    pad_b = jnp.zeros((LANES, GLA_KW), F32).at[GATE_RANK:2 * GATE_RANK].set(w_gate_b[0]).astype(BF16)
    g1 = norm1_g[0][None]
    w_in_t = jnp.transpose(w_in[0])

    h_c, l_c = _prenorm_call(ctx2, g1, mod, w_in_t, row=1, bm=ctx_len)
    p_c, = _in_proj_call(h_c, w_in_t, bm=ctx_len, n_tiles=2)
    zero_state = jnp.zeros((GLA_HEADS, GLA_DK, GLA_DV), F32)
    _, _, s_f, s_b = _gla_call(p_c, l_c, pad_f, b_gate_f, pad_b, b_gate_b, zero_state,
                               zero_state, tb=ctx_len)

    h_x, l_x = _prenorm_call(x2, g1, mod, w_in_t, row=0, bm=2048)
    p_x, wgu_b, mod_late = _in_proj_call(
        h_x, w_in_t, (w_ffn_gate[0], w_ffn_up[0], c_t, w_mod[0], b_mod, EARLY_MOD_COLS),
        bm=1024, n_tiles=P_COLS // P_BN)
    o_f, o_b, _, _, wd_b = _gla_call(p_x, l_x, pad_f, b_gate_f, pad_b, b_gate_b, s_f, s_b,
                                     w_ffn_down[0], tb=256)
    x_mid = _mix_call(o_f, o_b, p_x, x2, mod_late, gla_norm_g, conv_w[0], w_out[0], bm=256)
    out = _ffn_call(x_mid, norm2_g[0][None], mod_late, wgu_b, wd_b, final_g[None],
                    bm=1024)
    return out[None]
```

```python
import functools

import jax
import jax.numpy as jnp
from jax import lax
from jax.experimental import pallas as pl
from jax.experimental.pallas import tpu as pltpu

F32 = jnp.float32
BF16 = jnp.bfloat16

D_MODEL = 2048
GRID_W = 64
GLA_HEADS = 4
GLA_DK = 128
GLA_DV = 256
GLA_KW = GLA_HEADS * GLA_DK
GLA_VW = GLA_HEADS * GLA_DV
GATE_RANK = 16
GATE_NORM = 16.0
CHUNK = 64
CONV_W = D_MODEL - GLA_VW
FFN_HIDDEN = 5632
EPS = 1e-6

LANES = 128
SUBLANES = 8
MIB = 1024 * 1024
VMEM_BYTES = 64 * MIB
VMEM_RESERVED = 6 * MIB
SPILL_BYTES = 4 * MIB

GATE_COL = 2 * GLA_KW + 2 * GLA_VW
GATE_COLS = 2 * GATE_RANK
P_COLS = GATE_COL + 3 * CONV_W
P_BN = 1024
P_SHIFT_FROM = GATE_COL // P_BN

_NT_DIMS = (((1,), (1,)), ((), ()))

MOD_BN = 2048
EARLY_MOD_COLS = 2 * D_MODEL
LATE_MOD_BN = 256
FFN_BH = 512
NORM_ROWS = 16


def _params(grid_rank, vmem_bytes):
    return pltpu.CompilerParams(
        dimension_semantics=("arbitrary",) * grid_rank,
        vmem_limit_bytes=min(int(vmem_bytes), VMEM_BYTES - VMEM_RESERVED),
    )


def _silu(v):
    return v * jax.nn.sigmoid(v)


def _for_row_chunks(rows, body):
    for r0 in range(0, rows, NORM_ROWS):
        body(slice(r0, r0 + NORM_ROWS))


def _norm_mod_rows(x_ref, g, mul, sh, dst_ref, rows, copy_ref=None):
    def body(rs):
        xs = x_ref[rs, :]
        ms = jnp.mean(xs * xs, axis=-1, keepdims=True)
        y = xs * lax.rsqrt(ms + EPS)
        dst_ref[rs, :] = ((y * g) * mul + sh).astype(BF16)
        if copy_ref is not None:
            copy_ref[rs, :] = xs
    _for_row_chunks(rows, body)


def _mod_prepare(c_ref, sb_ref):
    s = _silu(c_ref[...])
    for r in range(2):
        sb_ref[r] = jnp.broadcast_to(s[:, r:r + 1], (D_MODEL, LANES))


def _mod_matvec(w_ref, b_ref, sb_ref, o_ref, *, unrolled):
    nt = w_ref.shape[1] // LANES

    def body(kk, accs):
        r0 = kk * SUBLANES if unrolled else pl.multiple_of(kk * SUBLANES, SUBLANES)
        s0 = sb_ref[0, pl.ds(r0, SUBLANES), :]
        s1 = sb_ref[1, pl.ds(r0, SUBLANES), :]
        new = []
        for t in range(nt):
            w8 = w_ref[pl.ds(r0, SUBLANES), t * LANES:(t + 1) * LANES]
            new.append(accs[2 * t] + w8 * s0)
            new.append(accs[2 * t + 1] + w8 * s1)
        return tuple(new)

    accs = (jnp.zeros((SUBLANES, LANES), F32),) * (2 * nt)
    if unrolled:
        for kk in range(D_MODEL // SUBLANES):
            accs = body(kk, accs)
    else:
        accs = lax.fori_loop(0, D_MODEL // SUBLANES, body, accs, unroll=8)
    o_ref[...] = jnp.zeros(o_ref.shape, F32)
    for t in range(nt):
        b = b_ref[:, t * LANES:(t + 1) * LANES]
        for r in range(2):
            o_ref[r:r + 1, t * LANES:(t + 1) * LANES] = (
                jnp.sum(accs[2 * t + r], axis=0, keepdims=True) + b)


def _mod_kernel(c_ref, w_ref, b_ref, o_ref, sb_ref):
    @pl.when(pl.program_id(0) == 0)
    def _():
        _mod_prepare(c_ref, sb_ref)

    _mod_matvec(w_ref, b_ref, sb_ref, o_ref, unrolled=False)


def _mod_call(c_t, w_mod, b_mod, n):
    return pl.pallas_call(
        _mod_kernel,
        grid=(n // MOD_BN,),
        in_specs=[
            pl.BlockSpec((D_MODEL, SUBLANES), lambda j: (0, 0)),
            pl.BlockSpec((D_MODEL, MOD_BN), lambda j: (0, j)),
            pl.BlockSpec((1, MOD_BN), lambda j: (0, j)),
        ],
        out_specs=pl.BlockSpec((SUBLANES, MOD_BN), lambda j: (0, j)),
        out_shape=jax.ShapeDtypeStruct((SUBLANES, n), F32),
        scratch_shapes=[pltpu.VMEM((2, D_MODEL, LANES), F32)],
        compiler_params=_params(1, 2 * D_MODEL * MOD_BN * 4 + 2 * SPILL_BYTES),
        name="mod",
    )(c_t, w_mod, b_mod)


def _prenorm_kernel(x_ref, g_ref, sh_ref, sc_ref, wl_ref, h_ref, l_ref, *, row, bm):
    _norm_mod_rows(x_ref, g_ref[...], 1.0 + sc_ref[row:row + 1, :], sh_ref[row:row + 1, :],
                   h_ref, bm)
    l_ref[...] = lax.dot_general(h_ref[...], wl_ref[...].astype(BF16), _NT_DIMS,
                                 preferred_element_type=F32).astype(BF16)


def _prenorm_call(x, g, mod, w_in_t, *, row, bm):
    m = x.shape[0]
    kern = functools.partial(_prenorm_kernel, row=row, bm=bm)
    vmem = (2 * bm * D_MODEL * (4 + 2) + 4 * D_MODEL * LANES * 4 + 8 * NORM_ROWS * D_MODEL * 4
            + SPILL_BYTES)
    return pl.pallas_call(
        kern,
        grid=(m // bm,),
        in_specs=[
            pl.BlockSpec((bm, D_MODEL), lambda i: (i, 0)),
            pl.BlockSpec((1, D_MODEL), lambda i: (0, 0)),
            pl.BlockSpec((SUBLANES, D_MODEL), lambda i: (0, 0)),
            pl.BlockSpec((SUBLANES, D_MODEL), lambda i: (0, 1)),
            pl.BlockSpec((LANES, D_MODEL), lambda i: (GATE_COL // LANES, 0)),
        ],
        out_specs=[pl.BlockSpec((bm, D_MODEL), lambda i: (i, 0)),
                   pl.BlockSpec((bm, LANES), lambda i: (i, 0))],
        out_shape=[jax.ShapeDtypeStruct((m, D_MODEL), BF16),
                   jax.ShapeDtypeStruct((m, LANES), BF16)],
        compiler_params=_params(1, vmem),
        name="prenorm",
    )(x, g, mod, mod, w_in_t)


W_ROWS = 256


class _CastJob:
    def __init__(self, sources, rows, stage_ref, out_ref, sem_in, sem_out):
        self.sources, self.rows = sources, rows
        self.stage_ref, self.out_ref, self.sem_in, self.sem_out = stage_ref, out_ref, sem_in, sem_out
        self.n_blocks = sum(n for _, _, n in sources)

    def _for_block(self, c, fn):
        first = 0
        for src, dst, n in self.sources:
            @pl.when((c >= first) & (c < first + n))
            def _(src=src, dst=dst, first=first):
                fn(src, dst, pl.multiple_of((c - first) * self.rows, self.rows), lax.rem(c, 2))
            first += n

    def _fetch(self, src, dst, r, slot):
        return pltpu.make_async_copy(src.at[pl.ds(r, self.rows), :], self.stage_ref.at[slot],
                                     self.sem_in.at[slot])

    def _store(self, src, dst, r, slot):
        return pltpu.make_async_copy(self.out_ref.at[slot], dst.at[pl.ds(r, self.rows), :],
                                     self.sem_out.at[slot])

    def before(self, step):
        @pl.when(step == 0)
        def _():
            self._for_block(step, lambda *a: self._fetch(*a).start())
        self._for_block(step, lambda *a: self._fetch(*a).wait())
        self._for_block(step + 1, lambda *a: self._fetch(*a).start())
        self._for_block(step - 2, lambda *a: self._store(*a).wait())

    def cast(self, step):
        slot = lax.rem(step, 2)
        self.out_ref[slot] = self.stage_ref[slot].astype(BF16)

    def after(self, step):
        self._for_block(step, lambda *a: self._store(*a).start())


def _cast_scratch(rows, cols):
    return [pltpu.VMEM((2, rows, cols), F32), pltpu.VMEM((2, rows, cols), BF16),
            pltpu.SemaphoreType.DMA((2,)), pltpu.SemaphoreType.DMA((2,))]


IN_PROJ_CAST_ROWS = 128
GLA_CAST_ROWS = 256


def _in_proj_kernel(h_ref, wa_ref, wb_ref, *refs, side_jobs):
    j = pl.program_id(0)
    if side_jobs:
        (wg_ref, wu_ref, c_ref, wm_ref, bm_ref, o_ref, wg_out, wu_out, mod_ref,
         w_ref, stage_ref, cast_ref, sem_in, sem_out, sb_ref) = refs
        n = D_MODEL // IN_PROJ_CAST_ROWS
        job = _CastJob(((wg_ref, wg_out, n), (wu_ref, wu_out, n)), IN_PROJ_CAST_ROWS,
                       stage_ref, cast_ref, sem_in, sem_out)
        step = j * pl.num_programs(1) + pl.program_id(1)
        job.before(step)

        @pl.when(step == 0)
        def _():
            _mod_prepare(c_ref, sb_ref)
    else:
        o_ref, w_ref = refs

    @pl.when(pl.program_id(1) == 0)
    def _():
        @pl.when(j < P_SHIFT_FROM)
        def _():
            for r0 in range(0, P_BN, W_ROWS):
                w_ref[r0:r0 + W_ROWS, :] = wa_ref[r0:r0 + W_ROWS, :].astype(BF16)

        @pl.when(j >= P_SHIFT_FROM)
        def _():
            keep = P_BN - GATE_COLS
            for r0 in range(0, keep, W_ROWS):
                r1 = min(r0 + W_ROWS, keep)
                w_ref[r0:r1, :] = wa_ref[GATE_COLS + r0:GATE_COLS + r1, :].astype(BF16)
            w_ref[keep:, :] = wb_ref[...].astype(BF16)

    o_ref[...] = lax.dot_general(h_ref[...], w_ref[...], _NT_DIMS,
                                 preferred_element_type=F32).astype(BF16)
    if side_jobs:
        job.cast(step)
        _mod_matvec(wm_ref, bm_ref, sb_ref, mod_ref, unrolled=True)
        job.after(step)


def _in_proj_call(h, w_in_t, side=None, *, bm, n_tiles):
    m = h.shape[0]
    ni = m // bm
    vmem = (2 * bm * D_MODEL * 2 + 2 * (P_BN + GATE_COLS) * D_MODEL * 4 + P_BN * D_MODEL * 2
            + 2 * bm * P_BN * 2 + bm * P_BN * 4 + 4 * W_ROWS * D_MODEL * 4 + SPILL_BYTES)
    in_specs = [
        pl.BlockSpec((bm, D_MODEL), lambda j, i: (i, 0)),
        pl.BlockSpec((P_BN, D_MODEL), lambda j, i: (j, 0)),
        pl.BlockSpec((GATE_COLS, D_MODEL), lambda j, i: ((j + 1) * (P_BN // GATE_COLS), 0)),
    ]
    out_specs = [pl.BlockSpec((bm, P_BN), lambda j, i: (i, j))]
    out_shape = [jax.ShapeDtypeStruct((m, n_tiles * P_BN), BF16)]
    scratch = [pltpu.VMEM((P_BN, D_MODEL), BF16)]
    operands = [h, w_in_t, w_in_t]
    if side is not None:
        w_gate, w_up, c_t, w_mod, b_mod, first_col = side
        n_steps = n_tiles * ni
        assert n_steps >= 2 * D_MODEL // IN_PROJ_CAST_ROWS + 2
        any_spec = pl.BlockSpec(memory_space=pl.ANY)
        n_mod = w_mod.shape[1] - first_col
        n_blocks = n_mod // LATE_MOD_BN
        assert n_mod % LATE_MOD_BN == 0 and first_col % LATE_MOD_BN == 0 and n_blocks <= n_steps
        mod_block = lambda j, i: jnp.minimum(j * ni + i, n_blocks - 1)
        in_specs += [
            any_spec, any_spec,
            pl.BlockSpec((D_MODEL, SUBLANES), lambda j, i: (0, 0)),
            pl.BlockSpec((D_MODEL, LATE_MOD_BN),
                         lambda j, i: (0, first_col // LATE_MOD_BN + mod_block(j, i))),
            pl.BlockSpec((1, LATE_MOD_BN),
                         lambda j, i: (0, first_col // LATE_MOD_BN + mod_block(j, i))),
        ]
        out_specs += [any_spec, any_spec,
                      pl.BlockSpec((SUBLANES, LATE_MOD_BN), lambda j, i: (0, mod_block(j, i)))]
        out_shape += [jax.ShapeDtypeStruct(w_gate.shape, BF16),
                      jax.ShapeDtypeStruct(w_up.shape, BF16),
                      jax.ShapeDtypeStruct((SUBLANES, n_mod), F32)]
        scratch += _cast_scratch(IN_PROJ_CAST_ROWS, FFN_HIDDEN)
        scratch += [pltpu.VMEM((2, D_MODEL, LANES), F32)]
        operands += [w_gate, w_up, c_t, w_mod, b_mod]
        vmem += (IN_PROJ_CAST_ROWS * FFN_HIDDEN * (2 * 4 + 2 * 2)
                 + 2 * D_MODEL * LATE_MOD_BN * 4 + 2 * D_MODEL * LANES * 4)
    return pl.pallas_call(
        functools.partial(_in_proj_kernel, side_jobs=side is not None),
        grid=(n_tiles, ni),
        in_specs=in_specs,
        out_specs=out_specs,
        out_shape=out_shape,
        scratch_shapes=scratch,
        compiler_params=_params(2, vmem),
        name="in_proj",
    )(*operands)


def _gla_log_decay_split(l_ref, wg_ref, bg_ref):
    z = jnp.dot(l_ref[...], wg_ref[...], preferred_element_type=F32) + bg_ref[...]
    log_a = (jnp.minimum(z, 0.0) - jnp.log(1.0 + jnp.exp(-jnp.abs(z)))) * (1.0 / GATE_NORM)
    hi = log_a.astype(BF16)
    lo = (log_a - hi.astype(F32)).astype(BF16)
    return hi, lo


def _gla_chunk_cumsum(pieces, *, reverse, tb):
    r = lax.broadcasted_iota(jnp.int32, (tb, tb), 0)
    cc = lax.broadcasted_iota(jnp.int32, (tb, tb), 1)
    same_chunk = (r // CHUNK) == (cc // CHUNK)
    tri = jnp.where(same_chunk & ((cc >= r) if reverse else (cc <= r)), 1.0, 0.0).astype(BF16)
    hi, lo = pieces
    return (jnp.dot(tri, hi, preferred_element_type=F32)
            + jnp.dot(tri, lo, preferred_element_type=F32))


def _gla_kernel(qkv_f, lf, lf_next, qkv_b, lb, lb_next, wgf, bgf, wgb, bgb, s0f, s0b,
                *refs, tb, cast):
    qf, kf, vf = (qkv_f.at[:, :GLA_KW], qkv_f.at[:, GLA_KW:2 * GLA_KW], qkv_f.at[:, 2 * GLA_KW:])
    qb, kb, vb = (qkv_b.at[:, :GLA_KW], qkv_b.at[:, GLA_KW:2 * GLA_KW], qkv_b.at[:, 2 * GLA_KW:])
    i = pl.program_id(0)
    if cast:
        (wd_ref, of, ob, sf, sb, wd_out, s_ref, cum_ref,
         stage_ref, cast_ref, sem_in, sem_out) = refs
        job = _CastJob(((wd_ref, wd_out, FFN_HIDDEN // GLA_CAST_ROWS),), GLA_CAST_ROWS,
                       stage_ref, cast_ref, sem_in, sem_out)
        job.before(i)
    else:
        of, ob, sf, sb, s_ref, cum_ref = refs

    @pl.when(i == 0)
    def _():
        s_ref[0] = s0f[...]
        s_ref[1] = s0b[...]
        cum_ref[0] = _gla_chunk_cumsum(_gla_log_decay_split(lf, wgf, bgf), reverse=False, tb=tb)
        cum_ref[1] = _gla_chunk_cumsum(_gla_log_decay_split(lb, wgb, bgb), reverse=True, tb=tb)

    dirs = ((qf, kf, vf, of, False), (qb, kb, vb, ob, True))
    ii = lax.broadcasted_iota(jnp.int32, (CHUNK, GLA_DK), 0)
    jj = lax.broadcasted_iota(jnp.int32, (CHUNK, GLA_DK), 1)
    nc = tb // CHUNK

    def start(step):
        items = []
        for d, (q_ref, k_ref, v_ref, o_ref, reverse) in enumerate(dirs):
            r0 = (nc - 1 - step if reverse else step) * CHUNK
            cum_c = cum_ref[d, r0:r0 + CHUNK, :]
            last = cum_c[0:1, :] if reverse else cum_c[CHUNK - 1:CHUNK, :]
            q_c = q_ref[r0:r0 + CHUNK, :].astype(F32)
            k_c = k_ref[r0:r0 + CHUNK, :].astype(F32)
            q_e = ((q_c * (GLA_DK ** -0.5)) * jnp.exp(cum_c)).astype(BF16)
            k_e = (k_c * jnp.exp(-cum_c)).astype(BF16)
            k_d = (k_c * jnp.exp(last - cum_c)).astype(BF16)
            decay = jnp.exp(last)
            mask = ((jj >= ii) & (jj < CHUNK)) if reverse else (jj <= ii)
            for h in range(GLA_HEADS):
                ks = slice(h * GLA_DK, (h + 1) * GLA_DK)
                vs = slice(h * GLA_DV, (h + 1) * GLA_DV)
                items.append(dict(
                    d=d, h=h, r0=r0, vs=vs, o_ref=o_ref, mask=mask,
                    q_e=q_e[:, ks], k_e=k_e[:, ks], k_d=k_d[:, ks], decay=decay[:, ks],
                    v=v_ref[r0:r0 + CHUNK, vs]))
        for it in items:
            k_pad = jnp.concatenate([it["k_e"], jnp.zeros((GLA_DK - CHUNK, GLA_DK), BF16)], axis=0)
            it["scores"] = lax.dot_general(it["q_e"], k_pad, _NT_DIMS,
                                           preferred_element_type=F32)
        return items

    items = start(0)
    next_cum = None
    for step in range(nc):
        following = start(step + 1) if step + 1 < nc else None
        if step == 1:
            next_f = _gla_log_decay_split(lf_next, wgf, bgf)
            next_b = _gla_log_decay_split(lb_next, wgb, bgb)
        if step == nc - 1:
            next_cum = (_gla_chunk_cumsum(next_f, reverse=False, tb=tb),
                        _gla_chunk_cumsum(next_b, reverse=True, tb=tb))
        for it in items:
            scores = jnp.where(it["mask"], it["scores"], 0.0).astype(BF16)
            s_old = s_ref[it["d"], it["h"]]
            lhs = jnp.concatenate([scores, it["q_e"]], axis=1)
            rhs = jnp.concatenate(
                [it["v"], jnp.zeros((GLA_DK - CHUNK, GLA_DV), BF16), s_old.astype(BF16)], axis=0)
            it["o_ref"][it["r0"]:it["r0"] + CHUNK, it["vs"]] = jnp.dot(
                lhs, rhs, preferred_element_type=F32).astype(BF16)
            ds = lax.dot_general(it["k_d"], it["v"], (((0,), (0,)), ((), ())),
                                 preferred_element_type=F32)
            dcol = jnp.transpose(jnp.broadcast_to(it["decay"], (SUBLANES, GLA_DK)))[:, 0:1]
            s_ref[it["d"], it["h"]] = dcol * s_old + ds
        items = following
    cum_ref[0] = next_cum[0]
    cum_ref[1] = next_cum[1]
    if cast:
        job.cast(i)
        job.after(i)

    @pl.when(i == pl.num_programs(0) - 1)
    def _():
        sf[...] = s_ref[0]
        sb[...] = s_ref[1]


def _gla_call(p, l, wgf, bgf, wgb, bgb, s0f, s0b, cast_weight=None, *, tb):
    n = p.shape[0]
    nb = n // tb
    fwd = lambda i: i
    bwd = lambda i: nb - 1 - i
    fwd_next = lambda i: jnp.minimum(i + 1, nb - 1)
    bwd_next = lambda i: jnp.maximum(nb - 2 - i, 0)

    def specs(pos, pos_next):
        return [
            pl.BlockSpec((tb, 2 * GLA_KW + GLA_VW), lambda i: (pos(i), 0)),
            pl.BlockSpec((tb, LANES), lambda i: (pos(i), 0)),
            pl.BlockSpec((tb, LANES), lambda i: (pos_next(i), 0)),
        ]

    const2 = lambda i: (0, 0)
    const3 = lambda i: (0, 0, 0)
    state_shape = (GLA_HEADS, GLA_DK, GLA_DV)
    state_spec = pl.BlockSpec(state_shape, const3)
    cast = cast_weight is not None
    kern = functools.partial(_gla_kernel, tb=tb, cast=cast)
    vmem = (2 * 2 * tb * (2 * GLA_KW + GLA_VW + LANES) * 2 + 2 * 2 * tb * GLA_VW * 2
            + 10 * GLA_HEADS * GLA_DK * GLA_DV * 4 + 24 * tb * GLA_KW * 4 + 2 * SPILL_BYTES)
    in_specs = specs(fwd, fwd_next) + specs(bwd, bwd_next) + [
        pl.BlockSpec((LANES, GLA_KW), const2), pl.BlockSpec((1, GLA_KW), const2),
        pl.BlockSpec((LANES, GLA_KW), const2), pl.BlockSpec((1, GLA_KW), const2),
        state_spec, state_spec,
    ]
    out_specs = [
        pl.BlockSpec((tb, GLA_VW), lambda i: (i, 0)),
        pl.BlockSpec((tb, GLA_VW), lambda i: (nb - 1 - i, 0)),
        state_spec, state_spec,
    ]
    out_shape = [
        jax.ShapeDtypeStruct((n, GLA_VW), BF16), jax.ShapeDtypeStruct((n, GLA_VW), BF16),
        jax.ShapeDtypeStruct(state_shape, F32), jax.ShapeDtypeStruct(state_shape, F32),
    ]
    scratch = [pltpu.VMEM((2,) + state_shape, F32), pltpu.VMEM((2, tb, GLA_KW), F32)]
    operands = [p, l, l, p, l, l, wgf, bgf, wgb, bgb, s0f, s0b]
    if cast:
        assert cast_weight.shape == (FFN_HIDDEN, D_MODEL)
        assert nb >= FFN_HIDDEN // GLA_CAST_ROWS + 2
        any_spec = pl.BlockSpec(memory_space=pl.ANY)
        in_specs.append(any_spec)
        out_specs.append(any_spec)
        out_shape.append(jax.ShapeDtypeStruct(cast_weight.shape, BF16))
        scratch += _cast_scratch(GLA_CAST_ROWS, D_MODEL)
        operands.append(cast_weight)
        vmem += GLA_CAST_ROWS * D_MODEL * (2 * 4 + 2 * 2)
    return pl.pallas_call(
        kern,
        grid=(nb,),
        in_specs=in_specs,
        out_specs=out_specs,
        out_shape=out_shape,
        scratch_shapes=scratch,
        compiler_params=_params(1, vmem),
        name="gla",
    )(*operands)


def _mix_kernel(of_ref, ob_ref, gcb_ref, ccx_ref, x_ref, gt_ref, gg_ref, cw_ref,
                wo_ref, o_ref, m_ref, w_ref, *, bm):
    g_ref, cb_ref = gcb_ref.at[:, :GLA_VW], gcb_ref.at[:, GLA_VW:]
    cc_ref, cx_ref = ccx_ref.at[:, :CONV_W], ccx_ref.at[:, CONV_W:]

    @pl.when(pl.program_id(0) == 0)
    def _():
        for r0 in range(0, D_MODEL, W_ROWS):
            w_ref[r0:r0 + W_ROWS, :] = wo_ref[r0:r0 + W_ROWS, :].astype(BF16)

    u = cc_ref[...].astype(F32) * cx_ref[...].astype(F32)
    col = lax.broadcasted_iota(jnp.int32, u.shape, 0) % GRID_W
    u_prev = jnp.where(col == 0, 0.0, pltpu.roll(u, 1, 0))
    u_next = jnp.where(col == GRID_W - 1, 0.0, pltpu.roll(u, bm - 1, 0))
    conv = cw_ref[0:1, :] * u_prev + cw_ref[1:2, :] * u + cw_ref[2:3, :] * u_next
    m_ref[:, GLA_VW:] = (cb_ref[...].astype(F32) * conv).astype(BF16)
    mixed = jnp.dot(m_ref[:, GLA_VW:], w_ref[GLA_VW:, :], preferred_element_type=F32)

    gg = gg_ref[...]
    for h in range(GLA_HEADS):
        vs = slice(h * GLA_DV, (h + 1) * GLA_DV)
        o = of_ref[:, vs].astype(F32) + ob_ref[:, vs].astype(F32)
        o32 = o * lax.rsqrt(jnp.mean(o * o, axis=-1, keepdims=True) + EPS)
        m_ref[:, vs] = ((o32 * gg) * _silu(g_ref[:, vs].astype(F32))).astype(BF16)
        mixed = mixed + jnp.dot(m_ref[:, vs], w_ref[vs, :], preferred_element_type=F32)
    o_ref[...] = x_ref[...] + gt_ref[0:1, :] * mixed


def _mix_call(o_f, o_b, p, x, mod, gla_g, conv_w, w_out, *, bm):
    n = x.shape[0]
    kern = functools.partial(_mix_kernel, bm=bm)
    assert GLA_VW == CONV_W
    pcol = lambda k: pl.BlockSpec((bm, 2 * CONV_W), lambda i, k=k: (i, k))
    const2 = lambda i: (0, 0)
    vmem = (2 * 6 * bm * GLA_VW * 2 + 2 * 2 * bm * D_MODEL * 4
            + D_MODEL * D_MODEL * (4 + 2) + bm * D_MODEL * 2 + 8 * bm * GLA_VW * 4
            + SPILL_BYTES)
    return pl.pallas_call(
        kern,
        grid=(n // bm,),
        in_specs=[
            pl.BlockSpec((bm, GLA_VW), lambda i: (i, 0)),
            pl.BlockSpec((bm, GLA_VW), lambda i: (i, 0)),
            pcol(1), pcol(2),
            pl.BlockSpec((bm, D_MODEL), lambda i: (i, 0)),
            pl.BlockSpec((SUBLANES, D_MODEL), lambda i: (0, 0)),
            pl.BlockSpec((1, GLA_DV), const2),
            pl.BlockSpec((3, CONV_W), const2),
            pl.BlockSpec((D_MODEL, D_MODEL), const2, pipeline_mode=pl.Buffered(1)),
        ],
        out_specs=pl.BlockSpec((bm, D_MODEL), lambda i: (i, 0)),
        out_shape=jax.ShapeDtypeStruct((n, D_MODEL), F32),
        scratch_shapes=[pltpu.VMEM((bm, D_MODEL), BF16), pltpu.VMEM((D_MODEL, D_MODEL), BF16)],
        compiler_params=_params(1, vmem),
        name="mix",
    )(o_f, o_b, p, p, x, mod, gla_g, conv_w, w_out)


def _ffn_kernel(x_ref, g_ref, sh_ref, sc_ref, gt_ref, wg_ref, wu_ref, wd_ref, fg_ref,
                o_ref, h_ref, *, bm):
    j = pl.program_id(1)

    @pl.when(j == 0)
    def _():
        _norm_mod_rows(x_ref, g_ref[...], 1.0 + sc_ref[0:1, :], sh_ref[0:1, :], h_ref, bm,
                       copy_ref=o_ref)

    h = h_ref[...]
    gate = jnp.dot(h, wg_ref[...], preferred_element_type=F32)
    up = jnp.dot(h, wu_ref[...], preferred_element_type=F32)
    part = jnp.dot((_silu(gate) * up).astype(BF16), wd_ref[...], preferred_element_type=F32)
    o_ref[...] += gt_ref[0:1, :] * part

    @pl.when(j == pl.num_programs(1) - 1)
    def _():
        fg = fg_ref[...]

        def final_norm(rs):
            xn = o_ref[rs, :]
            ms = jnp.mean(xn * xn, axis=-1, keepdims=True)
            o_ref[rs, :] = (xn * lax.rsqrt(ms + EPS)) * fg
        _for_row_chunks(bm, final_norm)


def _ffn_call(x, g, mod, wg, wu, wd, fg, *, bm):
    n = x.shape[0]
    kern = functools.partial(_ffn_kernel, bm=bm)
    modspec = lambda k: pl.BlockSpec((SUBLANES, D_MODEL), lambda i, j, k=k: (0, k))
    vmem = (3 * bm * D_MODEL * 4 + bm * D_MODEL * 2 + 2 * 3 * D_MODEL * FFN_BH * 2
            + 4 * bm * FFN_BH * 4 + bm * D_MODEL * 4 + SPILL_BYTES)
    return pl.pallas_call(
        kern,
        grid=(n // bm, FFN_HIDDEN // FFN_BH),
        in_specs=[
            pl.BlockSpec((bm, D_MODEL), lambda i, j: (i, 0)),
            pl.BlockSpec((1, D_MODEL), lambda i, j: (0, 0)),
            modspec(1), modspec(2), modspec(3),
            pl.BlockSpec((D_MODEL, FFN_BH), lambda i, j: (0, j)),
            pl.BlockSpec((D_MODEL, FFN_BH), lambda i, j: (0, j)),
            pl.BlockSpec((FFN_BH, D_MODEL), lambda i, j: (j, 0)),
            pl.BlockSpec((1, D_MODEL), lambda i, j: (0, 0)),
        ],
        out_specs=pl.BlockSpec((bm, D_MODEL), lambda i, j: (i, 0)),
        out_shape=jax.ShapeDtypeStruct((n, D_MODEL), F32),
        scratch_shapes=[pltpu.VMEM((bm, D_MODEL), BF16)],
        compiler_params=_params(2, vmem),
        name="ffn",
    )(x, g, mod, mod, mod, wg, wu, wd, fg)


def kernel(x, c, ctx, c_ctx, w_mod, b_mod, norm1_g, norm2_g, w_in, w_gate_f, b_gate_f,
           w_gate_b, b_gate_b, gla_norm_g, conv_w, w_out, w_ffn_gate, w_ffn_up, w_ffn_down,
           final_g):
    assert x.shape[0] == 1 and w_mod.shape[0] == 1, "batch 1, depth 1 only"
    seq, ctx_len = x.shape[1], ctx.shape[1]
    x2, ctx2 = x[0], ctx[0]

    c_t = jnp.concatenate(
        [c[0][:, None], c_ctx[:, None], jnp.zeros((D_MODEL, SUBLANES - 2), F32)], axis=1)
    mod = _mod_call(c_t, w_mod[0], b_mod, EARLY_MOD_COLS)

    pad_f = jnp.zeros((LANES, GLA_KW), F32).at[:GATE_RANK].set(w_gate_f[0]).astype(BF16)
    pad_b = jnp.zeros((LANES, GLA_KW), F32).at[GATE_RANK:2 * GATE_RANK].set(w_gate_b[0]).astype(BF16)
    g1 = norm1_g[0][None]
    w_in_t = jnp.transpose(w_in[0])

    h_c, l_c = _prenorm_call(ctx2, g1, mod, w_in_t, row=1, bm=ctx_len)
    p_c, = _in_proj_call(h_c, w_in_t, bm=ctx_len, n_tiles=2)
    zero_state = jnp.zeros((GLA_HEADS, GLA_DK, GLA_DV), F32)
    _, _, s_f, s_b = _gla_call(p_c, l_c, pad_f, b_gate_f, pad_b, b_gate_b, zero_state,
                               zero_state, tb=ctx_len)

    h_x, l_x = _prenorm_call(x2, g1, mod, w_in_t, row=0, bm=2048)
    p_x, wg_b, wu_b, mod_late = _in_proj_call(
        h_x, w_in_t, (w_ffn_gate[0], w_ffn_up[0], c_t, w_mod[0], b_mod, EARLY_MOD_COLS),
        bm=1024, n_tiles=P_COLS // P_BN)
    o_f, o_b, _, _, wd_b = _gla_call(p_x, l_x, pad_f, b_gate_f, pad_b, b_gate_b, s_f, s_b,
                                     w_ffn_down[0], tb=256)
    x_mid = _mix_call(o_f, o_b, p_x, x2, mod_late, gla_norm_g, conv_w[0], w_out[0], bm=256)
    out = _ffn_call(x_mid, norm2_g[0][None], mod_late, wg_b, wu_b, wd_b, final_g[None],
                    bm=1024)
    return out[None]
```

```python
import functools

import jax
import jax.numpy as jnp
from jax import lax
from jax.experimental import pallas as pl
from jax.experimental.pallas import tpu as pltpu

F32 = jnp.float32
BF16 = jnp.bfloat16

D_MODEL = 2048
GRID_W = 64
GLA_HEADS = 4
GLA_DK = 128
GLA_DV = 256
GLA_KW = GLA_HEADS * GLA_DK
GLA_VW = GLA_HEADS * GLA_DV
GATE_RANK = 16
GATE_NORM = 16.0
CHUNK = 64
CONV_W = D_MODEL - GLA_VW
FFN_HIDDEN = 5632
EPS = 1e-6

LANES = 128
SUBLANES = 8
MIB = 1024 * 1024
VMEM_BYTES = 64 * MIB
VMEM_RESERVED = 6 * MIB
SPILL_BYTES = 4 * MIB

GATE_COL = 2 * GLA_KW + 2 * GLA_VW
GATE_COLS = 2 * GATE_RANK
P_COLS = GATE_COL + 3 * CONV_W
P_BN = 1024
P_SHIFT_FROM = GATE_COL // P_BN

_NT_DIMS = (((1,), (1,)), ((), ()))

MOD_BN = 2048
EARLY_MOD_COLS = 2 * D_MODEL
LATE_MOD_BN = 256
FFN_BH = 512
NORM_ROWS = 16


def _params(grid_rank, vmem_bytes):
    return pltpu.CompilerParams(
        dimension_semantics=("arbitrary",) * grid_rank,
        vmem_limit_bytes=min(int(vmem_bytes), VMEM_BYTES - VMEM_RESERVED),
    )


def _silu(v):
    return v * jax.nn.sigmoid(v)


def _for_row_chunks(rows, body):
    for r0 in range(0, rows, NORM_ROWS):
        body(slice(r0, r0 + NORM_ROWS))


def _norm_mod_rows(x_ref, g, mul, sh, dst_ref, rows, copy_ref=None):
    def body(rs):
        xs = x_ref[rs, :]
        ms = jnp.mean(xs * xs, axis=-1, keepdims=True)
        y = xs * lax.rsqrt(ms + EPS)
        dst_ref[rs, :] = ((y * g) * mul + sh).astype(BF16)
        if copy_ref is not None:
            copy_ref[rs, :] = xs
    _for_row_chunks(rows, body)


def _mod_prepare(c_ref, sb_ref):
    s = _silu(c_ref[...])
    for r in range(2):
        sb_ref[r] = jnp.broadcast_to(s[:, r:r + 1], (D_MODEL, LANES))


def _mod_matvec(w_ref, b_ref, sb_ref, o_ref, *, unrolled):
    nt = w_ref.shape[1] // LANES

    def body(kk, accs):
        r0 = kk * SUBLANES if unrolled else pl.multiple_of(kk * SUBLANES, SUBLANES)
        s0 = sb_ref[0, pl.ds(r0, SUBLANES), :]
        s1 = sb_ref[1, pl.ds(r0, SUBLANES), :]
        new = []
        for t in range(nt):
            w8 = w_ref[pl.ds(r0, SUBLANES), t * LANES:(t + 1) * LANES]
            new.append(accs[2 * t] + w8 * s0)
            new.append(accs[2 * t + 1] + w8 * s1)
        return tuple(new)

    accs = (jnp.zeros((SUBLANES, LANES), F32),) * (2 * nt)
    if unrolled:
        for kk in range(D_MODEL // SUBLANES):
            accs = body(kk, accs)
    else:
        accs = lax.fori_loop(0, D_MODEL // SUBLANES, body, accs, unroll=8)
    o_ref[...] = jnp.zeros(o_ref.shape, F32)
    for t in range(nt):
        b = b_ref[:, t * LANES:(t + 1) * LANES]
        for r in range(2):
            o_ref[r:r + 1, t * LANES:(t + 1) * LANES] = (
                jnp.sum(accs[2 * t + r], axis=0, keepdims=True) + b)


def _mod_kernel(c_ref, w_ref, b_ref, o_ref, sb_ref):
    @pl.when(pl.program_id(0) == 0)
    def _():
        _mod_prepare(c_ref, sb_ref)

    _mod_matvec(w_ref, b_ref, sb_ref, o_ref, unrolled=False)


def _mod_call(c_t, w_mod, b_mod, n):
    return pl.pallas_call(
        _mod_kernel,
        grid=(n // MOD_BN,),
        in_specs=[
            pl.BlockSpec((D_MODEL, SUBLANES), lambda j: (0, 0)),
            pl.BlockSpec((D_MODEL, MOD_BN), lambda j: (0, j)),
            pl.BlockSpec((1, MOD_BN), lambda j: (0, j)),
        ],
        out_specs=pl.BlockSpec((SUBLANES, MOD_BN), lambda j: (0, j)),
        out_shape=jax.ShapeDtypeStruct((SUBLANES, n), F32),
        scratch_shapes=[pltpu.VMEM((2, D_MODEL, LANES), F32)],
        compiler_params=_params(1, 2 * D_MODEL * MOD_BN * 4 + 2 * SPILL_BYTES),
        name="mod",
    )(c_t, w_mod, b_mod)


def _prenorm_kernel(x_ref, g_ref, sh_ref, sc_ref, wl_ref, h_ref, l_ref, *, row, bm):
    _norm_mod_rows(x_ref, g_ref[...], 1.0 + sc_ref[row:row + 1, :], sh_ref[row:row + 1, :],
                   h_ref, bm)
    l_ref[...] = lax.dot_general(h_ref[...], wl_ref[...].astype(BF16), _NT_DIMS,
                                 preferred_element_type=F32).astype(BF16)


def _prenorm_call(x, g, mod, w_in_t, *, row, bm):
    m = x.shape[0]
    kern = functools.partial(_prenorm_kernel, row=row, bm=bm)
    vmem = (2 * bm * D_MODEL * (4 + 2) + 4 * D_MODEL * LANES * 4 + 8 * NORM_ROWS * D_MODEL * 4
            + SPILL_BYTES)
    return pl.pallas_call(
        kern,
        grid=(m // bm,),
        in_specs=[
            pl.BlockSpec((bm, D_MODEL), lambda i: (i, 0)),
            pl.BlockSpec((1, D_MODEL), lambda i: (0, 0)),
            pl.BlockSpec((SUBLANES, D_MODEL), lambda i: (0, 0)),
            pl.BlockSpec((SUBLANES, D_MODEL), lambda i: (0, 1)),
            pl.BlockSpec((LANES, D_MODEL), lambda i: (GATE_COL // LANES, 0)),
        ],
        out_specs=[pl.BlockSpec((bm, D_MODEL), lambda i: (i, 0)),
                   pl.BlockSpec((bm, LANES), lambda i: (i, 0))],
        out_shape=[jax.ShapeDtypeStruct((m, D_MODEL), BF16),
                   jax.ShapeDtypeStruct((m, LANES), BF16)],
        compiler_params=_params(1, vmem),
        name="prenorm",
    )(x, g, mod, mod, w_in_t)


W_ROWS = 256


class _CastJob:
    def __init__(self, sources, rows, stage_ref, out_ref, sem_in, sem_out):
        self.sources, self.rows = sources, rows
        self.stage_ref, self.out_ref, self.sem_in, self.sem_out = stage_ref, out_ref, sem_in, sem_out
        self.n_blocks = sum(n for _, _, n in sources)

    def _for_block(self, c, fn):
        first = 0
        for src, dst, n in self.sources:
            @pl.when((c >= first) & (c < first + n))
            def _(src=src, dst=dst, first=first):
                fn(src, dst, pl.multiple_of((c - first) * self.rows, self.rows), lax.rem(c, 2))
            first += n

    def _fetch(self, src, dst, r, slot):
        return pltpu.make_async_copy(src.at[pl.ds(r, self.rows), :], self.stage_ref.at[slot],
                                     self.sem_in.at[slot])

    def _store(self, src, dst, r, slot):
        return pltpu.make_async_copy(self.out_ref.at[slot], dst.at[pl.ds(r, self.rows), :],
                                     self.sem_out.at[slot])

    def before(self, step):
        @pl.when(step == 0)
        def _():
            self._for_block(step, lambda *a: self._fetch(*a).start())
        self._for_block(step, lambda *a: self._fetch(*a).wait())
        self._for_block(step + 1, lambda *a: self._fetch(*a).start())
        self._for_block(step - 2, lambda *a: self._store(*a).wait())

    def cast(self, step):
        slot = lax.rem(step, 2)
        self.out_ref[slot] = self.stage_ref[slot].astype(BF16)

    def after(self, step):
        self._for_block(step, lambda *a: self._store(*a).start())


def _cast_scratch(rows, cols):
    return [pltpu.VMEM((2, rows, cols), F32), pltpu.VMEM((2, rows, cols), BF16),
            pltpu.SemaphoreType.DMA((2,)), pltpu.SemaphoreType.DMA((2,))]


IN_PROJ_CAST_ROWS = 128
GLA_CAST_ROWS = 256


def _in_proj_kernel(h_ref, wa_ref, wb_ref, *refs, side_jobs):
    j = pl.program_id(0)
    if side_jobs:
        (wg_ref, wu_ref, c_ref, wm_ref, bm_ref, o_ref, wg_out, wu_out, mod_ref,
         w_ref, stage_ref, cast_ref, sem_in, sem_out, sb_ref) = refs
        n = D_MODEL // IN_PROJ_CAST_ROWS
        job = _CastJob(((wg_ref, wg_out, n), (wu_ref, wu_out, n)), IN_PROJ_CAST_ROWS,
                       stage_ref, cast_ref, sem_in, sem_out)
        step = j * pl.num_programs(1) + pl.program_id(1)
        job.before(step)

        @pl.when(step == 0)
        def _():
            _mod_prepare(c_ref, sb_ref)
    else:
        o_ref, w_ref = refs

    @pl.when(pl.program_id(1) == 0)
    def _():
        @pl.when(j < P_SHIFT_FROM)
        def _():
            for r0 in range(0, P_BN, W_ROWS):
                w_ref[r0:r0 + W_ROWS, :] = wa_ref[r0:r0 + W_ROWS, :].astype(BF16)

        @pl.when(j >= P_SHIFT_FROM)
        def _():
            keep = P_BN - GATE_COLS
            for r0 in range(0, keep, W_ROWS):
                r1 = min(r0 + W_ROWS, keep)
                w_ref[r0:r1, :] = wa_ref[GATE_COLS + r0:GATE_COLS + r1, :].astype(BF16)
            w_ref[keep:, :] = wb_ref[...].astype(BF16)

    o_ref[...] = lax.dot_general(h_ref[...], w_ref[...], _NT_DIMS,
                                 preferred_element_type=F32).astype(BF16)
    if side_jobs:
        job.cast(step)
        _mod_matvec(wm_ref, bm_ref, sb_ref, mod_ref, unrolled=True)
        job.after(step)


def _in_proj_call(h, w_in_t, side=None, *, bm, n_tiles):
    m = h.shape[0]
    ni = m // bm
    vmem = (2 * bm * D_MODEL * 2 + 2 * (P_BN + GATE_COLS) * D_MODEL * 4 + P_BN * D_MODEL * 2
            + 2 * bm * P_BN * 2 + bm * P_BN * 4 + 4 * W_ROWS * D_MODEL * 4 + SPILL_BYTES)
    in_specs = [
        pl.BlockSpec((bm, D_MODEL), lambda j, i: (i, 0)),
        pl.BlockSpec((P_BN, D_MODEL), lambda j, i: (j, 0)),
        pl.BlockSpec((GATE_COLS, D_MODEL), lambda j, i: ((j + 1) * (P_BN // GATE_COLS), 0)),
    ]
    out_specs = [pl.BlockSpec((bm, P_BN), lambda j, i: (i, j))]
    out_shape = [jax.ShapeDtypeStruct((m, n_tiles * P_BN), BF16)]
    scratch = [pltpu.VMEM((P_BN, D_MODEL), BF16)]
    operands = [h, w_in_t, w_in_t]
    if side is not None:
        w_gate, w_up, c_t, w_mod, b_mod, first_col = side
        n_steps = n_tiles * ni
        assert n_steps >= 2 * D_MODEL // IN_PROJ_CAST_ROWS + 2
        any_spec = pl.BlockSpec(memory_space=pl.ANY)
        n_mod = w_mod.shape[1] - first_col
        n_blocks = n_mod // LATE_MOD_BN
        assert n_mod % LATE_MOD_BN == 0 and first_col % LATE_MOD_BN == 0 and n_blocks <= n_steps
        mod_block = lambda j, i: jnp.minimum(j * ni + i, n_blocks - 1)
        in_specs += [
            any_spec, any_spec,
            pl.BlockSpec((D_MODEL, SUBLANES), lambda j, i: (0, 0)),
            pl.BlockSpec((D_MODEL, LATE_MOD_BN),
                         lambda j, i: (0, first_col // LATE_MOD_BN + mod_block(j, i))),
            pl.BlockSpec((1, LATE_MOD_BN),
                         lambda j, i: (0, first_col // LATE_MOD_BN + mod_block(j, i))),
        ]
        out_specs += [any_spec, any_spec,
                      pl.BlockSpec((SUBLANES, LATE_MOD_BN), lambda j, i: (0, mod_block(j, i)))]
        out_shape += [jax.ShapeDtypeStruct(w_gate.shape, BF16),
                      jax.ShapeDtypeStruct(w_up.shape, BF16),
                      jax.ShapeDtypeStruct((SUBLANES, n_mod), F32)]
        scratch += _cast_scratch(IN_PROJ_CAST_ROWS, FFN_HIDDEN)
        scratch += [pltpu.VMEM((2, D_MODEL, LANES), F32)]
        operands += [w_gate, w_up, c_t, w_mod, b_mod]
        vmem += (IN_PROJ_CAST_ROWS * FFN_HIDDEN * (2 * 4 + 2 * 2)
                 + 2 * D_MODEL * LATE_MOD_BN * 4 + 2 * D_MODEL * LANES * 4)
    return pl.pallas_call(
        functools.partial(_in_proj_kernel, side_jobs=side is not None),
        grid=(n_tiles, ni),
        in_specs=in_specs,
        out_specs=out_specs,
        out_shape=out_shape,
        scratch_shapes=scratch,
        compiler_params=_params(2, vmem),
        name="in_proj",
    )(*operands)


def _gla_log_decay_split(l_ref, wg_ref, bg_ref):
    z = jnp.dot(l_ref[...], wg_ref[...], preferred_element_type=F32) + bg_ref[...]
    log_a = (jnp.minimum(z, 0.0) - jnp.log(1.0 + jnp.exp(-jnp.abs(z)))) * (1.0 / GATE_NORM)
    hi = log_a.astype(BF16)
    lo = (log_a - hi.astype(F32)).astype(BF16)
    return hi, lo


def _gla_chunk_cumsum(pieces, *, reverse, tb):
    r = lax.broadcasted_iota(jnp.int32, (tb, tb), 0)
    cc = lax.broadcasted_iota(jnp.int32, (tb, tb), 1)
    same_chunk = (r // CHUNK) == (cc // CHUNK)
    tri = jnp.where(same_chunk & ((cc >= r) if reverse else (cc <= r)), 1.0, 0.0).astype(BF16)
    hi, lo = pieces
    return (jnp.dot(tri, hi, preferred_element_type=F32)
            + jnp.dot(tri, lo, preferred_element_type=F32))


def _gla_kernel(qkv_f, lf, lf_next, qkv_b, lb, lb_next, wgf, bgf, wgb, bgb, s0f, s0b,
                *refs, tb, cast):
    qf, kf, vf = (qkv_f.at[:, :GLA_KW], qkv_f.at[:, GLA_KW:2 * GLA_KW], qkv_f.at[:, 2 * GLA_KW:])
    qb, kb, vb = (qkv_b.at[:, :GLA_KW], qkv_b.at[:, GLA_KW:2 * GLA_KW], qkv_b.at[:, 2 * GLA_KW:])
    i = pl.program_id(0)
    if cast:
        (wd_ref, of, ob, sf, sb, wd_out, s_ref, cum_ref,
         stage_ref, cast_ref, sem_in, sem_out) = refs
        job = _CastJob(((wd_ref, wd_out, FFN_HIDDEN // GLA_CAST_ROWS),), GLA_CAST_ROWS,
                       stage_ref, cast_ref, sem_in, sem_out)
        job.before(i)
    else:
        of, ob, sf, sb, s_ref, cum_ref = refs

    @pl.when(i == 0)
    def _():
        s_ref[0] = s0f[...]
        s_ref[1] = s0b[...]
        cum_ref[0] = _gla_chunk_cumsum(_gla_log_decay_split(lf, wgf, bgf), reverse=False, tb=tb)
        cum_ref[1] = _gla_chunk_cumsum(_gla_log_decay_split(lb, wgb, bgb), reverse=True, tb=tb)

    dirs = ((qf, kf, vf, of, False), (qb, kb, vb, ob, True))
    ii = lax.broadcasted_iota(jnp.int32, (CHUNK, GLA_DK), 0)
    jj = lax.broadcasted_iota(jnp.int32, (CHUNK, GLA_DK), 1)
    nc = tb // CHUNK

    def start(step):
        items = []
        for d, (q_ref, k_ref, v_ref, o_ref, reverse) in enumerate(dirs):
            r0 = (nc - 1 - step if reverse else step) * CHUNK
            cum_c = cum_ref[d, r0:r0 + CHUNK, :]
            last = cum_c[0:1, :] if reverse else cum_c[CHUNK - 1:CHUNK, :]
            q_c = q_ref[r0:r0 + CHUNK, :].astype(F32)
            k_c = k_ref[r0:r0 + CHUNK, :].astype(F32)
            q_e = ((q_c * (GLA_DK ** -0.5)) * jnp.exp(cum_c)).astype(BF16)
            k_e = (k_c * jnp.exp(-cum_c)).astype(BF16)
            k_d = (k_c * jnp.exp(last - cum_c)).astype(BF16)
            decay = jnp.exp(last)
            mask = ((jj >= ii) & (jj < CHUNK)) if reverse else (jj <= ii)
            for h in range(GLA_HEADS):
                ks = slice(h * GLA_DK, (h + 1) * GLA_DK)
                vs = slice(h * GLA_DV, (h + 1) * GLA_DV)
                items.append(dict(
                    d=d, h=h, r0=r0, vs=vs, o_ref=o_ref, mask=mask,
                    q_e=q_e[:, ks], k_e=k_e[:, ks], k_d=k_d[:, ks], decay=decay[:, ks],
                    v=v_ref[r0:r0 + CHUNK, vs]))
        for it in items:
            k_pad = jnp.concatenate([it["k_e"], jnp.zeros((GLA_DK - CHUNK, GLA_DK), BF16)], axis=0)
            it["scores"] = lax.dot_general(it["q_e"], k_pad, _NT_DIMS,
                                           preferred_element_type=F32)
        return items

    items = start(0)
    next_cum = None
    for step in range(nc):
        following = start(step + 1) if step + 1 < nc else None
        if step == 1:
            next_f = _gla_log_decay_split(lf_next, wgf, bgf)
            next_b = _gla_log_decay_split(lb_next, wgb, bgb)
        if step == nc - 1:
            next_cum = (_gla_chunk_cumsum(next_f, reverse=False, tb=tb),
                        _gla_chunk_cumsum(next_b, reverse=True, tb=tb))
        for it in items:
            scores = jnp.where(it["mask"], it["scores"], 0.0).astype(BF16)
            s_old = s_ref[it["d"], it["h"]]
            lhs = jnp.concatenate([scores, it["q_e"]], axis=1)
            rhs = jnp.concatenate(
                [it["v"], jnp.zeros((GLA_DK - CHUNK, GLA_DV), BF16), s_old.astype(BF16)], axis=0)
            it["o_ref"][it["r0"]:it["r0"] + CHUNK, it["vs"]] = jnp.dot(
                lhs, rhs, preferred_element_type=F32).astype(BF16)
            ds = lax.dot_general(it["k_d"], it["v"], (((0,), (0,)), ((), ())),
                                 preferred_element_type=F32)
            dcol = jnp.transpose(jnp.broadcast_to(it["decay"], (SUBLANES, GLA_DK)))[:, 0:1]
            s_ref[it["d"], it["h"]] = dcol * s_old + ds
        items = following
    cum_ref[0] = next_cum[0]
    cum_ref[1] = next_cum[1]
    if cast:
        job.cast(i)
        job.after(i)

    @pl.when(i == pl.num_programs(0) - 1)
    def _():
        sf[...] = s_ref[0]
        sb[...] = s_ref[1]


def _gla_call(p, l, wgf, bgf, wgb, bgb, s0f, s0b, cast_weight=None, *, tb):
    n = p.shape[0]
    nb = n // tb
    fwd = lambda i: i
    bwd = lambda i: nb - 1 - i
    fwd_next = lambda i: jnp.minimum(i + 1, nb - 1)
    bwd_next = lambda i: jnp.maximum(nb - 2 - i, 0)

    def specs(pos, pos_next):
        return [
            pl.BlockSpec((tb, 2 * GLA_KW + GLA_VW), lambda i: (pos(i), 0)),
            pl.BlockSpec((tb, LANES), lambda i: (pos(i), 0)),
            pl.BlockSpec((tb, LANES), lambda i: (pos_next(i), 0)),
        ]

    const2 = lambda i: (0, 0)
    const3 = lambda i: (0, 0, 0)
    state_shape = (GLA_HEADS, GLA_DK, GLA_DV)
    state_spec = pl.BlockSpec(state_shape, const3)
    cast = cast_weight is not None
    kern = functools.partial(_gla_kernel, tb=tb, cast=cast)
    vmem = (2 * 2 * tb * (2 * GLA_KW + GLA_VW + LANES) * 2 + 2 * 2 * tb * GLA_VW * 2
            + 10 * GLA_HEADS * GLA_DK * GLA_DV * 4 + 24 * tb * GLA_KW * 4 + 2 * SPILL_BYTES)
    in_specs = specs(fwd, fwd_next) + specs(bwd, bwd_next) + [
        pl.BlockSpec((LANES, GLA_KW), const2), pl.BlockSpec((1, GLA_KW), const2),
        pl.BlockSpec((LANES, GLA_KW), const2), pl.BlockSpec((1, GLA_KW), const2),
        state_spec, state_spec,
    ]
    out_specs = [
        pl.BlockSpec((tb, GLA_VW), lambda i: (i, 0)),
        pl.BlockSpec((tb, GLA_VW), lambda i: (nb - 1 - i, 0)),
        state_spec, state_spec,
    ]
    out_shape = [
        jax.ShapeDtypeStruct((n, GLA_VW), BF16), jax.ShapeDtypeStruct((n, GLA_VW), BF16),
        jax.ShapeDtypeStruct(state_shape, F32), jax.ShapeDtypeStruct(state_shape, F32),
    ]
    scratch = [pltpu.VMEM((2,) + state_shape, F32), pltpu.VMEM((2, tb, GLA_KW), F32)]
    operands = [p, l, l, p, l, l, wgf, bgf, wgb, bgb, s0f, s0b]
    if cast:
        assert cast_weight.shape == (FFN_HIDDEN, D_MODEL)
        assert nb >= FFN_HIDDEN // GLA_CAST_ROWS + 2
        any_spec = pl.BlockSpec(memory_space=pl.ANY)
        in_specs.append(any_spec)
        out_specs.append(any_spec)
        out_shape.append(jax.ShapeDtypeStruct(cast_weight.shape, BF16))
        scratch += _cast_scratch(GLA_CAST_ROWS, D_MODEL)
        operands.append(cast_weight)
        vmem += GLA_CAST_ROWS * D_MODEL * (2 * 4 + 2 * 2)
    return pl.pallas_call(
        kern,
        grid=(nb,),
        in_specs=in_specs,
        out_specs=out_specs,
        out_shape=out_shape,
        scratch_shapes=scratch,
        compiler_params=_params(1, vmem),
        name="gla",
    )(*operands)


def _mix_kernel(of_ref, ob_ref, gcb_ref, ccx_ref, x_ref, gt_ref, gg_ref, cw_ref,
                wo_ref, o_ref, m_ref, *, bm):
    g_ref, cb_ref = gcb_ref.at[:, :GLA_VW], gcb_ref.at[:, GLA_VW:]
    cc_ref, cx_ref = ccx_ref.at[:, :CONV_W], ccx_ref.at[:, CONV_W:]

    u = cc_ref[...].astype(F32) * cx_ref[...].astype(F32)
    col = lax.broadcasted_iota(jnp.int32, u.shape, 0) % GRID_W
    u_prev = jnp.where(col == 0, 0.0, pltpu.roll(u, 1, 0))
    u_next = jnp.where(col == GRID_W - 1, 0.0, pltpu.roll(u, bm - 1, 0))
    conv = cw_ref[0:1, :] * u_prev + cw_ref[1:2, :] * u + cw_ref[2:3, :] * u_next
    m_ref[:, GLA_VW:] = (cb_ref[...].astype(F32) * conv).astype(BF16)
    mixed = jnp.dot(m_ref[:, GLA_VW:], wo_ref[GLA_VW:, :].astype(BF16),
                    preferred_element_type=F32)

    gg = gg_ref[...]
    for h in range(GLA_HEADS):
        vs = slice(h * GLA_DV, (h + 1) * GLA_DV)
        o = of_ref[:, vs].astype(F32) + ob_ref[:, vs].astype(F32)
        o32 = o * lax.rsqrt(jnp.mean(o * o, axis=-1, keepdims=True) + EPS)
        m_ref[:, vs] = ((o32 * gg) * _silu(g_ref[:, vs].astype(F32))).astype(BF16)
        mixed = mixed + jnp.dot(m_ref[:, vs], wo_ref[vs, :].astype(BF16),
                                preferred_element_type=F32)
    o_ref[...] = x_ref[...] + gt_ref[0:1, :] * mixed


def _mix_call(o_f, o_b, p, x, mod, gla_g, conv_w, w_out, *, bm):
    n = x.shape[0]
    kern = functools.partial(_mix_kernel, bm=bm)
    assert GLA_VW == CONV_W
    pcol = lambda k: pl.BlockSpec((bm, 2 * CONV_W), lambda i, k=k: (i, k))
    const2 = lambda i: (0, 0)
    vmem = (2 * 6 * bm * GLA_VW * 2 + 2 * 2 * bm * D_MODEL * 4
            + D_MODEL * D_MODEL * 4 + bm * D_MODEL * 2 + 8 * bm * GLA_VW * 4
            + SPILL_BYTES)
    return pl.pallas_call(
        kern,
        grid=(n // bm,),
        in_specs=[
            pl.BlockSpec((bm, GLA_VW), lambda i: (i, 0)),
            pl.BlockSpec((bm, GLA_VW), lambda i: (i, 0)),
            pcol(1), pcol(2),
            pl.BlockSpec((bm, D_MODEL), lambda i: (i, 0)),
            pl.BlockSpec((SUBLANES, D_MODEL), lambda i: (0, 0)),
            pl.BlockSpec((1, GLA_DV), const2),
            pl.BlockSpec((3, CONV_W), const2),
            pl.BlockSpec((D_MODEL, D_MODEL), const2, pipeline_mode=pl.Buffered(1)),
        ],
        out_specs=pl.BlockSpec((bm, D_MODEL), lambda i: (i, 0)),
        out_shape=jax.ShapeDtypeStruct((n, D_MODEL), F32),
        scratch_shapes=[pltpu.VMEM((bm, D_MODEL), BF16)],
        compiler_params=_params(1, vmem),
        name="mix",
    )(o_f, o_b, p, p, x, mod, gla_g, conv_w, w_out)


def _ffn_kernel(x_ref, g_ref, sh_ref, sc_ref, gt_ref, wg_ref, wu_ref, wd_ref, fg_ref,
                o_ref, h_ref, *, bm):
    j = pl.program_id(1)

    @pl.when(j == 0)
    def _():
        _norm_mod_rows(x_ref, g_ref[...], 1.0 + sc_ref[0:1, :], sh_ref[0:1, :], h_ref, bm,
                       copy_ref=o_ref)

    h = h_ref[...]
    gate = jnp.dot(h, wg_ref[...], preferred_element_type=F32)
    up = jnp.dot(h, wu_ref[...], preferred_element_type=F32)
    part = jnp.dot((_silu(gate) * up).astype(BF16), wd_ref[...], preferred_element_type=F32)
    o_ref[...] += gt_ref[0:1, :] * part

    @pl.when(j == pl.num_programs(1) - 1)
    def _():
        fg = fg_ref[...]

        def final_norm(rs):
            xn = o_ref[rs, :]
            ms = jnp.mean(xn * xn, axis=-1, keepdims=True)
            o_ref[rs, :] = (xn * lax.rsqrt(ms + EPS)) * fg
        _for_row_chunks(bm, final_norm)


def _ffn_call(x, g, mod, wg, wu, wd, fg, *, bm):
    n = x.shape[0]
    kern = functools.partial(_ffn_kernel, bm=bm)
    modspec = lambda k: pl.BlockSpec((SUBLANES, D_MODEL), lambda i, j, k=k: (0, k))
    vmem = (3 * bm * D_MODEL * 4 + bm * D_MODEL * 2 + 2 * 3 * D_MODEL * FFN_BH * 2
            + 4 * bm * FFN_BH * 4 + bm * D_MODEL * 4 + SPILL_BYTES)
    return pl.pallas_call(
        kern,
        grid=(n // bm, FFN_HIDDEN // FFN_BH),
        in_specs=[
            pl.BlockSpec((bm, D_MODEL), lambda i, j: (i, 0)),
            pl.BlockSpec((1, D_MODEL), lambda i, j: (0, 0)),
            modspec(1), modspec(2), modspec(3),
            pl.BlockSpec((D_MODEL, FFN_BH), lambda i, j: (0, j)),
            pl.BlockSpec((D_MODEL, FFN_BH), lambda i, j: (0, j)),
            pl.BlockSpec((FFN_BH, D_MODEL), lambda i, j: (j, 0)),
            pl.BlockSpec((1, D_MODEL), lambda i, j: (0, 0)),
        ],
        out_specs=pl.BlockSpec((bm, D_MODEL), lambda i, j: (i, 0)),
        out_shape=jax.ShapeDtypeStruct((n, D_MODEL), F32),
        scratch_shapes=[pltpu.VMEM((bm, D_MODEL), BF16)],
        compiler_params=_params(2, vmem),
        name="ffn",
    )(x, g, mod, mod, mod, wg, wu, wd, fg)


def kernel(x, c, ctx, c_ctx, w_mod, b_mod, norm1_g, norm2_g, w_in, w_gate_f, b_gate_f,
           w_gate_b, b_gate_b, gla_norm_g, conv_w, w_out, w_ffn_gate, w_ffn_up, w_ffn_down,
           final_g):
    assert x.shape[0] == 1 and w_mod.shape[0] == 1, "batch 1, depth 1 only"
    seq, ctx_len = x.shape[1], ctx.shape[1]
    x2, ctx2 = x[0], ctx[0]

    c_t = jnp.concatenate(
        [c[0][:, None], c_ctx[:, None], jnp.zeros((D_MODEL, SUBLANES - 2), F32)], axis=1)
    mod = _mod_call(c_t, w_mod[0], b_mod, EARLY_MOD_COLS)

    pad_f = jnp.zeros((LANES, GLA_KW), F32).at[:GATE_RANK].set(w_gate_f[0]).astype(BF16)
    pad_b = jnp.zeros((LANES, GLA_KW), F32).at[GATE_RANK:2 * GATE_RANK].set(w_gate_b[0]).astype(BF16)
    g1 = norm1_g[0][None]
    w_in_t = jnp.transpose(w_in[0])

    h_c, l_c = _prenorm_call(ctx2, g1, mod, w_in_t, row=1, bm=ctx_len)
    p_c, = _in_proj_call(h_c, w_in_t, bm=ctx_len, n_tiles=2)
    zero_state = jnp.zeros((GLA_HEADS, GLA_DK, GLA_DV), F32)
    _, _, s_f, s_b = _gla_call(p_c, l_c, pad_f, b_gate_f, pad_b, b_gate_b, zero_state,
                               zero_state, tb=ctx_len)

    h_x, l_x = _prenorm_call(x2, g1, mod, w_in_t, row=0, bm=2048)
    p_x, wg_b, wu_b, mod_late = _in_proj_call(
        h_x, w_in_t, (w_ffn_gate[0], w_ffn_up[0], c_t, w_mod[0], b_mod, EARLY_MOD_COLS),
        bm=1024, n_tiles=P_COLS // P_BN)
    o_f, o_b, _, _, wd_b = _gla_call(p_x, l_x, pad_f, b_gate_f, pad_b, b_gate_b, s_f, s_b,
                                     w_ffn_down[0], tb=256)
    x_mid = _mix_call(o_f, o_b, p_x, x2, mod_late, gla_norm_g, conv_w[0], w_out[0], bm=512)
    out = _ffn_call(x_mid, norm2_g[0][None], mod_late, wg_b, wu_b, wd_b, final_g[None],
                    bm=1024)
    return out[None]
```

```python
import functools

import jax
import jax.numpy as jnp
from jax import lax
from jax.experimental import pallas as pl
from jax.experimental.pallas import tpu as pltpu

F32 = jnp.float32
BF16 = jnp.bfloat16

D_MODEL = 2048
GRID_W = 64
GLA_HEADS = 4
GLA_DK = 128
GLA_DV = 256
GLA_KW = GLA_HEADS * GLA_DK
GLA_VW = GLA_HEADS * GLA_DV
GATE_RANK = 16
GATE_NORM = 16.0
CHUNK = 64
LOG2_E = 1.4426950408889634
CONV_W = D_MODEL - GLA_VW
FFN_HIDDEN = 5632
EPS = 1e-6

LANES = 128
SUBLANES = 8
MIB = 1024 * 1024
VMEM_BYTES = 64 * MIB
VMEM_RESERVED = 6 * MIB
SPILL_BYTES = 4 * MIB

GATE_COL = 2 * GLA_KW + 2 * GLA_VW
GATE_COLS = 2 * GATE_RANK
P_COLS = GATE_COL + 3 * CONV_W
P_BN = 1024
P_SHIFT_FROM = GATE_COL // P_BN

_NT_DIMS = (((1,), (1,)), ((), ()))

MOD_BN = 2048
EARLY_MOD_COLS = 2 * D_MODEL
LATE_MOD_BN = 256
FFN_BH = 512
NORM_ROWS = 16


def _params(grid_rank, vmem_bytes):
    return pltpu.CompilerParams(
        dimension_semantics=("arbitrary",) * grid_rank,
        vmem_limit_bytes=min(int(vmem_bytes), VMEM_BYTES - VMEM_RESERVED),
    )


def _silu(v):
    return v * jax.nn.sigmoid(v)


def _for_row_chunks(rows, body):
    for r0 in range(0, rows, NORM_ROWS):
        body(slice(r0, r0 + NORM_ROWS))


def _norm_mod_rows(x_ref, g, mul, sh, dst_ref, rows, copy_ref=None):
    gain = g * mul

    def body(rs):
        xs = x_ref[rs, :]
        ms = jnp.mean(xs * xs, axis=-1, keepdims=True)
        y = xs * lax.rsqrt(ms + EPS)
        dst_ref[rs, :] = (y * gain + sh).astype(BF16)
        if copy_ref is not None:
            copy_ref[rs, :] = xs
    _for_row_chunks(rows, body)


def _mod_prepare(c_ref, sb_ref):
    s = _silu(c_ref[...])
    for r in range(2):
        sb_ref[r] = jnp.broadcast_to(s[:, r:r + 1], (D_MODEL, LANES))


def _mod_matvec(w_ref, b_ref, sb_ref, o_ref, *, unrolled):
    nt = w_ref.shape[1] // LANES

    def body(kk, accs):
        r0 = kk * SUBLANES if unrolled else pl.multiple_of(kk * SUBLANES, SUBLANES)
        s0 = sb_ref[0, pl.ds(r0, SUBLANES), :]
        s1 = sb_ref[1, pl.ds(r0, SUBLANES), :]
        new = []
        for t in range(nt):
            w8 = w_ref[pl.ds(r0, SUBLANES), t * LANES:(t + 1) * LANES]
            new.append(accs[2 * t] + w8 * s0)
            new.append(accs[2 * t + 1] + w8 * s1)
        return tuple(new)

    accs = (jnp.zeros((SUBLANES, LANES), F32),) * (2 * nt)
    if unrolled:
        for kk in range(D_MODEL // SUBLANES):
            accs = body(kk, accs)
    else:
        accs = lax.fori_loop(0, D_MODEL // SUBLANES, body, accs, unroll=8)
    o_ref[...] = jnp.zeros(o_ref.shape, F32)
    for t in range(nt):
        b = b_ref[:, t * LANES:(t + 1) * LANES]
        for r in range(2):
            o_ref[r:r + 1, t * LANES:(t + 1) * LANES] = (
                jnp.sum(accs[2 * t + r], axis=0, keepdims=True) + b)


def _mod_kernel(c_ref, w_ref, b_ref, o_ref, sb_ref):
    @pl.when(pl.program_id(0) == 0)
    def _():
        _mod_prepare(c_ref, sb_ref)

    _mod_matvec(w_ref, b_ref, sb_ref, o_ref, unrolled=False)


def _mod_call(c_t, w_mod, b_mod, n):
    return pl.pallas_call(
        _mod_kernel,
        grid=(n // MOD_BN,),
        in_specs=[
            pl.BlockSpec((D_MODEL, SUBLANES), lambda j: (0, 0)),
            pl.BlockSpec((D_MODEL, MOD_BN), lambda j: (0, j)),
            pl.BlockSpec((1, MOD_BN), lambda j: (0, j)),
        ],
        out_specs=pl.BlockSpec((SUBLANES, MOD_BN), lambda j: (0, j)),
        out_shape=jax.ShapeDtypeStruct((SUBLANES, n), F32),
        scratch_shapes=[pltpu.VMEM((2, D_MODEL, LANES), F32)],
        compiler_params=_params(1, 2 * D_MODEL * MOD_BN * 4 + 2 * SPILL_BYTES),
        name="mod",
    )(c_t, w_mod, b_mod)


def _prenorm_kernel(x_ref, g_ref, sh_ref, sc_ref, wl_ref, h_ref, l_ref, *, row, bm):
    _norm_mod_rows(x_ref, g_ref[...], 1.0 + sc_ref[row:row + 1, :], sh_ref[row:row + 1, :],
                   h_ref, bm)
    l_ref[...] = lax.dot_general(h_ref[...], wl_ref[...].astype(BF16), _NT_DIMS,
                                 preferred_element_type=F32).astype(BF16)


def _prenorm_call(x, g, mod, w_in_t, *, row, bm):
    m = x.shape[0]
    kern = functools.partial(_prenorm_kernel, row=row, bm=bm)
    vmem = (2 * bm * D_MODEL * (4 + 2) + 4 * D_MODEL * LANES * 4 + 8 * NORM_ROWS * D_MODEL * 4
            + SPILL_BYTES)
    return pl.pallas_call(
        kern,
        grid=(m // bm,),
        in_specs=[
            pl.BlockSpec((bm, D_MODEL), lambda i: (i, 0)),
            pl.BlockSpec((1, D_MODEL), lambda i: (0, 0)),
            pl.BlockSpec((SUBLANES, D_MODEL), lambda i: (0, 0)),
            pl.BlockSpec((SUBLANES, D_MODEL), lambda i: (0, 1)),
            pl.BlockSpec((LANES, D_MODEL), lambda i: (GATE_COL // LANES, 0)),
        ],
        out_specs=[pl.BlockSpec((bm, D_MODEL), lambda i: (i, 0)),
                   pl.BlockSpec((bm, LANES), lambda i: (i, 0))],
        out_shape=[jax.ShapeDtypeStruct((m, D_MODEL), BF16),
                   jax.ShapeDtypeStruct((m, LANES), BF16)],
        compiler_params=_params(1, vmem),
        name="prenorm",
    )(x, g, mod, mod, w_in_t)


W_ROWS = 256


class _CastJob:
    def __init__(self, sources, rows, stage_ref, out_ref, sem_in, sem_out):
        self.sources, self.rows = sources, rows
        self.stage_ref, self.out_ref, self.sem_in, self.sem_out = stage_ref, out_ref, sem_in, sem_out
        self.n_blocks = sum(n for _, _, n in sources)

    def _for_block(self, c, fn):
        first = 0
        for src, dst, n in self.sources:
            @pl.when((c >= first) & (c < first + n))
            def _(src=src, dst=dst, first=first):
                fn(src, dst, pl.multiple_of((c - first) * self.rows, self.rows), lax.rem(c, 2))
            first += n

    def _fetch(self, src, dst, r, slot):
        return pltpu.make_async_copy(src.at[pl.ds(r, self.rows), :], self.stage_ref.at[slot],
                                     self.sem_in.at[slot])

    def _store(self, src, dst, r, slot):
        return pltpu.make_async_copy(self.out_ref.at[slot], dst.at[pl.ds(r, self.rows), :],
                                     self.sem_out.at[slot])

    def before(self, step):
        @pl.when(step == 0)
        def _():
            self._for_block(step, lambda *a: self._fetch(*a).start())
        self._for_block(step, lambda *a: self._fetch(*a).wait())
        self._for_block(step + 1, lambda *a: self._fetch(*a).start())
        self._for_block(step - 2, lambda *a: self._store(*a).wait())

    def cast(self, step):
        slot = lax.rem(step, 2)
        self.out_ref[slot] = self.stage_ref[slot].astype(BF16)

    def after(self, step):
        self._for_block(step, lambda *a: self._store(*a).start())


def _cast_scratch(rows, cols):
    return [pltpu.VMEM((2, rows, cols), F32), pltpu.VMEM((2, rows, cols), BF16),
            pltpu.SemaphoreType.DMA((2,)), pltpu.SemaphoreType.DMA((2,))]


IN_PROJ_CAST_ROWS = 128
GLA_CAST_ROWS = 256


def _in_proj_kernel(h_ref, wa_ref, wb_ref, *refs, side_jobs):
    j = pl.program_id(0)
    if side_jobs:
        (wg_ref, wu_ref, c_ref, wm_ref, bm_ref, o_ref, wg_out, wu_out, mod_ref,
         w_ref, stage_ref, cast_ref, sem_in, sem_out, sb_ref) = refs
        n = D_MODEL // IN_PROJ_CAST_ROWS
        job = _CastJob(((wg_ref, wg_out, n), (wu_ref, wu_out, n)), IN_PROJ_CAST_ROWS,
                       stage_ref, cast_ref, sem_in, sem_out)
        step = j * pl.num_programs(1) + pl.program_id(1)
        job.before(step)

        @pl.when(step == 0)
        def _():
            _mod_prepare(c_ref, sb_ref)
    else:
        o_ref, w_ref = refs

    @pl.when(pl.program_id(1) == 0)
    def _():
        @pl.when(j < P_SHIFT_FROM)
        def _():
            for r0 in range(0, P_BN, W_ROWS):
                w_ref[r0:r0 + W_ROWS, :] = wa_ref[r0:r0 + W_ROWS, :].astype(BF16)

        @pl.when(j >= P_SHIFT_FROM)
        def _():
            keep = P_BN - GATE_COLS
            for r0 in range(0, keep, W_ROWS):
                r1 = min(r0 + W_ROWS, keep)
                w_ref[r0:r1, :] = wa_ref[GATE_COLS + r0:GATE_COLS + r1, :].astype(BF16)
            w_ref[keep:, :] = wb_ref[...].astype(BF16)

    o_ref[...] = lax.dot_general(h_ref[...], w_ref[...], _NT_DIMS,
                                 preferred_element_type=F32).astype(BF16)
    if side_jobs:
        job.cast(step)
        _mod_matvec(wm_ref, bm_ref, sb_ref, mod_ref, unrolled=True)
        job.after(step)


def _in_proj_call(h, w_in_t, side=None, *, bm, n_tiles):
    m = h.shape[0]
    ni = m // bm
    vmem = (2 * bm * D_MODEL * 2 + 2 * (P_BN + GATE_COLS) * D_MODEL * 4 + P_BN * D_MODEL * 2
            + 2 * bm * P_BN * 2 + bm * P_BN * 4 + 4 * W_ROWS * D_MODEL * 4 + SPILL_BYTES)
    in_specs = [
        pl.BlockSpec((bm, D_MODEL), lambda j, i: (i, 0)),
        pl.BlockSpec((P_BN, D_MODEL), lambda j, i: (j, 0)),
        pl.BlockSpec((GATE_COLS, D_MODEL), lambda j, i: ((j + 1) * (P_BN // GATE_COLS), 0)),
    ]
    out_specs = [pl.BlockSpec((bm, P_BN), lambda j, i: (i, j))]
    out_shape = [jax.ShapeDtypeStruct((m, n_tiles * P_BN), BF16)]
    scratch = [pltpu.VMEM((P_BN, D_MODEL), BF16)]
    operands = [h, w_in_t, w_in_t]
    if side is not None:
        w_gate, w_up, c_t, w_mod, b_mod, first_col = side
        n_steps = n_tiles * ni
        assert n_steps >= 2 * D_MODEL // IN_PROJ_CAST_ROWS + 2
        any_spec = pl.BlockSpec(memory_space=pl.ANY)
        n_mod = w_mod.shape[1] - first_col
        n_blocks = n_mod // LATE_MOD_BN
        assert n_mod % LATE_MOD_BN == 0 and first_col % LATE_MOD_BN == 0 and n_blocks <= n_steps
        mod_block = lambda j, i: jnp.minimum(j * ni + i, n_blocks - 1)
        in_specs += [
            any_spec, any_spec,
            pl.BlockSpec((D_MODEL, SUBLANES), lambda j, i: (0, 0)),
            pl.BlockSpec((D_MODEL, LATE_MOD_BN),
                         lambda j, i: (0, first_col // LATE_MOD_BN + mod_block(j, i))),
            pl.BlockSpec((1, LATE_MOD_BN),
                         lambda j, i: (0, first_col // LATE_MOD_BN + mod_block(j, i))),
        ]
        out_specs += [any_spec, any_spec,
                      pl.BlockSpec((SUBLANES, LATE_MOD_BN), lambda j, i: (0, mod_block(j, i)))]
        out_shape += [jax.ShapeDtypeStruct(w_gate.shape, BF16),
                      jax.ShapeDtypeStruct(w_up.shape, BF16),
                      jax.ShapeDtypeStruct((SUBLANES, n_mod), F32)]
        scratch += _cast_scratch(IN_PROJ_CAST_ROWS, FFN_HIDDEN)
        scratch += [pltpu.VMEM((2, D_MODEL, LANES), F32)]
        operands += [w_gate, w_up, c_t, w_mod, b_mod]
        vmem += (IN_PROJ_CAST_ROWS * FFN_HIDDEN * (2 * 4 + 2 * 2)
                 + 2 * D_MODEL * LATE_MOD_BN * 4 + 2 * D_MODEL * LANES * 4)
    return pl.pallas_call(
        functools.partial(_in_proj_kernel, side_jobs=side is not None),
        grid=(n_tiles, ni),
        in_specs=in_specs,
        out_specs=out_specs,
        out_shape=out_shape,
        scratch_shapes=scratch,
        compiler_params=_params(2, vmem),
        name="in_proj",
    )(*operands)


def _gla_log_decay_split(l_ref, wg_ref, bg_ref):
    z = jnp.dot(l_ref[...], wg_ref[...], preferred_element_type=F32) + bg_ref[...]
    log_a = (jnp.minimum(z, 0.0) - jnp.log(1.0 + jnp.exp(-jnp.abs(z)))) * (1.0 / GATE_NORM)
    hi = log_a.astype(BF16)
    lo = (log_a - hi.astype(F32)).astype(BF16)
    return hi, lo


def _gla_chunk_cumsum(pieces, *, reverse, tb):
    r = lax.broadcasted_iota(jnp.int32, (tb, tb), 0)
    cc = lax.broadcasted_iota(jnp.int32, (tb, tb), 1)
    same_chunk = (r // CHUNK) == (cc // CHUNK)
    tri = jnp.where(same_chunk & ((cc >= r) if reverse else (cc <= r)), 1.0, 0.0).astype(BF16)
    hi, lo = pieces
    return (jnp.dot(tri, hi, preferred_element_type=F32)
            + jnp.dot(tri, lo, preferred_element_type=F32)) * LOG2_E


def _gla_kernel(qkv_f, lf, lf_next, qkv_b, lb, lb_next, wgf, bgf, wgb, bgb, s0f, s0b,
                *refs, tb, cast):
    qf, kf, vf = (qkv_f.at[:, :GLA_KW], qkv_f.at[:, GLA_KW:2 * GLA_KW], qkv_f.at[:, 2 * GLA_KW:])
    qb, kb, vb = (qkv_b.at[:, :GLA_KW], qkv_b.at[:, GLA_KW:2 * GLA_KW], qkv_b.at[:, 2 * GLA_KW:])
    i = pl.program_id(0)
    if cast:
        (wd_ref, of, ob, sf, sb, wd_out, s_ref, cum_ref,
         stage_ref, cast_ref, sem_in, sem_out) = refs
        job = _CastJob(((wd_ref, wd_out, FFN_HIDDEN // GLA_CAST_ROWS),), GLA_CAST_ROWS,
                       stage_ref, cast_ref, sem_in, sem_out)
        job.before(i)
    else:
        of, ob, sf, sb, s_ref, cum_ref = refs

    @pl.when(i == 0)
    def _():
        s_ref[0] = s0f[...]
        s_ref[1] = s0b[...]
        cum_ref[0] = _gla_chunk_cumsum(_gla_log_decay_split(lf, wgf, bgf), reverse=False, tb=tb)
        cum_ref[1] = _gla_chunk_cumsum(_gla_log_decay_split(lb, wgb, bgb), reverse=True, tb=tb)

    dirs = ((qf, kf, vf, of, False), (qb, kb, vb, ob, True))
    ii = lax.broadcasted_iota(jnp.int32, (CHUNK, GLA_DK), 0)
    jj = lax.broadcasted_iota(jnp.int32, (CHUNK, GLA_DK), 1)
    nc = tb // CHUNK

    def start(step):
        items = []
        for d, (q_ref, k_ref, v_ref, o_ref, reverse) in enumerate(dirs):
            r0 = (nc - 1 - step if reverse else step) * CHUNK
            cum_c = cum_ref[d, r0:r0 + CHUNK, :]
            last = cum_c[0:1, :] if reverse else cum_c[CHUNK - 1:CHUNK, :]
            q_c = q_ref[r0:r0 + CHUNK, :].astype(F32)
            k_c = k_ref[r0:r0 + CHUNK, :].astype(F32)
            q_e = ((q_c * (GLA_DK ** -0.5)) * jnp.exp2(cum_c)).astype(BF16)
            k_e = (k_c * jnp.exp2(-cum_c)).astype(BF16)
            k_d = (k_c * jnp.exp2(last - cum_c)).astype(BF16)
            decay = jnp.exp2(last)
            mask = ((jj >= ii) & (jj < CHUNK)) if reverse else (jj <= ii)
            for h in range(GLA_HEADS):
                ks = slice(h * GLA_DK, (h + 1) * GLA_DK)
                vs = slice(h * GLA_DV, (h + 1) * GLA_DV)
                items.append(dict(
                    d=d, h=h, r0=r0, vs=vs, o_ref=o_ref, mask=mask,
                    q_e=q_e[:, ks], k_e=k_e[:, ks], k_d=k_d[:, ks], decay=decay[:, ks],
                    v=v_ref[r0:r0 + CHUNK, vs]))
        for it in items:
            k_pad = jnp.concatenate([it["k_e"], jnp.zeros((GLA_DK - CHUNK, GLA_DK), BF16)], axis=0)
            it["scores"] = lax.dot_general(it["q_e"], k_pad, _NT_DIMS,
                                           preferred_element_type=F32)
        return items

    items = start(0)
    next_cum = None
    for step in range(nc):
        following = start(step + 1) if step + 1 < nc else None
        if step == 1:
            next_f = _gla_log_decay_split(lf_next, wgf, bgf)
            next_b = _gla_log_decay_split(lb_next, wgb, bgb)
        if step == nc - 1:
            next_cum = (_gla_chunk_cumsum(next_f, reverse=False, tb=tb),
                        _gla_chunk_cumsum(next_b, reverse=True, tb=tb))
        for it in items:
            scores = jnp.where(it["mask"], it["scores"], 0.0).astype(BF16)
            s_old = s_ref[it["d"], it["h"]]
            lhs = jnp.concatenate([scores, it["q_e"]], axis=1)
            rhs = jnp.concatenate(
                [it["v"], jnp.zeros((GLA_DK - CHUNK, GLA_DV), BF16), s_old.astype(BF16)], axis=0)
            it["o_ref"][it["r0"]:it["r0"] + CHUNK, it["vs"]] = jnp.dot(
                lhs, rhs, preferred_element_type=F32).astype(BF16)
            ds = lax.dot_general(it["k_d"], it["v"], (((0,), (0,)), ((), ())),
                                 preferred_element_type=F32)
            dcol = jnp.transpose(jnp.broadcast_to(it["decay"], (SUBLANES, GLA_DK)))[:, 0:1]
            s_ref[it["d"], it["h"]] = dcol * s_old + ds
        items = following
    cum_ref[0] = next_cum[0]
    cum_ref[1] = next_cum[1]
    if cast:
        job.cast(i)
        job.after(i)

    @pl.when(i == pl.num_programs(0) - 1)
    def _():
        sf[...] = s_ref[0]
        sb[...] = s_ref[1]


def _gla_call(p, l, wgf, bgf, wgb, bgb, s0f, s0b, cast_weight=None, *, tb):
    n = p.shape[0]
    nb = n // tb
    fwd = lambda i: i
    bwd = lambda i: nb - 1 - i
    fwd_next = lambda i: jnp.minimum(i + 1, nb - 1)
    bwd_next = lambda i: jnp.maximum(nb - 2 - i, 0)

    def specs(pos, pos_next):
        return [
            pl.BlockSpec((tb, 2 * GLA_KW + GLA_VW), lambda i: (pos(i), 0)),
            pl.BlockSpec((tb, LANES), lambda i: (pos(i), 0)),
            pl.BlockSpec((tb, LANES), lambda i: (pos_next(i), 0)),
        ]

    const2 = lambda i: (0, 0)
    const3 = lambda i: (0, 0, 0)
    state_shape = (GLA_HEADS, GLA_DK, GLA_DV)
    state_spec = pl.BlockSpec(state_shape, const3)
    cast = cast_weight is not None
    kern = functools.partial(_gla_kernel, tb=tb, cast=cast)
    vmem = (2 * 2 * tb * (2 * GLA_KW + GLA_VW + LANES) * 2 + 2 * 2 * tb * GLA_VW * 2
            + 10 * GLA_HEADS * GLA_DK * GLA_DV * 4 + 24 * tb * GLA_KW * 4 + 2 * SPILL_BYTES)
    in_specs = specs(fwd, fwd_next) + specs(bwd, bwd_next) + [
        pl.BlockSpec((LANES, GLA_KW), const2), pl.BlockSpec((1, GLA_KW), const2),
        pl.BlockSpec((LANES, GLA_KW), const2), pl.BlockSpec((1, GLA_KW), const2),
        state_spec, state_spec,
    ]
    out_specs = [
        pl.BlockSpec((tb, GLA_VW), lambda i: (i, 0)),
        pl.BlockSpec((tb, GLA_VW), lambda i: (nb - 1 - i, 0)),
        state_spec, state_spec,
    ]
    out_shape = [
        jax.ShapeDtypeStruct((n, GLA_VW), BF16), jax.ShapeDtypeStruct((n, GLA_VW), BF16),
        jax.ShapeDtypeStruct(state_shape, F32), jax.ShapeDtypeStruct(state_shape, F32),
    ]
    scratch = [pltpu.VMEM((2,) + state_shape, F32), pltpu.VMEM((2, tb, GLA_KW), F32)]
    operands = [p, l, l, p, l, l, wgf, bgf, wgb, bgb, s0f, s0b]
    if cast:
        assert cast_weight.shape == (FFN_HIDDEN, D_MODEL)
        assert nb >= FFN_HIDDEN // GLA_CAST_ROWS + 2
        any_spec = pl.BlockSpec(memory_space=pl.ANY)
        in_specs.append(any_spec)
        out_specs.append(any_spec)
        out_shape.append(jax.ShapeDtypeStruct(cast_weight.shape, BF16))
        scratch += _cast_scratch(GLA_CAST_ROWS, D_MODEL)
        operands.append(cast_weight)
        vmem += GLA_CAST_ROWS * D_MODEL * (2 * 4 + 2 * 2)
    return pl.pallas_call(
        kern,
        grid=(nb,),
        in_specs=in_specs,
        out_specs=out_specs,
        out_shape=out_shape,
        scratch_shapes=scratch,
        compiler_params=_params(1, vmem),
        name="gla",
    )(*operands)


def _mix_kernel(of_ref, ob_ref, gcb_ref, ccx_ref, x_ref, gt_ref, gg_ref, cw_ref,
                wo_ref, o_ref, m_ref, *, bm):
    g_ref, cb_ref = gcb_ref.at[:, :GLA_VW], gcb_ref.at[:, GLA_VW:]
    cc_ref, cx_ref = ccx_ref.at[:, :CONV_W], ccx_ref.at[:, CONV_W:]

    u = cc_ref[...].astype(F32) * cx_ref[...].astype(F32)
    col = lax.broadcasted_iota(jnp.int32, u.shape, 0) % GRID_W
    u_prev = jnp.where(col == 0, 0.0, pltpu.roll(u, 1, 0))
    u_next = jnp.where(col == GRID_W - 1, 0.0, pltpu.roll(u, bm - 1, 0))
    conv = cw_ref[0:1, :] * u_prev + cw_ref[1:2, :] * u + cw_ref[2:3, :] * u_next
    m_ref[:, GLA_VW:] = (cb_ref[...].astype(F32) * conv).astype(BF16)
    mixed = jnp.dot(m_ref[:, GLA_VW:], wo_ref[GLA_VW:, :].astype(BF16),
                    preferred_element_type=F32)

    gg = gg_ref[...]
    for h in range(GLA_HEADS):
        vs = slice(h * GLA_DV, (h + 1) * GLA_DV)
        o = of_ref[:, vs].astype(F32) + ob_ref[:, vs].astype(F32)
        o32 = o * lax.rsqrt(jnp.mean(o * o, axis=-1, keepdims=True) + EPS)
        m_ref[:, vs] = ((o32 * gg) * _silu(g_ref[:, vs].astype(F32))).astype(BF16)
        mixed = mixed + jnp.dot(m_ref[:, vs], wo_ref[vs, :].astype(BF16),
                                preferred_element_type=F32)
    o_ref[...] = x_ref[...] + gt_ref[0:1, :] * mixed


def _mix_call(o_f, o_b, p, x, mod, gla_g, conv_w, w_out, *, bm):
    n = x.shape[0]
    kern = functools.partial(_mix_kernel, bm=bm)
    assert GLA_VW == CONV_W
    pcol = lambda k: pl.BlockSpec((bm, 2 * CONV_W), lambda i, k=k: (i, k))
    const2 = lambda i: (0, 0)
    vmem = (2 * 6 * bm * GLA_VW * 2 + 2 * 2 * bm * D_MODEL * 4
            + D_MODEL * D_MODEL * 4 + bm * D_MODEL * 2 + 8 * bm * GLA_VW * 4
            + SPILL_BYTES)
    return pl.pallas_call(
        kern,
        grid=(n // bm,),
        in_specs=[
            pl.BlockSpec((bm, GLA_VW), lambda i: (i, 0)),
            pl.BlockSpec((bm, GLA_VW), lambda i: (i, 0)),
            pcol(1), pcol(2),
            pl.BlockSpec((bm, D_MODEL), lambda i: (i, 0)),
            pl.BlockSpec((SUBLANES, D_MODEL), lambda i: (0, 0)),
            pl.BlockSpec((1, GLA_DV), const2),
            pl.BlockSpec((3, CONV_W), const2),
            pl.BlockSpec((D_MODEL, D_MODEL), const2, pipeline_mode=pl.Buffered(1)),
        ],
        out_specs=pl.BlockSpec((bm, D_MODEL), lambda i: (i, 0)),
        out_shape=jax.ShapeDtypeStruct((n, D_MODEL), F32),
        scratch_shapes=[pltpu.VMEM((bm, D_MODEL), BF16)],
        compiler_params=_params(1, vmem),
        name="mix",
    )(o_f, o_b, p, p, x, mod, gla_g, conv_w, w_out)


def _ffn_kernel(x_ref, g_ref, sh_ref, sc_ref, gt_ref, wg_ref, wu_ref, wd_ref, fg_ref,
                o_ref, h_ref, *, bm):
    j = pl.program_id(1)

    @pl.when(j == 0)
    def _():
        _norm_mod_rows(x_ref, g_ref[...], 1.0 + sc_ref[0:1, :], sh_ref[0:1, :], h_ref, bm,
                       copy_ref=o_ref)

    h = h_ref[...]
    gate = jnp.dot(h, wg_ref[...], preferred_element_type=F32)
    up = jnp.dot(h, wu_ref[...], preferred_element_type=F32)
    part = jnp.dot((_silu(gate) * up).astype(BF16), wd_ref[...], preferred_element_type=F32)
    o_ref[...] += gt_ref[0:1, :] * part

    @pl.when(j == pl.num_programs(1) - 1)
    def _():
        fg = fg_ref[...]

        def final_norm(rs):
            xn = o_ref[rs, :]
            ms = jnp.mean(xn * xn, axis=-1, keepdims=True)
            o_ref[rs, :] = (xn * lax.rsqrt(ms + EPS)) * fg
        _for_row_chunks(bm, final_norm)


def _ffn_call(x, g, mod, wg, wu, wd, fg, *, bm):
    n = x.shape[0]
    kern = functools.partial(_ffn_kernel, bm=bm)
    modspec = lambda k: pl.BlockSpec((SUBLANES, D_MODEL), lambda i, j, k=k: (0, k))
    vmem = (3 * bm * D_MODEL * 4 + bm * D_MODEL * 2 + 2 * 3 * D_MODEL * FFN_BH * 2
            + 4 * bm * FFN_BH * 4 + bm * D_MODEL * 4 + SPILL_BYTES)
    return pl.pallas_call(
        kern,
        grid=(n // bm, FFN_HIDDEN // FFN_BH),
        in_specs=[
            pl.BlockSpec((bm, D_MODEL), lambda i, j: (i, 0)),
            pl.BlockSpec((1, D_MODEL), lambda i, j: (0, 0)),
            modspec(1), modspec(2), modspec(3),
            pl.BlockSpec((D_MODEL, FFN_BH), lambda i, j: (0, j)),
            pl.BlockSpec((D_MODEL, FFN_BH), lambda i, j: (0, j)),
            pl.BlockSpec((FFN_BH, D_MODEL), lambda i, j: (j, 0)),
            pl.BlockSpec((1, D_MODEL), lambda i, j: (0, 0)),
        ],
        out_specs=pl.BlockSpec((bm, D_MODEL), lambda i, j: (i, 0)),
        out_shape=jax.ShapeDtypeStruct((n, D_MODEL), F32),
        scratch_shapes=[pltpu.VMEM((bm, D_MODEL), BF16)],
        compiler_params=_params(2, vmem),
        name="ffn",
    )(x, g, mod, mod, mod, wg, wu, wd, fg)


def kernel(x, c, ctx, c_ctx, w_mod, b_mod, norm1_g, norm2_g, w_in, w_gate_f, b_gate_f,
           w_gate_b, b_gate_b, gla_norm_g, conv_w, w_out, w_ffn_gate, w_ffn_up, w_ffn_down,
           final_g):
    assert x.shape[0] == 1 and w_mod.shape[0] == 1, "batch 1, depth 1 only"
    seq, ctx_len = x.shape[1], ctx.shape[1]
    x2, ctx2 = x[0], ctx[0]

    c_t = jnp.concatenate(
        [c[0][:, None], c_ctx[:, None], jnp.zeros((D_MODEL, SUBLANES - 2), F32)], axis=1)
    mod = _mod_call(c_t, w_mod[0], b_mod, EARLY_MOD_COLS)

    pad_f = jnp.zeros((LANES, GLA_KW), F32).at[:GATE_RANK].set(w_gate_f[0]).astype(BF16)
    pad_b = jnp.zeros((LANES, GLA_KW), F32).at[GATE_RANK:2 * GATE_RANK].set(w_gate_b[0]).astype(BF16)
    g1 = norm1_g[0][None]
    w_in_t = jnp.transpose(w_in[0])

    h_c, l_c = _prenorm_call(ctx2, g1, mod, w_in_t, row=1, bm=ctx_len)
    p_c, = _in_proj_call(h_c, w_in_t, bm=ctx_len, n_tiles=2)
    zero_state = jnp.zeros((GLA_HEADS, GLA_DK, GLA_DV), F32)
    _, _, s_f, s_b = _gla_call(p_c, l_c, pad_f, b_gate_f, pad_b, b_gate_b, zero_state,
                               zero_state, tb=ctx_len)

    h_x, l_x = _prenorm_call(x2, g1, mod, w_in_t, row=0, bm=2048)
    p_x, wg_b, wu_b, mod_late = _in_proj_call(
        h_x, w_in_t, (w_ffn_gate[0], w_ffn_up[0], c_t, w_mod[0], b_mod, EARLY_MOD_COLS),
        bm=1024, n_tiles=P_COLS // P_BN)
    o_f, o_b, _, _, wd_b = _gla_call(p_x, l_x, pad_f, b_gate_f, pad_b, b_gate_b, s_f, s_b,
                                     w_ffn_down[0], tb=256)
    x_mid = _mix_call(o_f, o_b, p_x, x2, mod_late, gla_norm_g, conv_w[0], w_out[0], bm=512)
    out = _ffn_call(x_mid, norm2_g[0][None], mod_late, wg_b, wu_b, wd_b, final_g[None],
                    bm=1024)
    return out[None]
```

```python
import functools

import jax
import jax.numpy as jnp
from jax import lax
from jax.experimental import pallas as pl
from jax.experimental.pallas import tpu as pltpu

F32 = jnp.float32
BF16 = jnp.bfloat16

D_MODEL = 2048
GRID_W = 64
GLA_HEADS = 4
GLA_DK = 128
GLA_DV = 256
GLA_KW = GLA_HEADS * GLA_DK
GLA_VW = GLA_HEADS * GLA_DV
GATE_RANK = 16
GATE_NORM = 16.0
CHUNK = 64
LOG2_E = 1.4426950408889634
CONV_W = D_MODEL - GLA_VW
FFN_HIDDEN = 5632
EPS = 1e-6

LANES = 128
SUBLANES = 8
MIB = 1024 * 1024
VMEM_BYTES = 64 * MIB
VMEM_RESERVED = 6 * MIB
SPILL_BYTES = 4 * MIB

GATE_COL = 2 * GLA_KW + 2 * GLA_VW
GATE_COLS = 2 * GATE_RANK
P_COLS = GATE_COL + 3 * CONV_W
P_BN = 1024
P_SHIFT_FROM = GATE_COL // P_BN

_NT_DIMS = (((1,), (1,)), ((), ()))

MOD_BN = 2048
EARLY_MOD_COLS = 2 * D_MODEL
LATE_MOD_BN = 256
FFN_BH = 512
NORM_ROWS = 16


def _params(grid_rank, vmem_bytes):
    return pltpu.CompilerParams(
        dimension_semantics=("arbitrary",) * grid_rank,
        vmem_limit_bytes=min(int(vmem_bytes), VMEM_BYTES - VMEM_RESERVED),
    )


def _silu(v):
    return v * jax.nn.sigmoid(v)


def _for_row_chunks(rows, body):
    for r0 in range(0, rows, NORM_ROWS):
        body(slice(r0, r0 + NORM_ROWS))


def _norm_mod_rows(x_ref, g, mul, sh, dst_ref, rows, copy_ref=None):
    gain = g * mul

    def body(rs):
        xs = x_ref[rs, :]
        ms = jnp.mean(xs * xs, axis=-1, keepdims=True)
        y = xs * lax.rsqrt(ms + EPS)
        dst_ref[rs, :] = (y * gain + sh).astype(BF16)
        if copy_ref is not None:
            copy_ref[rs, :] = xs
    _for_row_chunks(rows, body)


def _mod_prepare(c_ref, sb_ref):
    s = _silu(c_ref[...])
    for r in range(2):
        sb_ref[r] = jnp.broadcast_to(s[:, r:r + 1], (D_MODEL, LANES))


def _mod_matvec(w_ref, b_ref, sb_ref, o_ref, *, unrolled):
    nt = w_ref.shape[1] // LANES

    def body(kk, accs):
        r0 = kk * SUBLANES if unrolled else pl.multiple_of(kk * SUBLANES, SUBLANES)
        s0 = sb_ref[0, pl.ds(r0, SUBLANES), :]
        s1 = sb_ref[1, pl.ds(r0, SUBLANES), :]
        new = []
        for t in range(nt):
            w8 = w_ref[pl.ds(r0, SUBLANES), t * LANES:(t + 1) * LANES]
            new.append(accs[2 * t] + w8 * s0)
            new.append(accs[2 * t + 1] + w8 * s1)
        return tuple(new)

    accs = (jnp.zeros((SUBLANES, LANES), F32),) * (2 * nt)
    if unrolled:
        for kk in range(D_MODEL // SUBLANES):
            accs = body(kk, accs)
    else:
        accs = lax.fori_loop(0, D_MODEL // SUBLANES, body, accs, unroll=8)
    o_ref[...] = jnp.zeros(o_ref.shape, F32)
    for t in range(nt):
        b = b_ref[:, t * LANES:(t + 1) * LANES]
        for r in range(2):
            o_ref[r:r + 1, t * LANES:(t + 1) * LANES] = (
                jnp.sum(accs[2 * t + r], axis=0, keepdims=True) + b)


def _mod_kernel(c_ref, w_ref, b_ref, o_ref, sb_ref):
    @pl.when(pl.program_id(0) == 0)
    def _():
        _mod_prepare(c_ref, sb_ref)

    _mod_matvec(w_ref, b_ref, sb_ref, o_ref, unrolled=False)


def _mod_call(c_t, w_mod, b_mod, n):
    return pl.pallas_call(
        _mod_kernel,
        grid=(n // MOD_BN,),
        in_specs=[
            pl.BlockSpec((D_MODEL, SUBLANES), lambda j: (0, 0)),
            pl.BlockSpec((D_MODEL, MOD_BN), lambda j: (0, j)),
            pl.BlockSpec((1, MOD_BN), lambda j: (0, j)),
        ],
        out_specs=pl.BlockSpec((SUBLANES, MOD_BN), lambda j: (0, j)),
        out_shape=jax.ShapeDtypeStruct((SUBLANES, n), F32),
        scratch_shapes=[pltpu.VMEM((2, D_MODEL, LANES), F32)],
        compiler_params=_params(1, 2 * D_MODEL * MOD_BN * 4 + 2 * SPILL_BYTES),
        name="mod",
    )(c_t, w_mod, b_mod)


def _prenorm_kernel(x_ref, g_ref, sh_ref, sc_ref, wl_ref, h_ref, l_ref, *, row, bm):
    _norm_mod_rows(x_ref, g_ref[...], 1.0 + sc_ref[row:row + 1, :], sh_ref[row:row + 1, :],
                   h_ref, bm)
    l_ref[...] = lax.dot_general(h_ref[...], wl_ref[...].astype(BF16), _NT_DIMS,
                                 preferred_element_type=F32).astype(BF16)


def _prenorm_call(x, g, mod, w_in_t, *, row, bm):
    m = x.shape[0]
    kern = functools.partial(_prenorm_kernel, row=row, bm=bm)
    vmem = (2 * bm * D_MODEL * (4 + 2) + 4 * D_MODEL * LANES * 4 + 8 * NORM_ROWS * D_MODEL * 4
            + SPILL_BYTES)
    return pl.pallas_call(
        kern,
        grid=(m // bm,),
        in_specs=[
            pl.BlockSpec((bm, D_MODEL), lambda i: (i, 0)),
            pl.BlockSpec((1, D_MODEL), lambda i: (0, 0)),
            pl.BlockSpec((SUBLANES, D_MODEL), lambda i: (0, 0)),
            pl.BlockSpec((SUBLANES, D_MODEL), lambda i: (0, 1)),
            pl.BlockSpec((LANES, D_MODEL), lambda i: (GATE_COL // LANES, 0)),
        ],
        out_specs=[pl.BlockSpec((bm, D_MODEL), lambda i: (i, 0)),
                   pl.BlockSpec((bm, LANES), lambda i: (i, 0))],
        out_shape=[jax.ShapeDtypeStruct((m, D_MODEL), BF16),
                   jax.ShapeDtypeStruct((m, LANES), BF16)],
        compiler_params=_params(1, vmem),
        name="prenorm",
    )(x, g, mod, mod, w_in_t)


W_ROWS = 256


class _CastJob:
    def __init__(self, sources, rows, stage_ref, out_ref, sem_in, sem_out, tile_cols=None):
        self.sources, self.rows, self.tile_cols = sources, rows, tile_cols
        self.stage_ref, self.out_ref, self.sem_in, self.sem_out = stage_ref, out_ref, sem_in, sem_out

    def _for_block(self, c, fn):
        first = 0
        for src, dst, n, col0 in self.sources:
            @pl.when((c >= first) & (c < first + n))
            def _(src=src, dst=dst, first=first, col0=col0):
                fn(src, dst, col0, pl.multiple_of((c - first) * self.rows, self.rows),
                   lax.rem(c, 2))
            first += n

    def _fetch(self, src, dst, col0, r, slot):
        return pltpu.make_async_copy(src.at[pl.ds(r, self.rows), :], self.stage_ref.at[slot],
                                     self.sem_in.at[slot])

    def _store(self, src, dst, col0, r, slot):
        if self.tile_cols is None:
            window = dst.at[pl.ds(r, self.rows), :]
        else:
            window = dst.at[:, pl.ds(r, self.rows), col0:col0 + self.tile_cols]
        return pltpu.make_async_copy(self.out_ref.at[slot], window, self.sem_out.at[slot])

    def before(self, step):
        @pl.when(step == 0)
        def _():
            self._for_block(step, lambda *a: self._fetch(*a).start())
        self._for_block(step, lambda *a: self._fetch(*a).wait())
        self._for_block(step + 1, lambda *a: self._fetch(*a).start())
        self._for_block(step - 2, lambda *a: self._store(*a).wait())

    def cast(self, step):
        slot = lax.rem(step, 2)
        if self.tile_cols is None:
            self.out_ref[slot] = self.stage_ref[slot].astype(BF16)
        else:
            for t in range(self.out_ref.shape[1]):
                cols = slice(t * self.tile_cols, (t + 1) * self.tile_cols)
                self.out_ref[slot, t] = self.stage_ref[slot, :, cols].astype(BF16)

    def after(self, step):
        self._for_block(step, lambda *a: self._store(*a).start())


def _cast_scratch(rows, cols, tile_cols=None):
    out_shape = (2, rows, cols) if tile_cols is None else (2, cols // tile_cols, rows, tile_cols)
    return [pltpu.VMEM((2, rows, cols), F32), pltpu.VMEM(out_shape, BF16),
            pltpu.SemaphoreType.DMA((2,)), pltpu.SemaphoreType.DMA((2,))]


IN_PROJ_CAST_ROWS = 128
GLA_CAST_ROWS = 256


def _in_proj_kernel(h_ref, wa_ref, wb_ref, *refs, side_jobs):
    j = pl.program_id(0)
    if side_jobs:
        (wg_ref, wu_ref, c_ref, wm_ref, bm_ref, o_ref, wgu_out, mod_ref,
         w_ref, stage_ref, cast_ref, sem_in, sem_out, sb_ref) = refs
        n = D_MODEL // IN_PROJ_CAST_ROWS
        job = _CastJob(((wg_ref, wgu_out, n, 0), (wu_ref, wgu_out, n, FFN_BH)),
                       IN_PROJ_CAST_ROWS, stage_ref, cast_ref, sem_in, sem_out, tile_cols=FFN_BH)
        step = j * pl.num_programs(1) + pl.program_id(1)
        job.before(step)

        @pl.when(step == 0)
        def _():
            _mod_prepare(c_ref, sb_ref)
    else:
        o_ref, w_ref = refs

    @pl.when(pl.program_id(1) == 0)
    def _():
        @pl.when(j < P_SHIFT_FROM)
        def _():
            for r0 in range(0, P_BN, W_ROWS):
                w_ref[r0:r0 + W_ROWS, :] = wa_ref[r0:r0 + W_ROWS, :].astype(BF16)

        @pl.when(j >= P_SHIFT_FROM)
        def _():
            keep = P_BN - GATE_COLS
            for r0 in range(0, keep, W_ROWS):
                r1 = min(r0 + W_ROWS, keep)
                w_ref[r0:r1, :] = wa_ref[GATE_COLS + r0:GATE_COLS + r1, :].astype(BF16)
            w_ref[keep:, :] = wb_ref[...].astype(BF16)

    o_ref[...] = lax.dot_general(h_ref[...], w_ref[...], _NT_DIMS,
                                 preferred_element_type=F32).astype(BF16)
    if side_jobs:
        job.cast(step)
        _mod_matvec(wm_ref, bm_ref, sb_ref, mod_ref, unrolled=True)
        job.after(step)


def _in_proj_call(h, w_in_t, side=None, *, bm, n_tiles):
    m = h.shape[0]
    ni = m // bm
    vmem = (2 * bm * D_MODEL * 2 + 2 * (P_BN + GATE_COLS) * D_MODEL * 4 + P_BN * D_MODEL * 2
            + 2 * bm * P_BN * 2 + bm * P_BN * 4 + 4 * W_ROWS * D_MODEL * 4 + SPILL_BYTES)
    in_specs = [
        pl.BlockSpec((bm, D_MODEL), lambda j, i: (i, 0)),
        pl.BlockSpec((P_BN, D_MODEL), lambda j, i: (j, 0)),
        pl.BlockSpec((GATE_COLS, D_MODEL), lambda j, i: ((j + 1) * (P_BN // GATE_COLS), 0)),
    ]
    out_specs = [pl.BlockSpec((bm, P_BN), lambda j, i: (i, j))]
    out_shape = [jax.ShapeDtypeStruct((m, n_tiles * P_BN), BF16)]
    scratch = [pltpu.VMEM((P_BN, D_MODEL), BF16)]
    operands = [h, w_in_t, w_in_t]
    if side is not None:
        w_gate, w_up, c_t, w_mod, b_mod, first_col = side
        n_steps = n_tiles * ni
        assert n_steps >= 2 * D_MODEL // IN_PROJ_CAST_ROWS + 2
        any_spec = pl.BlockSpec(memory_space=pl.ANY)
        n_mod = w_mod.shape[1] - first_col
        n_blocks = n_mod // LATE_MOD_BN
        assert n_mod % LATE_MOD_BN == 0 and first_col % LATE_MOD_BN == 0 and n_blocks <= n_steps
        mod_block = lambda j, i: jnp.minimum(j * ni + i, n_blocks - 1)
        in_specs += [
            any_spec, any_spec,
            pl.BlockSpec((D_MODEL, SUBLANES), lambda j, i: (0, 0)),
            pl.BlockSpec((D_MODEL, LATE_MOD_BN),
                         lambda j, i: (0, first_col // LATE_MOD_BN + mod_block(j, i))),
            pl.BlockSpec((1, LATE_MOD_BN),
                         lambda j, i: (0, first_col // LATE_MOD_BN + mod_block(j, i))),
        ]
        out_specs += [any_spec,
                      pl.BlockSpec((SUBLANES, LATE_MOD_BN), lambda j, i: (0, mod_block(j, i)))]
        assert w_gate.shape == w_up.shape == (D_MODEL, FFN_HIDDEN)
        out_shape += [jax.ShapeDtypeStruct((FFN_HIDDEN // FFN_BH, D_MODEL, 2 * FFN_BH), BF16),
                      jax.ShapeDtypeStruct((SUBLANES, n_mod), F32)]
        scratch += _cast_scratch(IN_PROJ_CAST_ROWS, FFN_HIDDEN, tile_cols=FFN_BH)
        scratch += [pltpu.VMEM((2, D_MODEL, LANES), F32)]
        operands += [w_gate, w_up, c_t, w_mod, b_mod]
        vmem += (IN_PROJ_CAST_ROWS * FFN_HIDDEN * (2 * 4 + 2 * 2)
                 + 2 * D_MODEL * LATE_MOD_BN * 4 + 2 * D_MODEL * LANES * 4)
    return pl.pallas_call(
        functools.partial(_in_proj_kernel, side_jobs=side is not None),
        grid=(n_tiles, ni),
        in_specs=in_specs,
        out_specs=out_specs,
        out_shape=out_shape,
        scratch_shapes=scratch,
        compiler_params=_params(2, vmem),
        name="in_proj",
    )(*operands)


def _gla_log_decay_split(l_ref, wg_ref, bg_ref):
    z = jnp.dot(l_ref[...], wg_ref[...], preferred_element_type=F32) + bg_ref[...]
    log_a = (jnp.minimum(z, 0.0) - jnp.log(1.0 + jnp.exp(-jnp.abs(z)))) * (1.0 / GATE_NORM)
    hi = log_a.astype(BF16)
    lo = (log_a - hi.astype(F32)).astype(BF16)
    return hi, lo


def _gla_chunk_cumsum(pieces, *, reverse, tb):
    r = lax.broadcasted_iota(jnp.int32, (tb, tb), 0)
    cc = lax.broadcasted_iota(jnp.int32, (tb, tb), 1)
    same_chunk = (r // CHUNK) == (cc // CHUNK)
    tri = jnp.where(same_chunk & ((cc >= r) if reverse else (cc <= r)), 1.0, 0.0).astype(BF16)
    hi, lo = pieces
    return (jnp.dot(tri, hi, preferred_element_type=F32)
            + jnp.dot(tri, lo, preferred_element_type=F32)) * LOG2_E


def _gla_kernel(qkv_f, lf, lf_next, qkv_b, lb, lb_next, wgf, bgf, wgb, bgb, s0f, s0b,
                *refs, tb, cast):
    qf, kf, vf = (qkv_f.at[:, :GLA_KW], qkv_f.at[:, GLA_KW:2 * GLA_KW], qkv_f.at[:, 2 * GLA_KW:])
    qb, kb, vb = (qkv_b.at[:, :GLA_KW], qkv_b.at[:, GLA_KW:2 * GLA_KW], qkv_b.at[:, 2 * GLA_KW:])
    i = pl.program_id(0)
    if cast:
        (wd_ref, of, ob, sf, sb, wd_out, s_ref, cum_ref,
         stage_ref, cast_ref, sem_in, sem_out) = refs
        job = _CastJob(((wd_ref, wd_out, FFN_HIDDEN // GLA_CAST_ROWS, 0),), GLA_CAST_ROWS,
                       stage_ref, cast_ref, sem_in, sem_out)
        job.before(i)
    else:
        of, ob, sf, sb, s_ref, cum_ref = refs

    @pl.when(i == 0)
    def _():
        s_ref[0] = s0f[...]
        s_ref[1] = s0b[...]
        cum_ref[0] = _gla_chunk_cumsum(_gla_log_decay_split(lf, wgf, bgf), reverse=False, tb=tb)
        cum_ref[1] = _gla_chunk_cumsum(_gla_log_decay_split(lb, wgb, bgb), reverse=True, tb=tb)

    dirs = ((qf, kf, vf, of, False), (qb, kb, vb, ob, True))
    ii = lax.broadcasted_iota(jnp.int32, (CHUNK, GLA_DK), 0)
    jj = lax.broadcasted_iota(jnp.int32, (CHUNK, GLA_DK), 1)
    nc = tb // CHUNK

    def start(step):
        items = []
        for d, (q_ref, k_ref, v_ref, o_ref, reverse) in enumerate(dirs):
            r0 = (nc - 1 - step if reverse else step) * CHUNK
            cum_c = cum_ref[d, r0:r0 + CHUNK, :]
            last = cum_c[0:1, :] if reverse else cum_c[CHUNK - 1:CHUNK, :]
            q_c = q_ref[r0:r0 + CHUNK, :].astype(F32)
            k_c = k_ref[r0:r0 + CHUNK, :].astype(F32)
            q_e = ((q_c * (GLA_DK ** -0.5)) * jnp.exp2(cum_c)).astype(BF16)
            k_e = (k_c * jnp.exp2(-cum_c)).astype(BF16)
            k_d = (k_c * jnp.exp2(last - cum_c)).astype(BF16)
            decay = jnp.exp2(last)
            mask = ((jj >= ii) & (jj < CHUNK)) if reverse else (jj <= ii)
            for h in range(GLA_HEADS):
                ks = slice(h * GLA_DK, (h + 1) * GLA_DK)
                vs = slice(h * GLA_DV, (h + 1) * GLA_DV)
                items.append(dict(
                    d=d, h=h, r0=r0, vs=vs, o_ref=o_ref, mask=mask,
                    q_e=q_e[:, ks], k_e=k_e[:, ks], k_d=k_d[:, ks], decay=decay[:, ks],
                    v=v_ref[r0:r0 + CHUNK, vs]))
        for it in items:
            k_pad = jnp.concatenate([it["k_e"], jnp.zeros((GLA_DK - CHUNK, GLA_DK), BF16)], axis=0)
            it["scores"] = lax.dot_general(it["q_e"], k_pad, _NT_DIMS,
                                           preferred_element_type=F32)
        return items

    items = start(0)
    next_cum = None
    for step in range(nc):
        following = start(step + 1) if step + 1 < nc else None
        if step == 1:
            next_f = _gla_log_decay_split(lf_next, wgf, bgf)
            next_b = _gla_log_decay_split(lb_next, wgb, bgb)
        if step == nc - 1:
            next_cum = (_gla_chunk_cumsum(next_f, reverse=False, tb=tb),
                        _gla_chunk_cumsum(next_b, reverse=True, tb=tb))
        for it in items:
            scores = jnp.where(it["mask"], it["scores"], 0.0).astype(BF16)
            s_old = s_ref[it["d"], it["h"]]
            lhs = jnp.concatenate([scores, it["q_e"]], axis=1)
            rhs = jnp.concatenate(
                [it["v"], jnp.zeros((GLA_DK - CHUNK, GLA_DV), BF16), s_old.astype(BF16)], axis=0)
            it["o_ref"][it["r0"]:it["r0"] + CHUNK, it["vs"]] = jnp.dot(
                lhs, rhs, preferred_element_type=F32).astype(BF16)
            ds = lax.dot_general(it["k_d"], it["v"], (((0,), (0,)), ((), ())),
                                 preferred_element_type=F32)
            dcol = jnp.transpose(jnp.broadcast_to(it["decay"], (SUBLANES, GLA_DK)))[:, 0:1]
            s_ref[it["d"], it["h"]] = dcol * s_old + ds
        items = following
    cum_ref[0] = next_cum[0]
    cum_ref[1] = next_cum[1]
    if cast:
        job.cast(i)
        job.after(i)

    @pl.when(i == pl.num_programs(0) - 1)
    def _():
        sf[...] = s_ref[0]
        sb[...] = s_ref[1]


def _gla_call(p, l, wgf, bgf, wgb, bgb, s0f, s0b, cast_weight=None, *, tb):
    n = p.shape[0]
    nb = n // tb
    fwd = lambda i: i
    bwd = lambda i: nb - 1 - i
    fwd_next = lambda i: jnp.minimum(i + 1, nb - 1)
    bwd_next = lambda i: jnp.maximum(nb - 2 - i, 0)

    def specs(pos, pos_next):
        return [
            pl.BlockSpec((tb, 2 * GLA_KW + GLA_VW), lambda i: (pos(i), 0)),
            pl.BlockSpec((tb, LANES), lambda i: (pos(i), 0)),
            pl.BlockSpec((tb, LANES), lambda i: (pos_next(i), 0)),
        ]

    const2 = lambda i: (0, 0)
    const3 = lambda i: (0, 0, 0)
    state_shape = (GLA_HEADS, GLA_DK, GLA_DV)
    state_spec = pl.BlockSpec(state_shape, const3)
    cast = cast_weight is not None
    kern = functools.partial(_gla_kernel, tb=tb, cast=cast)
    vmem = (2 * 2 * tb * (2 * GLA_KW + GLA_VW + LANES) * 2 + 2 * 2 * tb * GLA_VW * 2
            + 10 * GLA_HEADS * GLA_DK * GLA_DV * 4 + 24 * tb * GLA_KW * 4 + 2 * SPILL_BYTES)
    in_specs = specs(fwd, fwd_next) + specs(bwd, bwd_next) + [
        pl.BlockSpec((LANES, GLA_KW), const2), pl.BlockSpec((1, GLA_KW), const2),
        pl.BlockSpec((LANES, GLA_KW), const2), pl.BlockSpec((1, GLA_KW), const2),
        state_spec, state_spec,
    ]
    out_specs = [
        pl.BlockSpec((tb, GLA_VW), lambda i: (i, 0)),
        pl.BlockSpec((tb, GLA_VW), lambda i: (nb - 1 - i, 0)),
        state_spec, state_spec,
    ]
    out_shape = [
        jax.ShapeDtypeStruct((n, GLA_VW), BF16), jax.ShapeDtypeStruct((n, GLA_VW), BF16),
        jax.ShapeDtypeStruct(state_shape, F32), jax.ShapeDtypeStruct(state_shape, F32),
    ]
    scratch = [pltpu.VMEM((2,) + state_shape, F32), pltpu.VMEM((2, tb, GLA_KW), F32)]
    operands = [p, l, l, p, l, l, wgf, bgf, wgb, bgb, s0f, s0b]
    if cast:
        assert cast_weight.shape == (FFN_HIDDEN, D_MODEL)
        assert nb >= FFN_HIDDEN // GLA_CAST_ROWS + 2
        any_spec = pl.BlockSpec(memory_space=pl.ANY)
        in_specs.append(any_spec)
        out_specs.append(any_spec)
        out_shape.append(jax.ShapeDtypeStruct(cast_weight.shape, BF16))
        scratch += _cast_scratch(GLA_CAST_ROWS, D_MODEL)
        operands.append(cast_weight)
        vmem += GLA_CAST_ROWS * D_MODEL * (2 * 4 + 2 * 2)
    return pl.pallas_call(
        kern,
        grid=(nb,),
        in_specs=in_specs,
        out_specs=out_specs,
        out_shape=out_shape,
        scratch_shapes=scratch,
        compiler_params=_params(1, vmem),
        name="gla",
    )(*operands)


def _mix_kernel(of_ref, ob_ref, gcb_ref, ccx_ref, x_ref, gt_ref, gg_ref, cw_ref,
                wo_ref, o_ref, m_ref, *, bm):
    g_ref, cb_ref = gcb_ref.at[:, :GLA_VW], gcb_ref.at[:, GLA_VW:]
    cc_ref, cx_ref = ccx_ref.at[:, :CONV_W], ccx_ref.at[:, CONV_W:]

    u = cc_ref[...].astype(F32) * cx_ref[...].astype(F32)
    col = lax.broadcasted_iota(jnp.int32, u.shape, 0) % GRID_W
    u_prev = jnp.where(col == 0, 0.0, pltpu.roll(u, 1, 0))
    u_next = jnp.where(col == GRID_W - 1, 0.0, pltpu.roll(u, bm - 1, 0))
    conv = cw_ref[0:1, :] * u_prev + cw_ref[1:2, :] * u + cw_ref[2:3, :] * u_next
    m_ref[:, GLA_VW:] = (cb_ref[...].astype(F32) * conv).astype(BF16)
    mixed = jnp.dot(m_ref[:, GLA_VW:], wo_ref[GLA_VW:, :].astype(BF16),
                    preferred_element_type=F32)

    gg = gg_ref[...]
    for h in range(GLA_HEADS):
        vs = slice(h * GLA_DV, (h + 1) * GLA_DV)
        o = of_ref[:, vs].astype(F32) + ob_ref[:, vs].astype(F32)
        o32 = o * lax.rsqrt(jnp.mean(o * o, axis=-1, keepdims=True) + EPS)
        m_ref[:, vs] = ((o32 * gg) * _silu(g_ref[:, vs].astype(F32))).astype(BF16)
        mixed = mixed + jnp.dot(m_ref[:, vs], wo_ref[vs, :].astype(BF16),
                                preferred_element_type=F32)
    o_ref[...] = x_ref[...] + gt_ref[0:1, :] * mixed


def _mix_call(o_f, o_b, p, x, mod, gla_g, conv_w, w_out, *, bm):
    n = x.shape[0]
    kern = functools.partial(_mix_kernel, bm=bm)
    assert GLA_VW == CONV_W
    pcol = lambda k: pl.BlockSpec((bm, 2 * CONV_W), lambda i, k=k: (i, k))
    const2 = lambda i: (0, 0)
    vmem = (2 * 6 * bm * GLA_VW * 2 + 2 * 2 * bm * D_MODEL * 4
            + D_MODEL * D_MODEL * 4 + bm * D_MODEL * 2 + 8 * bm * GLA_VW * 4
            + SPILL_BYTES)
    return pl.pallas_call(
        kern,
        grid=(n // bm,),
        in_specs=[
            pl.BlockSpec((bm, GLA_VW), lambda i: (i, 0)),
            pl.BlockSpec((bm, GLA_VW), lambda i: (i, 0)),
            pcol(1), pcol(2),
            pl.BlockSpec((bm, D_MODEL), lambda i: (i, 0)),
            pl.BlockSpec((SUBLANES, D_MODEL), lambda i: (0, 0)),
            pl.BlockSpec((1, GLA_DV), const2),
            pl.BlockSpec((3, CONV_W), const2),
            pl.BlockSpec((D_MODEL, D_MODEL), const2, pipeline_mode=pl.Buffered(1)),
        ],
        out_specs=pl.BlockSpec((bm, D_MODEL), lambda i: (i, 0)),
        out_shape=jax.ShapeDtypeStruct((n, D_MODEL), F32),
        scratch_shapes=[pltpu.VMEM((bm, D_MODEL), BF16)],
        compiler_params=_params(1, vmem),
        name="mix",
    )(o_f, o_b, p, p, x, mod, gla_g, conv_w, w_out)


def _ffn_kernel(x_ref, g_ref, sh_ref, sc_ref, gt_ref, wgu_ref, wd_ref, fg_ref,
                o_ref, h_ref, *, bm):
    j = pl.program_id(1)

    @pl.when(j == 0)
    def _():
        _norm_mod_rows(x_ref, g_ref[...], 1.0 + sc_ref[0:1, :], sh_ref[0:1, :], h_ref, bm,
                       copy_ref=o_ref)

    gate_up = jnp.dot(h_ref[...], wgu_ref[0], preferred_element_type=F32)
    act = (_silu(gate_up[:, :FFN_BH]) * gate_up[:, FFN_BH:]).astype(BF16)
    part = jnp.dot(act, wd_ref[...], preferred_element_type=F32)
    o_ref[...] += gt_ref[0:1, :] * part

    @pl.when(j == pl.num_programs(1) - 1)
    def _():
        fg = fg_ref[...]

        def final_norm(rs):
            xn = o_ref[rs, :]
            ms = jnp.mean(xn * xn, axis=-1, keepdims=True)
            o_ref[rs, :] = (xn * lax.rsqrt(ms + EPS)) * fg
        _for_row_chunks(bm, final_norm)


def _ffn_call(x, g, mod, wgu, wd, fg, *, bm):
    n = x.shape[0]
    kern = functools.partial(_ffn_kernel, bm=bm)
    modspec = lambda k: pl.BlockSpec((SUBLANES, D_MODEL), lambda i, j, k=k: (0, k))
    vmem = (3 * bm * D_MODEL * 4 + bm * D_MODEL * 2 + 2 * 3 * D_MODEL * FFN_BH * 2
            + 4 * bm * FFN_BH * 4 + bm * D_MODEL * 4 + SPILL_BYTES)
    return pl.pallas_call(
        kern,
        grid=(n // bm, FFN_HIDDEN // FFN_BH),
        in_specs=[
            pl.BlockSpec((bm, D_MODEL), lambda i, j: (i, 0)),
            pl.BlockSpec((1, D_MODEL), lambda i, j: (0, 0)),
            modspec(1), modspec(2), modspec(3),
            pl.BlockSpec((1, D_MODEL, 2 * FFN_BH), lambda i, j: (j, 0, 0)),
            pl.BlockSpec((FFN_BH, D_MODEL), lambda i, j: (j, 0)),
            pl.BlockSpec((1, D_MODEL), lambda i, j: (0, 0)),
        ],
        out_specs=pl.BlockSpec((bm, D_MODEL), lambda i, j: (i, 0)),
        out_shape=jax.ShapeDtypeStruct((n, D_MODEL), F32),
        scratch_shapes=[pltpu.VMEM((bm, D_MODEL), BF16)],
        compiler_params=_params(2, vmem),
        name="ffn",
    )(x, g, mod, mod, mod, wgu, wd, fg)


def kernel(x, c, ctx, c_ctx, w_mod, b_mod, norm1_g, norm2_g, w_in, w_gate_f, b_gate_f,
           w_gate_b, b_gate_b, gla_norm_g, conv_w, w_out, w_ffn_gate, w_ffn_up, w_ffn_down,
           final_g):
    assert x.shape[0] == 1 and w_mod.shape[0] == 1, "batch 1, depth 1 only"
    seq, ctx_len = x.shape[1], ctx.shape[1]
    x2, ctx2 = x[0], ctx[0]

    c_t = jnp.concatenate(
        [c[0][:, None], c_ctx[:, None], jnp.zeros((D_MODEL, SUBLANES - 2), F32)], axis=1)
    mod = _mod_call(c_t, w_mod[0], b_mod, EARLY_MOD_COLS)

    pad_f = jnp.zeros((LANES, GLA_KW), F32).at[:GATE_RANK].set(w_gate_f[0]).astype(BF16)
    pad_b = jnp.zeros((LANES, GLA_KW), F32).at[GATE_RANK:2 * GATE_RANK].set(w_gate_b[0]).astype(BF16)
    g1 = norm1_g[0][None]
    w_in_t = jnp.transpose(w_in[0])

    h_c, l_c = _prenorm_call(ctx2, g1, mod, w_in_t, row=1, bm=ctx_len)
    p_c, = _in_proj_call(h_c, w_in_t, bm=ctx_len, n_tiles=2)
    zero_state = jnp.zeros((GLA_HEADS, GLA_DK, GLA_DV), F32)
    _, _, s_f, s_b = _gla_call(p_c, l_c, pad_f, b_gate_f, pad_b, b_gate_b, zero_state,
                               zero_state, tb=ctx_len)

    h_x, l_x = _prenorm_call(x2, g1, mod, w_in_t, row=0, bm=2048)
    p_x, wgu_b, mod_late = _in_proj_call(
        h_x, w_in_t, (w_ffn_gate[0], w_ffn_up[0], c_t, w_mod[0], b_mod, EARLY_MOD_COLS),
        bm=1024, n_tiles=P_COLS // P_BN)
    o_f, o_b, _, _, wd_b = _gla_call(p_x, l_x, pad_f, b_gate_f, pad_b, b_gate_b, s_f, s_b,
                                     w_ffn_down[0], tb=256)
    x_mid = _mix_call(o_f, o_b, p_x, x2, mod_late, gla_norm_g, conv_w[0], w_out[0], bm=512)
    out = _ffn_call(x_mid, norm2_g[0][None], mod_late, wgu_b, wd_b, final_g[None],
                    bm=1024)
    return out[None]
```

```python
import functools

import jax
import jax.numpy as jnp
from jax import lax
from jax.experimental import pallas as pl
from jax.experimental.pallas import tpu as pltpu

F32 = jnp.float32
BF16 = jnp.bfloat16

D_MODEL = 2048
GRID_W = 64
GLA_HEADS = 4
GLA_DK = 128
GLA_DV = 256
GLA_KW = GLA_HEADS * GLA_DK
GLA_VW = GLA_HEADS * GLA_DV
GATE_RANK = 16
GATE_NORM = 16.0
CHUNK = 64
LOG2_E = 1.4426950408889634
CONV_W = D_MODEL - GLA_VW
FFN_HIDDEN = 5632
EPS = 1e-6

LANES = 128
SUBLANES = 8
MIB = 1024 * 1024
VMEM_BYTES = 64 * MIB
VMEM_RESERVED = 6 * MIB
SPILL_BYTES = 4 * MIB

GATE_COL = 2 * GLA_KW + 2 * GLA_VW
GATE_COLS = 2 * GATE_RANK
P_COLS = GATE_COL + 3 * CONV_W
P_BN = 1024
P_SHIFT_FROM = GATE_COL // P_BN

_NT_DIMS = (((1,), (1,)), ((), ()))

MOD_BN = 2048
EARLY_MOD_COLS = 2 * D_MODEL
LATE_MOD_BN = 256
FFN_BH = 512
FFN_FIRST_GROUPS = 4
NORM_ROWS = 16


def _params(grid_rank, vmem_bytes):
    return pltpu.CompilerParams(
        dimension_semantics=("arbitrary",) * grid_rank,
        vmem_limit_bytes=min(int(vmem_bytes), VMEM_BYTES - VMEM_RESERVED),
    )


def _silu(v):
    return v * jax.nn.sigmoid(v)


def _for_row_chunks(rows, body):
    for r0 in range(0, rows, NORM_ROWS):
        body(slice(r0, r0 + NORM_ROWS))


def _norm_mod_rows(x_ref, g, mul, sh, dst_ref, rows, copy_ref=None):
    gain = g * mul

    def body(rs):
        xs = x_ref[rs, :]
        ms = jnp.mean(xs * xs, axis=-1, keepdims=True)
        y = xs * lax.rsqrt(ms + EPS)
        dst_ref[rs, :] = (y * gain + sh).astype(BF16)
        if copy_ref is not None:
            copy_ref[rs, :] = xs
    _for_row_chunks(rows, body)


def _mod_prepare(c_ref, sb_ref):
    s = _silu(c_ref[...])
    for r in range(2):
        sb_ref[r] = jnp.broadcast_to(s[:, r:r + 1], (D_MODEL, LANES))


def _mod_matvec(w_ref, b_ref, sb_ref, o_ref, *, unrolled):
    nt = w_ref.shape[1] // LANES

    def body(kk, accs):
        r0 = kk * SUBLANES if unrolled else pl.multiple_of(kk * SUBLANES, SUBLANES)
        s0 = sb_ref[0, pl.ds(r0, SUBLANES), :]
        s1 = sb_ref[1, pl.ds(r0, SUBLANES), :]
        new = []
        for t in range(nt):
            w8 = w_ref[pl.ds(r0, SUBLANES), t * LANES:(t + 1) * LANES]
            new.append(accs[2 * t] + w8 * s0)
            new.append(accs[2 * t + 1] + w8 * s1)
        return tuple(new)

    accs = (jnp.zeros((SUBLANES, LANES), F32),) * (2 * nt)
    if unrolled:
        for kk in range(D_MODEL // SUBLANES):
            accs = body(kk, accs)
    else:
        accs = lax.fori_loop(0, D_MODEL // SUBLANES, body, accs, unroll=8)
    o_ref[...] = jnp.zeros(o_ref.shape, F32)
    for t in range(nt):
        b = b_ref[:, t * LANES:(t + 1) * LANES]
        for r in range(2):
            o_ref[r:r + 1, t * LANES:(t + 1) * LANES] = (
                jnp.sum(accs[2 * t + r], axis=0, keepdims=True) + b)


def _mod_kernel(c_ref, w_ref, b_ref, o_ref, sb_ref):
    @pl.when(pl.program_id(0) == 0)
    def _():
        _mod_prepare(c_ref, sb_ref)

    _mod_matvec(w_ref, b_ref, sb_ref, o_ref, unrolled=False)


def _mod_call(c_t, w_mod, b_mod, n):
    return pl.pallas_call(
        _mod_kernel,
        grid=(n // MOD_BN,),
        in_specs=[
            pl.BlockSpec((D_MODEL, SUBLANES), lambda j: (0, 0)),
            pl.BlockSpec((D_MODEL, MOD_BN), lambda j: (0, j)),
            pl.BlockSpec((1, MOD_BN), lambda j: (0, j)),
        ],
        out_specs=pl.BlockSpec((SUBLANES, MOD_BN), lambda j: (0, j)),
        out_shape=jax.ShapeDtypeStruct((SUBLANES, n), F32),
        scratch_shapes=[pltpu.VMEM((2, D_MODEL, LANES), F32)],
        compiler_params=_params(1, 2 * D_MODEL * MOD_BN * 4 + 2 * SPILL_BYTES),
        name="mod",
    )(c_t, w_mod, b_mod)


def _prenorm_kernel(x_ref, g_ref, sh_ref, sc_ref, wl_ref, h_ref, l_ref, *, row, bm):
    _norm_mod_rows(x_ref, g_ref[...], 1.0 + sc_ref[row:row + 1, :], sh_ref[row:row + 1, :],
                   h_ref, bm)
    l_ref[...] = lax.dot_general(h_ref[...], wl_ref[...].astype(BF16), _NT_DIMS,
                                 preferred_element_type=F32).astype(BF16)


def _prenorm_call(x, g, mod, w_in_t, *, row, bm):
    m = x.shape[0]
    kern = functools.partial(_prenorm_kernel, row=row, bm=bm)
    vmem = (2 * bm * D_MODEL * (4 + 2) + 4 * D_MODEL * LANES * 4 + 8 * NORM_ROWS * D_MODEL * 4
            + SPILL_BYTES)
    return pl.pallas_call(
        kern,
        grid=(m // bm,),
        in_specs=[
            pl.BlockSpec((bm, D_MODEL), lambda i: (i, 0)),
            pl.BlockSpec((1, D_MODEL), lambda i: (0, 0)),
            pl.BlockSpec((SUBLANES, D_MODEL), lambda i: (0, 0)),
            pl.BlockSpec((SUBLANES, D_MODEL), lambda i: (0, 1)),
            pl.BlockSpec((LANES, D_MODEL), lambda i: (GATE_COL // LANES, 0)),
        ],
        out_specs=[pl.BlockSpec((bm, D_MODEL), lambda i: (i, 0)),
                   pl.BlockSpec((bm, LANES), lambda i: (i, 0))],
        out_shape=[jax.ShapeDtypeStruct((m, D_MODEL), BF16),
                   jax.ShapeDtypeStruct((m, LANES), BF16)],
        compiler_params=_params(1, vmem),
        name="prenorm",
    )(x, g, mod, mod, w_in_t)


W_ROWS = 256


class _CastJob:
    def __init__(self, sources, rows, stage_ref, out_ref, sem_in, sem_out):
        self.sources, self.rows = sources, rows
        self.stage_ref, self.out_ref, self.sem_in, self.sem_out = stage_ref, out_ref, sem_in, sem_out
        self.n_blocks = sum(n for _, _, n in sources)

    def _for_block(self, c, fn):
        first = 0
        for src, dst, n in self.sources:
            @pl.when((c >= first) & (c < first + n))
            def _(src=src, dst=dst, first=first):
                fn(src, dst, pl.multiple_of((c - first) * self.rows, self.rows), lax.rem(c, 2))
            first += n

    def _fetch(self, src, dst, r, slot):
        return pltpu.make_async_copy(src.at[pl.ds(r, self.rows), :], self.stage_ref.at[slot],
                                     self.sem_in.at[slot])

    def _store(self, src, dst, r, slot):
        return pltpu.make_async_copy(self.out_ref.at[slot], dst.at[pl.ds(r, self.rows), :],
                                     self.sem_out.at[slot])

    def before(self, step):
        @pl.when(step == 0)
        def _():
            self._for_block(step, lambda *a: self._fetch(*a).start())
        self._for_block(step, lambda *a: self._fetch(*a).wait())
        self._for_block(step + 1, lambda *a: self._fetch(*a).start())
        self._for_block(step - 2, lambda *a: self._store(*a).wait())

    def cast(self, step):
        slot = lax.rem(step, 2)
        self.out_ref[slot] = self.stage_ref[slot].astype(BF16)

    def after(self, step):
        self._for_block(step, lambda *a: self._store(*a).start())


def _cast_scratch(rows, cols):
    return [pltpu.VMEM((2, rows, cols), F32), pltpu.VMEM((2, rows, cols), BF16),
            pltpu.SemaphoreType.DMA((2,)), pltpu.SemaphoreType.DMA((2,))]


IN_PROJ_CAST_ROWS = 128
GLA_CAST_ROWS = 256


def _in_proj_kernel(h_ref, wa_ref, wb_ref, *refs, side_jobs):
    j = pl.program_id(0)
    if side_jobs:
        (wg_ref, wu_ref, c_ref, wm_ref, bm_ref, o_ref, wg_out, wu_out, mod_ref,
         w_ref, stage_ref, cast_ref, sem_in, sem_out, sb_ref) = refs
        n = D_MODEL // IN_PROJ_CAST_ROWS
        job = _CastJob(((wg_ref, wg_out, n), (wu_ref, wu_out, n)), IN_PROJ_CAST_ROWS,
                       stage_ref, cast_ref, sem_in, sem_out)
        step = j * pl.num_programs(1) + pl.program_id(1)
        job.before(step)

        @pl.when(step == 0)
        def _():
            _mod_prepare(c_ref, sb_ref)
    else:
        o_ref, w_ref = refs

    @pl.when(pl.program_id(1) == 0)
    def _():
        @pl.when(j < P_SHIFT_FROM)
        def _():
            for r0 in range(0, P_BN, W_ROWS):
                w_ref[r0:r0 + W_ROWS, :] = wa_ref[r0:r0 + W_ROWS, :].astype(BF16)

        @pl.when(j >= P_SHIFT_FROM)
        def _():
            keep = P_BN - GATE_COLS
            for r0 in range(0, keep, W_ROWS):
                r1 = min(r0 + W_ROWS, keep)
                w_ref[r0:r1, :] = wa_ref[GATE_COLS + r0:GATE_COLS + r1, :].astype(BF16)
            w_ref[keep:, :] = wb_ref[...].astype(BF16)

    o_ref[...] = lax.dot_general(h_ref[...], w_ref[...], _NT_DIMS,
                                 preferred_element_type=F32).astype(BF16)
    if side_jobs:
        job.cast(step)
        _mod_matvec(wm_ref, bm_ref, sb_ref, mod_ref, unrolled=True)
        job.after(step)


def _in_proj_call(h, w_in_t, side=None, *, bm, n_tiles):
    m = h.shape[0]
    ni = m // bm
    vmem = (2 * bm * D_MODEL * 2 + 2 * (P_BN + GATE_COLS) * D_MODEL * 4 + P_BN * D_MODEL * 2
            + 2 * bm * P_BN * 2 + bm * P_BN * 4 + 4 * W_ROWS * D_MODEL * 4 + SPILL_BYTES)
    in_specs = [
        pl.BlockSpec((bm, D_MODEL), lambda j, i: (i, 0)),
        pl.BlockSpec((P_BN, D_MODEL), lambda j, i: (j, 0)),
        pl.BlockSpec((GATE_COLS, D_MODEL), lambda j, i: ((j + 1) * (P_BN // GATE_COLS), 0)),
    ]
    out_specs = [pl.BlockSpec((bm, P_BN), lambda j, i: (i, j))]
    out_shape = [jax.ShapeDtypeStruct((m, n_tiles * P_BN), BF16)]
    scratch = [pltpu.VMEM((P_BN, D_MODEL), BF16)]
    operands = [h, w_in_t, w_in_t]
    if side is not None:
        w_gate, w_up, c_t, w_mod, b_mod, first_col = side
        n_steps = n_tiles * ni
        assert n_steps >= 2 * D_MODEL // IN_PROJ_CAST_ROWS + 2
        any_spec = pl.BlockSpec(memory_space=pl.ANY)
        n_mod = w_mod.shape[1] - first_col
        n_blocks = n_mod // LATE_MOD_BN
        assert n_mod % LATE_MOD_BN == 0 and first_col % LATE_MOD_BN == 0 and n_blocks <= n_steps
        mod_block = lambda j, i: jnp.minimum(j * ni + i, n_blocks - 1)
        in_specs += [
            any_spec, any_spec,
            pl.BlockSpec((D_MODEL, SUBLANES), lambda j, i: (0, 0)),
            pl.BlockSpec((D_MODEL, LATE_MOD_BN),
                         lambda j, i: (0, first_col // LATE_MOD_BN + mod_block(j, i))),
            pl.BlockSpec((1, LATE_MOD_BN),
                         lambda j, i: (0, first_col // LATE_MOD_BN + mod_block(j, i))),
        ]
        out_specs += [any_spec, any_spec,
                      pl.BlockSpec((SUBLANES, LATE_MOD_BN), lambda j, i: (0, mod_block(j, i)))]
        out_shape += [jax.ShapeDtypeStruct(w_gate.shape, BF16),
                      jax.ShapeDtypeStruct(w_up.shape, BF16),
                      jax.ShapeDtypeStruct((SUBLANES, n_mod), F32)]
        scratch += _cast_scratch(IN_PROJ_CAST_ROWS, FFN_HIDDEN)
        scratch += [pltpu.VMEM((2, D_MODEL, LANES), F32)]
        operands += [w_gate, w_up, c_t, w_mod, b_mod]
        vmem += (IN_PROJ_CAST_ROWS * FFN_HIDDEN * (2 * 4 + 2 * 2)
                 + 2 * D_MODEL * LATE_MOD_BN * 4 + 2 * D_MODEL * LANES * 4)
    return pl.pallas_call(
        functools.partial(_in_proj_kernel, side_jobs=side is not None),
        grid=(n_tiles, ni),
        in_specs=in_specs,
        out_specs=out_specs,
        out_shape=out_shape,
        scratch_shapes=scratch,
        compiler_params=_params(2, vmem),
        name="in_proj",
    )(*operands)


def _gla_log_decay_split(l_ref, wg_ref, bg_ref):
    z = jnp.dot(l_ref[...], wg_ref[...], preferred_element_type=F32) + bg_ref[...]
    log_a = (jnp.minimum(z, 0.0) - jnp.log(1.0 + jnp.exp(-jnp.abs(z)))) * (1.0 / GATE_NORM)
    hi = log_a.astype(BF16)
    lo = (log_a - hi.astype(F32)).astype(BF16)
    return hi, lo


def _gla_chunk_cumsum(pieces, *, reverse, tb):
    r = lax.broadcasted_iota(jnp.int32, (tb, tb), 0)
    cc = lax.broadcasted_iota(jnp.int32, (tb, tb), 1)
    same_chunk = (r // CHUNK) == (cc // CHUNK)
    tri = jnp.where(same_chunk & ((cc >= r) if reverse else (cc <= r)), 1.0, 0.0).astype(BF16)
    hi, lo = pieces
    return (jnp.dot(tri, hi, preferred_element_type=F32)
            + jnp.dot(tri, lo, preferred_element_type=F32)) * LOG2_E


def _gla_kernel(qkv_f, lf, lf_next, qkv_b, lb, lb_next, wgf, bgf, wgb, bgb, s0f, s0b,
                *refs, tb, cast):
    qf, kf, vf = (qkv_f.at[:, :GLA_KW], qkv_f.at[:, GLA_KW:2 * GLA_KW], qkv_f.at[:, 2 * GLA_KW:])
    qb, kb, vb = (qkv_b.at[:, :GLA_KW], qkv_b.at[:, GLA_KW:2 * GLA_KW], qkv_b.at[:, 2 * GLA_KW:])
    i = pl.program_id(0)
    if cast:
        (wd_ref, of, ob, sf, sb, wd_out, s_ref, cum_ref,
         stage_ref, cast_ref, sem_in, sem_out) = refs
        job = _CastJob(((wd_ref, wd_out, FFN_HIDDEN // GLA_CAST_ROWS),), GLA_CAST_ROWS,
                       stage_ref, cast_ref, sem_in, sem_out)
        job.before(i)
    else:
        of, ob, sf, sb, s_ref, cum_ref = refs

    @pl.when(i == 0)
    def _():
        s_ref[0] = s0f[...]
        s_ref[1] = s0b[...]
        cum_ref[0] = _gla_chunk_cumsum(_gla_log_decay_split(lf, wgf, bgf), reverse=False, tb=tb)
        cum_ref[1] = _gla_chunk_cumsum(_gla_log_decay_split(lb, wgb, bgb), reverse=True, tb=tb)

    dirs = ((qf, kf, vf, of, False), (qb, kb, vb, ob, True))
    ii = lax.broadcasted_iota(jnp.int32, (CHUNK, GLA_DK), 0)
    jj = lax.broadcasted_iota(jnp.int32, (CHUNK, GLA_DK), 1)
    nc = tb // CHUNK

    def start(step):
        items = []
        for d, (q_ref, k_ref, v_ref, o_ref, reverse) in enumerate(dirs):
            r0 = (nc - 1 - step if reverse else step) * CHUNK
            cum_c = cum_ref[d, r0:r0 + CHUNK, :]
            last = cum_c[0:1, :] if reverse else cum_c[CHUNK - 1:CHUNK, :]
            q_c = q_ref[r0:r0 + CHUNK, :].astype(F32)
            k_c = k_ref[r0:r0 + CHUNK, :].astype(F32)
            q_e = ((q_c * (GLA_DK ** -0.5)) * jnp.exp2(cum_c)).astype(BF16)
            k_e = (k_c * jnp.exp2(-cum_c)).astype(BF16)
            k_d = (k_c * jnp.exp2(last - cum_c)).astype(BF16)
            decay = jnp.exp2(last)
            mask = ((jj >= ii) & (jj < CHUNK)) if reverse else (jj <= ii)
            for h in range(GLA_HEADS):
                ks = slice(h * GLA_DK, (h + 1) * GLA_DK)
                vs = slice(h * GLA_DV, (h + 1) * GLA_DV)
                items.append(dict(
                    d=d, h=h, r0=r0, vs=vs, o_ref=o_ref, mask=mask,
                    q_e=q_e[:, ks], k_e=k_e[:, ks], k_d=k_d[:, ks], decay=decay[:, ks],
                    v=v_ref[r0:r0 + CHUNK, vs]))
        for it in items:
            k_pad = jnp.concatenate([it["k_e"], jnp.zeros((GLA_DK - CHUNK, GLA_DK), BF16)], axis=0)
            it["scores"] = lax.dot_general(it["q_e"], k_pad, _NT_DIMS,
                                           preferred_element_type=F32)
        return items

    items = start(0)
    next_cum = None
    for step in range(nc):
        following = start(step + 1) if step + 1 < nc else None
        if step == 1:
            next_f = _gla_log_decay_split(lf_next, wgf, bgf)
            next_b = _gla_log_decay_split(lb_next, wgb, bgb)
        if step == nc - 1:
            next_cum = (_gla_chunk_cumsum(next_f, reverse=False, tb=tb),
                        _gla_chunk_cumsum(next_b, reverse=True, tb=tb))
        for it in items:
            scores = jnp.where(it["mask"], it["scores"], 0.0).astype(BF16)
            s_old = s_ref[it["d"], it["h"]]
            lhs = jnp.concatenate([scores, it["q_e"]], axis=1)
            rhs = jnp.concatenate(
                [it["v"], jnp.zeros((GLA_DK - CHUNK, GLA_DV), BF16), s_old.astype(BF16)], axis=0)
            it["o_ref"][it["r0"]:it["r0"] + CHUNK, it["vs"]] = jnp.dot(
                lhs, rhs, preferred_element_type=F32).astype(BF16)
            ds = lax.dot_general(it["k_d"], it["v"], (((0,), (0,)), ((), ())),
                                 preferred_element_type=F32)
            dcol = jnp.transpose(jnp.broadcast_to(it["decay"], (SUBLANES, GLA_DK)))[:, 0:1]
            s_ref[it["d"], it["h"]] = dcol * s_old + ds
        items = following
    cum_ref[0] = next_cum[0]
    cum_ref[1] = next_cum[1]
    if cast:
        job.cast(i)
        job.after(i)

    @pl.when(i == pl.num_programs(0) - 1)
    def _():
        sf[...] = s_ref[0]
        sb[...] = s_ref[1]


def _gla_call(p, l, wgf, bgf, wgb, bgb, s0f, s0b, cast_weight=None, *, tb):
    n = p.shape[0]
    nb = n // tb
    fwd = lambda i: i
    bwd = lambda i: nb - 1 - i
    fwd_next = lambda i: jnp.minimum(i + 1, nb - 1)
    bwd_next = lambda i: jnp.maximum(nb - 2 - i, 0)

    def specs(pos, pos_next):
        return [
            pl.BlockSpec((tb, 2 * GLA_KW + GLA_VW), lambda i: (pos(i), 0)),
            pl.BlockSpec((tb, LANES), lambda i: (pos(i), 0)),
            pl.BlockSpec((tb, LANES), lambda i: (pos_next(i), 0)),
        ]

    const2 = lambda i: (0, 0)
    const3 = lambda i: (0, 0, 0)
    state_shape = (GLA_HEADS, GLA_DK, GLA_DV)
    state_spec = pl.BlockSpec(state_shape, const3)
    cast = cast_weight is not None
    kern = functools.partial(_gla_kernel, tb=tb, cast=cast)
    vmem = (2 * 2 * tb * (2 * GLA_KW + GLA_VW + LANES) * 2 + 2 * 2 * tb * GLA_VW * 2
            + 10 * GLA_HEADS * GLA_DK * GLA_DV * 4 + 24 * tb * GLA_KW * 4 + 2 * SPILL_BYTES)
    in_specs = specs(fwd, fwd_next) + specs(bwd, bwd_next) + [
        pl.BlockSpec((LANES, GLA_KW), const2), pl.BlockSpec((1, GLA_KW), const2),
        pl.BlockSpec((LANES, GLA_KW), const2), pl.BlockSpec((1, GLA_KW), const2),
        state_spec, state_spec,
    ]
    out_specs = [
        pl.BlockSpec((tb, GLA_VW), lambda i: (i, 0)),
        pl.BlockSpec((tb, GLA_VW), lambda i: (nb - 1 - i, 0)),
        state_spec, state_spec,
    ]
    out_shape = [
        jax.ShapeDtypeStruct((n, GLA_VW), BF16), jax.ShapeDtypeStruct((n, GLA_VW), BF16),
        jax.ShapeDtypeStruct(state_shape, F32), jax.ShapeDtypeStruct(state_shape, F32),
    ]
    scratch = [pltpu.VMEM((2,) + state_shape, F32), pltpu.VMEM((2, tb, GLA_KW), F32)]
    operands = [p, l, l, p, l, l, wgf, bgf, wgb, bgb, s0f, s0b]
    if cast:
        assert cast_weight.shape == (FFN_HIDDEN, D_MODEL)
        assert nb >= FFN_HIDDEN // GLA_CAST_ROWS + 2
        any_spec = pl.BlockSpec(memory_space=pl.ANY)
        in_specs.append(any_spec)
        out_specs.append(any_spec)
        out_shape.append(jax.ShapeDtypeStruct(cast_weight.shape, BF16))
        scratch += _cast_scratch(GLA_CAST_ROWS, D_MODEL)
        operands.append(cast_weight)
        vmem += GLA_CAST_ROWS * D_MODEL * (2 * 4 + 2 * 2)
    return pl.pallas_call(
        kern,
        grid=(nb,),
        in_specs=in_specs,
        out_specs=out_specs,
        out_shape=out_shape,
        scratch_shapes=scratch,
        compiler_params=_params(1, vmem),
        name="gla",
    )(*operands)


def _mix_kernel(of_ref, ob_ref, gcb_ref, ccx_ref, x_ref, gt_ref, gg_ref, cw_ref,
                wo_ref, o_ref, m_ref, *, bm):
    g_ref, cb_ref = gcb_ref.at[:, :GLA_VW], gcb_ref.at[:, GLA_VW:]
    cc_ref, cx_ref = ccx_ref.at[:, :CONV_W], ccx_ref.at[:, CONV_W:]

    u = cc_ref[...].astype(F32) * cx_ref[...].astype(F32)
    col = lax.broadcasted_iota(jnp.int32, u.shape, 0) % GRID_W
    u_prev = jnp.where(col == 0, 0.0, pltpu.roll(u, 1, 0))
    u_next = jnp.where(col == GRID_W - 1, 0.0, pltpu.roll(u, bm - 1, 0))
    conv = cw_ref[0:1, :] * u_prev + cw_ref[1:2, :] * u + cw_ref[2:3, :] * u_next
    m_ref[:, GLA_VW:] = (cb_ref[...].astype(F32) * conv).astype(BF16)
    mixed = jnp.dot(m_ref[:, GLA_VW:], wo_ref[GLA_VW:, :].astype(BF16),
                    preferred_element_type=F32)

    gg = gg_ref[...]
    for h in range(GLA_HEADS):
        vs = slice(h * GLA_DV, (h + 1) * GLA_DV)
        o = of_ref[:, vs].astype(F32) + ob_ref[:, vs].astype(F32)
        o32 = o * lax.rsqrt(jnp.mean(o * o, axis=-1, keepdims=True) + EPS)
        m_ref[:, vs] = ((o32 * gg) * _silu(g_ref[:, vs].astype(F32))).astype(BF16)
        mixed = mixed + jnp.dot(m_ref[:, vs], wo_ref[vs, :].astype(BF16),
                                preferred_element_type=F32)
    o_ref[...] = x_ref[...] + gt_ref[0:1, :] * mixed


def _mix_call(o_f, o_b, p, x, mod, gla_g, conv_w, w_out, *, bm):
    n = x.shape[0]
    kern = functools.partial(_mix_kernel, bm=bm)
    assert GLA_VW == CONV_W
    pcol = lambda k: pl.BlockSpec((bm, 2 * CONV_W), lambda i, k=k: (i, k))
    const2 = lambda i: (0, 0)
    vmem = (2 * 6 * bm * GLA_VW * 2 + 2 * 2 * bm * D_MODEL * 4
            + D_MODEL * D_MODEL * 4 + bm * D_MODEL * 2 + 8 * bm * GLA_VW * 4
            + SPILL_BYTES)
    return pl.pallas_call(
        kern,
        grid=(n // bm,),
        in_specs=[
            pl.BlockSpec((bm, GLA_VW), lambda i: (i, 0)),
            pl.BlockSpec((bm, GLA_VW), lambda i: (i, 0)),
            pcol(1), pcol(2),
            pl.BlockSpec((bm, D_MODEL), lambda i: (i, 0)),
            pl.BlockSpec((SUBLANES, D_MODEL), lambda i: (0, 0)),
            pl.BlockSpec((1, GLA_DV), const2),
            pl.BlockSpec((3, CONV_W), const2),
            pl.BlockSpec((D_MODEL, D_MODEL), const2, pipeline_mode=pl.Buffered(1)),
        ],
        out_specs=pl.BlockSpec((bm, D_MODEL), lambda i: (i, 0)),
        out_shape=jax.ShapeDtypeStruct((n, D_MODEL), F32),
        scratch_shapes=[pltpu.VMEM((bm, D_MODEL), BF16)],
        compiler_params=_params(1, vmem),
        name="mix",
    )(o_f, o_b, p, p, x, mod, gla_g, conv_w, w_out)


def _ffn_kernel(x_ref, g_ref, sh_ref, sc_ref, gt_ref, wg_ref, wu_ref, wd_ref, fg_ref,
                o_ref, h_ref, *, bm):
    j = pl.program_id(1)

    def accumulate(rows):
        h = h_ref[rows, :]
        gate = jnp.dot(h, wg_ref[...], preferred_element_type=F32)
        up = jnp.dot(h, wu_ref[...], preferred_element_type=F32)
        part = jnp.dot((_silu(gate) * up).astype(BF16), wd_ref[...], preferred_element_type=F32)
        o_ref[rows, :] += gt_ref[0:1, :] * part

    @pl.when(j == 0)
    def _():
        group = bm // FFN_FIRST_GROUPS
        for r0 in range(0, bm, group):
            rows = slice(r0, r0 + group)
            _norm_mod_rows(x_ref.at[rows, :], g_ref[...], 1.0 + sc_ref[0:1, :], sh_ref[0:1, :],
                           h_ref.at[rows, :], group, copy_ref=o_ref.at[rows, :])
            accumulate(rows)

    @pl.when(j > 0)
    def _():
        accumulate(slice(None))

    @pl.when(j == pl.num_programs(1) - 1)
    def _():
        fg = fg_ref[...]

        def final_norm(rs):
            xn = o_ref[rs, :]
            ms = jnp.mean(xn * xn, axis=-1, keepdims=True)
            o_ref[rs, :] = (xn * lax.rsqrt(ms + EPS)) * fg
        _for_row_chunks(bm, final_norm)


def _ffn_call(x, g, mod, wg, wu, wd, fg, *, bm):
    n = x.shape[0]
    kern = functools.partial(_ffn_kernel, bm=bm)
    modspec = lambda k: pl.BlockSpec((SUBLANES, D_MODEL), lambda i, j, k=k: (0, k))
    vmem = (3 * bm * D_MODEL * 4 + bm * D_MODEL * 2 + 2 * 3 * D_MODEL * FFN_BH * 2
            + 4 * bm * FFN_BH * 4 + bm * D_MODEL * 4 + SPILL_BYTES)
    return pl.pallas_call(
        kern,
        grid=(n // bm, FFN_HIDDEN // FFN_BH),
        in_specs=[
            pl.BlockSpec((bm, D_MODEL), lambda i, j: (i, 0)),
            pl.BlockSpec((1, D_MODEL), lambda i, j: (0, 0)),
            modspec(1), modspec(2), modspec(3),
            pl.BlockSpec((D_MODEL, FFN_BH), lambda i, j: (0, j)),
            pl.BlockSpec((D_MODEL, FFN_BH), lambda i, j: (0, j)),
            pl.BlockSpec((FFN_BH, D_MODEL), lambda i, j: (j, 0)),
            pl.BlockSpec((1, D_MODEL), lambda i, j: (0, 0)),
        ],
        out_specs=pl.BlockSpec((bm, D_MODEL), lambda i, j: (i, 0)),
        out_shape=jax.ShapeDtypeStruct((n, D_MODEL), F32),
        scratch_shapes=[pltpu.VMEM((bm, D_MODEL), BF16)],
        compiler_params=_params(2, vmem),
        name="ffn",
    )(x, g, mod, mod, mod, wg, wu, wd, fg)


def kernel(x, c, ctx, c_ctx, w_mod, b_mod, norm1_g, norm2_g, w_in, w_gate_f, b_gate_f,
           w_gate_b, b_gate_b, gla_norm_g, conv_w, w_out, w_ffn_gate, w_ffn_up, w_ffn_down,
           final_g):
    assert x.shape[0] == 1 and w_mod.shape[0] == 1, "batch 1, depth 1 only"
    seq, ctx_len = x.shape[1], ctx.shape[1]
    x2, ctx2 = x[0], ctx[0]

    c_t = jnp.concatenate(
        [c[0][:, None], c_ctx[:, None], jnp.zeros((D_MODEL, SUBLANES - 2), F32)], axis=1)
    mod = _mod_call(c_t, w_mod[0], b_mod, EARLY_MOD_COLS)

    pad_f = jnp.zeros((LANES, GLA_KW), F32).at[:GATE_RANK].set(w_gate_f[0]).astype(BF16)
    pad_b = jnp.zeros((LANES, GLA_KW), F32).at[GATE_RANK:2 * GATE_RANK].set(w_gate_b[0]).astype(BF16)
    g1 = norm1_g[0][None]
    w_in_t = jnp.transpose(w_in[0])

    h_c, l_c = _prenorm_call(ctx2, g1, mod, w_in_t, row=1, bm=ctx_len)
    p_c, = _in_proj_call(h_c, w_in_t, bm=ctx_len, n_tiles=2)
    zero_state = jnp.zeros((GLA_HEADS, GLA_DK, GLA_DV), F32)
    _, _, s_f, s_b = _gla_call(p_c, l_c, pad_f, b_gate_f, pad_b, b_gate_b, zero_state,
                               zero_state, tb=ctx_len)

    h_x, l_x = _prenorm_call(x2, g1, mod, w_in_t, row=0, bm=2048)
    p_x, wg_b, wu_b, mod_late = _in_proj_call(
        h_x, w_in_t, (w_ffn_gate[0], w_ffn_up[0], c_t, w_mod[0], b_mod, EARLY_MOD_COLS),
        bm=1024, n_tiles=P_COLS // P_BN)
    o_f, o_b, _, _, wd_b = _gla_call(p_x, l_x, pad_f, b_gate_f, pad_b, b_gate_b, s_f, s_b,
                                     w_ffn_down[0], tb=256)
    x_mid = _mix_call(o_f, o_b, p_x, x2, mod_late, gla_norm_g, conv_w[0], w_out[0], bm=512)
    out = _ffn_call(x_mid, norm2_g[0][None], mod_late, wg_b, wu_b, wd_b, final_g[None],
                    bm=1024)
    return out[None]
```

```python
import functools

import jax
import jax.numpy as jnp
from jax import lax
from jax.experimental import pallas as pl
from jax.experimental.pallas import tpu as pltpu

F32 = jnp.float32
BF16 = jnp.bfloat16

D_MODEL = 2048
GRID_W = 64
GLA_HEADS = 4
GLA_DK = 128
GLA_DV = 256
GLA_KW = GLA_HEADS * GLA_DK
GLA_VW = GLA_HEADS * GLA_DV
GATE_RANK = 16
GATE_NORM = 16.0
CHUNK = 64
LOG2_E = 1.4426950408889634
CONV_W = D_MODEL - GLA_VW
FFN_HIDDEN = 5632
EPS = 1e-6

LANES = 128
SUBLANES = 8
MIB = 1024 * 1024
VMEM_BYTES = 64 * MIB
VMEM_RESERVED = 6 * MIB
SPILL_BYTES = 4 * MIB

GATE_COL = 2 * GLA_KW + 2 * GLA_VW
GATE_COLS = 2 * GATE_RANK
P_COLS = GATE_COL + 3 * CONV_W
P_BN = 1024
P_SHIFT_FROM = GATE_COL // P_BN

_NT_DIMS = (((1,), (1,)), ((), ()))

MOD_BN = 1024
EARLY_MOD_COLS = 2 * D_MODEL
LATE_MOD_BN = 256
FFN_BH = 512
NORM_ROWS = 16


def _params(grid_rank, vmem_bytes):
    return pltpu.CompilerParams(
        dimension_semantics=("arbitrary",) * grid_rank,
        vmem_limit_bytes=min(int(vmem_bytes), VMEM_BYTES - VMEM_RESERVED),
    )


def _silu(v):
    return v * jax.nn.sigmoid(v)


def _for_row_chunks(rows, body):
    for r0 in range(0, rows, NORM_ROWS):
        body(slice(r0, r0 + NORM_ROWS))


def _norm_mod_rows(x_ref, g, mul, sh, dst_ref, rows, copy_ref=None):
    gain = g * mul

    def body(rs):
        xs = x_ref[rs, :]
        ms = jnp.mean(xs * xs, axis=-1, keepdims=True)
        y = xs * lax.rsqrt(ms + EPS)
        dst_ref[rs, :] = (y * gain + sh).astype(BF16)
        if copy_ref is not None:
            copy_ref[rs, :] = xs
    _for_row_chunks(rows, body)


def _mod_prepare(c_ref, sb_ref):
    s = _silu(c_ref[...])
    for r in range(2):
        sb_ref[r] = jnp.broadcast_to(s[:, r:r + 1], (D_MODEL, LANES))


def _mod_matvec(w_ref, b_ref, sb_ref, o_ref, *, unrolled):
    nt = w_ref.shape[1] // LANES

    def body(kk, accs):
        r0 = kk * SUBLANES if unrolled else pl.multiple_of(kk * SUBLANES, SUBLANES)
        s0 = sb_ref[0, pl.ds(r0, SUBLANES), :]
        s1 = sb_ref[1, pl.ds(r0, SUBLANES), :]
        new = []
        for t in range(nt):
            w8 = w_ref[pl.ds(r0, SUBLANES), t * LANES:(t + 1) * LANES]
            new.append(accs[2 * t] + w8 * s0)
            new.append(accs[2 * t + 1] + w8 * s1)
        return tuple(new)

    accs = (jnp.zeros((SUBLANES, LANES), F32),) * (2 * nt)
    if unrolled:
        for kk in range(D_MODEL // SUBLANES):
            accs = body(kk, accs)
    else:
        accs = lax.fori_loop(0, D_MODEL // SUBLANES, body, accs, unroll=8)
    o_ref[...] = jnp.zeros(o_ref.shape, F32)
    for t in range(nt):
        b = b_ref[:, t * LANES:(t + 1) * LANES]
        for r in range(2):
            o_ref[r:r + 1, t * LANES:(t + 1) * LANES] = (
                jnp.sum(accs[2 * t + r], axis=0, keepdims=True) + b)


STREAM_BUFFERS = 3


def _mod_kernel(c_ref, w_hbm, b_hbm, o_hbm, sb_ref, *, n):
    _mod_prepare(c_ref, sb_ref)

    def step(w_ref, b_ref, o_ref):
        _mod_matvec(w_ref, b_ref, sb_ref, o_ref, unrolled=False)

    pltpu.emit_pipeline(
        step,
        grid=(n // MOD_BN,),
        in_specs=[pl.BlockSpec((D_MODEL, MOD_BN), lambda j: (0, j),
                               pipeline_mode=pl.Buffered(STREAM_BUFFERS)),
                  pl.BlockSpec((1, MOD_BN), lambda j: (0, j))],
        out_specs=[pl.BlockSpec((SUBLANES, MOD_BN), lambda j: (0, j))],
    )(w_hbm, b_hbm, o_hbm)


def _mod_call(c_t, w_mod, b_mod, n):
    any_spec = pl.BlockSpec(memory_space=pl.ANY)
    vmem = STREAM_BUFFERS * D_MODEL * MOD_BN * 4 + 2 * D_MODEL * LANES * 4 + 2 * SPILL_BYTES
    return pl.pallas_call(
        functools.partial(_mod_kernel, n=n),
        in_specs=[pl.BlockSpec(memory_space=pltpu.VMEM), any_spec, any_spec],
        out_specs=any_spec,
        out_shape=jax.ShapeDtypeStruct((SUBLANES, n), F32),
        scratch_shapes=[pltpu.VMEM((2, D_MODEL, LANES), F32)],
        compiler_params=pltpu.CompilerParams(
            vmem_limit_bytes=min(int(vmem), VMEM_BYTES - VMEM_RESERVED)),
        name="mod",
    )(c_t, w_mod, b_mod)


def _prenorm_kernel(x_ref, g_ref, sh_ref, sc_ref, wl_ref, h_ref, l_ref, *, row, bm):
    _norm_mod_rows(x_ref, g_ref[...], 1.0 + sc_ref[row:row + 1, :], sh_ref[row:row + 1, :],
                   h_ref, bm)
    l_ref[...] = lax.dot_general(h_ref[...], wl_ref[...].astype(BF16), _NT_DIMS,
                                 preferred_element_type=F32).astype(BF16)


def _prenorm_streamed_kernel(x_hbm, g_ref, mod_ref, wl_ref, h_hbm, l_hbm, *, row, bm):
    def step(x_ref, h_ref, l_ref):
        _prenorm_kernel(x_ref, g_ref, mod_ref.at[:, 0:D_MODEL], mod_ref.at[:, D_MODEL:2 * D_MODEL],
                        wl_ref, h_ref, l_ref, row=row, bm=bm)

    pltpu.emit_pipeline(
        step,
        grid=(x_hbm.shape[0] // bm,),
        in_specs=[pl.BlockSpec((bm, D_MODEL), lambda i: (i, 0),
                               pipeline_mode=pl.Buffered(STREAM_BUFFERS))],
        out_specs=[pl.BlockSpec((bm, D_MODEL), lambda i: (i, 0)),
                   pl.BlockSpec((bm, LANES), lambda i: (i, 0))],
    )(x_hbm, h_hbm, l_hbm)


def _prenorm_call(x, g, mod, w_in_t, *, row, bm):
    m = x.shape[0]
    out_shape = [jax.ShapeDtypeStruct((m, D_MODEL), BF16), jax.ShapeDtypeStruct((m, LANES), BF16)]
    if m // bm >= STREAM_BUFFERS:
        any_spec = pl.BlockSpec(memory_space=pl.ANY)
        vmem_spec = pl.BlockSpec(memory_space=pltpu.VMEM)
        w_gate = lax.slice(w_in_t, (GATE_COL, 0), (GATE_COL + LANES, D_MODEL))
        vmem = (STREAM_BUFFERS * bm * D_MODEL * 4 + 2 * bm * D_MODEL * 2
                + 4 * D_MODEL * LANES * 4 + 8 * NORM_ROWS * D_MODEL * 4 + SPILL_BYTES)
        return pl.pallas_call(
            functools.partial(_prenorm_streamed_kernel, row=row, bm=bm),
            in_specs=[any_spec, vmem_spec, vmem_spec, vmem_spec],
            out_specs=[any_spec, any_spec],
            out_shape=out_shape,
            compiler_params=pltpu.CompilerParams(
                vmem_limit_bytes=min(int(vmem), VMEM_BYTES - VMEM_RESERVED)),
            name="prenorm",
        )(x, g, mod, w_gate)
    kern = functools.partial(_prenorm_kernel, row=row, bm=bm)
    vmem = (2 * bm * D_MODEL * (4 + 2) + 4 * D_MODEL * LANES * 4 + 8 * NORM_ROWS * D_MODEL * 4
            + SPILL_BYTES)
    return pl.pallas_call(
        kern,
        grid=(m // bm,),
        in_specs=[
            pl.BlockSpec((bm, D_MODEL), lambda i: (i, 0)),
            pl.BlockSpec((1, D_MODEL), lambda i: (0, 0)),
            pl.BlockSpec((SUBLANES, D_MODEL), lambda i: (0, 0)),
            pl.BlockSpec((SUBLANES, D_MODEL), lambda i: (0, 1)),
            pl.BlockSpec((LANES, D_MODEL), lambda i: (GATE_COL // LANES, 0)),
        ],
        out_specs=[pl.BlockSpec((bm, D_MODEL), lambda i: (i, 0)),
                   pl.BlockSpec((bm, LANES), lambda i: (i, 0))],
        out_shape=out_shape,
        compiler_params=_params(1, vmem),
        name="prenorm",
    )(x, g, mod, mod, w_in_t)


W_ROWS = 256


class _CastJob:
    def __init__(self, sources, rows, stage_ref, out_ref, sem_in, sem_out):
        self.sources, self.rows = sources, rows
        self.stage_ref, self.out_ref, self.sem_in, self.sem_out = stage_ref, out_ref, sem_in, sem_out
        self.n_blocks = sum(n for _, _, n in sources)

    def _for_block(self, c, fn):
        first = 0
        for src, dst, n in self.sources:
            @pl.when((c >= first) & (c < first + n))
            def _(src=src, dst=dst, first=first):
                fn(src, dst, pl.multiple_of((c - first) * self.rows, self.rows), lax.rem(c, 2))
            first += n

    def _fetch(self, src, dst, r, slot):
        return pltpu.make_async_copy(src.at[pl.ds(r, self.rows), :], self.stage_ref.at[slot],
                                     self.sem_in.at[slot])

    def _store(self, src, dst, r, slot):
        return pltpu.make_async_copy(self.out_ref.at[slot], dst.at[pl.ds(r, self.rows), :],
                                     self.sem_out.at[slot])

    def before(self, step):
        @pl.when(step == 0)
        def _():
            self._for_block(step, lambda *a: self._fetch(*a).start())
        self._for_block(step, lambda *a: self._fetch(*a).wait())
        self._for_block(step + 1, lambda *a: self._fetch(*a).start())
        self._for_block(step - 2, lambda *a: self._store(*a).wait())

    def cast(self, step):
        slot = lax.rem(step, 2)
        self.out_ref[slot] = self.stage_ref[slot].astype(BF16)

    def after(self, step):
        self._for_block(step, lambda *a: self._store(*a).start())


def _cast_scratch(rows, cols):
    return [pltpu.VMEM((2, rows, cols), F32), pltpu.VMEM((2, rows, cols), BF16),
            pltpu.SemaphoreType.DMA((2,)), pltpu.SemaphoreType.DMA((2,))]


IN_PROJ_CAST_ROWS = 128
GLA_CAST_ROWS = 256


def _in_proj_kernel(h_ref, wa_ref, wb_ref, *refs, side_jobs):
    j = pl.program_id(0)
    if side_jobs:
        (wg_ref, wu_ref, c_ref, wm_ref, bm_ref, o_ref, wg_out, wu_out, mod_ref,
         w_ref, stage_ref, cast_ref, sem_in, sem_out, sb_ref) = refs
        n = D_MODEL // IN_PROJ_CAST_ROWS
        job = _CastJob(((wg_ref, wg_out, n), (wu_ref, wu_out, n)), IN_PROJ_CAST_ROWS,
                       stage_ref, cast_ref, sem_in, sem_out)
        step = j * pl.num_programs(1) + pl.program_id(1)
        job.before(step)

        @pl.when(step == 0)
        def _():
            _mod_prepare(c_ref, sb_ref)
    else:
        o_ref, w_ref = refs

    @pl.when(pl.program_id(1) == 0)
    def _():
        @pl.when(j < P_SHIFT_FROM)
        def _():
            for r0 in range(0, P_BN, W_ROWS):
                w_ref[r0:r0 + W_ROWS, :] = wa_ref[r0:r0 + W_ROWS, :].astype(BF16)

        @pl.when(j >= P_SHIFT_FROM)
        def _():
            keep = P_BN - GATE_COLS
            for r0 in range(0, keep, W_ROWS):
                r1 = min(r0 + W_ROWS, keep)
                w_ref[r0:r1, :] = wa_ref[GATE_COLS + r0:GATE_COLS + r1, :].astype(BF16)
            w_ref[keep:, :] = wb_ref[...].astype(BF16)

    o_ref[...] = lax.dot_general(h_ref[...], w_ref[...], _NT_DIMS,
                                 preferred_element_type=F32).astype(BF16)
    if side_jobs:
        job.cast(step)
        _mod_matvec(wm_ref, bm_ref, sb_ref, mod_ref, unrolled=True)
        job.after(step)


def _in_proj_call(h, w_in_t, side=None, *, bm, n_tiles):
    m = h.shape[0]
    ni = m // bm
    vmem = (2 * bm * D_MODEL * 2 + 2 * (P_BN + GATE_COLS) * D_MODEL * 4 + P_BN * D_MODEL * 2
            + 2 * bm * P_BN * 2 + bm * P_BN * 4 + 4 * W_ROWS * D_MODEL * 4 + SPILL_BYTES)
    in_specs = [
        pl.BlockSpec((bm, D_MODEL), lambda j, i: (i, 0)),
        pl.BlockSpec((P_BN, D_MODEL), lambda j, i: (j, 0)),
        pl.BlockSpec((GATE_COLS, D_MODEL), lambda j, i: ((j + 1) * (P_BN // GATE_COLS), 0)),
    ]
    out_specs = [pl.BlockSpec((bm, P_BN), lambda j, i: (i, j))]
    out_shape = [jax.ShapeDtypeStruct((m, n_tiles * P_BN), BF16)]
    scratch = [pltpu.VMEM((P_BN, D_MODEL), BF16)]
    operands = [h, w_in_t, w_in_t]
    if side is not None:
        w_gate, w_up, c_t, w_mod, b_mod, first_col = side
        n_steps = n_tiles * ni
        assert n_steps >= 2 * D_MODEL // IN_PROJ_CAST_ROWS + 2
        any_spec = pl.BlockSpec(memory_space=pl.ANY)
        n_mod = w_mod.shape[1] - first_col
        n_blocks = n_mod // LATE_MOD_BN
        assert n_mod % LATE_MOD_BN == 0 and first_col % LATE_MOD_BN == 0 and n_blocks <= n_steps
        mod_block = lambda j, i: jnp.minimum(j * ni + i, n_blocks - 1)
        in_specs += [
            any_spec, any_spec,
            pl.BlockSpec((D_MODEL, SUBLANES), lambda j, i: (0, 0)),
            pl.BlockSpec((D_MODEL, LATE_MOD_BN),
                         lambda j, i: (0, first_col // LATE_MOD_BN + mod_block(j, i))),
            pl.BlockSpec((1, LATE_MOD_BN),
                         lambda j, i: (0, first_col // LATE_MOD_BN + mod_block(j, i))),
        ]
        out_specs += [any_spec, any_spec,
                      pl.BlockSpec((SUBLANES, LATE_MOD_BN), lambda j, i: (0, mod_block(j, i)))]
        out_shape += [jax.ShapeDtypeStruct(w_gate.shape, BF16),
                      jax.ShapeDtypeStruct(w_up.shape, BF16),
                      jax.ShapeDtypeStruct((SUBLANES, n_mod), F32)]
        scratch += _cast_scratch(IN_PROJ_CAST_ROWS, FFN_HIDDEN)
        scratch += [pltpu.VMEM((2, D_MODEL, LANES), F32)]
        operands += [w_gate, w_up, c_t, w_mod, b_mod]
        vmem += (IN_PROJ_CAST_ROWS * FFN_HIDDEN * (2 * 4 + 2 * 2)
                 + 2 * D_MODEL * LATE_MOD_BN * 4 + 2 * D_MODEL * LANES * 4)
    return pl.pallas_call(
        functools.partial(_in_proj_kernel, side_jobs=side is not None),
        grid=(n_tiles, ni),
        in_specs=in_specs,
        out_specs=out_specs,
        out_shape=out_shape,
        scratch_shapes=scratch,
        compiler_params=_params(2, vmem),
        name="in_proj",
    )(*operands)


def _gla_log_decay_split(l_ref, wg_ref, bg_ref):
    z = jnp.dot(l_ref[...], wg_ref[...], preferred_element_type=F32) + bg_ref[...]
    log_a = (jnp.minimum(z, 0.0) - jnp.log(1.0 + jnp.exp(-jnp.abs(z)))) * (1.0 / GATE_NORM)
    hi = log_a.astype(BF16)
    lo = (log_a - hi.astype(F32)).astype(BF16)
    return hi, lo


def _gla_chunk_cumsum(pieces, *, reverse, tb):
    r = lax.broadcasted_iota(jnp.int32, (tb, tb), 0)
    cc = lax.broadcasted_iota(jnp.int32, (tb, tb), 1)
    same_chunk = (r // CHUNK) == (cc // CHUNK)
    tri = jnp.where(same_chunk & ((cc >= r) if reverse else (cc <= r)), 1.0, 0.0).astype(BF16)
    hi, lo = pieces
    return (jnp.dot(tri, hi, preferred_element_type=F32)
            + jnp.dot(tri, lo, preferred_element_type=F32)) * LOG2_E


def _gla_kernel(qkv_f, lf, lf_next, qkv_b, lb, lb_next, wgf, bgf, wgb, bgb, s0f, s0b,
                *refs, tb, cast):
    qf, kf, vf = (qkv_f.at[:, :GLA_KW], qkv_f.at[:, GLA_KW:2 * GLA_KW], qkv_f.at[:, 2 * GLA_KW:])
    qb, kb, vb = (qkv_b.at[:, :GLA_KW], qkv_b.at[:, GLA_KW:2 * GLA_KW], qkv_b.at[:, 2 * GLA_KW:])
    i = pl.program_id(0)
    if cast:
        (wd_ref, of, ob, sf, sb, wd_out, s_ref, cum_ref,
         stage_ref, cast_ref, sem_in, sem_out) = refs
        job = _CastJob(((wd_ref, wd_out, FFN_HIDDEN // GLA_CAST_ROWS),), GLA_CAST_ROWS,
                       stage_ref, cast_ref, sem_in, sem_out)
        job.before(i)
    else:
        of, ob, sf, sb, s_ref, cum_ref = refs

    @pl.when(i == 0)
    def _():
        s_ref[0] = s0f[...]
        s_ref[1] = s0b[...]
        cum_ref[0] = _gla_chunk_cumsum(_gla_log_decay_split(lf, wgf, bgf), reverse=False, tb=tb)
        cum_ref[1] = _gla_chunk_cumsum(_gla_log_decay_split(lb, wgb, bgb), reverse=True, tb=tb)

    dirs = ((qf, kf, vf, of, False), (qb, kb, vb, ob, True))
    ii = lax.broadcasted_iota(jnp.int32, (CHUNK, GLA_DK), 0)
    jj = lax.broadcasted_iota(jnp.int32, (CHUNK, GLA_DK), 1)
    nc = tb // CHUNK

    def start(step):
        items = []
        for d, (q_ref, k_ref, v_ref, o_ref, reverse) in enumerate(dirs):
            r0 = (nc - 1 - step if reverse else step) * CHUNK
            cum_c = cum_ref[d, r0:r0 + CHUNK, :]
            last = cum_c[0:1, :] if reverse else cum_c[CHUNK - 1:CHUNK, :]
            q_c = q_ref[r0:r0 + CHUNK, :].astype(F32)
            k_c = k_ref[r0:r0 + CHUNK, :].astype(F32)
            q_e = ((q_c * (GLA_DK ** -0.5)) * jnp.exp2(cum_c)).astype(BF16)
            k_e = (k_c * jnp.exp2(-cum_c)).astype(BF16)
            k_d = (k_c * jnp.exp2(last - cum_c)).astype(BF16)
            decay = jnp.exp2(last)
            mask = ((jj >= ii) & (jj < CHUNK)) if reverse else (jj <= ii)
            for h in range(GLA_HEADS):
                ks = slice(h * GLA_DK, (h + 1) * GLA_DK)
                vs = slice(h * GLA_DV, (h + 1) * GLA_DV)
                items.append(dict(
                    d=d, h=h, r0=r0, vs=vs, o_ref=o_ref, mask=mask,
                    q_e=q_e[:, ks], k_e=k_e[:, ks], k_d=k_d[:, ks], decay=decay[:, ks],
                    v=v_ref[r0:r0 + CHUNK, vs]))
        for it in items:
            k_pad = jnp.concatenate([it["k_e"], jnp.zeros((GLA_DK - CHUNK, GLA_DK), BF16)], axis=0)
            it["scores"] = lax.dot_general(it["q_e"], k_pad, _NT_DIMS,
                                           preferred_element_type=F32)
        return items

    items = start(0)
    next_cum = None
    for step in range(nc):
        following = start(step + 1) if step + 1 < nc else None
        if step == 1:
            next_f = _gla_log_decay_split(lf_next, wgf, bgf)
            next_b = _gla_log_decay_split(lb_next, wgb, bgb)
        if step == nc - 1:
            next_cum = (_gla_chunk_cumsum(next_f, reverse=False, tb=tb),
                        _gla_chunk_cumsum(next_b, reverse=True, tb=tb))
        for it in items:
            scores = jnp.where(it["mask"], it["scores"], 0.0).astype(BF16)
            s_old = s_ref[it["d"], it["h"]]
            lhs = jnp.concatenate([scores, it["q_e"]], axis=1)
            rhs = jnp.concatenate(
                [it["v"], jnp.zeros((GLA_DK - CHUNK, GLA_DV), BF16), s_old.astype(BF16)], axis=0)
            it["o_ref"][it["r0"]:it["r0"] + CHUNK, it["vs"]] = jnp.dot(
                lhs, rhs, preferred_element_type=F32).astype(BF16)
            ds = lax.dot_general(it["k_d"], it["v"], (((0,), (0,)), ((), ())),
                                 preferred_element_type=F32)
            dcol = jnp.transpose(jnp.broadcast_to(it["decay"], (SUBLANES, GLA_DK)))[:, 0:1]
            s_ref[it["d"], it["h"]] = dcol * s_old + ds
        items = following
    cum_ref[0] = next_cum[0]
    cum_ref[1] = next_cum[1]
    if cast:
        job.cast(i)
        job.after(i)

    @pl.when(i == pl.num_programs(0) - 1)
    def _():
        sf[...] = s_ref[0]
        sb[...] = s_ref[1]


def _gla_call(p, l, wgf, bgf, wgb, bgb, s0f, s0b, cast_weight=None, *, tb):
    n = p.shape[0]
    nb = n // tb
    fwd = lambda i: i
    bwd = lambda i: nb - 1 - i
    fwd_next = lambda i: jnp.minimum(i + 1, nb - 1)
    bwd_next = lambda i: jnp.maximum(nb - 2 - i, 0)

    def specs(pos, pos_next):
        return [
            pl.BlockSpec((tb, 2 * GLA_KW + GLA_VW), lambda i: (pos(i), 0)),
            pl.BlockSpec((tb, LANES), lambda i: (pos(i), 0)),
            pl.BlockSpec((tb, LANES), lambda i: (pos_next(i), 0)),
        ]

    const2 = lambda i: (0, 0)
    const3 = lambda i: (0, 0, 0)
    state_shape = (GLA_HEADS, GLA_DK, GLA_DV)
    state_spec = pl.BlockSpec(state_shape, const3)
    cast = cast_weight is not None
    kern = functools.partial(_gla_kernel, tb=tb, cast=cast)
    vmem = (2 * 2 * tb * (2 * GLA_KW + GLA_VW + LANES) * 2 + 2 * 2 * tb * GLA_VW * 2
            + 10 * GLA_HEADS * GLA_DK * GLA_DV * 4 + 24 * tb * GLA_KW * 4 + 2 * SPILL_BYTES)
    in_specs = specs(fwd, fwd_next) + specs(bwd, bwd_next) + [
        pl.BlockSpec((LANES, GLA_KW), const2), pl.BlockSpec((1, GLA_KW), const2),
        pl.BlockSpec((LANES, GLA_KW), const2), pl.BlockSpec((1, GLA_KW), const2),
        state_spec, state_spec,
    ]
    out_specs = [
        pl.BlockSpec((tb, GLA_VW), lambda i: (i, 0)),
        pl.BlockSpec((tb, GLA_VW), lambda i: (nb - 1 - i, 0)),
        state_spec, state_spec,
    ]
    out_shape = [
        jax.ShapeDtypeStruct((n, GLA_VW), BF16), jax.ShapeDtypeStruct((n, GLA_VW), BF16),
        jax.ShapeDtypeStruct(state_shape, F32), jax.ShapeDtypeStruct(state_shape, F32),
    ]
    scratch = [pltpu.VMEM((2,) + state_shape, F32), pltpu.VMEM((2, tb, GLA_KW), F32)]
    operands = [p, l, l, p, l, l, wgf, bgf, wgb, bgb, s0f, s0b]
    if cast:
        assert cast_weight.shape == (FFN_HIDDEN, D_MODEL)
        assert nb >= FFN_HIDDEN // GLA_CAST_ROWS + 2
        any_spec = pl.BlockSpec(memory_space=pl.ANY)
        in_specs.append(any_spec)
        out_specs.append(any_spec)
        out_shape.append(jax.ShapeDtypeStruct(cast_weight.shape, BF16))
        scratch += _cast_scratch(GLA_CAST_ROWS, D_MODEL)
        operands.append(cast_weight)
        vmem += GLA_CAST_ROWS * D_MODEL * (2 * 4 + 2 * 2)
    return pl.pallas_call(
        kern,
        grid=(nb,),
        in_specs=in_specs,
        out_specs=out_specs,
        out_shape=out_shape,
        scratch_shapes=scratch,
        compiler_params=_params(1, vmem),
        name="gla",
    )(*operands)


def _mix_kernel(of_ref, ob_ref, gcb_ref, ccx_ref, x_ref, gt_ref, gg_ref, cw_ref,
                wo_ref, o_ref, m_ref, *, bm):
    g_ref, cb_ref = gcb_ref.at[:, :GLA_VW], gcb_ref.at[:, GLA_VW:]
    cc_ref, cx_ref = ccx_ref.at[:, :CONV_W], ccx_ref.at[:, CONV_W:]

    u = cc_ref[...].astype(F32) * cx_ref[...].astype(F32)
    col = lax.broadcasted_iota(jnp.int32, u.shape, 0) % GRID_W
    u_prev = jnp.where(col == 0, 0.0, pltpu.roll(u, 1, 0))
    u_next = jnp.where(col == GRID_W - 1, 0.0, pltpu.roll(u, bm - 1, 0))
    conv = cw_ref[0:1, :] * u_prev + cw_ref[1:2, :] * u + cw_ref[2:3, :] * u_next
    m_ref[:, GLA_VW:] = (cb_ref[...].astype(F32) * conv).astype(BF16)
    mixed = jnp.dot(m_ref[:, GLA_VW:], wo_ref[GLA_VW:, :].astype(BF16),
                    preferred_element_type=F32)

    gg = gg_ref[...]
    for h in range(GLA_HEADS):
        vs = slice(h * GLA_DV, (h + 1) * GLA_DV)
        o = of_ref[:, vs].astype(F32) + ob_ref[:, vs].astype(F32)
        o32 = o * lax.rsqrt(jnp.mean(o * o, axis=-1, keepdims=True) + EPS)
        m_ref[:, vs] = ((o32 * gg) * _silu(g_ref[:, vs].astype(F32))).astype(BF16)
        mixed = mixed + jnp.dot(m_ref[:, vs], wo_ref[vs, :].astype(BF16),
                                preferred_element_type=F32)
    o_ref[...] = x_ref[...] + gt_ref[0:1, :] * mixed


MIX_BUFFERS = 3


def _mix_pipelined_kernel(of_hbm, ob_hbm, p_hbm, x_hbm, mod_ref, gg_ref, cw_ref, wo_ref,
                          o_hbm, m_ref, w_ref, *, bm):
    for r0 in range(0, D_MODEL, W_ROWS):
        w_ref[r0:r0 + W_ROWS, :] = wo_ref[r0:r0 + W_ROWS, :].astype(BF16)

    def step(of_ref, ob_ref, gcb_ref, ccx_ref, x_ref, o_ref):
        _mix_kernel(of_ref, ob_ref, gcb_ref, ccx_ref, x_ref, mod_ref.at[:, 0:D_MODEL], gg_ref,
                    cw_ref, w_ref, o_ref, m_ref, bm=bm)

    deep = pl.Buffered(MIX_BUFFERS)
    rows = lambda w, k=0: pl.BlockSpec((bm, w), lambda i, k=k: (i, k), pipeline_mode=deep)
    pltpu.emit_pipeline(
        step,
        grid=(x_hbm.shape[0] // bm,),
        in_specs=[rows(GLA_VW), rows(GLA_VW), rows(2 * CONV_W, 1), rows(2 * CONV_W, 2),
                  rows(D_MODEL)],
        out_specs=[pl.BlockSpec((bm, D_MODEL), lambda i: (i, 0))],
    )(of_hbm, ob_hbm, p_hbm, p_hbm, x_hbm, o_hbm)


def _mix_call(o_f, o_b, p, x, mod, gla_g, conv_w, w_out, *, bm):
    n = x.shape[0]
    assert GLA_VW == CONV_W
    any_spec = pl.BlockSpec(memory_space=pl.ANY)
    vmem_spec = pl.BlockSpec(memory_space=pltpu.VMEM)
    vmem = (MIX_BUFFERS * bm * (2 * GLA_VW * 2 + 2 * 2 * CONV_W * 2 + D_MODEL * 4)
            + 2 * bm * D_MODEL * 4 + D_MODEL * D_MODEL * (4 + 2) + bm * D_MODEL * 2
            + 8 * bm * GLA_VW * 4 + SPILL_BYTES)
    return pl.pallas_call(
        functools.partial(_mix_pipelined_kernel, bm=bm),
        in_specs=[any_spec, any_spec, any_spec, any_spec,
                  vmem_spec, vmem_spec, vmem_spec, vmem_spec],
        out_specs=any_spec,
        out_shape=jax.ShapeDtypeStruct((n, D_MODEL), F32),
        scratch_shapes=[pltpu.VMEM((bm, D_MODEL), BF16), pltpu.VMEM((D_MODEL, D_MODEL), BF16)],
        compiler_params=pltpu.CompilerParams(
            vmem_limit_bytes=min(int(vmem), VMEM_BYTES - VMEM_RESERVED)),
        name="mix",
    )(o_f, o_b, p, x, mod, gla_g, conv_w, w_out)


def _ffn_kernel(x_ref, g_ref, sh_ref, sc_ref, gt_ref, wg_ref, wu_ref, wd_ref, fg_ref,
                o_ref, h_ref, *, bm):
    j = pl.program_id(1)

    @pl.when(j == 0)
    def _():
        _norm_mod_rows(x_ref, g_ref[...], 1.0 + sc_ref[0:1, :], sh_ref[0:1, :], h_ref, bm,
                       copy_ref=o_ref)

    h = h_ref[...]
    gate = jnp.dot(h, wg_ref[...], preferred_element_type=F32)
    up = jnp.dot(h, wu_ref[...], preferred_element_type=F32)
    part = jnp.dot((_silu(gate) * up).astype(BF16), wd_ref[...], preferred_element_type=F32)
    o_ref[...] += gt_ref[0:1, :] * part

    @pl.when(j == pl.num_programs(1) - 1)
    def _():
        fg = fg_ref[...]

        def final_norm(rs):
            xn = o_ref[rs, :]
            ms = jnp.mean(xn * xn, axis=-1, keepdims=True)
            o_ref[rs, :] = (xn * lax.rsqrt(ms + EPS)) * fg
        _for_row_chunks(bm, final_norm)


def _ffn_call(x, g, mod, wg, wu, wd, fg, *, bm):
    n = x.shape[0]
    kern = functools.partial(_ffn_kernel, bm=bm)
    modspec = lambda k: pl.BlockSpec((SUBLANES, D_MODEL), lambda i, j, k=k: (0, k))
    vmem = (3 * bm * D_MODEL * 4 + bm * D_MODEL * 2 + 2 * 3 * D_MODEL * FFN_BH * 2
            + 4 * bm * FFN_BH * 4 + bm * D_MODEL * 4 + SPILL_BYTES)
    return pl.pallas_call(
        kern,
        grid=(n // bm, FFN_HIDDEN // FFN_BH),
        in_specs=[
            pl.BlockSpec((bm, D_MODEL), lambda i, j: (i, 0)),
            pl.BlockSpec((1, D_MODEL), lambda i, j: (0, 0)),
            modspec(1), modspec(2), modspec(3),
            pl.BlockSpec((D_MODEL, FFN_BH), lambda i, j: (0, j)),
            pl.BlockSpec((D_MODEL, FFN_BH), lambda i, j: (0, j)),
            pl.BlockSpec((FFN_BH, D_MODEL), lambda i, j: (j, 0)),
            pl.BlockSpec((1, D_MODEL), lambda i, j: (0, 0)),
        ],
        out_specs=pl.BlockSpec((bm, D_MODEL), lambda i, j: (i, 0)),
        out_shape=jax.ShapeDtypeStruct((n, D_MODEL), F32),
        scratch_shapes=[pltpu.VMEM((bm, D_MODEL), BF16)],
        compiler_params=_params(2, vmem),
        name="ffn",
    )(x, g, mod, mod, mod, wg, wu, wd, fg)


def kernel(x, c, ctx, c_ctx, w_mod, b_mod, norm1_g, norm2_g, w_in, w_gate_f, b_gate_f,
           w_gate_b, b_gate_b, gla_norm_g, conv_w, w_out, w_ffn_gate, w_ffn_up, w_ffn_down,
           final_g):
    assert x.shape[0] == 1 and w_mod.shape[0] == 1, "batch 1, depth 1 only"
    seq, ctx_len = x.shape[1], ctx.shape[1]
    x2, ctx2 = x[0], ctx[0]

    c_t = jnp.concatenate(
        [c[0][:, None], c_ctx[:, None], jnp.zeros((D_MODEL, SUBLANES - 2), F32)], axis=1)
    mod = _mod_call(c_t, w_mod[0], b_mod, EARLY_MOD_COLS)

    pad_f = jnp.zeros((LANES, GLA_KW), F32).at[:GATE_RANK].set(w_gate_f[0]).astype(BF16)
    pad_b = jnp.zeros((LANES, GLA_KW), F32).at[GATE_RANK:2 * GATE_RANK].set(w_gate_b[0]).astype(BF16)
    g1 = norm1_g[0][None]
    w_in_t = jnp.transpose(w_in[0])

    h_c, l_c = _prenorm_call(ctx2, g1, mod, w_in_t, row=1, bm=ctx_len)
    p_c, = _in_proj_call(h_c, w_in_t, bm=ctx_len, n_tiles=2)
    zero_state = jnp.zeros((GLA_HEADS, GLA_DK, GLA_DV), F32)
    _, _, s_f, s_b = _gla_call(p_c, l_c, pad_f, b_gate_f, pad_b, b_gate_b, zero_state,
                               zero_state, tb=ctx_len)

    h_x, l_x = _prenorm_call(x2, g1, mod, w_in_t, row=0, bm=1024)
    p_x, wg_b, wu_b, mod_late = _in_proj_call(
        h_x, w_in_t, (w_ffn_gate[0], w_ffn_up[0], c_t, w_mod[0], b_mod, EARLY_MOD_COLS),
        bm=1024, n_tiles=P_COLS // P_BN)
    o_f, o_b, _, _, wd_b = _gla_call(p_x, l_x, pad_f, b_gate_f, pad_b, b_gate_b, s_f, s_b,
                                     w_ffn_down[0], tb=256)
    x_mid = _mix_call(o_f, o_b, p_x, x2, mod_late, gla_norm_g, conv_w[0], w_out[0], bm=256)
    out = _ffn_call(x_mid, norm2_g[0][None], mod_late, wg_b, wu_b, wd_b, final_g[None],
                    bm=1024)
    return out[None]
```

```python
import functools

import jax
import jax.numpy as jnp
from jax import lax
from jax.experimental import pallas as pl
from jax.experimental.pallas import tpu as pltpu

F32 = jnp.float32
BF16 = jnp.bfloat16

D_MODEL = 2048
GRID_W = 64
GLA_HEADS = 4
GLA_DK = 128
GLA_DV = 256
GLA_KW = GLA_HEADS * GLA_DK
GLA_VW = GLA_HEADS * GLA_DV
GATE_RANK = 16
GATE_NORM = 16.0
CHUNK = 64
LOG2_E = 1.4426950408889634
CONV_W = D_MODEL - GLA_VW
FFN_HIDDEN = 5632
EPS = 1e-6

LANES = 128
SUBLANES = 8
MIB = 1024 * 1024
VMEM_BYTES = 64 * MIB
VMEM_RESERVED = 6 * MIB
SPILL_BYTES = 4 * MIB

GATE_COL = 2 * GLA_KW + 2 * GLA_VW
GATE_COLS = 2 * GATE_RANK
P_COLS = GATE_COL + 3 * CONV_W
P_BN = 1024
P_SHIFT_FROM = GATE_COL // P_BN

_NT_DIMS = (((1,), (1,)), ((), ()))

MOD_BN = 1024
EARLY_MOD_COLS = 2 * D_MODEL
LATE_MOD_BN = 256
FFN_BH = 512
NORM_ROWS = 16


def _params(grid_rank, vmem_bytes):
    return pltpu.CompilerParams(
        dimension_semantics=("arbitrary",) * grid_rank,
        vmem_limit_bytes=min(int(vmem_bytes), VMEM_BYTES - VMEM_RESERVED),
    )


def _silu(v):
    return v * jax.nn.sigmoid(v)


def _for_row_chunks(rows, body):
    for r0 in range(0, rows, NORM_ROWS):
        body(slice(r0, r0 + NORM_ROWS))


def _norm_mod_rows(x_ref, g, mul, sh, dst_ref, rows, copy_ref=None):
    gain = g * mul

    def body(rs):
        xs = x_ref[rs, :]
        ms = jnp.mean(xs * xs, axis=-1, keepdims=True)
        y = xs * lax.rsqrt(ms + EPS)
        dst_ref[rs, :] = (y * gain + sh).astype(BF16)
        if copy_ref is not None:
            copy_ref[rs, :] = xs
    _for_row_chunks(rows, body)


def _mod_prepare(c_ref, sb_ref):
    s = _silu(c_ref[...])
    for r in range(2):
        sb_ref[r] = jnp.broadcast_to(s[:, r:r + 1], (D_MODEL, LANES))


def _mod_matvec(w_ref, b_ref, sb_ref, o_ref, *, unrolled):
    nt = w_ref.shape[1] // LANES

    def body(kk, accs):
        r0 = kk * SUBLANES if unrolled else pl.multiple_of(kk * SUBLANES, SUBLANES)
        s0 = sb_ref[0, pl.ds(r0, SUBLANES), :]
        s1 = sb_ref[1, pl.ds(r0, SUBLANES), :]
        new = []
        for t in range(nt):
            w8 = w_ref[pl.ds(r0, SUBLANES), t * LANES:(t + 1) * LANES]
            new.append(accs[2 * t] + w8 * s0)
            new.append(accs[2 * t + 1] + w8 * s1)
        return tuple(new)

    accs = (jnp.zeros((SUBLANES, LANES), F32),) * (2 * nt)
    if unrolled:
        for kk in range(D_MODEL // SUBLANES):
            accs = body(kk, accs)
    else:
        accs = lax.fori_loop(0, D_MODEL // SUBLANES, body, accs, unroll=8)
    o_ref[...] = jnp.zeros(o_ref.shape, F32)
    for t in range(nt):
        b = b_ref[:, t * LANES:(t + 1) * LANES]
        for r in range(2):
            o_ref[r:r + 1, t * LANES:(t + 1) * LANES] = (
                jnp.sum(accs[2 * t + r], axis=0, keepdims=True) + b)


STREAM_BUFFERS = 3


def _mod_kernel(c_ref, w_hbm, b_hbm, o_hbm, sb_ref, *, n):
    _mod_prepare(c_ref, sb_ref)

    def step(w_ref, b_ref, o_ref):
        _mod_matvec(w_ref, b_ref, sb_ref, o_ref, unrolled=False)

    pltpu.emit_pipeline(
        step,
        grid=(n // MOD_BN,),
        in_specs=[pl.BlockSpec((D_MODEL, MOD_BN), lambda j: (0, j),
                               pipeline_mode=pl.Buffered(STREAM_BUFFERS)),
                  pl.BlockSpec((1, MOD_BN), lambda j: (0, j))],
        out_specs=[pl.BlockSpec((SUBLANES, MOD_BN), lambda j: (0, j))],
    )(w_hbm, b_hbm, o_hbm)


def _mod_call(c_t, w_mod, b_mod, n):
    any_spec = pl.BlockSpec(memory_space=pl.ANY)
    vmem = STREAM_BUFFERS * D_MODEL * MOD_BN * 4 + 2 * D_MODEL * LANES * 4 + 2 * SPILL_BYTES
    return pl.pallas_call(
        functools.partial(_mod_kernel, n=n),
        in_specs=[pl.BlockSpec(memory_space=pltpu.VMEM), any_spec, any_spec],
        out_specs=any_spec,
        out_shape=jax.ShapeDtypeStruct((SUBLANES, n), F32),
        scratch_shapes=[pltpu.VMEM((2, D_MODEL, LANES), F32)],
        compiler_params=pltpu.CompilerParams(
            vmem_limit_bytes=min(int(vmem), VMEM_BYTES - VMEM_RESERVED)),
        name="mod",
    )(c_t, w_mod, b_mod)


def _prenorm_kernel(x_ref, g_ref, sh_ref, sc_ref, wl_ref, h_ref, l_ref, *, row, bm):
    _norm_mod_rows(x_ref, g_ref[...], 1.0 + sc_ref[row:row + 1, :], sh_ref[row:row + 1, :],
                   h_ref, bm)
    l_ref[...] = lax.dot_general(h_ref[...], wl_ref[...].astype(BF16), _NT_DIMS,
                                 preferred_element_type=F32).astype(BF16)


def _prenorm_streamed_kernel(x_hbm, g_ref, mod_ref, wl_ref, h_hbm, l_hbm, *, row, bm):
    def step(x_ref, h_ref, l_ref):
        _prenorm_kernel(x_ref, g_ref, mod_ref.at[:, 0:D_MODEL], mod_ref.at[:, D_MODEL:2 * D_MODEL],
                        wl_ref, h_ref, l_ref, row=row, bm=bm)

    pltpu.emit_pipeline(
        step,
        grid=(x_hbm.shape[0] // bm,),
        in_specs=[pl.BlockSpec((bm, D_MODEL), lambda i: (i, 0),
                               pipeline_mode=pl.Buffered(STREAM_BUFFERS))],
        out_specs=[pl.BlockSpec((bm, D_MODEL), lambda i: (i, 0)),
                   pl.BlockSpec((bm, LANES), lambda i: (i, 0))],
    )(x_hbm, h_hbm, l_hbm)


def _prenorm_call(x, g, mod, w_in_t, *, row, bm):
    m = x.shape[0]
    out_shape = [jax.ShapeDtypeStruct((m, D_MODEL), BF16), jax.ShapeDtypeStruct((m, LANES), BF16)]
    if m // bm >= STREAM_BUFFERS:
        any_spec = pl.BlockSpec(memory_space=pl.ANY)
        vmem_spec = pl.BlockSpec(memory_space=pltpu.VMEM)
        w_gate = lax.slice(w_in_t, (GATE_COL, 0), (GATE_COL + LANES, D_MODEL))
        vmem = (STREAM_BUFFERS * bm * D_MODEL * 4 + 2 * bm * D_MODEL * 2
                + 4 * D_MODEL * LANES * 4 + 8 * NORM_ROWS * D_MODEL * 4 + SPILL_BYTES)
        return pl.pallas_call(
            functools.partial(_prenorm_streamed_kernel, row=row, bm=bm),
            in_specs=[any_spec, vmem_spec, vmem_spec, vmem_spec],
            out_specs=[any_spec, any_spec],
            out_shape=out_shape,
            compiler_params=pltpu.CompilerParams(
                vmem_limit_bytes=min(int(vmem), VMEM_BYTES - VMEM_RESERVED)),
            name="prenorm",
        )(x, g, mod, w_gate)
    kern = functools.partial(_prenorm_kernel, row=row, bm=bm)
    vmem = (2 * bm * D_MODEL * (4 + 2) + 4 * D_MODEL * LANES * 4 + 8 * NORM_ROWS * D_MODEL * 4
            + SPILL_BYTES)
    return pl.pallas_call(
        kern,
        grid=(m // bm,),
        in_specs=[
            pl.BlockSpec((bm, D_MODEL), lambda i: (i, 0)),
            pl.BlockSpec((1, D_MODEL), lambda i: (0, 0)),
            pl.BlockSpec((SUBLANES, D_MODEL), lambda i: (0, 0)),
            pl.BlockSpec((SUBLANES, D_MODEL), lambda i: (0, 1)),
            pl.BlockSpec((LANES, D_MODEL), lambda i: (GATE_COL // LANES, 0)),
        ],
        out_specs=[pl.BlockSpec((bm, D_MODEL), lambda i: (i, 0)),
                   pl.BlockSpec((bm, LANES), lambda i: (i, 0))],
        out_shape=out_shape,
        compiler_params=_params(1, vmem),
        name="prenorm",
    )(x, g, mod, mod, w_in_t)


W_ROWS = 256


class _CastJob:
    def __init__(self, sources, rows, stage_ref, out_ref, sem_in, sem_out):
        self.sources, self.rows = sources, rows
        self.stage_ref, self.out_ref, self.sem_in, self.sem_out = stage_ref, out_ref, sem_in, sem_out
        self.n_blocks = sum(n for _, _, n in sources)

    def _for_block(self, c, fn):
        first = 0
        for src, dst, n in self.sources:
            @pl.when((c >= first) & (c < first + n))
            def _(src=src, dst=dst, first=first):
                fn(src, dst, pl.multiple_of((c - first) * self.rows, self.rows), lax.rem(c, 2))
            first += n

    def _fetch(self, src, dst, r, slot):
        return pltpu.make_async_copy(src.at[pl.ds(r, self.rows), :], self.stage_ref.at[slot],
                                     self.sem_in.at[slot])

    def _store(self, src, dst, r, slot):
        return pltpu.make_async_copy(self.out_ref.at[slot], dst.at[pl.ds(r, self.rows), :],
                                     self.sem_out.at[slot])

    def before(self, step):
        @pl.when(step == 0)
        def _():
            self._for_block(step, lambda *a: self._fetch(*a).start())
        self._for_block(step, lambda *a: self._fetch(*a).wait())
        self._for_block(step + 1, lambda *a: self._fetch(*a).start())
        self._for_block(step - 2, lambda *a: self._store(*a).wait())

    def cast(self, step):
        slot = lax.rem(step, 2)
        self.out_ref[slot] = self.stage_ref[slot].astype(BF16)

    def after(self, step):
        self._for_block(step, lambda *a: self._store(*a).start())


def _cast_scratch(rows, cols):
    return [pltpu.VMEM((2, rows, cols), F32), pltpu.VMEM((2, rows, cols), BF16),
            pltpu.SemaphoreType.DMA((2,)), pltpu.SemaphoreType.DMA((2,))]


IN_PROJ_CAST_ROWS = 128
GLA_CAST_ROWS = 256


def _in_proj_kernel(h_ref, wa_ref, wb_ref, *refs, side_jobs):
    j = pl.program_id(0)
    if side_jobs:
        (wg_ref, wu_ref, c_ref, wm_ref, bm_ref, o_ref, wg_out, wu_out, mod_ref,
         w_ref, stage_ref, cast_ref, sem_in, sem_out, sb_ref) = refs
        n = D_MODEL // IN_PROJ_CAST_ROWS
        job = _CastJob(((wg_ref, wg_out, n), (wu_ref, wu_out, n)), IN_PROJ_CAST_ROWS,
                       stage_ref, cast_ref, sem_in, sem_out)
        step = j * pl.num_programs(1) + pl.program_id(1)
        job.before(step)

        @pl.when(step == 0)
        def _():
            _mod_prepare(c_ref, sb_ref)
    else:
        o_ref, w_ref = refs

    @pl.when(pl.program_id(1) == 0)
    def _():
        @pl.when(j < P_SHIFT_FROM)
        def _():
            for r0 in range(0, P_BN, W_ROWS):
                w_ref[r0:r0 + W_ROWS, :] = wa_ref[r0:r0 + W_ROWS, :].astype(BF16)

        @pl.when(j >= P_SHIFT_FROM)
        def _():
            keep = P_BN - GATE_COLS
            for r0 in range(0, keep, W_ROWS):
                r1 = min(r0 + W_ROWS, keep)
                w_ref[r0:r1, :] = wa_ref[GATE_COLS + r0:GATE_COLS + r1, :].astype(BF16)
            w_ref[keep:, :] = wb_ref[...].astype(BF16)

    o_ref[...] = lax.dot_general(h_ref[...], w_ref[...], _NT_DIMS,
                                 preferred_element_type=F32).astype(BF16)
    if side_jobs:
        job.cast(step)
        _mod_matvec(wm_ref, bm_ref, sb_ref, mod_ref, unrolled=True)
        job.after(step)


def _in_proj_call(h, w_in_t, side=None, *, bm, n_tiles):
    m = h.shape[0]
    ni = m // bm
    vmem = (2 * bm * D_MODEL * 2 + 2 * (P_BN + GATE_COLS) * D_MODEL * 4 + P_BN * D_MODEL * 2
            + 2 * bm * P_BN * 2 + bm * P_BN * 4 + 4 * W_ROWS * D_MODEL * 4 + SPILL_BYTES)
    in_specs = [
        pl.BlockSpec((bm, D_MODEL), lambda j, i: (i, 0)),
        pl.BlockSpec((P_BN, D_MODEL), lambda j, i: (j, 0)),
        pl.BlockSpec((GATE_COLS, D_MODEL), lambda j, i: ((j + 1) * (P_BN // GATE_COLS), 0)),
    ]
    out_specs = [pl.BlockSpec((bm, P_BN), lambda j, i: (i, j))]
    out_shape = [jax.ShapeDtypeStruct((m, n_tiles * P_BN), BF16)]
    scratch = [pltpu.VMEM((P_BN, D_MODEL), BF16)]
    operands = [h, w_in_t, w_in_t]
    if side is not None:
        w_gate, w_up, c_t, w_mod, b_mod, first_col = side
        n_steps = n_tiles * ni
        assert n_steps >= 2 * D_MODEL // IN_PROJ_CAST_ROWS + 2
        any_spec = pl.BlockSpec(memory_space=pl.ANY)
        n_mod = w_mod.shape[1] - first_col
        n_blocks = n_mod // LATE_MOD_BN
        assert n_mod % LATE_MOD_BN == 0 and first_col % LATE_MOD_BN == 0 and n_blocks <= n_steps
        mod_block = lambda j, i: jnp.minimum(j * ni + i, n_blocks - 1)
        in_specs += [
            any_spec, any_spec,
            pl.BlockSpec((D_MODEL, SUBLANES), lambda j, i: (0, 0)),
            pl.BlockSpec((D_MODEL, LATE_MOD_BN),
                         lambda j, i: (0, first_col // LATE_MOD_BN + mod_block(j, i))),
            pl.BlockSpec((1, LATE_MOD_BN),
                         lambda j, i: (0, first_col // LATE_MOD_BN + mod_block(j, i))),
        ]
        out_specs += [any_spec, any_spec,
                      pl.BlockSpec((SUBLANES, LATE_MOD_BN), lambda j, i: (0, mod_block(j, i)))]
        out_shape += [jax.ShapeDtypeStruct(w_gate.shape, BF16),
                      jax.ShapeDtypeStruct(w_up.shape, BF16),
                      jax.ShapeDtypeStruct((SUBLANES, n_mod), F32)]
        scratch += _cast_scratch(IN_PROJ_CAST_ROWS, FFN_HIDDEN)
        scratch += [pltpu.VMEM((2, D_MODEL, LANES), F32)]
        operands += [w_gate, w_up, c_t, w_mod, b_mod]
        vmem += (IN_PROJ_CAST_ROWS * FFN_HIDDEN * (2 * 4 + 2 * 2)
                 + 2 * D_MODEL * LATE_MOD_BN * 4 + 2 * D_MODEL * LANES * 4)
    return pl.pallas_call(
        functools.partial(_in_proj_kernel, side_jobs=side is not None),
        grid=(n_tiles, ni),
        in_specs=in_specs,
        out_specs=out_specs,
        out_shape=out_shape,
        scratch_shapes=scratch,
        compiler_params=_params(2, vmem),
        name="in_proj",
    )(*operands)


def _gla_log_decay_split(l_ref, wg_ref, bg_ref):
    z = jnp.dot(l_ref[...], wg_ref[...], preferred_element_type=F32) + bg_ref[...]
    log_a = (jnp.minimum(z, 0.0) - jnp.log(1.0 + jnp.exp(-jnp.abs(z)))) * (1.0 / GATE_NORM)
    hi = log_a.astype(BF16)
    lo = (log_a - hi.astype(F32)).astype(BF16)
    return hi, lo


def _gla_chunk_cumsum(pieces, *, reverse, tb):
    r = lax.broadcasted_iota(jnp.int32, (tb, tb), 0)
    cc = lax.broadcasted_iota(jnp.int32, (tb, tb), 1)
    same_chunk = (r // CHUNK) == (cc // CHUNK)
    tri = jnp.where(same_chunk & ((cc >= r) if reverse else (cc <= r)), 1.0, 0.0).astype(BF16)
    hi, lo = pieces
    return (jnp.dot(tri, hi, preferred_element_type=F32)
            + jnp.dot(tri, lo, preferred_element_type=F32)) * LOG2_E


def _gla_kernel(qkv_f, lf, lf_next, qkv_b, lb, lb_next, wgf, bgf, wgb, bgb, s0f, s0b,
                *refs, tb, cast):
    qf, kf, vf = (qkv_f.at[:, :GLA_KW], qkv_f.at[:, GLA_KW:2 * GLA_KW], qkv_f.at[:, 2 * GLA_KW:])
    qb, kb, vb = (qkv_b.at[:, :GLA_KW], qkv_b.at[:, GLA_KW:2 * GLA_KW], qkv_b.at[:, 2 * GLA_KW:])
    i = pl.program_id(0)
    if cast:
        (wd_ref, wo_ref, of, ob, sf, sb, wd_out, wo_out, s_ref, cum_ref,
         stage_ref, cast_ref, sem_in, sem_out) = refs
        job = _CastJob(((wd_ref, wd_out, FFN_HIDDEN // GLA_CAST_ROWS),
                        (wo_ref, wo_out, D_MODEL // GLA_CAST_ROWS)), GLA_CAST_ROWS,
                       stage_ref, cast_ref, sem_in, sem_out)
        job.before(i)
    else:
        of, ob, sf, sb, s_ref, cum_ref = refs

    @pl.when(i == 0)
    def _():
        s_ref[0] = s0f[...]
        s_ref[1] = s0b[...]
        cum_ref[0] = _gla_chunk_cumsum(_gla_log_decay_split(lf, wgf, bgf), reverse=False, tb=tb)
        cum_ref[1] = _gla_chunk_cumsum(_gla_log_decay_split(lb, wgb, bgb), reverse=True, tb=tb)

    dirs = ((qf, kf, vf, of, False), (qb, kb, vb, ob, True))
    ii = lax.broadcasted_iota(jnp.int32, (CHUNK, GLA_DK), 0)
    jj = lax.broadcasted_iota(jnp.int32, (CHUNK, GLA_DK), 1)
    nc = tb // CHUNK

    def start(step):
        items = []
        for d, (q_ref, k_ref, v_ref, o_ref, reverse) in enumerate(dirs):
            r0 = (nc - 1 - step if reverse else step) * CHUNK
            cum_c = cum_ref[d, r0:r0 + CHUNK, :]
            last = cum_c[0:1, :] if reverse else cum_c[CHUNK - 1:CHUNK, :]
            q_c = q_ref[r0:r0 + CHUNK, :].astype(F32)
            k_c = k_ref[r0:r0 + CHUNK, :].astype(F32)
            q_e = ((q_c * (GLA_DK ** -0.5)) * jnp.exp2(cum_c)).astype(BF16)
            k_e = (k_c * jnp.exp2(-cum_c)).astype(BF16)
            k_d = (k_c * jnp.exp2(last - cum_c)).astype(BF16)
            decay = jnp.exp2(last)
            mask = ((jj >= ii) & (jj < CHUNK)) if reverse else (jj <= ii)
            for h in range(GLA_HEADS):
                ks = slice(h * GLA_DK, (h + 1) * GLA_DK)
                vs = slice(h * GLA_DV, (h + 1) * GLA_DV)
                items.append(dict(
                    d=d, h=h, r0=r0, vs=vs, o_ref=o_ref, mask=mask,
                    q_e=q_e[:, ks], k_e=k_e[:, ks], k_d=k_d[:, ks], decay=decay[:, ks],
                    v=v_ref[r0:r0 + CHUNK, vs]))
        for it in items:
            k_pad = jnp.concatenate([it["k_e"], jnp.zeros((GLA_DK - CHUNK, GLA_DK), BF16)], axis=0)
            it["scores"] = lax.dot_general(it["q_e"], k_pad, _NT_DIMS,
                                           preferred_element_type=F32)
        return items

    items = start(0)
    next_cum = None
    for step in range(nc):
        following = start(step + 1) if step + 1 < nc else None
        if step == 1:
            next_f = _gla_log_decay_split(lf_next, wgf, bgf)
            next_b = _gla_log_decay_split(lb_next, wgb, bgb)
        if step == nc - 1:
            next_cum = (_gla_chunk_cumsum(next_f, reverse=False, tb=tb),
                        _gla_chunk_cumsum(next_b, reverse=True, tb=tb))
        for it in items:
            scores = jnp.where(it["mask"], it["scores"], 0.0).astype(BF16)
            s_old = s_ref[it["d"], it["h"]]
            lhs = jnp.concatenate([scores, it["q_e"]], axis=1)
            rhs = jnp.concatenate(
                [it["v"], jnp.zeros((GLA_DK - CHUNK, GLA_DV), BF16), s_old.astype(BF16)], axis=0)
            it["o_ref"][it["r0"]:it["r0"] + CHUNK, it["vs"]] = jnp.dot(
                lhs, rhs, preferred_element_type=F32).astype(BF16)
            ds = lax.dot_general(it["k_d"], it["v"], (((0,), (0,)), ((), ())),
                                 preferred_element_type=F32)
            dcol = jnp.transpose(jnp.broadcast_to(it["decay"], (SUBLANES, GLA_DK)))[:, 0:1]
            s_ref[it["d"], it["h"]] = dcol * s_old + ds
        items = following
    cum_ref[0] = next_cum[0]
    cum_ref[1] = next_cum[1]
    if cast:
        job.cast(i)
        job.after(i)

    @pl.when(i == pl.num_programs(0) - 1)
    def _():
        sf[...] = s_ref[0]
        sb[...] = s_ref[1]


def _gla_call(p, l, wgf, bgf, wgb, bgb, s0f, s0b, cast_weight=None, *, tb):
    n = p.shape[0]
    nb = n // tb
    fwd = lambda i: i
    bwd = lambda i: nb - 1 - i
    fwd_next = lambda i: jnp.minimum(i + 1, nb - 1)
    bwd_next = lambda i: jnp.maximum(nb - 2 - i, 0)

    def specs(pos, pos_next):
        return [
            pl.BlockSpec((tb, 2 * GLA_KW + GLA_VW), lambda i: (pos(i), 0)),
            pl.BlockSpec((tb, LANES), lambda i: (pos(i), 0)),
            pl.BlockSpec((tb, LANES), lambda i: (pos_next(i), 0)),
        ]

    const2 = lambda i: (0, 0)
    const3 = lambda i: (0, 0, 0)
    state_shape = (GLA_HEADS, GLA_DK, GLA_DV)
    state_spec = pl.BlockSpec(state_shape, const3)
    cast = cast_weight is not None
    kern = functools.partial(_gla_kernel, tb=tb, cast=cast)
    vmem = (2 * 2 * tb * (2 * GLA_KW + GLA_VW + LANES) * 2 + 2 * 2 * tb * GLA_VW * 2
            + 10 * GLA_HEADS * GLA_DK * GLA_DV * 4 + 24 * tb * GLA_KW * 4 + 2 * SPILL_BYTES)
    in_specs = specs(fwd, fwd_next) + specs(bwd, bwd_next) + [
        pl.BlockSpec((LANES, GLA_KW), const2), pl.BlockSpec((1, GLA_KW), const2),
        pl.BlockSpec((LANES, GLA_KW), const2), pl.BlockSpec((1, GLA_KW), const2),
        state_spec, state_spec,
    ]
    out_specs = [
        pl.BlockSpec((tb, GLA_VW), lambda i: (i, 0)),
        pl.BlockSpec((tb, GLA_VW), lambda i: (nb - 1 - i, 0)),
        state_spec, state_spec,
    ]
    out_shape = [
        jax.ShapeDtypeStruct((n, GLA_VW), BF16), jax.ShapeDtypeStruct((n, GLA_VW), BF16),
        jax.ShapeDtypeStruct(state_shape, F32), jax.ShapeDtypeStruct(state_shape, F32),
    ]
    scratch = [pltpu.VMEM((2,) + state_shape, F32), pltpu.VMEM((2, tb, GLA_KW), F32)]
    operands = [p, l, l, p, l, l, wgf, bgf, wgb, bgb, s0f, s0b]
    if cast:
        w_down, w_out = cast_weight
        assert w_down.shape == (FFN_HIDDEN, D_MODEL) and w_out.shape == (D_MODEL, D_MODEL)
        assert nb >= (FFN_HIDDEN + D_MODEL) // GLA_CAST_ROWS + 2
        any_spec = pl.BlockSpec(memory_space=pl.ANY)
        in_specs += [any_spec, any_spec]
        out_specs += [any_spec, any_spec]
        out_shape += [jax.ShapeDtypeStruct(w_down.shape, BF16),
                      jax.ShapeDtypeStruct(w_out.shape, BF16)]
        scratch += _cast_scratch(GLA_CAST_ROWS, D_MODEL)
        operands += [w_down, w_out]
        vmem += GLA_CAST_ROWS * D_MODEL * (2 * 4 + 2 * 2)
    return pl.pallas_call(
        kern,
        grid=(nb,),
        in_specs=in_specs,
        out_specs=out_specs,
        out_shape=out_shape,
        scratch_shapes=scratch,
        compiler_params=_params(1, vmem),
        name="gla",
    )(*operands)


def _mix_kernel(of_ref, ob_ref, gcb_ref, ccx_ref, x_ref, gt_ref, gg_ref, cw_ref,
                wo_ref, o_ref, m_ref, *, bm):
    g_ref, cb_ref = gcb_ref.at[:, :GLA_VW], gcb_ref.at[:, GLA_VW:]
    cc_ref, cx_ref = ccx_ref.at[:, :CONV_W], ccx_ref.at[:, CONV_W:]

    u = cc_ref[...].astype(F32) * cx_ref[...].astype(F32)
    col = lax.broadcasted_iota(jnp.int32, u.shape, 0) % GRID_W
    u_prev = jnp.where(col == 0, 0.0, pltpu.roll(u, 1, 0))
    u_next = jnp.where(col == GRID_W - 1, 0.0, pltpu.roll(u, bm - 1, 0))
    conv = cw_ref[0:1, :] * u_prev + cw_ref[1:2, :] * u + cw_ref[2:3, :] * u_next
    m_ref[:, GLA_VW:] = (cb_ref[...].astype(F32) * conv).astype(BF16)
    mixed = jnp.dot(m_ref[:, GLA_VW:], wo_ref[GLA_VW:, :].astype(BF16),
                    preferred_element_type=F32)

    gg = gg_ref[...]
    for h in range(GLA_HEADS):
        vs = slice(h * GLA_DV, (h + 1) * GLA_DV)
        o = of_ref[:, vs].astype(F32) + ob_ref[:, vs].astype(F32)
        o32 = o * lax.rsqrt(jnp.mean(o * o, axis=-1, keepdims=True) + EPS)
        m_ref[:, vs] = ((o32 * gg) * _silu(g_ref[:, vs].astype(F32))).astype(BF16)
        mixed = mixed + jnp.dot(m_ref[:, vs], wo_ref[vs, :].astype(BF16),
                                preferred_element_type=F32)
    o_ref[...] = x_ref[...] + gt_ref[0:1, :] * mixed


MIX_BUFFERS = 3


def _mix_pipelined_kernel(of_hbm, ob_hbm, p_hbm, x_hbm, mod_ref, gg_ref, cw_ref, wo_ref,
                          o_hbm, m_ref, *, bm):
    def step(of_ref, ob_ref, gcb_ref, ccx_ref, x_ref, o_ref):
        _mix_kernel(of_ref, ob_ref, gcb_ref, ccx_ref, x_ref, mod_ref.at[:, 0:D_MODEL], gg_ref,
                    cw_ref, wo_ref, o_ref, m_ref, bm=bm)

    deep = pl.Buffered(MIX_BUFFERS)
    rows = lambda w, k=0: pl.BlockSpec((bm, w), lambda i, k=k: (i, k), pipeline_mode=deep)
    pltpu.emit_pipeline(
        step,
        grid=(x_hbm.shape[0] // bm,),
        in_specs=[rows(GLA_VW), rows(GLA_VW), rows(2 * CONV_W, 1), rows(2 * CONV_W, 2),
                  rows(D_MODEL)],
        out_specs=[pl.BlockSpec((bm, D_MODEL), lambda i: (i, 0))],
    )(of_hbm, ob_hbm, p_hbm, p_hbm, x_hbm, o_hbm)


def _mix_call(o_f, o_b, p, x, mod, gla_g, conv_w, w_out, *, bm):
    n = x.shape[0]
    assert GLA_VW == CONV_W
    any_spec = pl.BlockSpec(memory_space=pl.ANY)
    vmem_spec = pl.BlockSpec(memory_space=pltpu.VMEM)
    vmem = (MIX_BUFFERS * bm * (2 * GLA_VW * 2 + 2 * 2 * CONV_W * 2 + D_MODEL * 4)
            + 2 * bm * D_MODEL * 4 + D_MODEL * D_MODEL * 2 + bm * D_MODEL * 2
            + 8 * bm * GLA_VW * 4 + SPILL_BYTES)
    return pl.pallas_call(
        functools.partial(_mix_pipelined_kernel, bm=bm),
        in_specs=[any_spec, any_spec, any_spec, any_spec,
                  vmem_spec, vmem_spec, vmem_spec, vmem_spec],
        out_specs=any_spec,
        out_shape=jax.ShapeDtypeStruct((n, D_MODEL), F32),
        scratch_shapes=[pltpu.VMEM((bm, D_MODEL), BF16)],
        compiler_params=pltpu.CompilerParams(
            vmem_limit_bytes=min(int(vmem), VMEM_BYTES - VMEM_RESERVED)),
        name="mix",
    )(o_f, o_b, p, x, mod, gla_g, conv_w, w_out)


def _ffn_kernel(x_ref, g_ref, sh_ref, sc_ref, gt_ref, wg_ref, wu_ref, wd_ref, fg_ref,
                o_ref, h_ref, *, bm):
    j = pl.program_id(1)

    @pl.when(j == 0)
    def _():
        _norm_mod_rows(x_ref, g_ref[...], 1.0 + sc_ref[0:1, :], sh_ref[0:1, :], h_ref, bm,
                       copy_ref=o_ref)

    h = h_ref[...]
    gate = jnp.dot(h, wg_ref[...], preferred_element_type=F32)
    up = jnp.dot(h, wu_ref[...], preferred_element_type=F32)
    part = jnp.dot((_silu(gate) * up).astype(BF16), wd_ref[...], preferred_element_type=F32)
    o_ref[...] += gt_ref[0:1, :] * part

    @pl.when(j == pl.num_programs(1) - 1)
    def _():
        fg = fg_ref[...]

        def final_norm(rs):
            xn = o_ref[rs, :]
            ms = jnp.mean(xn * xn, axis=-1, keepdims=True)
            o_ref[rs, :] = (xn * lax.rsqrt(ms + EPS)) * fg
        _for_row_chunks(bm, final_norm)


def _ffn_call(x, g, mod, wg, wu, wd, fg, *, bm):
    n = x.shape[0]
    kern = functools.partial(_ffn_kernel, bm=bm)
    modspec = lambda k: pl.BlockSpec((SUBLANES, D_MODEL), lambda i, j, k=k: (0, k))
    vmem = (3 * bm * D_MODEL * 4 + bm * D_MODEL * 2 + 2 * 3 * D_MODEL * FFN_BH * 2
            + 4 * bm * FFN_BH * 4 + bm * D_MODEL * 4 + SPILL_BYTES)
    return pl.pallas_call(
        kern,
        grid=(n // bm, FFN_HIDDEN // FFN_BH),
        in_specs=[
            pl.BlockSpec((bm, D_MODEL), lambda i, j: (i, 0)),
            pl.BlockSpec((1, D_MODEL), lambda i, j: (0, 0)),
            modspec(1), modspec(2), modspec(3),
            pl.BlockSpec((D_MODEL, FFN_BH), lambda i, j: (0, j)),
            pl.BlockSpec((D_MODEL, FFN_BH), lambda i, j: (0, j)),
            pl.BlockSpec((FFN_BH, D_MODEL), lambda i, j: (j, 0)),
            pl.BlockSpec((1, D_MODEL), lambda i, j: (0, 0)),
        ],
        out_specs=pl.BlockSpec((bm, D_MODEL), lambda i, j: (i, 0)),
        out_shape=jax.ShapeDtypeStruct((n, D_MODEL), F32),
        scratch_shapes=[pltpu.VMEM((bm, D_MODEL), BF16)],
        compiler_params=_params(2, vmem),
        name="ffn",
    )(x, g, mod, mod, mod, wg, wu, wd, fg)


def kernel(x, c, ctx, c_ctx, w_mod, b_mod, norm1_g, norm2_g, w_in, w_gate_f, b_gate_f,
           w_gate_b, b_gate_b, gla_norm_g, conv_w, w_out, w_ffn_gate, w_ffn_up, w_ffn_down,
           final_g):
    assert x.shape[0] == 1 and w_mod.shape[0] == 1, "batch 1, depth 1 only"
    seq, ctx_len = x.shape[1], ctx.shape[1]
    x2, ctx2 = x[0], ctx[0]

    c_t = jnp.concatenate(
        [c[0][:, None], c_ctx[:, None], jnp.zeros((D_MODEL, SUBLANES - 2), F32)], axis=1)
    mod = _mod_call(c_t, w_mod[0], b_mod, EARLY_MOD_COLS)

    pad_f = jnp.zeros((LANES, GLA_KW), F32).at[:GATE_RANK].set(w_gate_f[0]).astype(BF16)
    pad_b = jnp.zeros((LANES, GLA_KW), F32).at[GATE_RANK:2 * GATE_RANK].set(w_gate_b[0]).astype(BF16)
    g1 = norm1_g[0][None]
    w_in_t = jnp.transpose(w_in[0])

    h_c, l_c = _prenorm_call(ctx2, g1, mod, w_in_t, row=1, bm=ctx_len)
    p_c, = _in_proj_call(h_c, w_in_t, bm=ctx_len, n_tiles=2)
    zero_state = jnp.zeros((GLA_HEADS, GLA_DK, GLA_DV), F32)
    _, _, s_f, s_b = _gla_call(p_c, l_c, pad_f, b_gate_f, pad_b, b_gate_b, zero_state,
                               zero_state, tb=ctx_len)

    h_x, l_x = _prenorm_call(x2, g1, mod, w_in_t, row=0, bm=1024)
    p_x, wg_b, wu_b, mod_late = _in_proj_call(
        h_x, w_in_t, (w_ffn_gate[0], w_ffn_up[0], c_t, w_mod[0], b_mod, EARLY_MOD_COLS),
        bm=1024, n_tiles=P_COLS // P_BN)
    o_f, o_b, _, _, wd_b, wo_b = _gla_call(p_x, l_x, pad_f, b_gate_f, pad_b, b_gate_b, s_f, s_b,
                                           (w_ffn_down[0], w_out[0]), tb=256)
    x_mid = _mix_call(o_f, o_b, p_x, x2, mod_late, gla_norm_g, conv_w[0], wo_b, bm=256)
    out = _ffn_call(x_mid, norm2_g[0][None], mod_late, wg_b, wu_b, wd_b, final_g[None],
                    bm=1024)
    return out[None]
```
